```python
import jax
import jax.numpy as jnp
from jax import lax
import numpy as np

D_MODEL = 2048
BATCH = 4
SEQ = 2048
DEPTH = 2
DEC_BATCH = 8
DEC_SEQ = 4
PAST_LEN = 16384
PAGE_SIZE = 128

N_GLA_LAYERS = (DEPTH + 1) // 2
N_NSA_LAYERS = DEPTH // 2
DN_ALPHA = (2 * DEPTH) ** 0.25
DN_BETA = (8 * DEPTH) ** -0.25
LN_EPS = 1e-5
D_FF = -(-8 * D_MODEL // (3 * 256)) * 256

GLA_HEADS = 4
GLA_DK = D_MODEL // 2 // GLA_HEADS
GLA_DV = D_MODEL // GLA_HEADS
GLA_GATE_RANK = 16
GLA_TAU = 16.0
GLA_CHUNK = 32
GLA_IN_W = 2 * GLA_HEADS * GLA_DK + 2 * GLA_HEADS * GLA_DV + GLA_GATE_RANK

NSA_HEADS = 16
NSA_KV_HEADS = 4
NSA_HEAD_DIM = D_MODEL // NSA_HEADS
NSA_GROUP = NSA_HEADS // NSA_KV_HEADS
NSA_BLOCK = 64
NSA_TOPK = 16
NSA_WINDOW = 512
NSA_QBLOCK = 32
NSA_IN_W = NSA_HEADS * NSA_HEAD_DIM + 6 * NSA_KV_HEADS * NSA_HEAD_DIM + 3 * NSA_HEADS

kernel_name = "gla_nsa_hybrid_deepnorm_adaln_step"


def layer_norm(x, g, b):
    xf = x.astype(jnp.float32)
    mu = jnp.mean(xf, axis=-1, keepdims=True)
    var = jnp.mean(jnp.square(xf - mu), axis=-1, keepdims=True)
    return ((xf - mu) * lax.rsqrt(var + LN_EPS) * g + b).astype(x.dtype)


def adaln(c, w, b):
    m = jax.nn.silu(c) @ w + b
    return jnp.split(m[:, None, :], 6, axis=-1)


def swiglu(h, w_in, w_out):
    a, u = jnp.split(h @ w_in, 2, axis=-1)
    return (jax.nn.silu(a) * u) @ w_out


def masked_softmax(s, mask):
    s = jnp.where(mask, s.astype(jnp.float32), -jnp.inf)
    m = jnp.max(s, axis=-1, keepdims=True)
    m = jnp.where(jnp.isfinite(m), m, 0.0)
    e = jnp.where(mask, jnp.exp(s - m), 0.0)
    return e / jnp.maximum(jnp.sum(e, axis=-1, keepdims=True), jnp.finfo(jnp.float32).tiny)


def gla_recurrence(q, k, v, log_a, s0):
    B, T = q.shape[:2]
    n_chunks = -(-T // GLA_CHUNK)
    pad = n_chunks * GLA_CHUNK - T

    def to_chunks(a):
        a = jnp.pad(a.astype(jnp.float32), ((0, 0), (0, pad), (0, 0), (0, 0)))
        return a.reshape(B, n_chunks, GLA_CHUNK, *a.shape[2:]).transpose(1, 0, 3, 2, 4)

    qc, kc, vc, lc = to_chunks(q), to_chunks(k), to_chunks(v), to_chunks(log_a)
    causal = jnp.tril(jnp.ones((GLA_CHUNK, GLA_CHUNK), dtype=bool))
    mid = GLA_CHUNK // 2

    def step(S, xs):
        qb, kb, vb, lb = xs
        b = jnp.cumsum(lb, axis=2)
        b_last = b[:, :, -1:, :]
        b_mid = b[:, :, mid - 1:mid, :]
        o_inter = jnp.einsum('bhcd,bhde->bhce', qb * jnp.exp(b), S)
        a = jnp.einsum('bhid,bhjd->bhij', qb * jnp.exp(b - b_mid), kb * jnp.exp(b_mid - b))
        o_intra = jnp.einsum('bhij,bhje->bhie', jnp.where(causal, a, 0.0), vb)
        S_new = (jnp.exp(b_last[:, :, 0, :])[..., None] * S
                 + jnp.einsum('bhcd,bhce->bhde', kb * jnp.exp(b_last - b), vb))
        return S_new, o_inter + o_intra

    s_final, o = lax.scan(step, s0.astype(jnp.float32), (qc, kc, vc, lc))
    o = o.transpose(1, 0, 3, 2, 4).reshape(B, n_chunks * GLA_CHUNK, GLA_HEADS, GLA_DV)[:, :T]
    return o, s_final


def gla_mixer(h, s0, w_in, w_alpha, b_alpha, norm_g, w_out):
    B, T, _ = h.shape
    qd, vd = GLA_HEADS * GLA_DK, GLA_HEADS * GLA_DV
    q, k, v, r, a_lr = jnp.split(h @ w_in, [qd, 2 * qd, 2 * qd + vd, 2 * qd + 2 * vd], axis=-1)
    log_a = jax.nn.log_sigmoid((a_lr @ w_alpha + b_alpha).astype(jnp.float32)) / GLA_TAU
    q = q.reshape(B, T, GLA_HEADS, GLA_DK) * (GLA_DK ** -0.5)
    k = k.reshape(B, T, GLA_HEADS, GLA_DK)
    v = v.reshape(B, T, GLA_HEADS, GLA_DV)
    o, s_new = gla_recurrence(q, k, v, log_a.reshape(B, T, GLA_HEADS, GLA_DK), s0)
    o = o * lax.rsqrt(jnp.mean(jnp.square(o), axis=-1, keepdims=True) + LN_EPS)
    o = (o.reshape(B, T, vd) * norm_g).astype(h.dtype)
    return (o * jax.nn.silu(r)) @ w_out, s_new.astype(s0.dtype)


def nsa_project(h, w_in, b_gate):
    B, T, _ = h.shape
    qd = NSA_HEADS * NSA_HEAD_DIM
    kvd = 2 * NSA_KV_HEADS * NSA_HEAD_DIM
    q, kv_c, kv_s, kv_w, g = jnp.split(h @ w_in, [qd, qd + kvd, qd + 2 * kvd, qd + 3 * kvd], axis=-1)
    kvshape = (B, T, 2, NSA_KV_HEADS, NSA_HEAD_DIM)
    return (q.reshape(B, T, NSA_HEADS, NSA_HEAD_DIM), kv_c.reshape(kvshape), kv_s.reshape(kvshape),
            kv_w.reshape(kvshape), (g + b_gate).reshape(B, T, NSA_HEADS, 3))


def compress_blocks(kv, n_blocks):
    B = kv.shape[0]
    c = kv[:, :n_blocks * NSA_BLOCK].astype(jnp.float32).reshape(
        B, n_blocks, NSA_BLOCK, 2, NSA_KV_HEADS, NSA_HEAD_DIM).mean(axis=2)
    return c[:, :, 0], c[:, :, 1]


def selection_blocks(kv, n_blocks):
    B, L = kv.shape[:2]
    kv = jnp.pad(kv, ((0, 0), (0, n_blocks * NSA_BLOCK - L), (0, 0), (0, 0), (0, 0)))
    kv = kv.reshape(B, n_blocks, NSA_BLOCK, 2, NSA_KV_HEADS, NSA_HEAD_DIM)
    return kv[:, :, :, 0].transpose(0, 3, 1, 2, 4), kv[:, :, :, 1].transpose(0, 3, 1, 2, 4)


def nsa_attend(q, gates, p0, kc, vc, ksel, vsel, kw, vw, kw_pos):
    B, T = q.shape[:2]
    qg = q.reshape(B, T, NSA_KV_HEADS, NSA_GROUP, NSA_HEAD_DIM).astype(jnp.float32) * (NSA_HEAD_DIM ** -0.5)
    qpos = p0 + jnp.arange(T, dtype=jnp.int32)
    nc = kc.shape[1]
    s_c = jnp.einsum('btkgd,bjkd->bkgtj', qg, kc.astype(jnp.float32))
    mask_c = ((jnp.arange(nc) + 1) * NSA_BLOCK - 1)[None, :] <= qpos[:, None]
    p_c = masked_softmax(s_c, mask_c)
    o_c = jnp.einsum('bkgtj,bjkd->btkgd', p_c, vc.astype(jnp.float32))
    ns = ksel.shape[2]
    imp = jnp.pad(jnp.sum(p_c, axis=2), ((0, 0), (0, 0), (0, 0), (0, ns - nc)))
    blk = jnp.arange(ns)[None, :]
    cur = (qpos // NSA_BLOCK)[:, None]
    forced = (blk == 0) | ((blk >= cur - 1) & (blk <= cur))
    score = jnp.where(blk > cur, -jnp.inf, jnp.where(forced, jnp.inf, imp))
    top_s, idx = lax.top_k(score, min(NSA_TOPK, ns))
    valid = top_s > -jnp.inf
    bi = jnp.arange(B)[:, None, None, None]
    ki = jnp.arange(NSA_KV_HEADS)[None, :, None, None]
    n_sel = idx.shape[-1] * NSA_BLOCK
    ks = ksel[bi, ki, idx].reshape(B, NSA_KV_HEADS, T, n_sel, NSA_HEAD_DIM).astype(jnp.float32)
    vs = vsel[bi, ki, idx].reshape(B, NSA_KV_HEADS, T, n_sel, NSA_HEAD_DIM).astype(jnp.float32)
    kpos = idx[..., None] * NSA_BLOCK + jnp.arange(NSA_BLOCK)
    mask_s = (valid[..., None] & (kpos <= qpos[None, None, :, None, None])).reshape(B, NSA_KV_HEADS, T, n_sel)
    s_s = jnp.einsum('btkgd,bktnd->bkgtn', qg, ks)
    p_s = masked_softmax(s_s, mask_s[:, :, None])
    o_s = jnp.einsum('bkgtn,bktnd->btkgd', p_s, vs)
    s_w = jnp.einsum('btkgd,bwkd->bkgtw', qg, kw.astype(jnp.float32))
    dpos = qpos[:, None] - kw_pos[None, :]
    mask_w = (dpos >= 0) & (dpos < NSA_WINDOW) & (kw_pos[None, :] >= 0)
    p_w = masked_softmax(s_w, mask_w)
    o_w = jnp.einsum('bkgtw,bwkd->btkgd', p_w, vw.astype(jnp.float32))
    g = jax.nn.sigmoid(gates.astype(jnp.float32)).reshape(B, T, NSA_KV_HEADS, NSA_GROUP, 3)
    o = g[..., 0:1] * o_c + g[..., 1:2] * o_s + g[..., 2:3] * o_w
    return o.reshape(B, T, NSA_HEADS * NSA_HEAD_DIM).astype(q.dtype)


def nsa_prompt(h, w_in, b_gate, w_out):
    B, T, _ = h.shape
    q, kv_c, kv_s, kv_w, gates = nsa_project(h, w_in, b_gate)
    kc, vc = compress_blocks(kv_c, T // NSA_BLOCK)
    ksel, vsel = selection_blocks(kv_s, -(-T // NSA_BLOCK))
    kw_pad = jnp.pad(kv_w, ((0, 0), (NSA_WINDOW, 0), (0, 0), (0, 0), (0, 0)))
    n_qb = T // NSA_QBLOCK
    qb = q.reshape(B, n_qb, NSA_QBLOCK, NSA_HEADS, NSA_HEAD_DIM).transpose(1, 0, 2, 3, 4)
    gb = gates.reshape(B, n_qb, NSA_QBLOCK, NSA_HEADS, 3).transpose(1, 0, 2, 3, 4)
    starts = jnp.arange(n_qb, dtype=jnp.int32) * NSA_QBLOCK

    def block(xs):
        q_blk, g_blk, p0 = xs
        kvw = lax.dynamic_slice_in_dim(kw_pad, p0, NSA_WINDOW + NSA_QBLOCK, axis=1)
        kw_pos = p0 - NSA_WINDOW + jnp.arange(NSA_WINDOW + NSA_QBLOCK, dtype=jnp.int32)
        return nsa_attend(q_blk, g_blk, p0, kc, vc, ksel, vsel, kvw[:, :, 0], kvw[:, :, 1], kw_pos)

    o = lax.map(block, (qb, gb, starts)).transpose(1, 0, 2, 3).reshape(B, T, NSA_HEADS * NSA_HEAD_DIM)
    win_state = kv_w[:, T - min(NSA_WINDOW, T):]
    return o @ w_out, kv_c, kv_s, win_state


def nsa_sample(h, past_cmp_pages, past_slc_pages, win_buf, w_in, b_gate, w_out):
    B, T, _ = h.shape
    past = past_cmp_pages.shape[1] * past_cmp_pages.shape[2]
    q, kv_c, kv_s, kv_w, gates = nsa_project(h, w_in, b_gate)
    kvshape = (B, past, 2, NSA_KV_HEADS, NSA_HEAD_DIM)
    full_c = jnp.concatenate([past_cmp_pages.reshape(kvshape), kv_c], axis=1)
    full_s = jnp.concatenate([past_slc_pages.reshape(kvshape), kv_s], axis=1)
    L = past + T
    kc, vc = compress_blocks(full_c, L // NSA_BLOCK)
    ksel, vsel = selection_blocks(full_s, -(-L // NSA_BLOCK))
    wb = win_buf.shape[1]
    all_w = jnp.concatenate([win_buf, kv_w], axis=1)
    kw_pos = past - wb + jnp.arange(wb + T, dtype=jnp.int32)
    o = nsa_attend(q, gates, past, kc, vc, ksel, vsel, all_w[:, :, 0], all_w[:, :, 1], kw_pos)
    keep = min(NSA_WINDOW, wb + T)
    return o @ w_out, kv_c, kv_s, all_w[:, wb + T - keep:]


def setup_inputs(seed: int = 0) -> dict:
    key = jax.random.key(seed)
    ks = jax.random.split(key, 32)

    def nrm(k, shape, scale):
        return jax.random.normal(k, shape, jnp.float32) * scale

    n_pages = PAST_LEN // PAGE_SIZE
    n_used = DEC_BATCH * n_pages
    n_phys = n_used + max(1, n_used // 4)
    wbuf = min(NSA_WINDOW, PAST_LEN)
    kvrow = (2, NSA_KV_HEADS, NSA_HEAD_DIM)
    page_table = jax.random.permutation(ks[8], n_phys)[:n_used].reshape(DEC_BATCH, n_pages).astype(jnp.int32)
    return {
        "x_prompt": nrm(ks[0], (BATCH, SEQ, D_MODEL), 1.0),
        "x_sample": nrm(ks[1], (DEC_BATCH, DEC_SEQ, D_MODEL), 1.0),
        "c_prompt": nrm(ks[2], (BATCH, D_MODEL), 1.0),
        "c_sample": nrm(ks[3], (DEC_BATCH, D_MODEL), 1.0),
        "state_gla": nrm(ks[4], (DEC_BATCH, N_GLA_LAYERS, GLA_HEADS, GLA_DK, GLA_DV), 1.0),
        "cache_cmp_kv": nrm(ks[5], (n_phys, N_NSA_LAYERS, PAGE_SIZE) + kvrow, 1.0),
        "cache_slc_kv": nrm(ks[6], (n_phys, N_NSA_LAYERS, PAGE_SIZE) + kvrow, 1.0),
        "cache_win_kv": nrm(ks[7], (DEC_BATCH, N_NSA_LAYERS, wbuf) + kvrow, 1.0),
        "page_table": page_table,
        "ada_w": nrm(ks[9], (DEPTH, D_MODEL, 6 * D_MODEL), 0.5 * D_MODEL ** -0.5),
        "ada_b": nrm(ks[10], (DEPTH, 6 * D_MODEL), 0.01),
        "gla_w_in": nrm(ks[11], (N_GLA_LAYERS, D_MODEL, GLA_IN_W), D_MODEL ** -0.5),
        "gla_w_alpha": nrm(ks[12], (N_GLA_LAYERS, GLA_GATE_RANK, GLA_HEADS * GLA_DK), GLA_GATE_RANK ** -0.5),
        "gla_b_alpha": nrm(ks[13], (N_GLA_LAYERS, GLA_HEADS * GLA_DK), 0.1),
        "gla_norm_g": 1.0 + nrm(ks[14], (N_GLA_LAYERS, GLA_HEADS * GLA_DV), 0.01),
        "gla_w_out": nrm(ks[15], (N_GLA_LAYERS, GLA_HEADS * GLA_DV, D_MODEL), DN_BETA * (GLA_HEADS * GLA_DV) ** -0.5),
        "nsa_w_in": nrm(ks[16], (N_NSA_LAYERS, D_MODEL, NSA_IN_W), D_MODEL ** -0.5),
        "nsa_b_gate": nrm(ks[17], (N_NSA_LAYERS, 3 * NSA_HEADS), 0.1),
        "nsa_w_out": nrm(ks[18], (N_NSA_LAYERS, NSA_HEADS * NSA_HEAD_DIM, D_MODEL), DN_BETA * (NSA_HEADS * NSA_HEAD_DIM) ** -0.5),
        "ln_mix_g": 1.0 + nrm(ks[19], (DEPTH, D_MODEL), 0.01),
        "ln_mix_b": nrm(ks[20], (DEPTH, D_MODEL), 0.01),
        "ffn_w_in": nrm(ks[21], (DEPTH, D_MODEL, 2 * D_FF), D_MODEL ** -0.5),
        "ffn_w_out": nrm(ks[22], (DEPTH, D_FF, D_MODEL), DN_BETA * D_FF ** -0.5),
        "ln_ffn_g": 1.0 + nrm(ks[23], (DEPTH, D_MODEL), 0.01),
        "ln_ffn_b": nrm(ks[24], (DEPTH, D_MODEL), 0.01),
    }


def reference(x_prompt, x_sample, c_prompt, c_sample, state_gla, cache_cmp_kv, cache_slc_kv,
              cache_win_kv, page_table, ada_w, ada_b, gla_w_in, gla_w_alpha, gla_b_alpha, gla_norm_g,
              gla_w_out, nsa_w_in, nsa_b_gate, nsa_w_out, ln_mix_g, ln_mix_b, ffn_w_in, ffn_w_out,
              ln_ffn_g, ln_ffn_b):
    xp, xd = x_prompt, x_sample
    gla_p, gla_s, cmp_p, cmp_s, slc_p, slc_s, win_p, win_s = [], [], [], [], [], [], [], []
    for i in range(DEPTH):
        j = i // 2
        shp1, scp1, gtp1, shp2, scp2, gtp2 = adaln(c_prompt, ada_w[i], ada_b[i])
        shd1, scd1, gtd1, shd2, scd2, gtd2 = adaln(c_sample, ada_w[i], ada_b[i])
        hp = xp * (1.0 + scp1) + shp1
        hd = xd * (1.0 + scd1) + shd1
        if i % 2 == 0:
            s0 = jnp.zeros((xp.shape[0], GLA_HEADS, GLA_DK, GLA_DV), xp.dtype)
            mp, sp = gla_mixer(hp, s0, gla_w_in[j], gla_w_alpha[j], gla_b_alpha[j], gla_norm_g[j], gla_w_out[j])
            md, sd = gla_mixer(hd, state_gla[:, j], gla_w_in[j], gla_w_alpha[j], gla_b_alpha[j], gla_norm_g[j], gla_w_out[j])
            gla_p.append(sp)
            gla_s.append(sd)
        else:
            mp, cp, lp, wp = nsa_prompt(hp, nsa_w_in[j], nsa_b_gate[j], nsa_w_out[j])
            md, cd, ld, wd = nsa_sample(hd, cache_cmp_kv[page_table, j], cache_slc_kv[page_table, j],
                                        cache_win_kv[:, j], nsa_w_in[j], nsa_b_gate[j], nsa_w_out[j])
            cmp_p.append(cp)
            cmp_s.append(cd)
            slc_p.append(lp)
            slc_s.append(ld)
            win_p.append(wp)
            win_s.append(wd)
        xp = layer_norm(DN_ALPHA * xp + gtp1 * mp, ln_mix_g[i], ln_mix_b[i])
        xd = layer_norm(DN_ALPHA * xd + gtd1 * md, ln_mix_g[i], ln_mix_b[i])
        fp = swiglu(xp * (1.0 + scp2) + shp2, ffn_w_in[i], ffn_w_out[i])
        fd = swiglu(xd * (1.0 + scd2) + shd2, ffn_w_in[i], ffn_w_out[i])
        xp = layer_norm(DN_ALPHA * xp + gtp2 * fp, ln_ffn_g[i], ln_ffn_b[i])
        xd = layer_norm(DN_ALPHA * xd + gtd2 * fd, ln_ffn_g[i], ln_ffn_b[i])
    gla_state_p = jnp.stack(gla_p, axis=1)
    gla_state_s = jnp.stack(gla_s, axis=1)
    cmp_kv_p = jnp.stack(cmp_p, axis=1)
    cmp_kv_s = jnp.stack(cmp_s, axis=1)
    slc_kv_p = jnp.stack(slc_p, axis=1)
    slc_kv_s = jnp.stack(slc_s, axis=1)
    win_kv_p = jnp.stack(win_p, axis=1)
    win_kv_s = jnp.stack(win_s, axis=1)
    return (xp, xd, gla_state_p, gla_state_s, cmp_kv_p, cmp_kv_s, slc_kv_p, slc_kv_s, win_kv_p, win_kv_s)
```

```python
import functools

import jax
import jax.numpy as jnp
from jax import lax
from jax.experimental import pallas as pl
from jax.experimental.pallas import tpu as pltpu

F32 = jnp.float32
BF16 = jnp.bfloat16

D_MODEL = 2048
DEPTH = 2
DN_ALPHA = (2 * DEPTH) ** 0.25
LN_EPS = 1e-5
D_FF = 5632
GLA_HEADS = 4
GLA_DK = 256
GLA_DV = 512
GLA_RANK = 16
GLA_TAU = 16.0
GLA_SUB = 32
GLA_MAIN_W = 2 * GLA_HEADS * GLA_DK + 2 * GLA_HEADS * GLA_DV
NSA_HEADS = 16
NSA_KV = 4
NSA_HD = 128
NSA_GROUP = NSA_HEADS // NSA_KV
NSA_BLOCK = 64
NSA_TOPK = 16
NSA_WINDOW = 512
NSA_Q_W = NSA_HEADS * NSA_HD
NSA_KV_W = 2 * NSA_KV * NSA_HD
NSA_MAIN_W = NSA_Q_W + 3 * NSA_KV_W
PAGE = 128

LANES = 128
V7X_VMEM_BYTES = 64 * 1024 * 1024
VMEM_LIMIT = V7X_VMEM_BYTES * 7 // 8

NEG_BIG = -1e30
TINY = float(jnp.finfo(jnp.float32).tiny)


def _params(*sem):
    return pltpu.CompilerParams(dimension_semantics=sem, vmem_limit_bytes=VMEM_LIMIT)


def _dot(a, b):
    return jnp.dot(a.astype(BF16), b.astype(BF16), preferred_element_type=F32)


def _dot_nt(a, b):
    return lax.dot_general(a.astype(BF16), b.astype(BF16), (((1,), (1,)), ((), ())),
                           preferred_element_type=F32)


def _dot_tn(a, b):
    return lax.dot_general(a.astype(BF16), b.astype(BF16), (((0,), (0,)), ((), ())),
                           preferred_element_type=F32)


def _split3(x):
    hi = x.astype(BF16)
    r1 = x - hi.astype(F32)
    mid = r1.astype(BF16)
    lo = (r1 - mid.astype(F32)).astype(BF16)
    return hi, mid, lo


def _silu(x):
    return x * jax.nn.sigmoid(x)


def _ind(cond, dtype=F32):
    return jnp.where(cond, 1.0, 0.0).astype(dtype)


def _masked_softmax(s, mask, axis):
    s = jnp.where(mask, s, -jnp.inf)
    m = jnp.max(s, axis=axis, keepdims=True)
    m = jnp.where(jnp.isfinite(m), m, 0.0)
    e = jnp.where(mask, jnp.exp(s - m), 0.0)
    return e / jnp.maximum(jnp.sum(e, axis=axis, keepdims=True), TINY)


def _adaln_kernel(c_ref, w_ref, b_ref, o_ref):
    o_ref[...] = _dot(_silu(c_ref[...]), w_ref[...]) + b_ref[...]


def _adaln(c_all, ada_w, ada_b):
    rows = c_all.shape[0]
    n = ada_w.shape[2]
    tn = 1024
    return pl.pallas_call(
        _adaln_kernel,
        grid=(DEPTH, n // tn),
        in_specs=[
            pl.BlockSpec((rows, D_MODEL), lambda l, j: (0, 0)),
            pl.BlockSpec((None, D_MODEL, tn), lambda l, j: (l, 0, j)),
            pl.BlockSpec((None, 1, tn), lambda l, j: (l, 0, j)),
        ],
        out_specs=pl.BlockSpec((None, rows, tn), lambda l, j: (l, 0, j)),
        out_shape=jax.ShapeDtypeStruct((DEPTH, rows, n), F32),
        compiler_params=_params("parallel", "parallel"),
        name="adaln",
    )(c_all, ada_w, ada_b.reshape(DEPTH, 1, n))


def _proj_kernel(x_ref, sc_ref, sh_ref, w_ref, wt_ref, o_ref, ot_ref, h_ref):
    @pl.when(pl.program_id(1) == 0)
    def _():
        h = (x_ref[...] * (1.0 + sc_ref[...]) + sh_ref[...]).astype(BF16)
        h_ref[...] = h
        ot_ref[...] = _dot(h, wt_ref[...])

    o_ref[...] = _dot(h_ref[...], w_ref[...])


def _proj(x, sc, sh, w, layer, n_main, tm, name):
    m = x.shape[0]
    n_tail = w.shape[2] - n_main
    w_tail = jnp.pad(w[layer, :, n_main:], ((0, 0), (0, LANES - n_tail)))
    tn = 512
    groups, mod_rows = sc.shape[0], sc.shape[1]
    tiles_per_group = m // tm // groups
    mod_spec = pl.BlockSpec((None, mod_rows, D_MODEL), lambda i, j: (i // tiles_per_group, 0, 0))
    return pl.pallas_call(
        _proj_kernel,
        grid=(m // tm, n_main // tn),
        in_specs=[
            pl.BlockSpec((tm, D_MODEL), lambda i, j: (i, 0)),
            mod_spec,
            mod_spec,
            pl.BlockSpec((None, D_MODEL, tn), lambda i, j: (layer, 0, j)),
            pl.BlockSpec((D_MODEL, LANES), lambda i, j: (0, 0)),
        ],
        out_specs=[
            pl.BlockSpec((tm, tn), lambda i, j: (i, j)),
            pl.BlockSpec((tm, LANES), lambda i, j: (i, 0)),
        ],
        out_shape=[jax.ShapeDtypeStruct((m, n_main), F32), jax.ShapeDtypeStruct((m, LANES), F32)],
        scratch_shapes=[pltpu.VMEM((tm, D_MODEL), BF16)],
        compiler_params=_params("parallel", "arbitrary"),
        name=name,
    )(x, sc, sh, w, w_tail)


def _out_ln_kernel(a_ref, w_ref, x_ref, gt_ref, g_ref, b_ref, o_ref, *, nk):
    k = pl.program_id(1)
    part = _dot(a_ref[...], w_ref[...])

    @pl.when(k == 0)
    def _():
        o_ref[...] = part

    @pl.when(k > 0)
    def _():
        o_ref[...] += part

    @pl.when(k == nk - 1)
    def _():
        y = DN_ALPHA * x_ref[...] + gt_ref[...] * o_ref[...]
        mu = jnp.mean(y, axis=-1, keepdims=True)
        yc = y - mu
        var = jnp.mean(yc * yc, axis=-1, keepdims=True)
        o_ref[...] = yc * lax.rsqrt(var + LN_EPS) * g_ref[...] + b_ref[...]


def _out_ln(a, w, w_layer, x, gate, ln_g, ln_b, ln_layer, tm, name):
    m, kdim = a.shape
    tk = 512
    nk = kdim // tk
    groups, mod_rows = gate.shape[0], gate.shape[1]
    tiles_per_group = m // tm // groups
    vec_spec = pl.BlockSpec((None, 1, D_MODEL), lambda i, k: (ln_layer, 0, 0))
    return pl.pallas_call(
        functools.partial(_out_ln_kernel, nk=nk),
        grid=(m // tm, nk),
        in_specs=[
            pl.BlockSpec((tm, tk), lambda i, k: (i, k)),
            pl.BlockSpec((None, tk, D_MODEL), lambda i, k: (w_layer, k, 0)),
            pl.BlockSpec((tm, D_MODEL), lambda i, k: (i, 0)),
            pl.BlockSpec((None, mod_rows, D_MODEL), lambda i, k: (i // tiles_per_group, 0, 0)),
            vec_spec,
            vec_spec,
        ],
        out_specs=pl.BlockSpec((tm, D_MODEL), lambda i, k: (i, 0)),
        out_shape=jax.ShapeDtypeStruct((m, D_MODEL), F32),
        compiler_params=_params("parallel", "arbitrary"),
        name=name,
    )(a, w, x, gate, ln_g.reshape(-1, 1, D_MODEL), ln_b.reshape(-1, 1, D_MODEL))


def _ffn_in_kernel(x_ref, sc_ref, sh_ref, wa_ref, wu_ref, o_ref, h_ref):
    @pl.when(pl.program_id(1) == 0)
    def _():
        h_ref[...] = (x_ref[...] * (1.0 + sc_ref[...]) + sh_ref[...]).astype(BF16)

    h = h_ref[...]
    a = _dot(h, wa_ref[...])
    u = _dot(h, wu_ref[...])
    o_ref[...] = (_silu(a) * u).astype(BF16)


def _ffn_in(x, sc, sh, w, layer, tm, name):
    m = x.shape[0]
    tn = 512
    nj = D_FF // tn
    groups, mod_rows = sc.shape[0], sc.shape[1]
    tiles_per_group = m // tm // groups
    mod_spec = pl.BlockSpec((None, mod_rows, D_MODEL), lambda i, j: (i // tiles_per_group, 0, 0))
    return pl.pallas_call(
        _ffn_in_kernel,
        grid=(m // tm, nj),
        in_specs=[
            pl.BlockSpec((tm, D_MODEL), lambda i, j: (i, 0)),
            mod_spec,
            mod_spec,
            pl.BlockSpec((None, D_MODEL, tn), lambda i, j: (layer, 0, j)),
            pl.BlockSpec((None, D_MODEL, tn), lambda i, j: (layer, 0, j + nj)),
        ],
        out_specs=pl.BlockSpec((tm, tn), lambda i, j: (i, j)),
        out_shape=jax.ShapeDtypeStruct((m, D_FF), BF16),
        scratch_shapes=[pltpu.VMEM((tm, D_MODEL), BF16)],
        compiler_params=_params("parallel", "arbitrary"),
        name=name,
    )(x, sc, sh, w, w)


def _gla_kernel(*refs, chunk, t_valid, has_s0):
    if has_s0:
        q_ref, k_ref, v_ref, r_ref, a_ref, wa_ref, ba_ref, g_ref, s0_ref, o_ref, st_ref = refs
    else:
        q_ref, k_ref, v_ref, r_ref, a_ref, wa_ref, ba_ref, g_ref, o_ref, st_ref = refs
    sub = GLA_SUB
    n_sub = chunk // sub
    anchor = sub // 2 - 1

    @pl.when(pl.program_id(1) == 0)
    def _():
        if has_s0:
            st_ref[...] = s0_ref[...]
        else:
            st_ref[...] = jnp.zeros(st_ref.shape, F32)

    row = lax.broadcasted_iota(jnp.int32, (chunk, 1), 0)
    tri = _ind(lax.broadcasted_iota(jnp.int32, (chunk, chunk), 1)
               <= lax.broadcasted_iota(jnp.int32, (chunk, chunk), 0), BF16)
    ones = jnp.ones((chunk, LANES), BF16)
    causal = (lax.broadcasted_iota(jnp.int32, (sub, sub), 1)
              <= lax.broadcasted_iota(jnp.int32, (sub, sub), 0))
    z_all = _dot(a_ref[...], wa_ref[...]) + ba_ref[...]

    for h in range(GLA_HEADS):
        dk = slice(h * GLA_DK, (h + 1) * GLA_DK)
        dv = slice(h * GLA_DV, (h + 1) * GLA_DV)
        z = z_all[:, dk]
        lb = (jnp.minimum(z, 0.0) - jnp.log1p(jnp.exp(-jnp.abs(z)))) * (1.0 / GLA_TAU)
        kh = k_ref[:, dk]
        if t_valid < chunk:
            lb = jnp.where(row < t_valid, lb, 0.0)
            kh = jnp.where(row < t_valid, kh, 0.0)
        qh = q_ref[:, dk] * (GLA_DK ** -0.5)
        vh = v_ref[:, dv].astype(BF16)
        lb3 = _split3(lb)
        b = sum(jnp.dot(tri, t, preferred_element_type=F32) for t in lb3)
        b_last = b[chunk - 1:chunk, :]
        s_old = st_ref[0, h]
        o = _dot(qh * jnp.exp(b), s_old)
        o_rows = []
        for s in range(n_sub):
            rs = slice(s * sub, (s + 1) * sub)
            bs, qs, ks = b[rs], qh[rs], kh[rs]
            b_mid = b[s * sub + anchor:s * sub + anchor + 1, :]
            a_diag = _dot_nt(qs * jnp.exp(bs - b_mid), ks * jnp.exp(b_mid - bs))
            o_s = _dot(jnp.where(causal, a_diag, 0.0), vh[rs])
            if s > 0:
                prev = slice(0, s * sub)
                b_in = b[s * sub - 1:s * sub, :]
                a_off = _dot_nt(qs * jnp.exp(bs - b_in), kh[prev] * jnp.exp(b_in - b[prev]))
                o_s = o_s + _dot(a_off, vh[prev])
            o_rows.append(o_s)
        o = o + jnp.concatenate(o_rows, axis=0)
        dec = jnp.exp(sum(lax.dot_general(t, ones, (((0,), (0,)), ((), ())),
                                          preferred_element_type=F32) for t in lb3))
        st_ref[0, h] = s_old * jnp.tile(dec, (1, GLA_DV // LANES)) + _dot_tn(kh * jnp.exp(b_last - b), vh)
        o = o * lax.rsqrt(jnp.mean(o * o, axis=-1, keepdims=True) + LN_EPS)
        o_ref[:, dv] = (o * g_ref[:, dv] * _silu(r_ref[:, dv])).astype(BF16)


def _gla(main, tail, w_alpha, b_alpha, norm_g, s0, batch, seq, chunk, t_valid):
    n_chunks = seq // chunk
    qk_w = GLA_HEADS * GLA_DK
    v_w = GLA_HEADS * GLA_DV
    wa = jnp.pad(w_alpha, ((0, LANES - GLA_RANK), (0, 0)))
    row_map = lambda b, c: (b * n_chunks + c, 0)
    const = lambda b, c: (0, 0)
    in_specs = [
        pl.BlockSpec((chunk, qk_w), row_map),
        pl.BlockSpec((chunk, qk_w), lambda b, c: (b * n_chunks + c, 1)),
        pl.BlockSpec((chunk, v_w), lambda b, c: (b * n_chunks + c, 1)),
        pl.BlockSpec((chunk, v_w), lambda b, c: (b * n_chunks + c, 2)),
        pl.BlockSpec((chunk, LANES), row_map),
        pl.BlockSpec((LANES, qk_w), const),
        pl.BlockSpec((1, qk_w), const),
        pl.BlockSpec((1, v_w), const),
    ]
    args = [main, main, main, main, tail, wa, b_alpha.reshape(1, qk_w), norm_g.reshape(1, v_w)]
    state_spec = pl.BlockSpec((1, GLA_HEADS, GLA_DK, GLA_DV), lambda b, c: (b, 0, 0, 0))
    if s0 is not None:
        in_specs.append(state_spec)
        args.append(s0)
    return pl.pallas_call(
        functools.partial(_gla_kernel, chunk=chunk, t_valid=t_valid, has_s0=s0 is not None),
        grid=(batch, n_chunks),
        in_specs=in_specs,
        out_specs=[pl.BlockSpec((chunk, v_w), row_map), state_spec],
        out_shape=[jax.ShapeDtypeStruct((batch * seq, v_w), BF16),
                   jax.ShapeDtypeStruct((batch, GLA_HEADS, GLA_DK, GLA_DV), F32)],
        compiler_params=_params("parallel", "arbitrary"),
        name="gla_prompt" if s0 is None else "gla_sample",
    )(*args)


def _select_blocks(imp_t, blk, cur):
    n = imp_t.shape[0]
    forced = (blk == 0) | ((blk >= cur - 1) & (blk <= cur))
    score = jnp.where(blk > cur, -jnp.inf, jnp.where(forced, jnp.inf, imp_t))
    rank = jnp.zeros(score.shape, F32)
    for i in range(n):
        si = score[i:i + 1, :]
        rank = rank + jnp.where(blk > i, _ind(si >= score), _ind(si > score))
    return (rank < NSA_TOPK) & (score > -jnp.inf)


def _nsa_prompt_kernel(q_ref, kc_ref, vc_ref, ks_ref, vs_ref, kw_ref, vw_ref, gt_ref, bg_ref, o_ref,
                       kcm, vcm, ksb, vsb, kwb, vwb, qs_ref, msk_ref, m_ref, l_ref, acc_ref, os_ref,
                       *, seq, tq, tk):
    qi = pl.program_id(2)
    n_blk = seq // NSA_BLOCK
    g_rows = NSA_GROUP * tq

    @pl.when(qi == 0)
    def _():
        kcm[...] = jnp.sum(kc_ref[...].reshape(n_blk, NSA_BLOCK, NSA_HD), axis=1) * (1.0 / NSA_BLOCK)
        vcm[...] = jnp.sum(vc_ref[...].reshape(n_blk, NSA_BLOCK, NSA_HD), axis=1) * (1.0 / NSA_BLOCK)
        ksb[...] = ks_ref[...].astype(BF16)
        vsb[...] = vs_ref[...].astype(BF16)
        kwb[...] = kw_ref[...].astype(BF16)
        vwb[...] = vw_ref[...].astype(BF16)

    t0 = qi * tq
    q = q_ref[...] * (NSA_HD ** -0.5)
    qs_ref[...] = jnp.concatenate(
        [q[:, g * NSA_HD:(g + 1) * NSA_HD] for g in range(NSA_GROUP)], axis=0).astype(BF16)
    qs = qs_ref[...]
    t_col = t0 + lax.broadcasted_iota(jnp.int32, (tq, 1), 0)
    t_col_g = jnp.concatenate([t_col] * NSA_GROUP, axis=0)

    kc_b = kcm[...].astype(BF16)
    vc_b = vcm[...].astype(BF16)
    blk_row = lax.broadcasted_iota(jnp.int32, (1, n_blk), 1)
    p_c = _masked_softmax(_dot_nt(qs, kc_b), (blk_row + 1) * NSA_BLOCK - 1 <= t_col_g, axis=-1)
    o_c = _dot(p_c, vc_b)

    blk_col = lax.broadcasted_iota(jnp.int32, (n_blk, 1), 0)
    t_row = t0 + lax.broadcasted_iota(jnp.int32, (1, tq), 1)
    t_row_g = jnp.concatenate([t_row] * NSA_GROUP, axis=1)
    p_t = _masked_softmax(_dot_nt(kc_b, qs), (blk_col + 1) * NSA_BLOCK - 1 <= t_row_g, axis=0)
    imp_t = sum(p_t[:, g * tq:(g + 1) * tq] for g in range(NSA_GROUP))
    sel_t = _ind(_select_blocks(imp_t, blk_col, t_row // NSA_BLOCK), BF16)
    n_kt = seq // tk
    for kt in range(n_kt):
        n_row = kt * tk + lax.broadcasted_iota(jnp.int32, (1, tk), 1)
        expand = _ind(n_row // NSA_BLOCK == blk_col, BF16)
        chosen = _dot_tn(sel_t, expand) > 0.5
        msk_ref[kt] = _ind(chosen & (n_row <= t_col))

    def attend(k_ref, v_ref, lo, hi, mask_fn):
        m_ref[...] = jnp.full(m_ref.shape, NEG_BIG, F32)
        l_ref[...] = jnp.zeros(l_ref.shape, F32)
        acc_ref[...] = jnp.zeros(acc_ref.shape, F32)

        def body(kt, carry):
            k0 = pl.multiple_of(kt * tk, tk)
            mask = mask_fn(kt, k0)
            s = jnp.where(mask, _dot_nt(qs_ref[...], k_ref[pl.ds(k0, tk), :]), NEG_BIG)
            m_old = m_ref[...]
            m_new = jnp.maximum(m_old, jnp.max(s, axis=-1, keepdims=True))
            p = jnp.where(mask, jnp.exp(s - m_new), 0.0)
            alpha = jnp.exp(m_old - m_new)
            l_ref[...] = alpha * l_ref[...] + jnp.sum(p, axis=-1, keepdims=True)
            acc_ref[...] = alpha * acc_ref[...] + _dot(p, v_ref[pl.ds(k0, tk), :])
            m_ref[...] = m_new
            return carry

        lax.fori_loop(lo, hi, body, 0)
        return acc_ref[...] / jnp.maximum(l_ref[...], TINY)

    def sel_mask(kt, k0):
        return jnp.concatenate([msk_ref[kt]] * NSA_GROUP, axis=0) > 0.5

    def win_mask(kt, k0):
        d = t_col_g - (k0 + lax.broadcasted_iota(jnp.int32, (1, tk), 1))
        return (d >= 0) & (d < NSA_WINDOW)

    last_kt = (t0 + tq - 1) // tk
    os_ref[...] = attend(ksb, vsb, 0, last_kt + 1, sel_mask)
    first_kt = jnp.maximum(t0 - (NSA_WINDOW - 1), 0) // tk
    o_w = attend(kwb, vwb, first_kt, last_kt + 1, win_mask)
    o_s = os_ref[...]

    gates = jax.nn.sigmoid(gt_ref[...] + bg_ref[...])
    for g in range(NSA_GROUP):
        rows = slice(g * tq, (g + 1) * tq)
        o = (gates[:, 3 * g:3 * g + 1] * o_c[rows] + gates[:, 3 * g + 1:3 * g + 2] * o_s[rows]
             + gates[:, 3 * g + 2:3 * g + 3] * o_w[rows])
        o_ref[:, g * NSA_HD:(g + 1) * NSA_HD] = o.astype(BF16)


def _nsa_prompt(main, gates, b_gate, batch, seq):
    tq, tk = 256, 256
    nq = seq // tq
    q_lanes = NSA_GROUP * NSA_HD
    col0 = NSA_Q_W // NSA_HD

    def kv_spec(branch, part):
        off = col0 + (2 * branch + part) * NSA_KV
        return pl.BlockSpec((seq, NSA_HD), lambda b, kv, qi: (b, off + kv))

    g_w = 3 * NSA_GROUP
    g_rows = NSA_GROUP * tq
    return pl.pallas_call(
        functools.partial(_nsa_prompt_kernel, seq=seq, tq=tq, tk=tk),
        grid=(batch, NSA_KV, nq),
        in_specs=[pl.BlockSpec((tq, q_lanes), lambda b, kv, qi: (b * nq + qi, kv))]
        + [kv_spec(br, part) for br in range(3) for part in range(2)]
        + [pl.BlockSpec((None, tq, g_w), lambda b, kv, qi: (kv, b * nq + qi, 0)),
           pl.BlockSpec((None, 1, g_w), lambda b, kv, qi: (kv, 0, 0))],
        out_specs=pl.BlockSpec((tq, q_lanes), lambda b, kv, qi: (b * nq + qi, kv)),
        out_shape=jax.ShapeDtypeStruct((batch * seq, NSA_Q_W), BF16),
        scratch_shapes=[
            pltpu.VMEM((seq // NSA_BLOCK, NSA_HD), F32),
            pltpu.VMEM((seq // NSA_BLOCK, NSA_HD), F32),
            pltpu.VMEM((seq, NSA_HD), BF16),
            pltpu.VMEM((seq, NSA_HD), BF16),
            pltpu.VMEM((seq, NSA_HD), BF16),
            pltpu.VMEM((seq, NSA_HD), BF16),
            pltpu.VMEM((g_rows, NSA_HD), BF16),
            pltpu.VMEM((seq // tk, tq, tk), F32),
            pltpu.VMEM((g_rows, 1), F32),
            pltpu.VMEM((g_rows, 1), F32),
            pltpu.VMEM((g_rows, NSA_HD), F32),
            pltpu.VMEM((g_rows, NSA_HD), F32),
        ],
        compiler_params=_params("parallel", "parallel", "arbitrary"),
        name="nsa_prompt",
    )(main, main, main, main, main, main, main, gates, b_gate)


def _page_means_kernel(pt_ref, x_ref, o_ref):
    x = x_ref[0]
    o_ref[0, 0] = jnp.sum(x.reshape(PAGE // NSA_BLOCK, NSA_BLOCK, NSA_KV_W), axis=1) * (1.0 / NSA_BLOCK)


def _page_means(cache, page_table):
    seqs, n_pages = page_table.shape
    per_page = PAGE // NSA_BLOCK
    return pl.pallas_call(
        _page_means_kernel,
        grid_spec=pltpu.PrefetchScalarGridSpec(
            num_scalar_prefetch=1,
            grid=(seqs, n_pages),
            in_specs=[pl.BlockSpec((1, PAGE, NSA_KV_W), lambda b, p, pt: (pt[b, p], 0, 0))],
            out_specs=pl.BlockSpec((1, 1, per_page, NSA_KV_W), lambda b, p, pt: (b, p, 0, 0)),
        ),
        out_shape=jax.ShapeDtypeStruct((seqs, n_pages, per_page, NSA_KV_W), F32),
        compiler_params=_params("parallel", "arbitrary"),
        name="nsa_page_means",
    )(page_table, cache)


def _nsa_sample_kernel(pt_ref, q_ref, cm_ref, pg_ref, kvn_ref, wb_ref, kwn_ref, gt_ref, bg_ref, o_ref,
                       qr_ref, sel_ref, m_ref, l_ref, acc_ref, oc_ref, ow_ref,
                       *, past, n_new, n_pages):
    p = pl.program_id(1)
    k_w = NSA_KV * NSA_HD
    cols = NSA_KV * n_new * NSA_GROUP
    per_kv = n_new * NSA_GROUP
    n_cmp = past // NSA_BLOCK
    n_sel = -(-(past + n_new) // NSA_BLOCK)
    n_sel_pad = sel_ref.shape[0]
    lane = lax.broadcasted_iota(jnp.int32, (1, cols), 1)
    q_pos = past + (lane // NSA_GROUP) % n_new

    def scores(keys):
        return _dot_nt(keys, qr_ref[...])

    def own_head(x_t):
        out = jnp.zeros((NSA_HD, cols), F32)
        for kv in range(NSA_KV):
            mine = (lane // per_kv) == kv
            out = out + jnp.where(mine, x_t[kv * NSA_HD:(kv + 1) * NSA_HD, :], 0.0)
        return out

    def attend_once(keys, vals, mask):
        p_t = _masked_softmax(scores(keys), mask, axis=0)
        return own_head(_dot_tn(vals, p_t))

    def online(keys, vals, mask):
        s = jnp.where(mask, scores(keys), NEG_BIG)
        m_old = m_ref[...]
        m_new = jnp.maximum(m_old, jnp.max(s, axis=0, keepdims=True))
        p_t = jnp.where(mask, jnp.exp(s - m_new), 0.0)
        alpha = jnp.exp(m_old - m_new)
        l_ref[...] = alpha * l_ref[...] + jnp.sum(p_t, axis=0, keepdims=True)
        acc_ref[...] = alpha * acc_ref[...] + own_head(_dot_tn(vals, p_t))
        m_ref[...] = m_new

    @pl.when(p == 0)
    def _():
        q = q_ref[0] * (NSA_HD ** -0.5)
        r_kv = lax.broadcasted_iota(jnp.int32, (cols, k_w), 0) // per_kv
        c_kv = lax.broadcasted_iota(jnp.int32, (cols, k_w), 1) // NSA_HD
        qr_ref[...] = jnp.where(r_kv == c_kv, jnp.tile(q, (1, NSA_KV)), 0.0).astype(BF16)
        cm = cm_ref[0]
        blk = lax.broadcasted_iota(jnp.int32, (n_cmp, 1), 0)
        p_t = _masked_softmax(scores(cm[:, :k_w]), (blk + 1) * NSA_BLOCK - 1 <= q_pos, axis=0)
        oc_ref[...] = own_head(_dot_tn(cm[:, k_w:], p_t))
        same = _ind(lax.broadcasted_iota(jnp.int32, (cols, cols), 0) // NSA_GROUP
                    == lax.broadcasted_iota(jnp.int32, (cols, cols), 1) // NSA_GROUP, BF16)
        imp = sum(jnp.dot(t, same, preferred_element_type=F32) for t in _split3(p_t))
        imp = jnp.concatenate([imp, jnp.zeros((n_sel_pad - n_cmp, cols), F32)], axis=0)
        blk_s = lax.broadcasted_iota(jnp.int32, (n_sel_pad, 1), 0)
        cur = q_pos // NSA_BLOCK
        forced = (blk_s == 0) | ((blk_s >= cur - 1) & (blk_s <= cur))
        score = jnp.where((blk_s > cur) | (blk_s >= n_sel), -jnp.inf, jnp.where(forced, jnp.inf, imp))
        sel_ref[...] = score

        def rank_body(i, rank):
            si = sel_ref[pl.ds(i, 1), :]
            return rank + jnp.where(blk_s > i, _ind(si >= score), _ind(si > score))

        rank = lax.fori_loop(0, n_sel, rank_body, jnp.zeros((n_sel_pad, cols), F32))
        sel_ref[...] = _ind((rank < NSA_TOPK) & (score > -jnp.inf))
        wb = wb_ref[0]
        n_wb = wb.shape[0]
        kw_all = jnp.concatenate([wb[:, :k_w], kwn_ref[0][:, :k_w]], axis=0)
        vw_all = jnp.concatenate([wb[:, k_w:], kwn_ref[0][:, k_w:]], axis=0)
        n_w = kw_all.shape[0]
        w_row = lax.broadcasted_iota(jnp.int32, (n_w, 1), 0)
        w_pos = past - n_wb + w_row
        d = q_pos - w_pos
        ow_ref[...] = attend_once(kw_all, vw_all,
                                  (d >= 0) & (d < NSA_WINDOW) & (w_pos >= 0) & (w_row < n_wb + n_new))
        m_ref[...] = jnp.full(m_ref.shape, NEG_BIG, F32)
        l_ref[...] = jnp.zeros(l_ref.shape, F32)
        acc_ref[...] = jnp.zeros(acc_ref.shape, F32)

    page = pg_ref[0]
    k_row = lax.broadcasted_iota(jnp.int32, (PAGE, 1), 0)
    per_page = PAGE // NSA_BLOCK
    chosen = jnp.zeros((PAGE, cols), F32)
    for j in range(per_page):
        chosen = jnp.where(k_row // NSA_BLOCK == j, sel_ref[pl.ds(p * per_page + j, 1), :], chosen)
    online(page[:, :k_w], page[:, k_w:], (chosen > 0.5) & (p * PAGE + k_row <= q_pos))

    @pl.when(p == n_pages - 1)
    def _():
        kvn = kvn_ref[0]
        n_row = lax.broadcasted_iota(jnp.int32, (kvn.shape[0], 1), 0)
        chosen_new = sel_ref[pl.ds(past // NSA_BLOCK, 1), :] > 0.5
        online(kvn[:, :k_w], kvn[:, k_w:], chosen_new & (n_row < n_new) & (past + n_row <= q_pos))
        o_s = acc_ref[...] / jnp.maximum(l_ref[...], TINY)
        gates = jax.nn.sigmoid(gt_ref[0] + bg_ref[...])
        o_t = gates[0:1, :] * oc_ref[...] + gates[1:2, :] * o_s + gates[2:3, :] * ow_ref[...]
        eye = _ind(lax.broadcasted_iota(jnp.int32, (NSA_HD, NSA_HD), 0)
                   == lax.broadcasted_iota(jnp.int32, (NSA_HD, NSA_HD), 1), BF16)
        o_ref[0] = sum(lax.dot_general(t, eye, (((0,), (0,)), ((), ())), preferred_element_type=F32)
                       for t in _split3(o_t))


def _nsa_sample(q_rows, cmp_means, cache_slc, page_table, kv_new, win_buf, kw_new, gates, b_gate,
                past, n_new):
    seqs, n_pages = page_table.shape
    cols = q_rows.shape[1]
    n_cmp = cmp_means.shape[1]
    n_sel_pad = -(-(-(-(past + n_new) // NSA_BLOCK)) // 8) * 8
    n_wb = win_buf.shape[1]
    pad_new = kv_new.shape[1]
    seq_map = lambda b, p, pt: (b, 0, 0)
    return pl.pallas_call(
        functools.partial(_nsa_sample_kernel, past=past, n_new=n_new, n_pages=n_pages),
        grid_spec=pltpu.PrefetchScalarGridSpec(
            num_scalar_prefetch=1,
            grid=(seqs, n_pages),
            in_specs=[
                pl.BlockSpec((1, cols, NSA_HD), seq_map),
                pl.BlockSpec((1, n_cmp, NSA_KV_W), seq_map),
                pl.BlockSpec((1, PAGE, NSA_KV_W), lambda b, p, pt: (pt[b, p], 0, 0)),
                pl.BlockSpec((1, pad_new, NSA_KV_W), seq_map),
                pl.BlockSpec((1, n_wb, NSA_KV_W), seq_map),
                pl.BlockSpec((1, pad_new, NSA_KV_W), seq_map),
                pl.BlockSpec((1, 3, cols), seq_map),
                pl.BlockSpec((3, cols), lambda b, p, pt: (0, 0)),
            ],
            out_specs=pl.BlockSpec((1, cols, NSA_HD), seq_map),
            scratch_shapes=[
                pltpu.VMEM((cols, NSA_KV * NSA_HD), BF16),
                pltpu.VMEM((n_sel_pad, cols), F32),
                pltpu.VMEM((1, cols), F32),
                pltpu.VMEM((1, cols), F32),
                pltpu.VMEM((NSA_HD, cols), F32),
                pltpu.VMEM((NSA_HD, cols), F32),
                pltpu.VMEM((NSA_HD, cols), F32),
            ],
        ),
        out_shape=jax.ShapeDtypeStruct((seqs, cols, NSA_HD), F32),
        compiler_params=_params("parallel", "arbitrary"),
        name="nsa_sample",
    )(page_table, q_rows, cmp_means, cache_slc, kv_new, win_buf, kw_new, gates, b_gate)


def kernel(x_prompt, x_sample, c_prompt, c_sample, state_gla, cache_cmp_kv, cache_slc_kv, cache_win_kv,
           page_table, ada_w, ada_b, gla_w_in, gla_w_alpha, gla_b_alpha, gla_norm_g, gla_w_out, nsa_w_in,
           nsa_b_gate, nsa_w_out, ln_mix_g, ln_mix_b, ffn_w_in, ffn_w_out, ln_ffn_g, ln_ffn_b):
    batch, seq, _ = x_prompt.shape
    seqs, n_new, _ = x_sample.shape
    n_pages = page_table.shape[1]
    past = n_pages * PAGE
    rows_s = seqs * n_new
    tm_p = 1024

    pad_rows = 16 - (batch + seqs)
    c_all = jnp.concatenate([c_prompt, c_sample, jnp.zeros((pad_rows, D_MODEL), F32)], axis=0)
    mod = _adaln(c_all, ada_w, ada_b).reshape(DEPTH, 16, 6, D_MODEL)

    def mods(layer):
        mp = [mod[layer, :batch, i].reshape(batch, 1, D_MODEL) for i in range(6)]
        ms = [jnp.repeat(mod[layer, batch:batch + seqs, i], n_new, axis=0).reshape(1, rows_s, D_MODEL)
              for i in range(6)]
        return mp, ms

    def mixer_out(o, w_out, x, layer, m, tm, tag):
        return _out_ln(o, w_out, layer // 2, x, m[2], ln_mix_g, ln_mix_b, layer, min(tm, 512), tag)

    def ffn(x, layer, m, tm, tag):
        act = _ffn_in(x, m[4], m[3], ffn_w_in, layer, tm, "ffn_in_" + tag)
        return _out_ln(act, ffn_w_out, layer, x, m[5], ln_ffn_g, ln_ffn_b, layer, min(tm, 512),
                       "ffn_out_" + tag)

    xp = x_prompt.reshape(batch * seq, D_MODEL)
    xs = x_sample.reshape(rows_s, D_MODEL)

    mp, ms = mods(0)
    main_p, tail_p = _proj(xp, mp[1], mp[0], gla_w_in, 0, GLA_MAIN_W, tm_p, "gla_in_prompt")
    gated_p, gla_state_p = _gla(main_p, tail_p, gla_w_alpha[0], gla_b_alpha[0], gla_norm_g[0], None,
                                batch, seq, 128, 128)
    main_s, tail_s = _proj(xs, ms[1], ms[0], gla_w_in, 0, GLA_MAIN_W, rows_s, "gla_in_sample")
    chunk_s = GLA_SUB

    def pad_steps(a):
        a = a.reshape(seqs, n_new, a.shape[-1])
        return jnp.pad(a, ((0, 0), (0, chunk_s - n_new), (0, 0))).reshape(seqs * chunk_s, a.shape[-1])

    gated_s, gla_state_s = _gla(pad_steps(main_s), pad_steps(tail_s), gla_w_alpha[0], gla_b_alpha[0],
                                gla_norm_g[0], state_gla[:, 0], seqs, chunk_s, chunk_s, n_new)
    gated_s = gated_s.reshape(seqs, chunk_s, -1)[:, :n_new].reshape(rows_s, -1)
    xp = mixer_out(gated_p, gla_w_out, xp, 0, mp, tm_p, "gla_out_prompt")
    xs = mixer_out(gated_s, gla_w_out, xs, 0, ms, rows_s, "gla_out_sample")
    xp = ffn(xp, 0, mp, tm_p, "prompt0")
    xs = ffn(xs, 0, ms, rows_s, "sample0")

    mp, ms = mods(1)
    main_p, tail_p = _proj(xp, mp[1], mp[0], nsa_w_in, 0, NSA_MAIN_W, tm_p, "nsa_in_prompt")
    main_s, tail_s = _proj(xs, ms[1], ms[0], nsa_w_in, 0, NSA_MAIN_W, rows_s, "nsa_in_sample")
    n_gate = 3 * NSA_HEADS
    g_w = 3 * NSA_GROUP
    gates_p = tail_p[:, :n_gate].reshape(batch * seq, NSA_KV, g_w).transpose(1, 0, 2)
    b_gate_p = nsa_b_gate[0].reshape(NSA_KV, 1, g_w)
    o_p = _nsa_prompt(main_p, gates_p, b_gate_p, batch, seq)

    kv_rows = lambda m, br: m[:, NSA_Q_W + br * NSA_KV_W:NSA_Q_W + (br + 1) * NSA_KV_W]
    kv_shape = (2, NSA_KV, NSA_HD)
    cmp_kv_p = kv_rows(main_p, 0).reshape(batch, 1, seq, *kv_shape)
    slc_kv_p = kv_rows(main_p, 1).reshape(batch, 1, seq, *kv_shape)
    keep_p = min(NSA_WINDOW, seq)
    win_kv_p = kv_rows(main_p, 2).reshape(batch, seq, NSA_KV_W)[:, seq - keep_p:].reshape(batch, 1, keep_p, *kv_shape)

    cmp_kv_s = kv_rows(main_s, 0).reshape(seqs, 1, n_new, *kv_shape)
    slc_kv_s = kv_rows(main_s, 1).reshape(seqs, 1, n_new, *kv_shape)
    kw_new = kv_rows(main_s, 2).reshape(seqs, n_new, NSA_KV_W)
    win_buf = cache_win_kv[:, 0].reshape(seqs, -1, NSA_KV_W)
    n_wb = win_buf.shape[1]
    all_w = jnp.concatenate([win_buf, kw_new], axis=1)
    keep_s = min(NSA_WINDOW, n_wb + n_new)
    win_kv_s = all_w[:, n_wb + n_new - keep_s:].reshape(seqs, 1, keep_s, *kv_shape)

    n_phys = cache_cmp_kv.shape[0]
    cmp_means = _page_means(cache_cmp_kv.reshape(n_phys, PAGE, NSA_KV_W), page_table)
    cmp_means = cmp_means.reshape(seqs, past // NSA_BLOCK, NSA_KV_W)
    pad_new = 8
    pad8 = lambda a: jnp.pad(a, ((0, 0), (0, pad_new - n_new), (0, 0)))
    cols = NSA_KV * n_new * NSA_GROUP
    q_rows = (main_s[:, :NSA_Q_W].reshape(seqs, n_new, NSA_KV, NSA_GROUP, NSA_HD)
              .transpose(0, 2, 1, 3, 4).reshape(seqs, cols, NSA_HD))
    gates_s = (tail_s[:, :n_gate].reshape(seqs, n_new, NSA_KV, NSA_GROUP, 3)
               .transpose(0, 4, 2, 1, 3).reshape(seqs, 3, cols))
    b_gate_s = jnp.broadcast_to(nsa_b_gate[0].reshape(1, NSA_KV, NSA_GROUP, 3), (n_new, NSA_KV, NSA_GROUP, 3))
    b_gate_s = b_gate_s.transpose(3, 1, 0, 2).reshape(3, cols)
    o_s = _nsa_sample(q_rows, cmp_means, cache_slc_kv.reshape(n_phys, PAGE, NSA_KV_W), page_table,
                      pad8(slc_kv_s.reshape(seqs, n_new, NSA_KV_W)), win_buf, pad8(kw_new), gates_s, b_gate_s,
                      past, n_new)
    o_s = (o_s.reshape(seqs, NSA_KV, n_new, NSA_GROUP, NSA_HD).transpose(0, 2, 1, 3, 4)
           .reshape(rows_s, NSA_Q_W).astype(BF16))

    xp = mixer_out(o_p, nsa_w_out, xp, 1, mp, tm_p, "nsa_out_prompt")
    xs = mixer_out(o_s, nsa_w_out, xs, 1, ms, rows_s, "nsa_out_sample")
    xp = ffn(xp, 1, mp, tm_p, "prompt1")
    xs = ffn(xs, 1, ms, rows_s, "sample1")

    return (xp.reshape(batch, seq, D_MODEL), xs.reshape(seqs, n_new, D_MODEL),
            gla_state_p[:, None], gla_state_s[:, None], cmp_kv_p, cmp_kv_s, slc_kv_p, slc_kv_s,
            win_kv_p, win_kv_s)
```

```python
import functools

import jax
import jax.numpy as jnp
from jax import lax
from jax.experimental import pallas as pl
from jax.experimental.pallas import tpu as pltpu

F32 = jnp.float32
BF16 = jnp.bfloat16

D_MODEL = 2048
DEPTH = 2
DN_ALPHA = (2 * DEPTH) ** 0.25
LN_EPS = 1e-5
D_FF = 5632
GLA_HEADS = 4
GLA_DK = 256
GLA_DV = 512
GLA_RANK = 16
GLA_TAU = 16.0
GLA_SUB = 32
GLA_MAIN_W = 2 * GLA_HEADS * GLA_DK + 2 * GLA_HEADS * GLA_DV
NSA_HEADS = 16
NSA_KV = 4
NSA_HD = 128
NSA_GROUP = NSA_HEADS // NSA_KV
NSA_BLOCK = 64
NSA_TOPK = 16
NSA_WINDOW = 512
NSA_Q_W = NSA_HEADS * NSA_HD
NSA_KV_W = 2 * NSA_KV * NSA_HD
NSA_MAIN_W = NSA_Q_W + 3 * NSA_KV_W
PAGE = 128
KV_ROWS = 2 * NSA_KV

LANES = 128
V7X_VMEM_BYTES = 64 * 1024 * 1024
VMEM_LIMIT = V7X_VMEM_BYTES * 7 // 8

NEG_BIG = -1e30
TINY = float(jnp.finfo(jnp.float32).tiny)


def _params(*sem):
    return pltpu.CompilerParams(dimension_semantics=sem, vmem_limit_bytes=VMEM_LIMIT)


def _dot(a, b):
    return jnp.dot(a.astype(BF16), b.astype(BF16), preferred_element_type=F32)


def _dot_nt(a, b):
    return lax.dot_general(a.astype(BF16), b.astype(BF16), (((1,), (1,)), ((), ())),
                           preferred_element_type=F32)


def _dot_tn(a, b):
    return lax.dot_general(a.astype(BF16), b.astype(BF16), (((0,), (0,)), ((), ())),
                           preferred_element_type=F32)


def _split3(x):
    hi = x.astype(BF16)
    r1 = x - hi.astype(F32)
    mid = r1.astype(BF16)
    lo = (r1 - mid.astype(F32)).astype(BF16)
    return hi, mid, lo


def _silu(x):
    return x * jax.nn.sigmoid(x)


def _ind(cond, dtype=F32):
    return jnp.where(cond, 1.0, 0.0).astype(dtype)


def _masked_softmax(s, mask, axis):
    s = jnp.where(mask, s, -jnp.inf)
    m = jnp.max(s, axis=axis, keepdims=True)
    m = jnp.where(jnp.isfinite(m), m, 0.0)
    e = jnp.where(mask, jnp.exp(s - m), 0.0)
    return e / jnp.maximum(jnp.sum(e, axis=axis, keepdims=True), TINY)


def _adaln_kernel(c_ref, w_ref, b_ref, o_ref):
    o_ref[...] = _dot(_silu(c_ref[...]), w_ref[...]) + b_ref[...]


def _adaln(c_all, ada_w, ada_b):
    rows = c_all.shape[0]
    n = ada_w.shape[2]
    tn = 1024
    return pl.pallas_call(
        _adaln_kernel,
        grid=(DEPTH, n // tn),
        in_specs=[
            pl.BlockSpec((rows, D_MODEL), lambda l, j: (0, 0)),
            pl.BlockSpec((None, D_MODEL, tn), lambda l, j: (l, 0, j)),
            pl.BlockSpec((None, 1, tn), lambda l, j: (l, 0, j)),
        ],
        out_specs=pl.BlockSpec((None, rows, tn), lambda l, j: (l, 0, j)),
        out_shape=jax.ShapeDtypeStruct((DEPTH, rows, n), F32),
        compiler_params=_params("parallel", "parallel"),
        name="adaln",
    )(c_all, ada_w, ada_b.reshape(DEPTH, 1, n))


def _proj_kernel(x_ref, sc_ref, sh_ref, w_ref, wt_ref, o_ref, ot_ref, h_ref):
    @pl.when(pl.program_id(1) == 0)
    def _():
        h = (x_ref[...] * (1.0 + sc_ref[...]) + sh_ref[...]).astype(BF16)
        h_ref[...] = h
        ot_ref[...] = _dot(h, wt_ref[...])

    o_ref[...] = _dot(h_ref[...], w_ref[...])


def _proj(x, sc, sh, w, layer, n_main, tm, name):
    m = x.shape[0]
    n_tail = w.shape[2] - n_main
    w_tail = jnp.pad(w[layer, :, n_main:], ((0, 0), (0, LANES - n_tail)))
    tn = 512
    groups, mod_rows = sc.shape[0], sc.shape[1]
    tiles_per_group = m // tm // groups
    mod_spec = pl.BlockSpec((None, mod_rows, D_MODEL), lambda i, j: (i // tiles_per_group, 0, 0))
    return pl.pallas_call(
        _proj_kernel,
        grid=(m // tm, n_main // tn),
        in_specs=[
            pl.BlockSpec((tm, D_MODEL), lambda i, j: (i, 0)),
            mod_spec,
            mod_spec,
            pl.BlockSpec((None, D_MODEL, tn), lambda i, j: (layer, 0, j)),
            pl.BlockSpec((D_MODEL, LANES), lambda i, j: (0, 0)),
        ],
        out_specs=[
            pl.BlockSpec((tm, tn), lambda i, j: (i, j)),
            pl.BlockSpec((tm, LANES), lambda i, j: (i, 0)),
        ],
        out_shape=[jax.ShapeDtypeStruct((m, n_main), F32), jax.ShapeDtypeStruct((m, LANES), F32)],
        scratch_shapes=[pltpu.VMEM((tm, D_MODEL), BF16)],
        compiler_params=_params("parallel", "arbitrary"),
        name=name,
    )(x, sc, sh, w, w_tail)


def _out_ln_kernel(a_ref, w_ref, x_ref, gt_ref, g_ref, b_ref, o_ref, *, nk):
    k = pl.program_id(1)
    part = _dot(a_ref[...], w_ref[...])

    @pl.when(k == 0)
    def _():
        o_ref[...] = part

    @pl.when(k > 0)
    def _():
        o_ref[...] += part

    @pl.when(k == nk - 1)
    def _():
        y = DN_ALPHA * x_ref[...] + gt_ref[...] * o_ref[...]
        mu = jnp.mean(y, axis=-1, keepdims=True)
        yc = y - mu
        var = jnp.mean(yc * yc, axis=-1, keepdims=True)
        o_ref[...] = yc * lax.rsqrt(var + LN_EPS) * g_ref[...] + b_ref[...]


def _out_ln(a, w, w_layer, x, gate, ln_g, ln_b, ln_layer, tm, name):
    m, kdim = a.shape
    tk = 512
    nk = kdim // tk
    groups, mod_rows = gate.shape[0], gate.shape[1]
    tiles_per_group = m // tm // groups
    vec_spec = pl.BlockSpec((None, 1, D_MODEL), lambda i, k: (ln_layer, 0, 0))
    return pl.pallas_call(
        functools.partial(_out_ln_kernel, nk=nk),
        grid=(m // tm, nk),
        in_specs=[
            pl.BlockSpec((tm, tk), lambda i, k: (i, k)),
            pl.BlockSpec((None, tk, D_MODEL), lambda i, k: (w_layer, k, 0)),
            pl.BlockSpec((tm, D_MODEL), lambda i, k: (i, 0)),
            pl.BlockSpec((None, mod_rows, D_MODEL), lambda i, k: (i // tiles_per_group, 0, 0)),
            vec_spec,
            vec_spec,
        ],
        out_specs=pl.BlockSpec((tm, D_MODEL), lambda i, k: (i, 0)),
        out_shape=jax.ShapeDtypeStruct((m, D_MODEL), F32),
        compiler_params=_params("parallel", "arbitrary"),
        name=name,
    )(a, w, x, gate, ln_g.reshape(-1, 1, D_MODEL), ln_b.reshape(-1, 1, D_MODEL))


def _ffn_in_kernel(x_ref, sc_ref, sh_ref, wa_ref, wu_ref, o_ref, h_ref):
    @pl.when(pl.program_id(1) == 0)
    def _():
        h_ref[...] = (x_ref[...] * (1.0 + sc_ref[...]) + sh_ref[...]).astype(BF16)

    h = h_ref[...]
    a = _dot(h, wa_ref[...])
    u = _dot(h, wu_ref[...])
    o_ref[...] = (_silu(a) * u).astype(BF16)


def _ffn_in(x, sc, sh, w, layer, tm, name):
    m = x.shape[0]
    tn = 512
    nj = D_FF // tn
    groups, mod_rows = sc.shape[0], sc.shape[1]
    tiles_per_group = m // tm // groups
    mod_spec = pl.BlockSpec((None, mod_rows, D_MODEL), lambda i, j: (i // tiles_per_group, 0, 0))
    return pl.pallas_call(
        _ffn_in_kernel,
        grid=(m // tm, nj),
        in_specs=[
            pl.BlockSpec((tm, D_MODEL), lambda i, j: (i, 0)),
            mod_spec,
            mod_spec,
            pl.BlockSpec((None, D_MODEL, tn), lambda i, j: (layer, 0, j)),
            pl.BlockSpec((None, D_MODEL, tn), lambda i, j: (layer, 0, j + nj)),
        ],
        out_specs=pl.BlockSpec((tm, tn), lambda i, j: (i, j)),
        out_shape=jax.ShapeDtypeStruct((m, D_FF), BF16),
        scratch_shapes=[pltpu.VMEM((tm, D_MODEL), BF16)],
        compiler_params=_params("parallel", "arbitrary"),
        name=name,
    )(x, sc, sh, w, w)


def _gla_kernel(*refs, chunk, t_valid, has_s0):
    if has_s0:
        q_ref, k_ref, v_ref, r_ref, a_ref, wa_ref, ba_ref, g_ref, s0_ref, o_ref, st_ref = refs
    else:
        q_ref, k_ref, v_ref, r_ref, a_ref, wa_ref, ba_ref, g_ref, o_ref, st_ref = refs
    sub = GLA_SUB
    n_sub = chunk // sub
    anchor = sub // 2 - 1

    @pl.when(pl.program_id(1) == 0)
    def _():
        if has_s0:
            st_ref[...] = s0_ref[...]
        else:
            st_ref[...] = jnp.zeros(st_ref.shape, F32)

    row = lax.broadcasted_iota(jnp.int32, (chunk, 1), 0)
    tri = _ind(lax.broadcasted_iota(jnp.int32, (chunk, chunk), 1)
               <= lax.broadcasted_iota(jnp.int32, (chunk, chunk), 0), BF16)
    ones = jnp.ones((chunk, LANES), BF16)
    causal = (lax.broadcasted_iota(jnp.int32, (sub, sub), 1)
              <= lax.broadcasted_iota(jnp.int32, (sub, sub), 0))
    z_all = _dot(a_ref[...], wa_ref[...]) + ba_ref[...]

    for h in range(GLA_HEADS):
        dk = slice(h * GLA_DK, (h + 1) * GLA_DK)
        dv = slice(h * GLA_DV, (h + 1) * GLA_DV)
        z = z_all[:, dk]
        lb = (jnp.minimum(z, 0.0) - jnp.log1p(jnp.exp(-jnp.abs(z)))) * (1.0 / GLA_TAU)
        kh = k_ref[:, dk]
        if t_valid < chunk:
            lb = jnp.where(row < t_valid, lb, 0.0)
            kh = jnp.where(row < t_valid, kh, 0.0)
        qh = q_ref[:, dk] * (GLA_DK ** -0.5)
        vh = v_ref[:, dv].astype(BF16)
        lb3 = _split3(lb)
        b = sum(jnp.dot(tri, t, preferred_element_type=F32) for t in lb3)
        b_last = b[chunk - 1:chunk, :]
        s_old = st_ref[0, h]
        o = _dot(qh * jnp.exp(b), s_old)
        o_rows = []
        for s in range(n_sub):
            rs = slice(s * sub, (s + 1) * sub)
            bs, qs, ks = b[rs], qh[rs], kh[rs]
            b_mid = b[s * sub + anchor:s * sub + anchor + 1, :]
            a_diag = _dot_nt(qs * jnp.exp(bs - b_mid), ks * jnp.exp(b_mid - bs))
            o_s = _dot(jnp.where(causal, a_diag, 0.0), vh[rs])
            if s > 0:
                prev = slice(0, s * sub)
                b_in = b[s * sub - 1:s * sub, :]
                a_off = _dot_nt(qs * jnp.exp(bs - b_in), kh[prev] * jnp.exp(b_in - b[prev]))
                o_s = o_s + _dot(a_off, vh[prev])
            o_rows.append(o_s)
        o = o + jnp.concatenate(o_rows, axis=0)
        dec = jnp.exp(sum(lax.dot_general(t, ones, (((0,), (0,)), ((), ())),
                                          preferred_element_type=F32) for t in lb3))
        st_ref[0, h] = s_old * jnp.tile(dec, (1, GLA_DV // LANES)) + _dot_tn(kh * jnp.exp(b_last - b), vh)
        o = o * lax.rsqrt(jnp.mean(o * o, axis=-1, keepdims=True) + LN_EPS)
        o_ref[:, dv] = (o * g_ref[:, dv] * _silu(r_ref[:, dv])).astype(BF16)


def _gla(main, tail, w_alpha, b_alpha, norm_g, s0, batch, seq, chunk, t_valid):
    n_chunks = seq // chunk
    qk_w = GLA_HEADS * GLA_DK
    v_w = GLA_HEADS * GLA_DV
    wa = jnp.pad(w_alpha, ((0, LANES - GLA_RANK), (0, 0)))
    row_map = lambda b, c: (b * n_chunks + c, 0)
    const = lambda b, c: (0, 0)
    in_specs = [
        pl.BlockSpec((chunk, qk_w), row_map),
        pl.BlockSpec((chunk, qk_w), lambda b, c: (b * n_chunks + c, 1)),
        pl.BlockSpec((chunk, v_w), lambda b, c: (b * n_chunks + c, 1)),
        pl.BlockSpec((chunk, v_w), lambda b, c: (b * n_chunks + c, 2)),
        pl.BlockSpec((chunk, LANES), row_map),
        pl.BlockSpec((LANES, qk_w), const),
        pl.BlockSpec((1, qk_w), const),
        pl.BlockSpec((1, v_w), const),
    ]
    args = [main, main, main, main, tail, wa, b_alpha.reshape(1, qk_w), norm_g.reshape(1, v_w)]
    state_spec = pl.BlockSpec((1, GLA_HEADS, GLA_DK, GLA_DV), lambda b, c: (b, 0, 0, 0))
    if s0 is not None:
        in_specs.append(state_spec)
        args.append(s0)
    return pl.pallas_call(
        functools.partial(_gla_kernel, chunk=chunk, t_valid=t_valid, has_s0=s0 is not None),
        grid=(batch, n_chunks),
        in_specs=in_specs,
        out_specs=[pl.BlockSpec((chunk, v_w), row_map), state_spec],
        out_shape=[jax.ShapeDtypeStruct((batch * seq, v_w), BF16),
                   jax.ShapeDtypeStruct((batch, GLA_HEADS, GLA_DK, GLA_DV), F32)],
        compiler_params=_params("parallel", "arbitrary"),
        name="gla_prompt" if s0 is None else "gla_sample",
    )(*args)


def _select_blocks(imp_t, blk, cur):
    n = imp_t.shape[0]
    forced = (blk == 0) | ((blk >= cur - 1) & (blk <= cur))
    score = jnp.where(blk > cur, -jnp.inf, jnp.where(forced, jnp.inf, imp_t))
    rank = jnp.zeros(score.shape, F32)
    for i in range(n):
        si = score[i:i + 1, :]
        rank = rank + jnp.where(blk > i, _ind(si >= score), _ind(si > score))
    return (rank < NSA_TOPK) & (score > -jnp.inf)


def _nsa_prompt_kernel(q_ref, kc_ref, vc_ref, ks_ref, vs_ref, kw_ref, vw_ref, gt_ref, bg_ref, o_ref,
                       kcm, vcm, ksb, vst, kwb, vwt, qs_ref, sel_ref, m_ref, l_ref, acc_ref, os_ref,
                       *, seq, tq, tk):
    qi = pl.program_id(2)
    n_blk = seq // NSA_BLOCK
    n_kt = seq // tk
    blk_per_kt = tk // NSA_BLOCK

    @pl.when(qi == 0)
    def _():
        kcm[...] = jnp.sum(kc_ref[...].reshape(n_blk, NSA_BLOCK, NSA_HD), axis=1) * (1.0 / NSA_BLOCK)
        vcm[...] = jnp.sum(vc_ref[...].reshape(n_blk, NSA_BLOCK, NSA_HD), axis=1) * (1.0 / NSA_BLOCK)
        ksb[...] = ks_ref[...].astype(BF16)
        kwb[...] = kw_ref[...].astype(BF16)
        for kt in range(n_kt):
            rows = slice(kt * tk, (kt + 1) * tk)
            vst[kt] = vs_ref[rows, :].T.astype(BF16)
            vwt[kt] = vw_ref[rows, :].T.astype(BF16)

    t0 = qi * tq
    q = q_ref[...] * (NSA_HD ** -0.5)
    qs_ref[...] = jnp.concatenate(
        [q[:, g * NSA_HD:(g + 1) * NSA_HD] for g in range(NSA_GROUP)], axis=0).astype(BF16)
    qs = qs_ref[...]
    t_row = t0 + lax.broadcasted_iota(jnp.int32, (1, tq), 1)
    t_row_g = jnp.concatenate([t_row] * NSA_GROUP, axis=1)

    kc_b = kcm[...].astype(BF16)
    blk_col = lax.broadcasted_iota(jnp.int32, (n_blk, 1), 0)
    p_t = _masked_softmax(_dot_nt(kc_b, qs), (blk_col + 1) * NSA_BLOCK - 1 <= t_row_g, axis=0)
    o_c = _dot_tn(vcm[...], p_t)
    imp_t = sum(p_t[:, g * tq:(g + 1) * tq] for g in range(NSA_GROUP))
    sel_ref[...] = _ind(_select_blocks(imp_t, blk_col, t_row // NSA_BLOCK))
    n_col = lax.broadcasted_iota(jnp.int32, (tk, 1), 0)

    def attend(k_ref, vt_ref, lo, hi, mask_fn):
        m_ref[...] = jnp.full(m_ref.shape, NEG_BIG, F32)
        l_ref[...] = jnp.zeros(l_ref.shape, F32)
        acc_ref[...] = jnp.zeros(acc_ref.shape, F32)

        def body(kt, carry):
            k0 = pl.multiple_of(kt * tk, tk)
            mask = mask_fn(kt, k0)
            s = jnp.where(mask, _dot_nt(k_ref[pl.ds(k0, tk), :], qs_ref[...]), NEG_BIG)
            m_old = m_ref[...]
            m_new = jnp.maximum(m_old, jnp.max(s, axis=0, keepdims=True))
            p = jnp.where(mask, jnp.exp(s - m_new), 0.0)
            alpha = jnp.exp(m_old - m_new)
            l_ref[...] = alpha * l_ref[...] + jnp.sum(p, axis=0, keepdims=True)
            acc_ref[...] = alpha * acc_ref[...] + _dot(vt_ref[kt], p)
            m_ref[...] = m_new
            return carry

        lax.fori_loop(lo, hi, body, 0)
        return acc_ref[...] / jnp.maximum(l_ref[...], TINY)

    def sel_mask(kt, k0):
        chosen = jnp.concatenate(
            [jnp.broadcast_to(sel_ref[pl.ds(kt * blk_per_kt + j, 1), :], (NSA_BLOCK, tq))
             for j in range(blk_per_kt)], axis=0)
        return (jnp.concatenate([chosen] * NSA_GROUP, axis=1) > 0.5) & (k0 + n_col <= t_row_g)

    def win_mask(kt, k0):
        d = t_row_g - (k0 + n_col)
        return (d >= 0) & (d < NSA_WINDOW)

    last_kt = (t0 + tq - 1) // tk
    os_ref[...] = attend(ksb, vst, 0, last_kt + 1, sel_mask)
    first_kt = jnp.maximum(t0 - (NSA_WINDOW - 1), 0) // tk
    o_w = attend(kwb, vwt, first_kt, last_kt + 1, win_mask)

    gates = jax.nn.sigmoid(gt_ref[...] + bg_ref[...])
    g_c, g_s, g_w = (jnp.concatenate([gates[3 * g + br:3 * g + br + 1, :] for g in range(NSA_GROUP)], axis=1)
                     for br in range(3))
    o_t = g_c * o_c + g_s * os_ref[...] + g_w * o_w
    for g in range(NSA_GROUP):
        o_ref[:, g * NSA_HD:(g + 1) * NSA_HD] = o_t[:, g * tq:(g + 1) * tq].T.astype(BF16)


def _nsa_prompt(main, gates, b_gate, batch, seq):
    tq, tk = 256, 256
    nq = seq // tq
    q_lanes = NSA_GROUP * NSA_HD
    col0 = NSA_Q_W // NSA_HD

    def kv_spec(branch, part):
        off = col0 + (2 * branch + part) * NSA_KV
        return pl.BlockSpec((seq, NSA_HD), lambda b, kv, qi: (b, off + kv))

    g_w = 3 * NSA_GROUP
    g_rows = NSA_GROUP * tq
    return pl.pallas_call(
        functools.partial(_nsa_prompt_kernel, seq=seq, tq=tq, tk=tk),
        grid=(batch, NSA_KV, nq),
        in_specs=[pl.BlockSpec((tq, q_lanes), lambda b, kv, qi: (b * nq + qi, kv))]
        + [kv_spec(br, part) for br in range(3) for part in range(2)]
        + [pl.BlockSpec((None, g_w, tq), lambda b, kv, qi: (kv, 0, b * nq + qi)),
           pl.BlockSpec((None, g_w, 1), lambda b, kv, qi: (kv, 0, 0))],
        out_specs=pl.BlockSpec((tq, q_lanes), lambda b, kv, qi: (b * nq + qi, kv)),
        out_shape=jax.ShapeDtypeStruct((batch * seq, NSA_Q_W), BF16),
        scratch_shapes=[
            pltpu.VMEM((seq // NSA_BLOCK, NSA_HD), F32),
            pltpu.VMEM((seq // NSA_BLOCK, NSA_HD), F32),
            pltpu.VMEM((seq, NSA_HD), BF16),
            pltpu.VMEM((seq // tk, NSA_HD, tk), BF16),
            pltpu.VMEM((seq, NSA_HD), BF16),
            pltpu.VMEM((seq // tk, NSA_HD, tk), BF16),
            pltpu.VMEM((g_rows, NSA_HD), BF16),
            pltpu.VMEM((seq // NSA_BLOCK, tq), F32),
            pltpu.VMEM((1, g_rows), F32),
            pltpu.VMEM((1, g_rows), F32),
            pltpu.VMEM((NSA_HD, g_rows), F32),
            pltpu.VMEM((NSA_HD, g_rows), F32),
        ],
        compiler_params=_params("parallel", "parallel", "arbitrary"),
        name="nsa_prompt",
    )(main, main, main, main, main, main, main, gates, b_gate)


def _page_means_kernel(pt_ref, x_ref, o_ref):
    x = x_ref[0].reshape(PAGE // NSA_BLOCK, NSA_BLOCK, KV_ROWS, NSA_HD)
    o_ref[0, 0] = jnp.sum(x, axis=1) * (1.0 / NSA_BLOCK)


def _page_means(cache, page_table):
    seqs, n_pages = page_table.shape
    per_page = PAGE // NSA_BLOCK
    return pl.pallas_call(
        _page_means_kernel,
        grid_spec=pltpu.PrefetchScalarGridSpec(
            num_scalar_prefetch=1,
            grid=(seqs, n_pages),
            in_specs=[pl.BlockSpec((1, PAGE * KV_ROWS, NSA_HD), lambda b, p, pt: (pt[b, p], 0, 0))],
            out_specs=pl.BlockSpec((1, 1, per_page, KV_ROWS, NSA_HD), lambda b, p, pt: (b, p, 0, 0, 0)),
        ),
        out_shape=jax.ShapeDtypeStruct((seqs, n_pages, per_page, KV_ROWS, NSA_HD), F32),
        compiler_params=_params("parallel", "arbitrary"),
        name="nsa_page_means",
    )(page_table, cache)


def _nsa_sample_kernel(pt_ref, q_ref, cm_ref, pg_ref, kvn_ref, wb_ref, kwn_ref, gt_ref, bg_ref, o_ref,
                       qr_ref, sel_ref, m_ref, l_ref, acc_ref, oc_ref, ow_ref,
                       *, past, n_new, n_pages):
    p = pl.program_id(1)
    k_w = NSA_KV * NSA_HD
    cols = NSA_KV * n_new * NSA_GROUP
    per_kv = n_new * NSA_GROUP
    n_cmp = past // NSA_BLOCK
    n_sel = -(-(past + n_new) // NSA_BLOCK)
    n_sel_pad = sel_ref.shape[0]
    lane = lax.broadcasted_iota(jnp.int32, (1, cols), 1)
    q_pos = past + (lane // NSA_GROUP) % n_new

    def scores(keys):
        return _dot_nt(keys, qr_ref[...])

    def own_head(x_t):
        out = jnp.zeros((NSA_HD, cols), F32)
        for kv in range(NSA_KV):
            mine = (lane // per_kv) == kv
            out = out + jnp.where(mine, x_t[kv * NSA_HD:(kv + 1) * NSA_HD, :], 0.0)
        return out

    def attend_once(keys, vals, mask):
        p_t = _masked_softmax(scores(keys), mask, axis=0)
        return own_head(_dot_tn(vals, p_t))

    def online(keys, vals, mask):
        s = jnp.where(mask, scores(keys), NEG_BIG)
        m_old = m_ref[...]
        m_new = jnp.maximum(m_old, jnp.max(s, axis=0, keepdims=True))
        p_t = jnp.where(mask, jnp.exp(s - m_new), 0.0)
        alpha = jnp.exp(m_old - m_new)
        l_ref[...] = alpha * l_ref[...] + jnp.sum(p_t, axis=0, keepdims=True)
        acc_ref[...] = alpha * acc_ref[...] + own_head(_dot_tn(vals, p_t))
        m_ref[...] = m_new

    @pl.when(p == 0)
    def _():
        q = q_ref[0] * (NSA_HD ** -0.5)
        r_kv = lax.broadcasted_iota(jnp.int32, (cols, k_w), 0) // per_kv
        c_kv = lax.broadcasted_iota(jnp.int32, (cols, k_w), 1) // NSA_HD
        qr_ref[...] = jnp.where(r_kv == c_kv, jnp.tile(q, (1, NSA_KV)), 0.0).astype(BF16)
        cm = cm_ref[0]
        blk = lax.broadcasted_iota(jnp.int32, (n_cmp, 1), 0)
        p_t = _masked_softmax(scores(cm[:, :k_w]), (blk + 1) * NSA_BLOCK - 1 <= q_pos, axis=0)
        oc_ref[...] = own_head(_dot_tn(cm[:, k_w:], p_t))
        same = _ind(lax.broadcasted_iota(jnp.int32, (cols, cols), 0) // NSA_GROUP
                    == lax.broadcasted_iota(jnp.int32, (cols, cols), 1) // NSA_GROUP, BF16)
        imp = sum(jnp.dot(t, same, preferred_element_type=F32) for t in _split3(p_t))
        imp = jnp.concatenate([imp, jnp.zeros((n_sel_pad - n_cmp, cols), F32)], axis=0)
        blk_s = lax.broadcasted_iota(jnp.int32, (n_sel_pad, 1), 0)
        cur = q_pos // NSA_BLOCK
        forced = (blk_s == 0) | ((blk_s >= cur - 1) & (blk_s <= cur))
        score = jnp.where((blk_s > cur) | (blk_s >= n_sel), -jnp.inf, jnp.where(forced, jnp.inf, imp))
        sel_ref[...] = score

        def rank_body(i, rank):
            si = sel_ref[pl.ds(i, 1), :]
            return rank + jnp.where(blk_s > i, _ind(si >= score), _ind(si > score))

        rank = lax.fori_loop(0, n_sel, rank_body, jnp.zeros((n_sel_pad, cols), F32))
        sel_ref[...] = _ind((rank < NSA_TOPK) & (score > -jnp.inf))
        wb = wb_ref[0]
        n_wb = wb.shape[0]
        kw_all = jnp.concatenate([wb[:, :k_w], kwn_ref[0][:, :k_w]], axis=0)
        vw_all = jnp.concatenate([wb[:, k_w:], kwn_ref[0][:, k_w:]], axis=0)
        n_w = kw_all.shape[0]
        w_row = lax.broadcasted_iota(jnp.int32, (n_w, 1), 0)
        w_pos = past - n_wb + w_row
        d = q_pos - w_pos
        ow_ref[...] = attend_once(kw_all, vw_all,
                                  (d >= 0) & (d < NSA_WINDOW) & (w_pos >= 0) & (w_row < n_wb + n_new))
        m_ref[...] = jnp.full(m_ref.shape, NEG_BIG, F32)
        l_ref[...] = jnp.zeros(l_ref.shape, F32)
        acc_ref[...] = jnp.zeros(acc_ref.shape, F32)

    def page_part(part):
        return jnp.concatenate(
            [pg_ref[0, pl.ds(part * NSA_KV + kv, PAGE, stride=KV_ROWS), :] for kv in range(NSA_KV)], axis=1)

    k_row = lax.broadcasted_iota(jnp.int32, (PAGE, 1), 0)
    per_page = PAGE // NSA_BLOCK
    chosen = jnp.zeros((PAGE, cols), F32)
    for j in range(per_page):
        chosen = jnp.where(k_row // NSA_BLOCK == j, sel_ref[pl.ds(p * per_page + j, 1), :], chosen)
    online(page_part(0), page_part(1), (chosen > 0.5) & (p * PAGE + k_row <= q_pos))

    @pl.when(p == n_pages - 1)
    def _():
        kvn = kvn_ref[0]
        n_row = lax.broadcasted_iota(jnp.int32, (kvn.shape[0], 1), 0)
        chosen_new = sel_ref[pl.ds(past // NSA_BLOCK, 1), :] > 0.5
        online(kvn[:, :k_w], kvn[:, k_w:], chosen_new & (n_row < n_new) & (past + n_row <= q_pos))
        o_s = acc_ref[...] / jnp.maximum(l_ref[...], TINY)
        gates = jax.nn.sigmoid(gt_ref[0] + bg_ref[...])
        o_t = gates[0:1, :] * oc_ref[...] + gates[1:2, :] * o_s + gates[2:3, :] * ow_ref[...]
        eye = _ind(lax.broadcasted_iota(jnp.int32, (NSA_HD, NSA_HD), 0)
                   == lax.broadcasted_iota(jnp.int32, (NSA_HD, NSA_HD), 1), BF16)
        o_ref[0] = sum(lax.dot_general(t, eye, (((0,), (0,)), ((), ())), preferred_element_type=F32)
                       for t in _split3(o_t))


def _nsa_sample(q_rows, cmp_means, cache_slc, page_table, kv_new, win_buf, kw_new, gates, b_gate,
                past, n_new):
    seqs, n_pages = page_table.shape
    cols = q_rows.shape[1]
    n_cmp = cmp_means.shape[1]
    n_sel_pad = -(-(-(-(past + n_new) // NSA_BLOCK)) // 8) * 8
    n_wb = win_buf.shape[1]
    pad_new = kv_new.shape[1]
    seq_map = lambda b, p, pt: (b, 0, 0)
    return pl.pallas_call(
        functools.partial(_nsa_sample_kernel, past=past, n_new=n_new, n_pages=n_pages),
        grid_spec=pltpu.PrefetchScalarGridSpec(
            num_scalar_prefetch=1,
            grid=(seqs, n_pages),
            in_specs=[
                pl.BlockSpec((1, cols, NSA_HD), seq_map),
                pl.BlockSpec((1, n_cmp, NSA_KV_W), seq_map),
                pl.BlockSpec((1, PAGE * KV_ROWS, NSA_HD), lambda b, p, pt: (pt[b, p], 0, 0)),
                pl.BlockSpec((1, pad_new, NSA_KV_W), seq_map),
                pl.BlockSpec((1, n_wb, NSA_KV_W), seq_map),
                pl.BlockSpec((1, pad_new, NSA_KV_W), seq_map),
                pl.BlockSpec((1, 3, cols), seq_map),
                pl.BlockSpec((3, cols), lambda b, p, pt: (0, 0)),
            ],
            out_specs=pl.BlockSpec((1, cols, NSA_HD), seq_map),
            scratch_shapes=[
                pltpu.VMEM((cols, NSA_KV * NSA_HD), BF16),
                pltpu.VMEM((n_sel_pad, cols), F32),
                pltpu.VMEM((1, cols), F32),
                pltpu.VMEM((1, cols), F32),
                pltpu.VMEM((NSA_HD, cols), F32),
                pltpu.VMEM((NSA_HD, cols), F32),
                pltpu.VMEM((NSA_HD, cols), F32),
            ],
        ),
        out_shape=jax.ShapeDtypeStruct((seqs, cols, NSA_HD), F32),
        compiler_params=_params("parallel", "arbitrary"),
        name="nsa_sample",
    )(page_table, q_rows, cmp_means, cache_slc, kv_new, win_buf, kw_new, gates, b_gate)


def kernel(x_prompt, x_sample, c_prompt, c_sample, state_gla, cache_cmp_kv, cache_slc_kv, cache_win_kv,
           page_table, ada_w, ada_b, gla_w_in, gla_w_alpha, gla_b_alpha, gla_norm_g, gla_w_out, nsa_w_in,
           nsa_b_gate, nsa_w_out, ln_mix_g, ln_mix_b, ffn_w_in, ffn_w_out, ln_ffn_g, ln_ffn_b):
    batch, seq, _ = x_prompt.shape
    seqs, n_new, _ = x_sample.shape
    n_pages = page_table.shape[1]
    past = n_pages * PAGE
    rows_s = seqs * n_new
    tm_p = 1024

    pad_rows = 16 - (batch + seqs)
    c_all = jnp.concatenate([c_prompt, c_sample, jnp.zeros((pad_rows, D_MODEL), F32)], axis=0)
    mod = _adaln(c_all, ada_w, ada_b).reshape(DEPTH, 16, 6, D_MODEL)

    def mods(layer):
        mp = [mod[layer, :batch, i].reshape(batch, 1, D_MODEL) for i in range(6)]
        ms = [jnp.repeat(mod[layer, batch:batch + seqs, i], n_new, axis=0).reshape(1, rows_s, D_MODEL)
              for i in range(6)]
        return mp, ms

    def mixer_out(o, w_out, x, layer, m, tm, tag):
        return _out_ln(o, w_out, layer // 2, x, m[2], ln_mix_g, ln_mix_b, layer, min(tm, 512), tag)

    def ffn(x, layer, m, tm, tag):
        act = _ffn_in(x, m[4], m[3], ffn_w_in, layer, tm, "ffn_in_" + tag)
        return _out_ln(act, ffn_w_out, layer, x, m[5], ln_ffn_g, ln_ffn_b, layer, min(tm, 512),
                       "ffn_out_" + tag)

    xp = x_prompt.reshape(batch * seq, D_MODEL)
    xs = x_sample.reshape(rows_s, D_MODEL)

    mp, ms = mods(0)
    main_p, tail_p = _proj(xp, mp[1], mp[0], gla_w_in, 0, GLA_MAIN_W, tm_p, "gla_in_prompt")
    gated_p, gla_state_p = _gla(main_p, tail_p, gla_w_alpha[0], gla_b_alpha[0], gla_norm_g[0], None,
                                batch, seq, 128, 128)
    main_s, tail_s = _proj(xs, ms[1], ms[0], gla_w_in, 0, GLA_MAIN_W, rows_s, "gla_in_sample")
    chunk_s = GLA_SUB

    def pad_steps(a):
        a = a.reshape(seqs, n_new, a.shape[-1])
        return jnp.pad(a, ((0, 0), (0, chunk_s - n_new), (0, 0))).reshape(seqs * chunk_s, a.shape[-1])

    gated_s, gla_state_s = _gla(pad_steps(main_s), pad_steps(tail_s), gla_w_alpha[0], gla_b_alpha[0],
                                gla_norm_g[0], state_gla[:, 0], seqs, chunk_s, chunk_s, n_new)
    gated_s = gated_s.reshape(seqs, chunk_s, -1)[:, :n_new].reshape(rows_s, -1)
    xp = mixer_out(gated_p, gla_w_out, xp, 0, mp, tm_p, "gla_out_prompt")
    xs = mixer_out(gated_s, gla_w_out, xs, 0, ms, rows_s, "gla_out_sample")
    xp = ffn(xp, 0, mp, tm_p, "prompt0")
    xs = ffn(xs, 0, ms, rows_s, "sample0")

    mp, ms = mods(1)
    main_p, tail_p = _proj(xp, mp[1], mp[0], nsa_w_in, 0, NSA_MAIN_W, tm_p, "nsa_in_prompt")
    main_s, tail_s = _proj(xs, ms[1], ms[0], nsa_w_in, 0, NSA_MAIN_W, rows_s, "nsa_in_sample")
    n_gate = 3 * NSA_HEADS
    g_w = 3 * NSA_GROUP
    gates_p = tail_p[:, :n_gate].reshape(batch * seq, NSA_KV, g_w).transpose(1, 2, 0)
    b_gate_p = nsa_b_gate[0].reshape(NSA_KV, g_w, 1)
    o_p = _nsa_prompt(main_p, gates_p, b_gate_p, batch, seq)

    kv_rows = lambda m, br: m[:, NSA_Q_W + br * NSA_KV_W:NSA_Q_W + (br + 1) * NSA_KV_W]
    kv_shape = (2, NSA_KV, NSA_HD)
    cmp_kv_p = kv_rows(main_p, 0).reshape(batch, 1, seq, *kv_shape)
    slc_kv_p = kv_rows(main_p, 1).reshape(batch, 1, seq, *kv_shape)
    keep_p = min(NSA_WINDOW, seq)
    win_kv_p = kv_rows(main_p, 2).reshape(batch, seq, NSA_KV_W)[:, seq - keep_p:].reshape(batch, 1, keep_p, *kv_shape)

    cmp_kv_s = kv_rows(main_s, 0).reshape(seqs, 1, n_new, *kv_shape)
    slc_kv_s = kv_rows(main_s, 1).reshape(seqs, 1, n_new, *kv_shape)
    kw_new = kv_rows(main_s, 2).reshape(seqs, n_new, NSA_KV_W)
    win_buf = cache_win_kv[:, 0].reshape(seqs, -1, NSA_KV_W)
    n_wb = win_buf.shape[1]
    all_w = jnp.concatenate([win_buf, kw_new], axis=1)
    keep_s = min(NSA_WINDOW, n_wb + n_new)
    win_kv_s = all_w[:, n_wb + n_new - keep_s:].reshape(seqs, 1, keep_s, *kv_shape)

    n_phys = cache_cmp_kv.shape[0]
    page_rows = lambda cache: cache.reshape(n_phys, PAGE * KV_ROWS, NSA_HD)
    cmp_means = _page_means(page_rows(cache_cmp_kv), page_table)
    cmp_means = cmp_means.reshape(seqs, past // NSA_BLOCK, NSA_KV_W)
    pad_new = 8
    pad8 = lambda a: jnp.pad(a, ((0, 0), (0, pad_new - n_new), (0, 0)))
    cols = NSA_KV * n_new * NSA_GROUP
    q_rows = (main_s[:, :NSA_Q_W].reshape(seqs, n_new, NSA_KV, NSA_GROUP, NSA_HD)
              .transpose(0, 2, 1, 3, 4).reshape(seqs, cols, NSA_HD))
    gates_s = (tail_s[:, :n_gate].reshape(seqs, n_new, NSA_KV, NSA_GROUP, 3)
               .transpose(0, 4, 2, 1, 3).reshape(seqs, 3, cols))
    b_gate_s = jnp.broadcast_to(nsa_b_gate[0].reshape(1, NSA_KV, NSA_GROUP, 3), (n_new, NSA_KV, NSA_GROUP, 3))
    b_gate_s = b_gate_s.transpose(3, 1, 0, 2).reshape(3, cols)
    o_s = _nsa_sample(q_rows, cmp_means, page_rows(cache_slc_kv), page_table,
                      pad8(slc_kv_s.reshape(seqs, n_new, NSA_KV_W)), win_buf, pad8(kw_new), gates_s, b_gate_s,
                      past, n_new)
    o_s = (o_s.reshape(seqs, NSA_KV, n_new, NSA_GROUP, NSA_HD).transpose(0, 2, 1, 3, 4)
           .reshape(rows_s, NSA_Q_W).astype(BF16))

    xp = mixer_out(o_p, nsa_w_out, xp, 1, mp, tm_p, "nsa_out_prompt")
    xs = mixer_out(o_s, nsa_w_out, xs, 1, ms, rows_s, "nsa_out_sample")
    xp = ffn(xp, 1, mp, tm_p, "prompt1")
    xs = ffn(xs, 1, ms, rows_s, "sample1")

    return (xp.reshape(batch, seq, D_MODEL), xs.reshape(seqs, n_new, D_MODEL),
            gla_state_p[:, None], gla_state_s[:, None], cmp_kv_p, cmp_kv_s, slc_kv_p, slc_kv_s,
            win_kv_p, win_kv_s)
```

```python
import functools

import jax
import jax.numpy as jnp
from jax import lax
from jax.experimental import pallas as pl
from jax.experimental.pallas import tpu as pltpu

F32 = jnp.float32
BF16 = jnp.bfloat16

D_MODEL = 2048
DEPTH = 2
DN_ALPHA = (2 * DEPTH) ** 0.25
LN_EPS = 1e-5
D_FF = 5632
GLA_HEADS = 4
GLA_DK = 256
GLA_DV = 512
GLA_RANK = 16
GLA_TAU = 16.0
GLA_SUB = 32
GLA_MAIN_W = 2 * GLA_HEADS * GLA_DK + 2 * GLA_HEADS * GLA_DV
NSA_HEADS = 16
NSA_KV = 4
NSA_HD = 128
NSA_GROUP = NSA_HEADS // NSA_KV
NSA_BLOCK = 64
NSA_TOPK = 16
NSA_WINDOW = 512
NSA_Q_W = NSA_HEADS * NSA_HD
NSA_KV_W = 2 * NSA_KV * NSA_HD
NSA_MAIN_W = NSA_Q_W + 3 * NSA_KV_W
PAGE = 128
KV_ROWS = 2 * NSA_KV

LANES = 128
BF16_ROWS = 16
LOG2_E = 1.4426950408889634
V7X_VMEM_BYTES = 64 * 1024 * 1024
VMEM_LIMIT = V7X_VMEM_BYTES * 7 // 8

NEG_BIG = -1e30
TINY = float(jnp.finfo(jnp.float32).tiny)


def _params(*sem):
    return pltpu.CompilerParams(dimension_semantics=sem, vmem_limit_bytes=VMEM_LIMIT)


def _dot(a, b):
    return jnp.dot(a.astype(BF16), b.astype(BF16), preferred_element_type=F32)


def _dot_nt(a, b):
    return lax.dot_general(a.astype(BF16), b.astype(BF16), (((1,), (1,)), ((), ())),
                           preferred_element_type=F32)


def _dot_tn(a, b):
    return lax.dot_general(a.astype(BF16), b.astype(BF16), (((0,), (0,)), ((), ())),
                           preferred_element_type=F32)


def _split3(x):
    hi = x.astype(BF16)
    r1 = x - hi.astype(F32)
    mid = r1.astype(BF16)
    lo = (r1 - mid.astype(F32)).astype(BF16)
    return hi, mid, lo


def _silu(x):
    return x * jax.nn.sigmoid(x)


def _ind(cond, dtype=F32):
    return jnp.where(cond, 1.0, 0.0).astype(dtype)


def _masked_softmax2(s2, mask, axis):
    s2 = jnp.where(mask, s2, -jnp.inf)
    m = jnp.max(s2, axis=axis, keepdims=True)
    m = jnp.where(jnp.isfinite(m), m, 0.0)
    e = jnp.where(mask, jnp.exp2(s2 - m), 0.0)
    return e / jnp.maximum(jnp.sum(e, axis=axis, keepdims=True), TINY)


def _adaln_kernel(c_ref, w_ref, b_ref, o_ref):
    o_ref[...] = _dot(_silu(c_ref[...]), w_ref[...]) + b_ref[...]


def _adaln(c_all, ada_w, ada_b):
    rows = c_all.shape[0]
    n = ada_w.shape[2]
    tn = 1024
    return pl.pallas_call(
        _adaln_kernel,
        grid=(DEPTH, n // tn),
        in_specs=[
            pl.BlockSpec((rows, D_MODEL), lambda l, j: (0, 0)),
            pl.BlockSpec((None, D_MODEL, tn), lambda l, j: (l, 0, j)),
            pl.BlockSpec((None, 1, tn), lambda l, j: (l, 0, j)),
        ],
        out_specs=pl.BlockSpec((None, rows, tn), lambda l, j: (l, 0, j)),
        out_shape=jax.ShapeDtypeStruct((DEPTH, rows, n), F32),
        compiler_params=_params("parallel", "parallel"),
        name="adaln",
    )(c_all, ada_w, ada_b.reshape(DEPTH, 1, n))


def _proj_kernel(x_ref, sc_ref, sh_ref, w_ref, wt_ref, o_ref, ot_ref, h_ref):
    @pl.when(pl.program_id(1) == 0)
    def _():
        h = (x_ref[...] * (1.0 + sc_ref[...]) + sh_ref[...]).astype(BF16)
        h_ref[...] = h
        ot_ref[...] = _dot(h, wt_ref[...])

    o_ref[...] = _dot(h_ref[...], w_ref[...])


def _proj(x, sc, sh, w, layer, n_main, tm, name):
    m = x.shape[0]
    n_tail = w.shape[2] - n_main
    w_tail = jnp.pad(w[layer, :, n_main:], ((0, 0), (0, LANES - n_tail)))
    tn = 512
    groups, mod_rows = sc.shape[0], sc.shape[1]
    tiles_per_group = m // tm // groups
    mod_spec = pl.BlockSpec((None, mod_rows, D_MODEL), lambda i, j: (i // tiles_per_group, 0, 0))
    return pl.pallas_call(
        _proj_kernel,
        grid=(m // tm, n_main // tn),
        in_specs=[
            pl.BlockSpec((tm, D_MODEL), lambda i, j: (i, 0), pipeline_mode=pl.Buffered(1)),
            mod_spec,
            mod_spec,
            pl.BlockSpec((None, D_MODEL, tn), lambda i, j: (layer, 0, j)),
            pl.BlockSpec((D_MODEL, LANES), lambda i, j: (0, 0)),
        ],
        out_specs=[
            pl.BlockSpec((tm, tn), lambda i, j: (i, j)),
            pl.BlockSpec((tm, LANES), lambda i, j: (i, 0)),
        ],
        out_shape=[jax.ShapeDtypeStruct((m, n_main), F32), jax.ShapeDtypeStruct((m, LANES), F32)],
        scratch_shapes=[pltpu.VMEM((tm, D_MODEL), BF16)],
        compiler_params=_params("parallel", "arbitrary"),
        name=name,
    )(x, sc, sh, w, w_tail)


def _out_ln_kernel(a_ref, w_ref, x_ref, gt_ref, g_ref, b_ref, o_ref):
    y = DN_ALPHA * x_ref[...] + gt_ref[...] * jnp.dot(a_ref[...], w_ref[...], preferred_element_type=F32)
    mu = jnp.mean(y, axis=-1, keepdims=True)
    yc = y - mu
    var = jnp.mean(yc * yc, axis=-1, keepdims=True)
    o_ref[...] = yc * lax.rsqrt(var + LN_EPS) * g_ref[...] + b_ref[...]


def _out_ln(a, w, w_layer, x, gate, ln_g, ln_b, ln_layer, tm, name):
    m, kdim = a.shape
    groups, mod_rows = gate.shape[0], gate.shape[1]
    tiles_per_group = m // tm // groups
    vec_spec = pl.BlockSpec((None, 1, D_MODEL), lambda i: (ln_layer, 0, 0))
    return pl.pallas_call(
        _out_ln_kernel,
        grid=(m // tm,),
        in_specs=[
            pl.BlockSpec((tm, kdim), lambda i: (i, 0)),
            pl.BlockSpec((None, kdim, D_MODEL), lambda i: (w_layer, 0, 0), pipeline_mode=pl.Buffered(1)),
            pl.BlockSpec((tm, D_MODEL), lambda i: (i, 0)),
            pl.BlockSpec((None, mod_rows, D_MODEL), lambda i: (i // tiles_per_group, 0, 0)),
            vec_spec,
            vec_spec,
        ],
        out_specs=pl.BlockSpec((tm, D_MODEL), lambda i: (i, 0)),
        out_shape=jax.ShapeDtypeStruct((m, D_MODEL), F32),
        compiler_params=_params("parallel"),
        name=name,
    )(a, w, x, gate, ln_g.reshape(-1, 1, D_MODEL), ln_b.reshape(-1, 1, D_MODEL))


def _ffn_in_kernel(x_ref, sc_ref, sh_ref, wa_ref, wu_ref, o_ref, h_ref):
    @pl.when(pl.program_id(1) == 0)
    def _():
        h_ref[...] = (x_ref[...] * (1.0 + sc_ref[...]) + sh_ref[...]).astype(BF16)

    h = h_ref[...]
    a = _dot(h, wa_ref[...])
    u = _dot(h, wu_ref[...])
    o_ref[...] = (_silu(a) * u).astype(BF16)


def _ffn_in(x, sc, sh, w, layer, tm, name):
    m = x.shape[0]
    tn = 256
    nj = D_FF // tn
    groups, mod_rows = sc.shape[0], sc.shape[1]
    tiles_per_group = m // tm // groups
    mod_spec = pl.BlockSpec((None, mod_rows, D_MODEL), lambda i, j: (i // tiles_per_group, 0, 0))
    return pl.pallas_call(
        _ffn_in_kernel,
        grid=(m // tm, nj),
        in_specs=[
            pl.BlockSpec((tm, D_MODEL), lambda i, j: (i, 0), pipeline_mode=pl.Buffered(1)),
            mod_spec,
            mod_spec,
            pl.BlockSpec((None, D_MODEL, tn), lambda i, j: (layer, 0, j)),
            pl.BlockSpec((None, D_MODEL, tn), lambda i, j: (layer, 0, j + nj)),
        ],
        out_specs=pl.BlockSpec((tm, tn), lambda i, j: (i, j)),
        out_shape=jax.ShapeDtypeStruct((m, D_FF), BF16),
        scratch_shapes=[pltpu.VMEM((tm, D_MODEL), BF16)],
        compiler_params=_params("parallel", "arbitrary"),
        name=name,
    )(x, sc, sh, w, w)


def _gla_kernel(*refs, chunk, t_valid, has_s0):
    if has_s0:
        q_ref, k_ref, v_ref, r_ref, a_ref, wa_ref, ba_ref, g_ref, s0_ref, o_ref, st_ref = refs
    else:
        q_ref, k_ref, v_ref, r_ref, a_ref, wa_ref, ba_ref, g_ref, o_ref, st_ref = refs
    sub = GLA_SUB
    n_sub = chunk // sub
    anchor = sub // 2 - 1

    @pl.when(pl.program_id(1) == 0)
    def _():
        if has_s0:
            st_ref[...] = s0_ref[...]
        else:
            st_ref[...] = jnp.zeros(st_ref.shape, F32)

    row = lax.broadcasted_iota(jnp.int32, (chunk, 1), 0)
    tri = _ind(lax.broadcasted_iota(jnp.int32, (chunk, chunk), 1)
               <= lax.broadcasted_iota(jnp.int32, (chunk, chunk), 0), BF16)
    ones = jnp.ones((chunk, LANES), BF16)
    causal = (lax.broadcasted_iota(jnp.int32, (sub, sub), 1)
              <= lax.broadcasted_iota(jnp.int32, (sub, sub), 0))
    z_all = _dot(a_ref[...], wa_ref[...]) + ba_ref[...]

    for h in range(GLA_HEADS):
        dk = slice(h * GLA_DK, (h + 1) * GLA_DK)
        dv = slice(h * GLA_DV, (h + 1) * GLA_DV)
        z = z_all[:, dk]
        lb = (jnp.minimum(z, 0.0) - jnp.log1p(jnp.exp(-jnp.abs(z)))) * (1.0 / GLA_TAU)
        kh = k_ref[:, dk]
        if t_valid < chunk:
            lb = jnp.where(row < t_valid, lb, 0.0)
            kh = jnp.where(row < t_valid, kh, 0.0)
        qh = q_ref[:, dk] * (GLA_DK ** -0.5)
        vh = v_ref[:, dv].astype(BF16)
        lb3 = _split3(lb)
        b = sum(jnp.dot(tri, t, preferred_element_type=F32) for t in lb3)
        b_last = b[chunk - 1:chunk, :]
        s_old = st_ref[0, h]
        o = _dot(qh * jnp.exp(b), s_old)
        o_rows = []
        for s in range(n_sub):
            rs = slice(s * sub, (s + 1) * sub)
            bs, qs, ks = b[rs], qh[rs], kh[rs]
            b_mid = b[s * sub + anchor:s * sub + anchor + 1, :]
            a_diag = _dot_nt(qs * jnp.exp(bs - b_mid), ks * jnp.exp(b_mid - bs))
            o_s = _dot(jnp.where(causal, a_diag, 0.0), vh[rs])
            if s > 0:
                prev = slice(0, s * sub)
                b_in = b[s * sub - 1:s * sub, :]
                a_off = _dot_nt(qs * jnp.exp(bs - b_in), kh[prev] * jnp.exp(b_in - b[prev]))
                o_s = o_s + _dot(a_off, vh[prev])
            o_rows.append(o_s)
        o = o + jnp.concatenate(o_rows, axis=0)
        dec = jnp.exp(sum(lax.dot_general(t, ones, (((0,), (0,)), ((), ())),
                                          preferred_element_type=F32) for t in lb3))
        st_ref[0, h] = s_old * jnp.tile(dec, (1, GLA_DV // LANES)) + _dot_tn(kh * jnp.exp(b_last - b), vh)
        o = o * lax.rsqrt(jnp.mean(o * o, axis=-1, keepdims=True) + LN_EPS)
        o_ref[:, dv] = (o * g_ref[:, dv] * _silu(r_ref[:, dv])).astype(BF16)


def _gla(main, tail, w_alpha, b_alpha, norm_g, s0, batch, seq, chunk, t_valid):
    n_chunks = seq // chunk
    qk_w = GLA_HEADS * GLA_DK
    v_w = GLA_HEADS * GLA_DV
    wa = jnp.pad(w_alpha, ((0, LANES - GLA_RANK), (0, 0)))
    row_map = lambda b, c: (b * n_chunks + c, 0)
    const = lambda b, c: (0, 0)
    in_specs = [
        pl.BlockSpec((chunk, qk_w), row_map),
        pl.BlockSpec((chunk, qk_w), lambda b, c: (b * n_chunks + c, 1)),
        pl.BlockSpec((chunk, v_w), lambda b, c: (b * n_chunks + c, 1)),
        pl.BlockSpec((chunk, v_w), lambda b, c: (b * n_chunks + c, 2)),
        pl.BlockSpec((chunk, LANES), row_map),
        pl.BlockSpec((LANES, qk_w), const),
        pl.BlockSpec((1, qk_w), const),
        pl.BlockSpec((1, v_w), const),
    ]
    args = [main, main, main, main, tail, wa, b_alpha.reshape(1, qk_w), norm_g.reshape(1, v_w)]
    state_spec = pl.BlockSpec((1, GLA_HEADS, GLA_DK, GLA_DV), lambda b, c: (b, 0, 0, 0))
    if s0 is not None:
        in_specs.append(state_spec)
        args.append(s0)
    return pl.pallas_call(
        functools.partial(_gla_kernel, chunk=chunk, t_valid=t_valid, has_s0=s0 is not None),
        grid=(batch, n_chunks),
        in_specs=in_specs,
        out_specs=[pl.BlockSpec((chunk, v_w), row_map), state_spec],
        out_shape=[jax.ShapeDtypeStruct((batch * seq, v_w), BF16),
                   jax.ShapeDtypeStruct((batch, GLA_HEADS, GLA_DK, GLA_DV), F32)],
        compiler_params=_params("parallel", "arbitrary"),
        name="gla_prompt" if s0 is None else "gla_sample",
    )(*args)


def _select_blocks(imp_t, blk, cur):
    n = imp_t.shape[0]
    forced = (blk == 0) | ((blk >= cur - 1) & (blk <= cur))
    score = jnp.where(blk > cur, -jnp.inf, jnp.where(forced, jnp.inf, imp_t))
    rank = jnp.zeros(score.shape, F32)
    for i in range(n):
        si = score[i:i + 1, :]
        rank = rank + jnp.where(blk > i, _ind(si >= score), _ind(si > score))
    return (rank < NSA_TOPK) & (score > -jnp.inf)


def _nsa_prompt_kernel(q_ref, kc_ref, vc_ref, ks_ref, vs_ref, kw_ref, vw_ref, gt_ref, bg_ref, o_ref,
                       kcm, vcm, ksb, vst, kwb, vwt, qs_ref, sel_ref, m_ref, acc_ref, os_ref,
                       *, seq, tq, tk):
    qi = pl.program_id(2)
    n_blk = seq // NSA_BLOCK
    n_kt = seq // tk
    blk_per_kt = tk // NSA_BLOCK

    @pl.when(qi == 0)
    def _():
        kcm[...] = jnp.sum(kc_ref[...].reshape(n_blk, NSA_BLOCK, NSA_HD), axis=1) * (1.0 / NSA_BLOCK)
        vcm[...] = jnp.sum(vc_ref[...].reshape(n_blk, NSA_BLOCK, NSA_HD), axis=1) * (1.0 / NSA_BLOCK)
        ksb[...] = ks_ref[...].astype(BF16)
        kwb[...] = kw_ref[...].astype(BF16)
        ones_rows = _ind(lax.broadcasted_iota(jnp.int32, (BF16_ROWS, tk), 0) == 0, BF16)
        for kt in range(n_kt):
            rows = slice(kt * tk, (kt + 1) * tk)
            vst[kt] = jnp.concatenate([vs_ref[rows, :].T.astype(BF16), ones_rows], axis=0)
            vwt[kt] = jnp.concatenate([vw_ref[rows, :].T.astype(BF16), ones_rows], axis=0)

    t0 = qi * tq
    q = q_ref[...] * (NSA_HD ** -0.5 * LOG2_E)
    qs_ref[...] = jnp.concatenate(
        [q[:, g * NSA_HD:(g + 1) * NSA_HD] for g in range(NSA_GROUP)], axis=0).astype(BF16)
    qs = qs_ref[...]
    t_row = t0 + lax.broadcasted_iota(jnp.int32, (1, tq), 1)
    t_row_g = jnp.concatenate([t_row] * NSA_GROUP, axis=1)

    kc_b = kcm[...].astype(BF16)
    blk_col = lax.broadcasted_iota(jnp.int32, (n_blk, 1), 0)
    p_t = _masked_softmax2(_dot_nt(kc_b, qs), (blk_col + 1) * NSA_BLOCK - 1 <= t_row_g, axis=0)
    o_c = _dot_tn(vcm[...], p_t)
    imp_t = sum(p_t[:, g * tq:(g + 1) * tq] for g in range(NSA_GROUP))
    sel_ref[...] = jnp.where(_select_blocks(imp_t, blk_col, t_row // NSA_BLOCK), 0.0, NEG_BIG)

    n_loc = lax.broadcasted_iota(jnp.int32, (tk, tq), 0)
    t_loc = lax.broadcasted_iota(jnp.int32, (tk, tq), 1)
    on_lanes = lambda a: jnp.concatenate([a] * NSA_GROUP, axis=1)
    causal = on_lanes(jnp.where(n_loc <= t_loc, 0.0, NEG_BIG))
    far = on_lanes(jnp.where(n_loc > t_loc, 0.0, NEG_BIG))

    def sel_bias(kt):
        return on_lanes(jnp.concatenate(
            [jnp.broadcast_to(sel_ref[pl.ds(kt * blk_per_kt + j, 1), :], (NSA_BLOCK, tq))
             for j in range(blk_per_kt)], axis=0))

    def reset():
        m_ref[...] = jnp.full(m_ref.shape, NEG_BIG, F32)
        acc_ref[...] = jnp.zeros(acc_ref.shape, F32)

    def step(k_ref, vt_ref, kt, bias):
        s = _dot_nt(k_ref[pl.ds(pl.multiple_of(kt * tk, tk), tk), :], qs_ref[...])
        if bias is not None:
            s = s + bias
        m_old = m_ref[...]
        m_new = jnp.maximum(m_old, jnp.max(s, axis=0, keepdims=True))
        acc_ref[...] = jnp.exp2(m_old - m_new) * acc_ref[...] + _dot(vt_ref[kt], jnp.exp2(s - m_new))
        m_ref[...] = m_new

    def result():
        acc = acc_ref[...]
        return acc[:NSA_HD] / jnp.maximum(acc[NSA_HD:NSA_HD + 1], TINY)

    reset()
    step(ksb, vst, qi, sel_bias(qi) + causal)

    def sel_body(i, carry):
        kt = qi - 1 - i
        step(ksb, vst, kt, sel_bias(kt))
        return carry

    lax.fori_loop(0, qi, sel_body, 0)
    os_ref[...] = result()

    reset()
    step(kwb, vwt, qi, causal)

    @pl.when(qi >= 1)
    def _():
        step(kwb, vwt, qi - 1, None)

    @pl.when(qi >= 2)
    def _():
        step(kwb, vwt, qi - 2, far)

    o_w = result()

    gates = jax.nn.sigmoid(gt_ref[...] + bg_ref[...])
    g_c, g_s, g_w = (jnp.concatenate([gates[3 * g + br:3 * g + br + 1, :] for g in range(NSA_GROUP)], axis=1)
                     for br in range(3))
    o_t = g_c * o_c + g_s * os_ref[...] + g_w * o_w
    for g in range(NSA_GROUP):
        o_ref[:, g * NSA_HD:(g + 1) * NSA_HD] = o_t[:, g * tq:(g + 1) * tq].T.astype(BF16)


def _nsa_prompt(main, gates, b_gate, batch, seq):
    tq, tk = 256, 256
    assert tq == tk and NSA_WINDOW == 2 * tk and seq % tk == 0
    vt_rows = NSA_HD + BF16_ROWS
    nq = seq // tq
    q_lanes = NSA_GROUP * NSA_HD
    col0 = NSA_Q_W // NSA_HD

    def kv_spec(branch, part):
        off = col0 + (2 * branch + part) * NSA_KV
        return pl.BlockSpec((seq, NSA_HD), lambda b, kv, qi: (b, off + kv))

    g_w = 3 * NSA_GROUP
    g_rows = NSA_GROUP * tq
    return pl.pallas_call(
        functools.partial(_nsa_prompt_kernel, seq=seq, tq=tq, tk=tk),
        grid=(batch, NSA_KV, nq),
        in_specs=[pl.BlockSpec((tq, q_lanes), lambda b, kv, qi: (b * nq + qi, kv))]
        + [kv_spec(br, part) for br in range(3) for part in range(2)]
        + [pl.BlockSpec((None, g_w, tq), lambda b, kv, qi: (kv, 0, b * nq + qi)),
           pl.BlockSpec((None, g_w, 1), lambda b, kv, qi: (kv, 0, 0))],
        out_specs=pl.BlockSpec((tq, q_lanes), lambda b, kv, qi: (b * nq + qi, kv)),
        out_shape=jax.ShapeDtypeStruct((batch * seq, NSA_Q_W), BF16),
        scratch_shapes=[
            pltpu.VMEM((seq // NSA_BLOCK, NSA_HD), F32),
            pltpu.VMEM((seq // NSA_BLOCK, NSA_HD), F32),
            pltpu.VMEM((seq, NSA_HD), BF16),
            pltpu.VMEM((seq // tk, vt_rows, tk), BF16),
            pltpu.VMEM((seq, NSA_HD), BF16),
            pltpu.VMEM((seq // tk, vt_rows, tk), BF16),
            pltpu.VMEM((g_rows, NSA_HD), BF16),
            pltpu.VMEM((seq // NSA_BLOCK, tq), F32),
            pltpu.VMEM((1, g_rows), F32),
            pltpu.VMEM((vt_rows, g_rows), F32),
            pltpu.VMEM((NSA_HD, g_rows), F32),
        ],
        compiler_params=_params("parallel", "parallel", "arbitrary"),
        name="nsa_prompt",
    )(main, main, main, main, main, main, main, gates, b_gate)


def _page_specs(n):
    return [pl.BlockSpec((1, PAGE * KV_ROWS, NSA_HD), lambda b, p, pt, i=i: (pt[b, p * n + i], 0, 0))
            for i in range(n)]


def _page_means_kernel(pt_ref, *refs):
    x_refs, o_ref = refs[:-1], refs[-1]
    for i, x_ref in enumerate(x_refs):
        x = x_ref[0].reshape(PAGE // NSA_BLOCK, NSA_BLOCK, KV_ROWS, NSA_HD)
        o_ref[0, i] = jnp.sum(x, axis=1) * (1.0 / NSA_BLOCK)


def _page_means(cache, page_table):
    seqs, n_pages = page_table.shape
    per_page = PAGE // NSA_BLOCK
    per_step = 8
    return pl.pallas_call(
        _page_means_kernel,
        grid_spec=pltpu.PrefetchScalarGridSpec(
            num_scalar_prefetch=1,
            grid=(seqs, n_pages // per_step),
            in_specs=_page_specs(per_step),
            out_specs=pl.BlockSpec((1, per_step, per_page, KV_ROWS, NSA_HD), lambda b, p, pt: (b, p, 0, 0, 0)),
        ),
        out_shape=jax.ShapeDtypeStruct((seqs, n_pages, per_page, KV_ROWS, NSA_HD), F32),
        compiler_params=_params("parallel", "arbitrary"),
        name="nsa_page_means",
    )(page_table, *([cache] * per_step))


def _nsa_sample_kernel(pt_ref, q_ref, cm_ref, *refs, past, n_new, n_steps, per_step):
    pg_refs = refs[:per_step]
    (kvn_ref, wb_ref, kwn_ref, gt_ref, bg_ref, o_ref,
     qr_ref, sel_ref, m_ref, l_ref, acc_ref, oc_ref, ow_ref) = refs[per_step:]
    p = pl.program_id(1)
    k_w = NSA_KV * NSA_HD
    cols = NSA_KV * n_new * NSA_GROUP
    per_kv = n_new * NSA_GROUP
    n_cmp = past // NSA_BLOCK
    n_sel = -(-(past + n_new) // NSA_BLOCK)
    n_sel_pad = sel_ref.shape[0]
    lane = lax.broadcasted_iota(jnp.int32, (1, cols), 1)
    q_pos = past + (lane // NSA_GROUP) % n_new

    def scores(keys):
        return _dot_nt(keys, qr_ref[...])

    def own_head(x_t):
        out = jnp.zeros((NSA_HD, cols), F32)
        for kv in range(NSA_KV):
            mine = (lane // per_kv) == kv
            out = out + jnp.where(mine, x_t[kv * NSA_HD:(kv + 1) * NSA_HD, :], 0.0)
        return out

    def attend_once(keys, vals, mask):
        p_t = _masked_softmax2(scores(keys), mask, axis=0)
        return own_head(_dot_tn(vals, p_t))

    def online(keys, vals, bias):
        s = scores(keys) + bias
        m_old = m_ref[...]
        m_new = jnp.maximum(m_old, jnp.max(s, axis=0, keepdims=True))
        p_t = jnp.exp2(s - m_new)
        alpha = jnp.exp2(m_old - m_new)
        l_ref[...] = alpha * l_ref[...] + jnp.sum(p_t, axis=0, keepdims=True)
        acc_ref[...] = alpha * acc_ref[...] + own_head(_dot_tn(vals, p_t))
        m_ref[...] = m_new

    @pl.when(p == 0)
    def _():
        q = q_ref[0] * (NSA_HD ** -0.5 * LOG2_E)
        r_kv = lax.broadcasted_iota(jnp.int32, (cols, k_w), 0) // per_kv
        c_kv = lax.broadcasted_iota(jnp.int32, (cols, k_w), 1) // NSA_HD
        qr_ref[...] = jnp.where(r_kv == c_kv, jnp.tile(q, (1, NSA_KV)), 0.0).astype(BF16)
        cm = cm_ref[0]
        blk = lax.broadcasted_iota(jnp.int32, (n_cmp, 1), 0)
        p_t = _masked_softmax2(scores(cm[:, :k_w]), (blk + 1) * NSA_BLOCK - 1 <= q_pos, axis=0)
        oc_ref[...] = own_head(_dot_tn(cm[:, k_w:], p_t))
        same = _ind(lax.broadcasted_iota(jnp.int32, (cols, cols), 0) // NSA_GROUP
                    == lax.broadcasted_iota(jnp.int32, (cols, cols), 1) // NSA_GROUP, BF16)
        imp = sum(jnp.dot(t, same, preferred_element_type=F32) for t in _split3(p_t))
        imp = jnp.concatenate([imp, jnp.zeros((n_sel_pad - n_cmp, cols), F32)], axis=0)
        blk_s = lax.broadcasted_iota(jnp.int32, (n_sel_pad, 1), 0)
        cur = q_pos // NSA_BLOCK
        forced = (blk_s == 0) | ((blk_s >= cur - 1) & (blk_s <= cur))
        score = jnp.where((blk_s > cur) | (blk_s >= n_sel), -jnp.inf, jnp.where(forced, jnp.inf, imp))
        sel_ref[...] = score

        def rank_body(i, rank):
            si = sel_ref[pl.ds(i, 1), :]
            return rank + jnp.where(blk_s > i, _ind(si >= score), _ind(si > score))

        rank = lax.fori_loop(0, n_sel, rank_body, jnp.zeros((n_sel_pad, cols), F32))
        sel_ref[...] = jnp.where((rank < NSA_TOPK) & (score > -jnp.inf), 0.0, NEG_BIG)
        wb = wb_ref[0]
        n_wb = wb.shape[0]
        kw_all = jnp.concatenate([wb[:, :k_w], kwn_ref[0][:, :k_w]], axis=0)
        vw_all = jnp.concatenate([wb[:, k_w:], kwn_ref[0][:, k_w:]], axis=0)
        n_w = kw_all.shape[0]
        w_row = lax.broadcasted_iota(jnp.int32, (n_w, 1), 0)
        w_pos = past - n_wb + w_row
        d = q_pos - w_pos
        ow_ref[...] = attend_once(kw_all, vw_all,
                                  (d >= 0) & (d < NSA_WINDOW) & (w_pos >= 0) & (w_row < n_wb + n_new))
        m_ref[...] = jnp.full(m_ref.shape, NEG_BIG, F32)
        l_ref[...] = jnp.zeros(l_ref.shape, F32)
        acc_ref[...] = jnp.zeros(acc_ref.shape, F32)

    def page_part(part):
        return jnp.concatenate(
            [jnp.concatenate([pg_ref[0, pl.ds(part * NSA_KV + kv, PAGE, stride=KV_ROWS), :]
                              for kv in range(NSA_KV)], axis=1) for pg_ref in pg_refs], axis=0)

    blk_per_step = per_step * PAGE // NSA_BLOCK
    online(page_part(0), page_part(1), jnp.concatenate(
        [jnp.broadcast_to(sel_ref[pl.ds(p * blk_per_step + j, 1), :], (NSA_BLOCK, cols))
         for j in range(blk_per_step)], axis=0))

    @pl.when(p == n_steps - 1)
    def _():
        kvn = kvn_ref[0]
        n_row = lax.broadcasted_iota(jnp.int32, (kvn.shape[0], 1), 0)
        bias_new = jnp.where((n_row < n_new) & (past + n_row <= q_pos), 0.0, NEG_BIG)
        online(kvn[:, :k_w], kvn[:, k_w:], bias_new + sel_ref[pl.ds(past // NSA_BLOCK, 1), :])
        o_s = acc_ref[...] / jnp.maximum(l_ref[...], TINY)
        gates = jax.nn.sigmoid(gt_ref[0] + bg_ref[...])
        o_t = gates[0:1, :] * oc_ref[...] + gates[1:2, :] * o_s + gates[2:3, :] * ow_ref[...]
        eye = _ind(lax.broadcasted_iota(jnp.int32, (NSA_HD, NSA_HD), 0)
                   == lax.broadcasted_iota(jnp.int32, (NSA_HD, NSA_HD), 1), BF16)
        o_ref[0] = sum(lax.dot_general(t, eye, (((0,), (0,)), ((), ())), preferred_element_type=F32)
                       for t in _split3(o_t))


def _nsa_sample(q_rows, cmp_means, cache_slc, page_table, kv_new, win_buf, kw_new, gates, b_gate,
                past, n_new):
    seqs, n_pages = page_table.shape
    cols = q_rows.shape[1]
    n_cmp = cmp_means.shape[1]
    n_sel_pad = -(-(-(-(past + n_new) // NSA_BLOCK)) // 8) * 8
    n_wb = win_buf.shape[1]
    pad_new = kv_new.shape[1]
    per_step = 4
    n_steps = n_pages // per_step
    seq_map = lambda b, p, pt: (b, 0, 0)
    return pl.pallas_call(
        functools.partial(_nsa_sample_kernel, past=past, n_new=n_new, n_steps=n_steps, per_step=per_step),
        grid_spec=pltpu.PrefetchScalarGridSpec(
            num_scalar_prefetch=1,
            grid=(seqs, n_steps),
            in_specs=[
                pl.BlockSpec((1, cols, NSA_HD), seq_map),
                pl.BlockSpec((1, n_cmp, NSA_KV_W), seq_map),
                *_page_specs(per_step),
                pl.BlockSpec((1, pad_new, NSA_KV_W), seq_map),
                pl.BlockSpec((1, n_wb, NSA_KV_W), seq_map),
                pl.BlockSpec((1, pad_new, NSA_KV_W), seq_map),
                pl.BlockSpec((1, 3, cols), seq_map),
                pl.BlockSpec((3, cols), lambda b, p, pt: (0, 0)),
            ],
            out_specs=pl.BlockSpec((1, cols, NSA_HD), seq_map),
            scratch_shapes=[
                pltpu.VMEM((cols, NSA_KV * NSA_HD), BF16),
                pltpu.VMEM((n_sel_pad, cols), F32),
                pltpu.VMEM((1, cols), F32),
                pltpu.VMEM((1, cols), F32),
                pltpu.VMEM((NSA_HD, cols), F32),
                pltpu.VMEM((NSA_HD, cols), F32),
                pltpu.VMEM((NSA_HD, cols), F32),
            ],
        ),
        out_shape=jax.ShapeDtypeStruct((seqs, cols, NSA_HD), F32),
        compiler_params=_params("parallel", "arbitrary"),
        name="nsa_sample",
    )(page_table, q_rows, cmp_means, *([cache_slc] * per_step), kv_new, win_buf, kw_new, gates, b_gate)


def kernel(x_prompt, x_sample, c_prompt, c_sample, state_gla, cache_cmp_kv, cache_slc_kv, cache_win_kv,
           page_table, ada_w, ada_b, gla_w_in, gla_w_alpha, gla_b_alpha, gla_norm_g, gla_w_out, nsa_w_in,
           nsa_b_gate, nsa_w_out, ln_mix_g, ln_mix_b, ffn_w_in, ffn_w_out, ln_ffn_g, ln_ffn_b):
    batch, seq, _ = x_prompt.shape
    seqs, n_new, _ = x_sample.shape
    n_pages = page_table.shape[1]
    past = n_pages * PAGE
    rows_s = seqs * n_new
    tm_p = seq
    w_out_mix = (gla_w_out.astype(BF16), nsa_w_out.astype(BF16))
    w_out_ffn = ffn_w_out.astype(BF16)

    pad_rows = 16 - (batch + seqs)
    c_all = jnp.concatenate([c_prompt, c_sample, jnp.zeros((pad_rows, D_MODEL), F32)], axis=0)
    mod = _adaln(c_all, ada_w, ada_b).reshape(DEPTH, 16, 6, D_MODEL)

    def mods(layer):
        mp = [mod[layer, :batch, i].reshape(batch, 1, D_MODEL) for i in range(6)]
        ms = [jnp.repeat(mod[layer, batch:batch + seqs, i], n_new, axis=0).reshape(1, rows_s, D_MODEL)
              for i in range(6)]
        return mp, ms

    def mixer_out(o, x, layer, m, tm, tag):
        return _out_ln(o, w_out_mix[layer % 2], layer // 2, x, m[2], ln_mix_g, ln_mix_b, layer, min(tm, 512), tag)

    def ffn(x, layer, m, tm, tag):
        act = _ffn_in(x, m[4], m[3], ffn_w_in, layer, tm, "ffn_in_" + tag)
        return _out_ln(act, w_out_ffn, layer, x, m[5], ln_ffn_g, ln_ffn_b, layer, min(tm, 256),
                       "ffn_out_" + tag)

    xp = x_prompt.reshape(batch * seq, D_MODEL)
    xs = x_sample.reshape(rows_s, D_MODEL)

    mp, ms = mods(0)
    main_p, tail_p = _proj(xp, mp[1], mp[0], gla_w_in, 0, GLA_MAIN_W, tm_p, "gla_in_prompt")
    gated_p, gla_state_p = _gla(main_p, tail_p, gla_w_alpha[0], gla_b_alpha[0], gla_norm_g[0], None,
                                batch, seq, 128, 128)
    main_s, tail_s = _proj(xs, ms[1], ms[0], gla_w_in, 0, GLA_MAIN_W, rows_s, "gla_in_sample")
    chunk_s = GLA_SUB

    def pad_steps(a):
        a = a.reshape(seqs, n_new, a.shape[-1])
        return jnp.pad(a, ((0, 0), (0, chunk_s - n_new), (0, 0))).reshape(seqs * chunk_s, a.shape[-1])

    gated_s, gla_state_s = _gla(pad_steps(main_s), pad_steps(tail_s), gla_w_alpha[0], gla_b_alpha[0],
                                gla_norm_g[0], state_gla[:, 0], seqs, chunk_s, chunk_s, n_new)
    gated_s = gated_s.reshape(seqs, chunk_s, -1)[:, :n_new].reshape(rows_s, -1)
    xp = mixer_out(gated_p, xp, 0, mp, tm_p, "gla_out_prompt")
    xs = mixer_out(gated_s, xs, 0, ms, rows_s, "gla_out_sample")
    xp = ffn(xp, 0, mp, tm_p, "prompt0")
    xs = ffn(xs, 0, ms, rows_s, "sample0")

    mp, ms = mods(1)
    main_p, tail_p = _proj(xp, mp[1], mp[0], nsa_w_in, 0, NSA_MAIN_W, tm_p, "nsa_in_prompt")
    main_s, tail_s = _proj(xs, ms[1], ms[0], nsa_w_in, 0, NSA_MAIN_W, rows_s, "nsa_in_sample")
    n_gate = 3 * NSA_HEADS
    g_w = 3 * NSA_GROUP
    gates_p = tail_p[:, :n_gate].reshape(batch * seq, NSA_KV, g_w).transpose(1, 2, 0)
    b_gate_p = nsa_b_gate[0].reshape(NSA_KV, g_w, 1)
    o_p = _nsa_prompt(main_p, gates_p, b_gate_p, batch, seq)

    kv_rows = lambda m, br: m[:, NSA_Q_W + br * NSA_KV_W:NSA_Q_W + (br + 1) * NSA_KV_W]
    kv_shape = (2, NSA_KV, NSA_HD)
    cmp_kv_p = kv_rows(main_p, 0).reshape(batch, 1, seq, *kv_shape)
    slc_kv_p = kv_rows(main_p, 1).reshape(batch, 1, seq, *kv_shape)
    keep_p = min(NSA_WINDOW, seq)
    win_kv_p = kv_rows(main_p, 2).reshape(batch, seq, NSA_KV_W)[:, seq - keep_p:].reshape(batch, 1, keep_p, *kv_shape)

    cmp_kv_s = kv_rows(main_s, 0).reshape(seqs, 1, n_new, *kv_shape)
    slc_kv_s = kv_rows(main_s, 1).reshape(seqs, 1, n_new, *kv_shape)
    kw_new = kv_rows(main_s, 2).reshape(seqs, n_new, NSA_KV_W)
    win_buf = cache_win_kv[:, 0].reshape(seqs, -1, NSA_KV_W)
    n_wb = win_buf.shape[1]
    all_w = jnp.concatenate([win_buf, kw_new], axis=1)
    keep_s = min(NSA_WINDOW, n_wb + n_new)
    win_kv_s = all_w[:, n_wb + n_new - keep_s:].reshape(seqs, 1, keep_s, *kv_shape)

    n_phys = cache_cmp_kv.shape[0]
    page_rows = lambda cache: cache.reshape(n_phys, PAGE * KV_ROWS, NSA_HD)
    cmp_means = _page_means(page_rows(cache_cmp_kv), page_table)
    cmp_means = cmp_means.reshape(seqs, past // NSA_BLOCK, NSA_KV_W)
    pad_new = 8
    pad8 = lambda a: jnp.pad(a, ((0, 0), (0, pad_new - n_new), (0, 0)))
    cols = NSA_KV * n_new * NSA_GROUP
    q_rows = (main_s[:, :NSA_Q_W].reshape(seqs, n_new, NSA_KV, NSA_GROUP, NSA_HD)
              .transpose(0, 2, 1, 3, 4).reshape(seqs, cols, NSA_HD))
    gates_s = (tail_s[:, :n_gate].reshape(seqs, n_new, NSA_KV, NSA_GROUP, 3)
               .transpose(0, 4, 2, 1, 3).reshape(seqs, 3, cols))
    b_gate_s = jnp.broadcast_to(nsa_b_gate[0].reshape(1, NSA_KV, NSA_GROUP, 3), (n_new, NSA_KV, NSA_GROUP, 3))
    b_gate_s = b_gate_s.transpose(3, 1, 0, 2).reshape(3, cols)
    o_s = _nsa_sample(q_rows, cmp_means, page_rows(cache_slc_kv), page_table,
                      pad8(slc_kv_s.reshape(seqs, n_new, NSA_KV_W)), win_buf, pad8(kw_new), gates_s, b_gate_s,
                      past, n_new)
    o_s = (o_s.reshape(seqs, NSA_KV, n_new, NSA_GROUP, NSA_HD).transpose(0, 2, 1, 3, 4)
           .reshape(rows_s, NSA_Q_W).astype(BF16))

    xp = mixer_out(o_p, xp, 1, mp, tm_p, "nsa_out_prompt")
    xs = mixer_out(o_s, xs, 1, ms, rows_s, "nsa_out_sample")
    xp = ffn(xp, 1, mp, tm_p, "prompt1")
    xs = ffn(xs, 1, ms, rows_s, "sample1")

    return (xp.reshape(batch, seq, D_MODEL), xs.reshape(seqs, n_new, D_MODEL),
            gla_state_p[:, None], gla_state_s[:, None], cmp_kv_p, cmp_kv_s, slc_kv_p, slc_kv_s,
            win_kv_p, win_kv_s)
```

```python
import functools

import jax
import jax.numpy as jnp
from jax import lax
from jax.experimental import pallas as pl
from jax.experimental.pallas import tpu as pltpu

F32 = jnp.float32
BF16 = jnp.bfloat16

D_MODEL = 2048
DEPTH = 2
DN_ALPHA = (2 * DEPTH) ** 0.25
LN_EPS = 1e-5
D_FF = 5632
GLA_HEADS = 4
GLA_DK = 256
GLA_DV = 512
GLA_RANK = 16
GLA_TAU = 16.0
GLA_SUB = 32
GLA_MAIN_W = 2 * GLA_HEADS * GLA_DK + 2 * GLA_HEADS * GLA_DV
NSA_HEADS = 16
NSA_KV = 4
NSA_HD = 128
NSA_GROUP = NSA_HEADS // NSA_KV
NSA_BLOCK = 64
NSA_TOPK = 16
NSA_WINDOW = 512
NSA_Q_W = NSA_HEADS * NSA_HD
NSA_KV_W = 2 * NSA_KV * NSA_HD
NSA_MAIN_W = NSA_Q_W + 3 * NSA_KV_W
PAGE = 128
KV_ROWS = 2 * NSA_KV

LANES = 128
BF16_ROWS = 16
LOG2_E = 1.4426950408889634
V7X_VMEM_BYTES = 64 * 1024 * 1024
VMEM_LIMIT = V7X_VMEM_BYTES * 7 // 8

NEG_BIG = -1e30
TINY = float(jnp.finfo(jnp.float32).tiny)


def _params(*sem):
    return pltpu.CompilerParams(dimension_semantics=sem, vmem_limit_bytes=VMEM_LIMIT)


def _dot(a, b):
    return jnp.dot(a.astype(BF16), b.astype(BF16), preferred_element_type=F32)


def _dot_nt(a, b):
    return lax.dot_general(a.astype(BF16), b.astype(BF16), (((1,), (1,)), ((), ())),
                           preferred_element_type=F32)


def _dot_tn(a, b):
    return lax.dot_general(a.astype(BF16), b.astype(BF16), (((0,), (0,)), ((), ())),
                           preferred_element_type=F32)


def _split3(x):
    hi = x.astype(BF16)
    r1 = x - hi.astype(F32)
    mid = r1.astype(BF16)
    lo = (r1 - mid.astype(F32)).astype(BF16)
    return hi, mid, lo


def _silu(x):
    return x * jax.nn.sigmoid(x)


def _ind(cond, dtype=F32):
    return jnp.where(cond, 1.0, 0.0).astype(dtype)


def _masked_softmax2(s2, mask, axis):
    s2 = jnp.where(mask, s2, -jnp.inf)
    m = jnp.max(s2, axis=axis, keepdims=True)
    m = jnp.where(jnp.isfinite(m), m, 0.0)
    e = jnp.where(mask, jnp.exp2(s2 - m), 0.0)
    return e / jnp.maximum(jnp.sum(e, axis=axis, keepdims=True), TINY)


def _adaln_kernel(c_ref, w_ref, b_ref, o_ref):
    o_ref[...] = _dot(_silu(c_ref[...]), w_ref[...]) + b_ref[...]


def _adaln(c_all, ada_w, ada_b):
    rows = c_all.shape[0]
    n = ada_w.shape[2]
    tn = 1024
    return pl.pallas_call(
        _adaln_kernel,
        grid=(DEPTH, n // tn),
        in_specs=[
            pl.BlockSpec((rows, D_MODEL), lambda l, j: (0, 0)),
            pl.BlockSpec((None, D_MODEL, tn), lambda l, j: (l, 0, j)),
            pl.BlockSpec((None, 1, tn), lambda l, j: (l, 0, j)),
        ],
        out_specs=pl.BlockSpec((None, rows, tn), lambda l, j: (l, 0, j)),
        out_shape=jax.ShapeDtypeStruct((DEPTH, rows, n), F32),
        compiler_params=_params("parallel", "parallel"),
        name="adaln",
    )(c_all, ada_w, ada_b.reshape(DEPTH, 1, n))


def _modulate(x_ref, sc_ref, sh_ref):
    return (x_ref[...] * (1.0 + sc_ref[...]) + sh_ref[...]).astype(BF16)


def _proj_kernel(x_ref, sc_ref, sh_ref, xs_ref, scs_ref, shs_ref, w_ref, wt_ref,
                 o_ref, ot_ref, os_ref, ots_ref, h_ref, hs_ref, *, n_i):
    i, j = pl.program_id(0), pl.program_id(1)

    @pl.when(j == 0)
    def _():
        h_ref[...] = _modulate(x_ref, sc_ref, sh_ref)
        ot_ref[...] = _dot(h_ref[...], wt_ref[...])

    o_ref[...] = _dot(h_ref[...], w_ref[...])

    @pl.when(i == n_i - 1)
    def _():
        @pl.when(j == 0)
        def _():
            hs_ref[...] = _modulate(xs_ref, scs_ref, shs_ref)
            ots_ref[...] = _dot(hs_ref[...], wt_ref[...])

        os_ref[...] = _dot(hs_ref[...], w_ref[...])


def _proj(x, sc, sh, xs, scs, shs, w, layer, n_main, tm, name):
    m, ms = x.shape[0], xs.shape[0]
    n_tail = w.shape[2] - n_main
    w_tail = jnp.pad(w[layer, :, n_main:], ((0, 0), (0, LANES - n_tail)))
    tn = 512
    n_i = m // tm
    tiles_per_group = n_i // sc.shape[0]
    mod_spec = pl.BlockSpec((None, sc.shape[1], D_MODEL), lambda i, j: (i // tiles_per_group, 0, 0))
    const2 = lambda i, j: (0, 0)
    const3 = lambda i, j: (0, 0, 0)
    last_tile_col = lambda i, j: (0, jnp.where(i == n_i - 1, j, 0))
    return pl.pallas_call(
        functools.partial(_proj_kernel, n_i=n_i),
        grid=(n_i, n_main // tn),
        in_specs=[
            pl.BlockSpec((tm, D_MODEL), lambda i, j: (i, 0), pipeline_mode=pl.Buffered(1)),
            mod_spec,
            mod_spec,
            pl.BlockSpec((ms, D_MODEL), const2),
            pl.BlockSpec((None, ms, D_MODEL), const3),
            pl.BlockSpec((None, ms, D_MODEL), const3),
            pl.BlockSpec((None, D_MODEL, tn), lambda i, j: (layer, 0, j)),
            pl.BlockSpec((D_MODEL, LANES), const2),
        ],
        out_specs=[
            pl.BlockSpec((tm, tn), lambda i, j: (i, j)),
            pl.BlockSpec((tm, LANES), lambda i, j: (i, 0)),
            pl.BlockSpec((ms, tn), last_tile_col),
            pl.BlockSpec((ms, LANES), const2),
        ],
        out_shape=[jax.ShapeDtypeStruct((m, n_main), F32), jax.ShapeDtypeStruct((m, LANES), F32),
                   jax.ShapeDtypeStruct((ms, n_main), F32), jax.ShapeDtypeStruct((ms, LANES), F32)],
        scratch_shapes=[pltpu.VMEM((tm, D_MODEL), BF16), pltpu.VMEM((ms, D_MODEL), BF16)],
        compiler_params=_params("arbitrary", "arbitrary"),
        name=name,
    )(x, sc, sh, xs, scs, shs, w, w_tail)


def _out_ln_kernel(a_ref, w_ref, x_ref, gt_ref, g_ref, b_ref, o_ref):
    y = DN_ALPHA * x_ref[...] + gt_ref[...] * jnp.dot(a_ref[...], w_ref[...], preferred_element_type=F32)
    mu = jnp.mean(y, axis=-1, keepdims=True)
    yc = y - mu
    var = jnp.mean(yc * yc, axis=-1, keepdims=True)
    o_ref[...] = yc * lax.rsqrt(var + LN_EPS) * g_ref[...] + b_ref[...]


def _out_ln(a, w, w_layer, x, gate, ln_g, ln_b, ln_layer, tm, name):
    m, kdim = a.shape
    groups, mod_rows = gate.shape[0], gate.shape[1]
    tiles_per_group = m // tm // groups
    vec_spec = pl.BlockSpec((None, 1, D_MODEL), lambda i: (ln_layer, 0, 0))
    return pl.pallas_call(
        _out_ln_kernel,
        grid=(m // tm,),
        in_specs=[
            pl.BlockSpec((tm, kdim), lambda i: (i, 0)),
            pl.BlockSpec((None, kdim, D_MODEL), lambda i: (w_layer, 0, 0), pipeline_mode=pl.Buffered(1)),
            pl.BlockSpec((tm, D_MODEL), lambda i: (i, 0)),
            pl.BlockSpec((None, mod_rows, D_MODEL), lambda i: (i // tiles_per_group, 0, 0)),
            vec_spec,
            vec_spec,
        ],
        out_specs=pl.BlockSpec((tm, D_MODEL), lambda i: (i, 0)),
        out_shape=jax.ShapeDtypeStruct((m, D_MODEL), F32),
        compiler_params=_params("parallel"),
        name=name,
    )(a, w, x, gate, ln_g.reshape(-1, 1, D_MODEL), ln_b.reshape(-1, 1, D_MODEL))


def _ffn_in_kernel(x_ref, sc_ref, sh_ref, xs_ref, scs_ref, shs_ref, wa_ref, wu_ref,
                   o_ref, os_ref, h_ref, hs_ref, *, n_i):
    i, j = pl.program_id(0), pl.program_id(1)

    @pl.when(j == 0)
    def _():
        h_ref[...] = _modulate(x_ref, sc_ref, sh_ref)

    wa = wa_ref[...].astype(BF16)
    wu = wu_ref[...].astype(BF16)

    def swiglu_in(h):
        return (_silu(_dot(h, wa)) * _dot(h, wu)).astype(BF16)

    o_ref[...] = swiglu_in(h_ref[...])

    @pl.when(i == n_i - 1)
    def _():
        @pl.when(j == 0)
        def _():
            hs_ref[...] = _modulate(xs_ref, scs_ref, shs_ref)

        os_ref[...] = swiglu_in(hs_ref[...])


def _ffn_in(x, sc, sh, xs, scs, shs, w, layer, tm, name):
    m, ms = x.shape[0], xs.shape[0]
    tn = 256
    nj = D_FF // tn
    n_i = m // tm
    tiles_per_group = n_i // sc.shape[0]
    mod_spec = pl.BlockSpec((None, sc.shape[1], D_MODEL), lambda i, j: (i // tiles_per_group, 0, 0))
    const3 = lambda i, j: (0, 0, 0)
    return pl.pallas_call(
        functools.partial(_ffn_in_kernel, n_i=n_i),
        grid=(n_i, nj),
        in_specs=[
            pl.BlockSpec((tm, D_MODEL), lambda i, j: (i, 0), pipeline_mode=pl.Buffered(1)),
            mod_spec,
            mod_spec,
            pl.BlockSpec((ms, D_MODEL), lambda i, j: (0, 0)),
            pl.BlockSpec((None, ms, D_MODEL), const3),
            pl.BlockSpec((None, ms, D_MODEL), const3),
            pl.BlockSpec((None, D_MODEL, tn), lambda i, j: (layer, 0, j)),
            pl.BlockSpec((None, D_MODEL, tn), lambda i, j: (layer, 0, j + nj)),
        ],
        out_specs=[
            pl.BlockSpec((tm, tn), lambda i, j: (i, j)),
            pl.BlockSpec((ms, tn), lambda i, j: (0, jnp.where(i == n_i - 1, j, 0))),
        ],
        out_shape=[jax.ShapeDtypeStruct((m, D_FF), BF16), jax.ShapeDtypeStruct((ms, D_FF), BF16)],
        scratch_shapes=[pltpu.VMEM((tm, D_MODEL), BF16), pltpu.VMEM((ms, D_MODEL), BF16)],
        compiler_params=_params("arbitrary", "arbitrary"),
        name=name,
    )(x, sc, sh, xs, scs, shs, w, w)


def _gla_kernel(*refs, chunk, t_valid, has_s0):
    if has_s0:
        q_ref, k_ref, v_ref, r_ref, a_ref, wa_ref, ba_ref, g_ref, s0_ref, o_ref, st_ref = refs
    else:
        q_ref, k_ref, v_ref, r_ref, a_ref, wa_ref, ba_ref, g_ref, o_ref, st_ref = refs
    sub = GLA_SUB
    n_sub = chunk // sub
    anchor = sub // 2 - 1

    @pl.when(pl.program_id(1) == 0)
    def _():
        if has_s0:
            st_ref[...] = s0_ref[...]
        else:
            st_ref[...] = jnp.zeros(st_ref.shape, F32)

    row = lax.broadcasted_iota(jnp.int32, (chunk, 1), 0)
    tri = _ind(lax.broadcasted_iota(jnp.int32, (chunk, chunk), 1)
               <= lax.broadcasted_iota(jnp.int32, (chunk, chunk), 0), BF16)
    ones = jnp.ones((chunk, LANES), BF16)
    causal = (lax.broadcasted_iota(jnp.int32, (sub, sub), 1)
              <= lax.broadcasted_iota(jnp.int32, (sub, sub), 0))
    z_all = _dot(a_ref[...], wa_ref[...]) + ba_ref[...]

    for h in range(GLA_HEADS):
        dk = slice(h * GLA_DK, (h + 1) * GLA_DK)
        dv = slice(h * GLA_DV, (h + 1) * GLA_DV)
        z = z_all[:, dk]
        lb = (jnp.minimum(z, 0.0) - jnp.log1p(jnp.exp(-jnp.abs(z)))) * (1.0 / GLA_TAU)
        kh = k_ref[:, dk]
        if t_valid < chunk:
            lb = jnp.where(row < t_valid, lb, 0.0)
            kh = jnp.where(row < t_valid, kh, 0.0)
        qh = q_ref[:, dk] * (GLA_DK ** -0.5)
        vh = v_ref[:, dv].astype(BF16)
        lb3 = _split3(lb)
        b = sum(jnp.dot(tri, t, preferred_element_type=F32) for t in lb3)
        b_last = b[chunk - 1:chunk, :]
        s_old = st_ref[0, h]
        o = _dot(qh * jnp.exp(b), s_old)
        o_rows = []
        for s in range(n_sub):
            rs = slice(s * sub, (s + 1) * sub)
            bs, qs, ks = b[rs], qh[rs], kh[rs]
            b_mid = b[s * sub + anchor:s * sub + anchor + 1, :]
            a_diag = _dot_nt(qs * jnp.exp(bs - b_mid), ks * jnp.exp(b_mid - bs))
            o_s = _dot(jnp.where(causal, a_diag, 0.0), vh[rs])
            if s > 0:
                prev = slice(0, s * sub)
                b_in = b[s * sub - 1:s * sub, :]
                a_off = _dot_nt(qs * jnp.exp(bs - b_in), kh[prev] * jnp.exp(b_in - b[prev]))
                o_s = o_s + _dot(a_off, vh[prev])
            o_rows.append(o_s)
        o = o + jnp.concatenate(o_rows, axis=0)
        dec = jnp.exp(sum(lax.dot_general(t, ones, (((0,), (0,)), ((), ())),
                                          preferred_element_type=F32) for t in lb3))
        st_ref[0, h] = s_old * jnp.tile(dec, (1, GLA_DV // LANES)) + _dot_tn(kh * jnp.exp(b_last - b), vh)
        o = o * lax.rsqrt(jnp.mean(o * o, axis=-1, keepdims=True) + LN_EPS)
        o_ref[:, dv] = (o * g_ref[:, dv] * _silu(r_ref[:, dv])).astype(BF16)


def _gla(main, tail, w_alpha, b_alpha, norm_g, s0, batch, seq, chunk, t_valid):
    n_chunks = seq // chunk
    qk_w = GLA_HEADS * GLA_DK
    v_w = GLA_HEADS * GLA_DV
    wa = jnp.pad(w_alpha, ((0, LANES - GLA_RANK), (0, 0)))
    row_map = lambda b, c: (b * n_chunks + c, 0)
    const = lambda b, c: (0, 0)
    in_specs = [
        pl.BlockSpec((chunk, qk_w), row_map),
        pl.BlockSpec((chunk, qk_w), lambda b, c: (b * n_chunks + c, 1)),
        pl.BlockSpec((chunk, v_w), lambda b, c: (b * n_chunks + c, 1)),
        pl.BlockSpec((chunk, v_w), lambda b, c: (b * n_chunks + c, 2)),
        pl.BlockSpec((chunk, LANES), row_map),
        pl.BlockSpec((LANES, qk_w), const),
        pl.BlockSpec((1, qk_w), const),
        pl.BlockSpec((1, v_w), const),
    ]
    args = [main, main, main, main, tail, wa, b_alpha.reshape(1, qk_w), norm_g.reshape(1, v_w)]
    state_spec = pl.BlockSpec((1, GLA_HEADS, GLA_DK, GLA_DV), lambda b, c: (b, 0, 0, 0))
    if s0 is not None:
        in_specs.append(state_spec)
        args.append(s0)
    return pl.pallas_call(
        functools.partial(_gla_kernel, chunk=chunk, t_valid=t_valid, has_s0=s0 is not None),
        grid=(batch, n_chunks),
        in_specs=in_specs,
        out_specs=[pl.BlockSpec((chunk, v_w), row_map), state_spec],
        out_shape=[jax.ShapeDtypeStruct((batch * seq, v_w), BF16),
                   jax.ShapeDtypeStruct((batch, GLA_HEADS, GLA_DK, GLA_DV), F32)],
        compiler_params=_params("parallel", "arbitrary"),
        name="gla_prompt" if s0 is None else "gla_sample",
    )(*args)


def _select_blocks(imp_t, blk, cur):
    n = imp_t.shape[0]
    forced = (blk == 0) | ((blk >= cur - 1) & (blk <= cur))
    score = jnp.where(blk > cur, -jnp.inf, jnp.where(forced, jnp.inf, imp_t))
    rank = jnp.zeros(score.shape, F32)
    for i in range(n):
        si = score[i:i + 1, :]
        rank = rank + jnp.where(blk > i, _ind(si >= score), _ind(si > score))
    return (rank < NSA_TOPK) & (score > -jnp.inf)


def _nsa_prompt_kernel(q_ref, kc_ref, vc_ref, ks_ref, vs_ref, kw_ref, vw_ref, gt_ref, bg_ref, o_ref,
                       kcm, vcm, ksb, vst, kwb, vwt, qs_ref, sel_ref, m_ref, acc_ref, os_ref,
                       *, seq, tq, tk):
    qi = pl.program_id(2)
    n_blk = seq // NSA_BLOCK
    n_kt = seq // tk
    blk_per_kt = tk // NSA_BLOCK

    @pl.when(qi == 0)
    def _():
        kcm[...] = jnp.sum(kc_ref[...].reshape(n_blk, NSA_BLOCK, NSA_HD), axis=1) * (1.0 / NSA_BLOCK)
        vcm[...] = jnp.sum(vc_ref[...].reshape(n_blk, NSA_BLOCK, NSA_HD), axis=1) * (1.0 / NSA_BLOCK)
        ksb[...] = ks_ref[...].astype(BF16)
        kwb[...] = kw_ref[...].astype(BF16)
        ones_rows = _ind(lax.broadcasted_iota(jnp.int32, (BF16_ROWS, tk), 0) == 0, BF16)
        for kt in range(n_kt):
            rows = slice(kt * tk, (kt + 1) * tk)
            vst[kt] = jnp.concatenate([vs_ref[rows, :].T.astype(BF16), ones_rows], axis=0)
            vwt[kt] = jnp.concatenate([vw_ref[rows, :].T.astype(BF16), ones_rows], axis=0)

    t0 = qi * tq
    q = q_ref[...] * (NSA_HD ** -0.5 * LOG2_E)
    qs_ref[...] = jnp.concatenate(
        [q[:, g * NSA_HD:(g + 1) * NSA_HD] for g in range(NSA_GROUP)], axis=0).astype(BF16)
    qs = qs_ref[...]
    t_row = t0 + lax.broadcasted_iota(jnp.int32, (1, tq), 1)
    t_row_g = jnp.concatenate([t_row] * NSA_GROUP, axis=1)

    kc_b = kcm[...].astype(BF16)
    blk_col = lax.broadcasted_iota(jnp.int32, (n_blk, 1), 0)
    p_t = _masked_softmax2(_dot_nt(kc_b, qs), (blk_col + 1) * NSA_BLOCK - 1 <= t_row_g, axis=0)
    o_c = _dot_tn(vcm[...], p_t)
    imp_t = sum(p_t[:, g * tq:(g + 1) * tq] for g in range(NSA_GROUP))
    sel_ref[...] = jnp.where(_select_blocks(imp_t, blk_col, t_row // NSA_BLOCK), 0.0, NEG_BIG)

    n_loc = lax.broadcasted_iota(jnp.int32, (tk, tq), 0)
    t_loc = lax.broadcasted_iota(jnp.int32, (tk, tq), 1)
    on_lanes = lambda a: jnp.concatenate([a] * NSA_GROUP, axis=1)
    causal = on_lanes(jnp.where(n_loc <= t_loc, 0.0, NEG_BIG))
    far = on_lanes(jnp.where(n_loc > t_loc, 0.0, NEG_BIG))

    def sel_bias(kt):
        return on_lanes(jnp.concatenate(
            [jnp.broadcast_to(sel_ref[pl.ds(kt * blk_per_kt + j, 1), :], (NSA_BLOCK, tq))
             for j in range(blk_per_kt)], axis=0))

    def reset():
        m_ref[...] = jnp.full(m_ref.shape, NEG_BIG, F32)
        acc_ref[...] = jnp.zeros(acc_ref.shape, F32)

    def step(k_ref, vt_ref, kt, bias):
        s = _dot_nt(k_ref[pl.ds(pl.multiple_of(kt * tk, tk), tk), :], qs_ref[...])
        if bias is not None:
            s = s + bias
        m_old = m_ref[...]
        m_new = jnp.maximum(m_old, jnp.max(s, axis=0, keepdims=True))
        acc_ref[...] = jnp.exp2(m_old - m_new) * acc_ref[...] + _dot(vt_ref[kt], jnp.exp2(s - m_new))
        m_ref[...] = m_new

    def result():
        acc = acc_ref[...]
        return acc[:NSA_HD] / jnp.maximum(acc[NSA_HD:NSA_HD + 1], TINY)

    reset()
    step(ksb, vst, qi, sel_bias(qi) + causal)

    def sel_body(i, carry):
        kt = qi - 1 - i
        step(ksb, vst, kt, sel_bias(kt))
        return carry

    lax.fori_loop(0, qi, sel_body, 0)
    os_ref[...] = result()

    reset()
    step(kwb, vwt, qi, causal)

    @pl.when(qi >= 1)
    def _():
        step(kwb, vwt, qi - 1, far)

    o_w = result()

    gates = jax.nn.sigmoid(gt_ref[...] + bg_ref[...])
    g_c, g_s, g_w = (jnp.concatenate([gates[3 * g + br:3 * g + br + 1, :] for g in range(NSA_GROUP)], axis=1)
                     for br in range(3))
    o_t = g_c * o_c + g_s * os_ref[...] + g_w * o_w
    for g in range(NSA_GROUP):
        o_ref[:, g * NSA_HD:(g + 1) * NSA_HD] = o_t[:, g * tq:(g + 1) * tq].T.astype(BF16)


def _nsa_prompt(main, gates, b_gate, batch, seq):
    tq = tk = NSA_WINDOW
    assert seq % tk == 0
    vt_rows = NSA_HD + BF16_ROWS
    nq = seq // tq
    q_lanes = NSA_GROUP * NSA_HD
    col0 = NSA_Q_W // NSA_HD

    def kv_spec(branch, part):
        off = col0 + (2 * branch + part) * NSA_KV
        return pl.BlockSpec((seq, NSA_HD), lambda b, kv, qi: (b, off + kv))

    g_w = 3 * NSA_GROUP
    g_rows = NSA_GROUP * tq
    return pl.pallas_call(
        functools.partial(_nsa_prompt_kernel, seq=seq, tq=tq, tk=tk),
        grid=(batch, NSA_KV, nq),
        in_specs=[pl.BlockSpec((tq, q_lanes), lambda b, kv, qi: (b * nq + qi, kv))]
        + [kv_spec(br, part) for br in range(3) for part in range(2)]
        + [pl.BlockSpec((None, g_w, tq), lambda b, kv, qi: (kv, 0, b * nq + qi)),
           pl.BlockSpec((None, g_w, 1), lambda b, kv, qi: (kv, 0, 0))],
        out_specs=pl.BlockSpec((tq, q_lanes), lambda b, kv, qi: (b * nq + qi, kv)),
        out_shape=jax.ShapeDtypeStruct((batch * seq, NSA_Q_W), BF16),
        scratch_shapes=[
            pltpu.VMEM((seq // NSA_BLOCK, NSA_HD), F32),
            pltpu.VMEM((seq // NSA_BLOCK, NSA_HD), F32),
            pltpu.VMEM((seq, NSA_HD), BF16),
            pltpu.VMEM((seq // tk, vt_rows, tk), BF16),
            pltpu.VMEM((seq, NSA_HD), BF16),
            pltpu.VMEM((seq // tk, vt_rows, tk), BF16),
            pltpu.VMEM((g_rows, NSA_HD), BF16),
            pltpu.VMEM((seq // NSA_BLOCK, tq), F32),
            pltpu.VMEM((1, g_rows), F32),
            pltpu.VMEM((vt_rows, g_rows), F32),
            pltpu.VMEM((NSA_HD, g_rows), F32),
        ],
        compiler_params=_params("parallel", "parallel", "arbitrary"),
        name="nsa_prompt",
    )(main, main, main, main, main, main, main, gates, b_gate)


def _page_specs(n):
    return [pl.BlockSpec((1, PAGE * KV_ROWS, NSA_HD), lambda b, p, pt, i=i: (pt[b, p * n + i], 0, 0))
            for i in range(n)]


def _page_means_kernel(pt_ref, *refs):
    x_refs, o_ref = refs[:-1], refs[-1]
    for i, x_ref in enumerate(x_refs):
        x = x_ref[0].reshape(PAGE // NSA_BLOCK, NSA_BLOCK, KV_ROWS, NSA_HD)
        o_ref[0, i] = jnp.sum(x, axis=1) * (1.0 / NSA_BLOCK)


def _page_means(cache, page_table):
    seqs, n_pages = page_table.shape
    per_page = PAGE // NSA_BLOCK
    per_step = 8
    return pl.pallas_call(
        _page_means_kernel,
        grid_spec=pltpu.PrefetchScalarGridSpec(
            num_scalar_prefetch=1,
            grid=(seqs, n_pages // per_step),
            in_specs=_page_specs(per_step),
            out_specs=pl.BlockSpec((1, per_step, per_page, KV_ROWS, NSA_HD), lambda b, p, pt: (b, p, 0, 0, 0)),
        ),
        out_shape=jax.ShapeDtypeStruct((seqs, n_pages, per_page, KV_ROWS, NSA_HD), F32),
        compiler_params=_params("parallel", "arbitrary"),
        name="nsa_page_means",
    )(page_table, *([cache] * per_step))


def _nsa_sample_kernel(pt_ref, q_ref, cm_ref, *refs, past, n_new, n_steps, per_step):
    pg_refs = refs[:per_step]
    (kvn_ref, wb_ref, kwn_ref, gt_ref, bg_ref, o_ref,
     qr_ref, sel_ref, m_ref, l_ref, acc_ref, oc_ref, ow_ref) = refs[per_step:]
    p = pl.program_id(1)
    k_w = NSA_KV * NSA_HD
    cols = NSA_KV * n_new * NSA_GROUP
    per_kv = n_new * NSA_GROUP
    n_cmp = past // NSA_BLOCK
    n_sel = -(-(past + n_new) // NSA_BLOCK)
    n_sel_pad = sel_ref.shape[0]
    lane = lax.broadcasted_iota(jnp.int32, (1, cols), 1)
    q_pos = past + (lane // NSA_GROUP) % n_new

    def scores(keys):
        return _dot_nt(keys, qr_ref[...])

    def own_head(x_t):
        out = jnp.zeros((NSA_HD, cols), F32)
        for kv in range(NSA_KV):
            mine = (lane // per_kv) == kv
            out = out + jnp.where(mine, x_t[kv * NSA_HD:(kv + 1) * NSA_HD, :], 0.0)
        return out

    def attend_once(keys, vals, mask):
        p_t = _masked_softmax2(scores(keys), mask, axis=0)
        return own_head(_dot_tn(vals, p_t))

    def online(keys, vals, bias):
        s = scores(keys) + bias
        m_old = m_ref[...]
        m_new = jnp.maximum(m_old, jnp.max(s, axis=0, keepdims=True))
        p_t = jnp.exp2(s - m_new)
        alpha = jnp.exp2(m_old - m_new)
        l_ref[...] = alpha * l_ref[...] + jnp.sum(p_t, axis=0, keepdims=True)
        acc_ref[...] = alpha * acc_ref[...] + own_head(_dot_tn(vals, p_t))
        m_ref[...] = m_new

    @pl.when(p == 0)
    def _():
        q = q_ref[0] * (NSA_HD ** -0.5 * LOG2_E)
        r_kv = lax.broadcasted_iota(jnp.int32, (cols, k_w), 0) // per_kv
        c_kv = lax.broadcasted_iota(jnp.int32, (cols, k_w), 1) // NSA_HD
        qr_ref[...] = jnp.where(r_kv == c_kv, jnp.tile(q, (1, NSA_KV)), 0.0).astype(BF16)
        cm = cm_ref[0]
        blk = lax.broadcasted_iota(jnp.int32, (n_cmp, 1), 0)
        p_t = _masked_softmax2(scores(cm[:, :k_w]), (blk + 1) * NSA_BLOCK - 1 <= q_pos, axis=0)
        oc_ref[...] = own_head(_dot_tn(cm[:, k_w:], p_t))
        same = _ind(lax.broadcasted_iota(jnp.int32, (cols, cols), 0) // NSA_GROUP
                    == lax.broadcasted_iota(jnp.int32, (cols, cols), 1) // NSA_GROUP, BF16)
        imp = sum(jnp.dot(t, same, preferred_element_type=F32) for t in _split3(p_t))
        imp = jnp.concatenate([imp, jnp.zeros((n_sel_pad - n_cmp, cols), F32)], axis=0)
        blk_s = lax.broadcasted_iota(jnp.int32, (n_sel_pad, 1), 0)
        cur = q_pos // NSA_BLOCK
        forced = (blk_s == 0) | ((blk_s >= cur - 1) & (blk_s <= cur))
        score = jnp.where((blk_s > cur) | (blk_s >= n_sel), -jnp.inf, jnp.where(forced, jnp.inf, imp))
        blk_f = blk_s.astype(F32)

        def pick(_, carry):
            left, bias = carry
            top = jnp.max(left, axis=0, keepdims=True)
            first = jnp.min(jnp.where(left == top, blk_f, float(n_sel_pad)), axis=0, keepdims=True)
            hit = (blk_f == first) & (top > -jnp.inf)
            return jnp.where(hit, -jnp.inf, left), jnp.where(hit, 0.0, bias)

        _, bias = lax.fori_loop(0, NSA_TOPK, pick, (score, jnp.full(score.shape, NEG_BIG, F32)))
        sel_ref[...] = bias
        wb = wb_ref[0]
        n_wb = wb.shape[0]
        kw_all = jnp.concatenate([wb[:, :k_w], kwn_ref[0][:, :k_w]], axis=0)
        vw_all = jnp.concatenate([wb[:, k_w:], kwn_ref[0][:, k_w:]], axis=0)
        n_w = kw_all.shape[0]
        w_row = lax.broadcasted_iota(jnp.int32, (n_w, 1), 0)
        w_pos = past - n_wb + w_row
        d = q_pos - w_pos
        ow_ref[...] = attend_once(kw_all, vw_all,
                                  (d >= 0) & (d < NSA_WINDOW) & (w_pos >= 0) & (w_row < n_wb + n_new))
        m_ref[...] = jnp.full(m_ref.shape, NEG_BIG, F32)
        l_ref[...] = jnp.zeros(l_ref.shape, F32)
        acc_ref[...] = jnp.zeros(acc_ref.shape, F32)

    def page_part(part):
        return jnp.concatenate(
            [jnp.concatenate([pg_ref[0, pl.ds(part * NSA_KV + kv, PAGE, stride=KV_ROWS), :]
                              for kv in range(NSA_KV)], axis=1) for pg_ref in pg_refs], axis=0)

    blk_per_step = per_step * PAGE // NSA_BLOCK
    online(page_part(0), page_part(1), jnp.concatenate(
        [jnp.broadcast_to(sel_ref[pl.ds(p * blk_per_step + j, 1), :], (NSA_BLOCK, cols))
         for j in range(blk_per_step)], axis=0))

    @pl.when(p == n_steps - 1)
    def _():
        kvn = kvn_ref[0]
        n_row = lax.broadcasted_iota(jnp.int32, (kvn.shape[0], 1), 0)
        bias_new = jnp.where((n_row < n_new) & (past + n_row <= q_pos), 0.0, NEG_BIG)
        online(kvn[:, :k_w], kvn[:, k_w:], bias_new + sel_ref[pl.ds(past // NSA_BLOCK, 1), :])
        o_s = acc_ref[...] / jnp.maximum(l_ref[...], TINY)
        gates = jax.nn.sigmoid(gt_ref[0] + bg_ref[...])
        o_t = gates[0:1, :] * oc_ref[...] + gates[1:2, :] * o_s + gates[2:3, :] * ow_ref[...]
        eye = _ind(lax.broadcasted_iota(jnp.int32, (NSA_HD, NSA_HD), 0)
                   == lax.broadcasted_iota(jnp.int32, (NSA_HD, NSA_HD), 1), BF16)
        o_ref[0] = sum(lax.dot_general(t, eye, (((0,), (0,)), ((), ())), preferred_element_type=F32)
                       for t in _split3(o_t))


def _nsa_sample(q_rows, cmp_means, cache_slc, page_table, kv_new, win_buf, kw_new, gates, b_gate,
                past, n_new):
    seqs, n_pages = page_table.shape
    cols = q_rows.shape[1]
    n_cmp = cmp_means.shape[1]
    n_sel_pad = -(-(-(-(past + n_new) // NSA_BLOCK)) // 8) * 8
    n_wb = win_buf.shape[1]
    pad_new = kv_new.shape[1]
    per_step = 8
    n_steps = n_pages // per_step
    seq_map = lambda b, p, pt: (b, 0, 0)
    return pl.pallas_call(
        functools.partial(_nsa_sample_kernel, past=past, n_new=n_new, n_steps=n_steps, per_step=per_step),
        grid_spec=pltpu.PrefetchScalarGridSpec(
            num_scalar_prefetch=1,
            grid=(seqs, n_steps),
            in_specs=[
                pl.BlockSpec((1, cols, NSA_HD), seq_map),
                pl.BlockSpec((1, n_cmp, NSA_KV_W), seq_map),
                *_page_specs(per_step),
                pl.BlockSpec((1, pad_new, NSA_KV_W), seq_map),
                pl.BlockSpec((1, n_wb, NSA_KV_W), seq_map),
                pl.BlockSpec((1, pad_new, NSA_KV_W), seq_map),
                pl.BlockSpec((1, 3, cols), seq_map),
                pl.BlockSpec((3, cols), lambda b, p, pt: (0, 0)),
            ],
            out_specs=pl.BlockSpec((1, cols, NSA_HD), seq_map),
            scratch_shapes=[
                pltpu.VMEM((cols, NSA_KV * NSA_HD), BF16),
                pltpu.VMEM((n_sel_pad, cols), F32),
                pltpu.VMEM((1, cols), F32),
                pltpu.VMEM((1, cols), F32),
                pltpu.VMEM((NSA_HD, cols), F32),
                pltpu.VMEM((NSA_HD, cols), F32),
                pltpu.VMEM((NSA_HD, cols), F32),
            ],
        ),
        out_shape=jax.ShapeDtypeStruct((seqs, cols, NSA_HD), F32),
        compiler_params=_params("parallel", "arbitrary"),
        name="nsa_sample",
    )(page_table, q_rows, cmp_means, *([cache_slc] * per_step), kv_new, win_buf, kw_new, gates, b_gate)


def kernel(x_prompt, x_sample, c_prompt, c_sample, state_gla, cache_cmp_kv, cache_slc_kv, cache_win_kv,
           page_table, ada_w, ada_b, gla_w_in, gla_w_alpha, gla_b_alpha, gla_norm_g, gla_w_out, nsa_w_in,
           nsa_b_gate, nsa_w_out, ln_mix_g, ln_mix_b, ffn_w_in, ffn_w_out, ln_ffn_g, ln_ffn_b):
    batch, seq, _ = x_prompt.shape
    seqs, n_new, _ = x_sample.shape
    n_pages = page_table.shape[1]
    past = n_pages * PAGE
    rows_s = seqs * n_new
    tm_p = seq
    w_out_mix = (gla_w_out.astype(BF16), nsa_w_out.astype(BF16))
    w_out_ffn = ffn_w_out.astype(BF16)

    pad_rows = 16 - (batch + seqs)
    c_all = jnp.concatenate([c_prompt, c_sample, jnp.zeros((pad_rows, D_MODEL), F32)], axis=0)
    mod = _adaln(c_all, ada_w, ada_b).reshape(DEPTH, 16, 6, D_MODEL)

    def mods(layer):
        mp = [mod[layer, :batch, i].reshape(batch, 1, D_MODEL) for i in range(6)]
        ms = [jnp.repeat(mod[layer, batch:batch + seqs, i], n_new, axis=0).reshape(1, rows_s, D_MODEL)
              for i in range(6)]
        return mp, ms

    def mixer_out(o, x, layer, m, tm, tag):
        return _out_ln(o, w_out_mix[layer % 2], layer // 2, x, m[2], ln_mix_g, ln_mix_b, layer, min(tm, 512), tag)

    def ffn(xp, xs, layer, mp, ms):
        act_p, act_s = _ffn_in(xp, mp[4], mp[3], xs, ms[4], ms[3], ffn_w_in, layer, tm_p, "ffn_in_%d" % layer)
        out = lambda act, x, m, tm, tag: _out_ln(act, w_out_ffn, layer, x, m[5], ln_ffn_g, ln_ffn_b, layer, tm,
                                                  "ffn_out_%s%d" % (tag, layer))
        return out(act_p, xp, mp, 256, "prompt"), out(act_s, xs, ms, rows_s, "sample")

    xp = x_prompt.reshape(batch * seq, D_MODEL)
    xs = x_sample.reshape(rows_s, D_MODEL)

    mp, ms = mods(0)
    main_p, tail_p, main_s, tail_s = _proj(xp, mp[1], mp[0], xs, ms[1], ms[0], gla_w_in.astype(BF16), 0,
                                           GLA_MAIN_W, tm_p, "gla_in")
    gated_p, gla_state_p = _gla(main_p, tail_p, gla_w_alpha[0], gla_b_alpha[0], gla_norm_g[0], None,
                                batch, seq, 128, 128)
    chunk_s = GLA_SUB

    def pad_steps(a):
        a = a.reshape(seqs, n_new, a.shape[-1])
        return jnp.pad(a, ((0, 0), (0, chunk_s - n_new), (0, 0))).reshape(seqs * chunk_s, a.shape[-1])

    gated_s, gla_state_s = _gla(pad_steps(main_s), pad_steps(tail_s), gla_w_alpha[0], gla_b_alpha[0],
                                gla_norm_g[0], state_gla[:, 0], seqs, chunk_s, chunk_s, n_new)
    gated_s = gated_s.reshape(seqs, chunk_s, -1)[:, :n_new].reshape(rows_s, -1)
    xp = mixer_out(gated_p, xp, 0, mp, tm_p, "gla_out_prompt")
    xs = mixer_out(gated_s, xs, 0, ms, rows_s, "gla_out_sample")
    xp, xs = ffn(xp, xs, 0, mp, ms)

    mp, ms = mods(1)
    main_p, tail_p, main_s, tail_s = _proj(xp, mp[1], mp[0], xs, ms[1], ms[0], nsa_w_in.astype(BF16), 0,
                                           NSA_MAIN_W, tm_p, "nsa_in")
    n_gate = 3 * NSA_HEADS
    g_w = 3 * NSA_GROUP
    gates_p = tail_p[:, :n_gate].reshape(batch * seq, NSA_KV, g_w).transpose(1, 2, 0)
    b_gate_p = nsa_b_gate[0].reshape(NSA_KV, g_w, 1)
    o_p = _nsa_prompt(main_p, gates_p, b_gate_p, batch, seq)

    kv_rows = lambda m, br: m[:, NSA_Q_W + br * NSA_KV_W:NSA_Q_W + (br + 1) * NSA_KV_W]
    kv_shape = (2, NSA_KV, NSA_HD)
    cmp_kv_p = kv_rows(main_p, 0).reshape(batch, 1, seq, *kv_shape)
    slc_kv_p = kv_rows(main_p, 1).reshape(batch, 1, seq, *kv_shape)
    keep_p = min(NSA_WINDOW, seq)
    win_kv_p = kv_rows(main_p, 2).reshape(batch, seq, NSA_KV_W)[:, seq - keep_p:].reshape(batch, 1, keep_p, *kv_shape)

    cmp_kv_s = kv_rows(main_s, 0).reshape(seqs, 1, n_new, *kv_shape)
    slc_kv_s = kv_rows(main_s, 1).reshape(seqs, 1, n_new, *kv_shape)
    kw_new = kv_rows(main_s, 2).reshape(seqs, n_new, NSA_KV_W)
    win_buf = cache_win_kv[:, 0].reshape(seqs, -1, NSA_KV_W)
    n_wb = win_buf.shape[1]
    all_w = jnp.concatenate([win_buf, kw_new], axis=1)
    keep_s = min(NSA_WINDOW, n_wb + n_new)
    win_kv_s = all_w[:, n_wb + n_new - keep_s:].reshape(seqs, 1, keep_s, *kv_shape)

    n_phys = cache_cmp_kv.shape[0]
    page_rows = lambda cache: cache.reshape(n_phys, PAGE * KV_ROWS, NSA_HD)
    cmp_means = _page_means(page_rows(cache_cmp_kv), page_table)
    cmp_means = cmp_means.reshape(seqs, past // NSA_BLOCK, NSA_KV_W)
    pad_new = 8
    pad8 = lambda a: jnp.pad(a, ((0, 0), (0, pad_new - n_new), (0, 0)))
    cols = NSA_KV * n_new * NSA_GROUP
    q_rows = (main_s[:, :NSA_Q_W].reshape(seqs, n_new, NSA_KV, NSA_GROUP, NSA_HD)
              .transpose(0, 2, 1, 3, 4).reshape(seqs, cols, NSA_HD))
    gates_s = (tail_s[:, :n_gate].reshape(seqs, n_new, NSA_KV, NSA_GROUP, 3)
               .transpose(0, 4, 2, 1, 3).reshape(seqs, 3, cols))
    b_gate_s = jnp.broadcast_to(nsa_b_gate[0].reshape(1, NSA_KV, NSA_GROUP, 3), (n_new, NSA_KV, NSA_GROUP, 3))
    b_gate_s = b_gate_s.transpose(3, 1, 0, 2).reshape(3, cols)
    o_s = _nsa_sample(q_rows, cmp_means, page_rows(cache_slc_kv), page_table,
                      pad8(slc_kv_s.reshape(seqs, n_new, NSA_KV_W)), win_buf, pad8(kw_new), gates_s, b_gate_s,
                      past, n_new)
    o_s = (o_s.reshape(seqs, NSA_KV, n_new, NSA_GROUP, NSA_HD).transpose(0, 2, 1, 3, 4)
           .reshape(rows_s, NSA_Q_W).astype(BF16))

    xp = mixer_out(o_p, xp, 1, mp, tm_p, "nsa_out_prompt")
    xs = mixer_out(o_s, xs, 1, ms, rows_s, "nsa_out_sample")
    xp, xs = ffn(xp, xs, 1, mp, ms)

    return (xp.reshape(batch, seq, D_MODEL), xs.reshape(seqs, n_new, D_MODEL),
            gla_state_p[:, None], gla_state_s[:, None], cmp_kv_p, cmp_kv_s, slc_kv_p, slc_kv_s,
            win_kv_p, win_kv_s)
```

```python
import functools

import jax
import jax.numpy as jnp
from jax import lax
from jax.experimental import pallas as pl
from jax.experimental.pallas import tpu as pltpu

F32 = jnp.float32
BF16 = jnp.bfloat16

D_MODEL = 2048
DEPTH = 2
DN_ALPHA = (2 * DEPTH) ** 0.25
LN_EPS = 1e-5
D_FF = 5632
GLA_HEADS = 4
GLA_DK = 256
GLA_DV = 512
GLA_RANK = 16
GLA_TAU = 16.0
GLA_SUB = 32
GLA_MAIN_W = 2 * GLA_HEADS * GLA_DK + 2 * GLA_HEADS * GLA_DV
NSA_HEADS = 16
NSA_KV = 4
NSA_HD = 128
NSA_GROUP = NSA_HEADS // NSA_KV
NSA_BLOCK = 64
NSA_TOPK = 16
NSA_WINDOW = 512
NSA_Q_W = NSA_HEADS * NSA_HD
NSA_KV_W = 2 * NSA_KV * NSA_HD
NSA_MAIN_W = NSA_Q_W + 3 * NSA_KV_W
PAGE = 128
KV_ROWS = 2 * NSA_KV

LANES = 128
BF16_ROWS = 16
LOG2_E = 1.4426950408889634
V7X_VMEM_BYTES = 64 * 1024 * 1024
VMEM_LIMIT = V7X_VMEM_BYTES * 7 // 8

NEG_BIG = -1e30
TINY = float(jnp.finfo(jnp.float32).tiny)


def _params(*sem):
    return pltpu.CompilerParams(dimension_semantics=sem, vmem_limit_bytes=VMEM_LIMIT)


def _dot(a, b):
    return jnp.dot(a.astype(BF16), b.astype(BF16), preferred_element_type=F32)


def _dot_nt(a, b):
    return lax.dot_general(a.astype(BF16), b.astype(BF16), (((1,), (1,)), ((), ())),
                           preferred_element_type=F32)


def _dot_tn(a, b):
    return lax.dot_general(a.astype(BF16), b.astype(BF16), (((0,), (0,)), ((), ())),
                           preferred_element_type=F32)


def _split3(x):
    hi = x.astype(BF16)
    r1 = x - hi.astype(F32)
    mid = r1.astype(BF16)
    lo = (r1 - mid.astype(F32)).astype(BF16)
    return hi, mid, lo


def _silu(x):
    return x * jax.nn.sigmoid(x)


def _ind(cond, dtype=F32):
    return jnp.where(cond, 1.0, 0.0).astype(dtype)


def _masked_softmax2(s2, mask, axis):
    s2 = jnp.where(mask, s2, -jnp.inf)
    m = jnp.max(s2, axis=axis, keepdims=True)
    m = jnp.where(jnp.isfinite(m), m, 0.0)
    e = jnp.where(mask, jnp.exp2(s2 - m), 0.0)
    return e / jnp.maximum(jnp.sum(e, axis=axis, keepdims=True), TINY)


def _adaln_kernel(c_ref, w_ref, b_ref, o_ref):
    o_ref[...] = _dot(_silu(c_ref[...]), w_ref[...]) + b_ref[...]


def _adaln(c_all, ada_w, ada_b):
    rows = c_all.shape[0]
    n = ada_w.shape[2]
    tn = 1024
    return pl.pallas_call(
        _adaln_kernel,
        grid=(DEPTH, n // tn),
        in_specs=[
            pl.BlockSpec((rows, D_MODEL), lambda l, j: (0, 0)),
            pl.BlockSpec((None, D_MODEL, tn), lambda l, j: (l, 0, j)),
            pl.BlockSpec((None, 1, tn), lambda l, j: (l, 0, j)),
        ],
        out_specs=pl.BlockSpec((None, rows, tn), lambda l, j: (l, 0, j)),
        out_shape=jax.ShapeDtypeStruct((DEPTH, rows, n), F32),
        compiler_params=_params("parallel", "parallel"),
        name="adaln",
    )(c_all, ada_w, ada_b.reshape(DEPTH, 1, n))


def _modulate(x_ref, sc_ref, sh_ref):
    return (x_ref[...] * (1.0 + sc_ref[...]) + sh_ref[...]).astype(BF16)


def _proj_kernel(x_ref, sc_ref, sh_ref, xs_ref, scs_ref, shs_ref, w_ref, wt_ref,
                 o_ref, ot_ref, os_ref, ots_ref, h_ref, hs_ref, *, n_i):
    i, j = pl.program_id(0), pl.program_id(1)

    @pl.when(j == 0)
    def _():
        h_ref[...] = _modulate(x_ref, sc_ref, sh_ref)
        ot_ref[...] = _dot(h_ref[...], wt_ref[...])

    o_ref[...] = _dot(h_ref[...], w_ref[...])

    @pl.when(i == n_i - 1)
    def _():
        @pl.when(j == 0)
        def _():
            hs_ref[...] = _modulate(xs_ref, scs_ref, shs_ref)
            ots_ref[...] = _dot(hs_ref[...], wt_ref[...])

        os_ref[...] = _dot(hs_ref[...], w_ref[...])


def _proj(x, sc, sh, xs, scs, shs, w, layer, n_main, tm, name):
    m, ms = x.shape[0], xs.shape[0]
    n_tail = w.shape[2] - n_main
    w_tail = jnp.pad(w[layer, :, n_main:], ((0, 0), (0, LANES - n_tail)))
    tn = 512
    n_i = m // tm
    tiles_per_group = n_i // sc.shape[0]
    mod_spec = pl.BlockSpec((None, sc.shape[1], D_MODEL), lambda i, j: (i // tiles_per_group, 0, 0))
    const2 = lambda i, j: (0, 0)
    const3 = lambda i, j: (0, 0, 0)
    last_tile_col = lambda i, j: (0, jnp.where(i == n_i - 1, j, 0))
    return pl.pallas_call(
        functools.partial(_proj_kernel, n_i=n_i),
        grid=(n_i, n_main // tn),
        in_specs=[
            pl.BlockSpec((tm, D_MODEL), lambda i, j: (i, 0), pipeline_mode=pl.Buffered(1)),
            mod_spec,
            mod_spec,
            pl.BlockSpec((ms, D_MODEL), const2),
            pl.BlockSpec((None, ms, D_MODEL), const3),
            pl.BlockSpec((None, ms, D_MODEL), const3),
            pl.BlockSpec((None, D_MODEL, tn), lambda i, j: (layer, 0, j)),
            pl.BlockSpec((D_MODEL, LANES), const2),
        ],
        out_specs=[
            pl.BlockSpec((tm, tn), lambda i, j: (i, j)),
            pl.BlockSpec((tm, LANES), lambda i, j: (i, 0)),
            pl.BlockSpec((ms, tn), last_tile_col),
            pl.BlockSpec((ms, LANES), const2),
        ],
        out_shape=[jax.ShapeDtypeStruct((m, n_main), F32), jax.ShapeDtypeStruct((m, LANES), F32),
                   jax.ShapeDtypeStruct((ms, n_main), F32), jax.ShapeDtypeStruct((ms, LANES), F32)],
        scratch_shapes=[pltpu.VMEM((tm, D_MODEL), BF16), pltpu.VMEM((ms, D_MODEL), BF16)],
        compiler_params=_params("arbitrary", "arbitrary"),
        name=name,
    )(x, sc, sh, xs, scs, shs, w, w_tail)


def _out_ln_kernel(a_ref, w_ref, x_ref, gt_ref, g_ref, b_ref, o_ref):
    y = DN_ALPHA * x_ref[...] + gt_ref[...] * jnp.dot(a_ref[...], w_ref[...], preferred_element_type=F32)
    mu = jnp.mean(y, axis=-1, keepdims=True)
    yc = y - mu
    var = jnp.mean(yc * yc, axis=-1, keepdims=True)
    o_ref[...] = yc * lax.rsqrt(var + LN_EPS) * g_ref[...] + b_ref[...]


def _out_ln(a, w, w_layer, x, gate, ln_g, ln_b, ln_layer, tm, name):
    m, kdim = a.shape
    groups, mod_rows = gate.shape[0], gate.shape[1]
    tiles_per_group = m // tm // groups
    vec_spec = pl.BlockSpec((None, 1, D_MODEL), lambda i: (ln_layer, 0, 0))
    return pl.pallas_call(
        _out_ln_kernel,
        grid=(m // tm,),
        in_specs=[
            pl.BlockSpec((tm, kdim), lambda i: (i, 0)),
            pl.BlockSpec((None, kdim, D_MODEL), lambda i: (w_layer, 0, 0), pipeline_mode=pl.Buffered(1)),
            pl.BlockSpec((tm, D_MODEL), lambda i: (i, 0)),
            pl.BlockSpec((None, mod_rows, D_MODEL), lambda i: (i // tiles_per_group, 0, 0)),
            vec_spec,
            vec_spec,
        ],
        out_specs=pl.BlockSpec((tm, D_MODEL), lambda i: (i, 0)),
        out_shape=jax.ShapeDtypeStruct((m, D_MODEL), F32),
        compiler_params=_params("parallel"),
        name=name,
    )(a, w, x, gate, ln_g.reshape(-1, 1, D_MODEL), ln_b.reshape(-1, 1, D_MODEL))


def _ffn_in_kernel(x_ref, sc_ref, sh_ref, xs_ref, scs_ref, shs_ref, wa_ref, wu_ref,
                   o_ref, os_ref, h_ref, hs_ref, *, n_i):
    i, j = pl.program_id(0), pl.program_id(1)

    @pl.when(j == 0)
    def _():
        h_ref[...] = _modulate(x_ref, sc_ref, sh_ref)

    wa = wa_ref[...].astype(BF16)
    wu = wu_ref[...].astype(BF16)

    def swiglu_in(h):
        return (_silu(_dot(h, wa)) * _dot(h, wu)).astype(BF16)

    o_ref[...] = swiglu_in(h_ref[...])

    @pl.when(i == n_i - 1)
    def _():
        @pl.when(j == 0)
        def _():
            hs_ref[...] = _modulate(xs_ref, scs_ref, shs_ref)

        os_ref[...] = swiglu_in(hs_ref[...])


def _ffn_in(x, sc, sh, xs, scs, shs, w, layer, tm, name):
    m, ms = x.shape[0], xs.shape[0]
    tn = 256
    nj = D_FF // tn
    n_i = m // tm
    tiles_per_group = n_i // sc.shape[0]
    mod_spec = pl.BlockSpec((None, sc.shape[1], D_MODEL), lambda i, j: (i // tiles_per_group, 0, 0))
    const3 = lambda i, j: (0, 0, 0)
    return pl.pallas_call(
        functools.partial(_ffn_in_kernel, n_i=n_i),
        grid=(n_i, nj),
        in_specs=[
            pl.BlockSpec((tm, D_MODEL), lambda i, j: (i, 0), pipeline_mode=pl.Buffered(1)),
            mod_spec,
            mod_spec,
            pl.BlockSpec((ms, D_MODEL), lambda i, j: (0, 0)),
            pl.BlockSpec((None, ms, D_MODEL), const3),
            pl.BlockSpec((None, ms, D_MODEL), const3),
            pl.BlockSpec((None, D_MODEL, tn), lambda i, j: (layer, 0, j)),
            pl.BlockSpec((None, D_MODEL, tn), lambda i, j: (layer, 0, j + nj)),
        ],
        out_specs=[
            pl.BlockSpec((tm, tn), lambda i, j: (i, j)),
            pl.BlockSpec((ms, tn), lambda i, j: (0, jnp.where(i == n_i - 1, j, 0))),
        ],
        out_shape=[jax.ShapeDtypeStruct((m, D_FF), BF16), jax.ShapeDtypeStruct((ms, D_FF), BF16)],
        scratch_shapes=[pltpu.VMEM((tm, D_MODEL), BF16), pltpu.VMEM((ms, D_MODEL), BF16)],
        compiler_params=_params("arbitrary", "arbitrary"),
        name=name,
    )(x, sc, sh, xs, scs, shs, w, w)


def _gla_kernel(*refs, chunk, t_valid, has_s0):
    if has_s0:
        q_ref, k_ref, v_ref, r_ref, a_ref, wa_ref, ba_ref, g_ref, s0_ref, o_ref, st_ref = refs
    else:
        q_ref, k_ref, v_ref, r_ref, a_ref, wa_ref, ba_ref, g_ref, o_ref, st_ref = refs
    sub = GLA_SUB
    n_sub = chunk // sub
    anchor = sub // 2 - 1

    @pl.when(pl.program_id(1) == 0)
    def _():
        if has_s0:
            st_ref[...] = s0_ref[...]
        else:
            st_ref[...] = jnp.zeros(st_ref.shape, F32)

    row = lax.broadcasted_iota(jnp.int32, (chunk, 1), 0)
    tri = _ind(lax.broadcasted_iota(jnp.int32, (chunk, chunk), 1)
               <= lax.broadcasted_iota(jnp.int32, (chunk, chunk), 0), BF16)
    ones = jnp.ones((chunk, LANES), BF16)
    causal = (lax.broadcasted_iota(jnp.int32, (sub, sub), 1)
              <= lax.broadcasted_iota(jnp.int32, (sub, sub), 0))
    heads = range(GLA_HEADS)
    dk = [slice(h * GLA_DK, (h + 1) * GLA_DK) for h in heads]
    dv = [slice(h * GLA_DV, (h + 1) * GLA_DV) for h in heads]
    subs = [slice(s * sub, (s + 1) * sub) for s in range(n_sub)]

    z = _dot(a_ref[...], wa_ref[...]) + ba_ref[...]
    lb = (jnp.minimum(z, 0.0) - jnp.log1p(jnp.exp(-jnp.abs(z)))) * (1.0 / GLA_TAU)
    k = k_ref[...]
    if t_valid < chunk:
        lb = jnp.where(row < t_valid, lb, 0.0)
        k = jnp.where(row < t_valid, k, 0.0)
    q = q_ref[...] * (GLA_DK ** -0.5)
    vb = v_ref[...].astype(BF16)
    lb3 = _split3(lb)
    b = sum(jnp.dot(tri, t, preferred_element_type=F32) for t in lb3)
    dec = jnp.exp(sum(lax.dot_general(t, ones, (((0,), (0,)), ((), ())), preferred_element_type=F32)
                      for t in lb3))
    b_last = b[chunk - 1:chunk, :]
    q_in = (q * jnp.exp(b)).astype(BF16)
    k_out = (k * jnp.exp(b_last - b)).astype(BF16)
    q_diag, k_diag, q_off, k_off = [], [], [], []
    for s, rs in enumerate(subs):
        b_mid = b[s * sub + anchor:s * sub + anchor + 1, :]
        q_diag.append((q[rs] * jnp.exp(b[rs] - b_mid)).astype(BF16))
        k_diag.append((k[rs] * jnp.exp(b_mid - b[rs])).astype(BF16))
        if s > 0:
            prev = slice(0, s * sub)
            b_in = b[s * sub - 1:s * sub, :]
            q_off.append((q[rs] * jnp.exp(b[rs] - b_in)).astype(BF16))
            k_off.append((k[prev] * jnp.exp(b_in - b[prev])).astype(BF16))

    s_old = [st_ref[0, h] for h in heads]
    o_inter = [_dot(q_in[:, dk[h]], s_old[h]) for h in heads]
    a_diag = [[_dot_nt(q_diag[s][:, dk[h]], k_diag[s][:, dk[h]]) for h in heads] for s in range(n_sub)]
    a_off = [[_dot_nt(q_off[s][:, dk[h]], k_off[s][:, dk[h]]) for h in heads] for s in range(n_sub - 1)]
    a_diag = [[jnp.where(causal, a, 0.0).astype(BF16) for a in row_] for row_ in a_diag]
    a_off = [[a.astype(BF16) for a in row_] for row_ in a_off]
    o_intra = []
    for s, rs in enumerate(subs):
        o_s = [_dot(a_diag[s][h], vb[rs, dv[h]]) for h in heads]
        if s > 0:
            o_s = [o_s[h] + _dot(a_off[s - 1][h], vb[0:s * sub, dv[h]]) for h in heads]
        o_intra.append(o_s)
    kv = [_dot_tn(k_out[:, dk[h]], vb[:, dv[h]]) for h in heads]
    for h in heads:
        st_ref[0, h] = s_old[h] * jnp.tile(dec[dk[h], :], (1, GLA_DV // LANES)) + kv[h]
    for h in heads:
        o = o_inter[h] + jnp.concatenate([o_intra[s][h] for s in range(n_sub)], axis=0)
        o = o * lax.rsqrt(jnp.mean(o * o, axis=-1, keepdims=True) + LN_EPS)
        o_ref[:, dv[h]] = (o * g_ref[:, dv[h]] * _silu(r_ref[:, dv[h]])).astype(BF16)


def _gla(main, tail, w_alpha, b_alpha, norm_g, s0, batch, seq, chunk, t_valid):
    n_chunks = seq // chunk
    qk_w = GLA_HEADS * GLA_DK
    v_w = GLA_HEADS * GLA_DV
    wa = jnp.pad(w_alpha, ((0, LANES - GLA_RANK), (0, 0)))
    row_map = lambda b, c: (b * n_chunks + c, 0)
    const = lambda b, c: (0, 0)
    in_specs = [
        pl.BlockSpec((chunk, qk_w), row_map),
        pl.BlockSpec((chunk, qk_w), lambda b, c: (b * n_chunks + c, 1)),
        pl.BlockSpec((chunk, v_w), lambda b, c: (b * n_chunks + c, 1)),
        pl.BlockSpec((chunk, v_w), lambda b, c: (b * n_chunks + c, 2)),
        pl.BlockSpec((chunk, LANES), row_map),
        pl.BlockSpec((LANES, qk_w), const),
        pl.BlockSpec((1, qk_w), const),
        pl.BlockSpec((1, v_w), const),
    ]
    args = [main, main, main, main, tail, wa, b_alpha.reshape(1, qk_w), norm_g.reshape(1, v_w)]
    state_spec = pl.BlockSpec((1, GLA_HEADS, GLA_DK, GLA_DV), lambda b, c: (b, 0, 0, 0))
    if s0 is not None:
        in_specs.append(state_spec)
        args.append(s0)
    return pl.pallas_call(
        functools.partial(_gla_kernel, chunk=chunk, t_valid=t_valid, has_s0=s0 is not None),
        grid=(batch, n_chunks),
        in_specs=in_specs,
        out_specs=[pl.BlockSpec((chunk, v_w), row_map), state_spec],
        out_shape=[jax.ShapeDtypeStruct((batch * seq, v_w), BF16),
                   jax.ShapeDtypeStruct((batch, GLA_HEADS, GLA_DK, GLA_DV), F32)],
        compiler_params=_params("parallel", "arbitrary"),
        name="gla_prompt" if s0 is None else "gla_sample",
    )(*args)


def _select_blocks(imp_t, blk, cur):
    n = imp_t.shape[0]
    forced = (blk == 0) | ((blk >= cur - 1) & (blk <= cur))
    score = jnp.where(blk > cur, -jnp.inf, jnp.where(forced, jnp.inf, imp_t))
    rank = jnp.zeros(score.shape, F32)
    for i in range(n):
        si = score[i:i + 1, :]
        rank = rank + jnp.where(blk > i, _ind(si >= score), _ind(si > score))
    return (rank < NSA_TOPK) & (score > -jnp.inf)


def _nsa_prompt_kernel(q_ref, kc_ref, vc_ref, ks_ref, vs_ref, kw_ref, vw_ref, gt_ref, bg_ref, o_ref,
                       kcm, vcm, ksb, vst, kwb, vwt, qs_ref, sel_ref, m_ref, acc_ref, os_ref,
                       *, seq, tq, tk):
    qi = pl.program_id(2)
    n_blk = seq // NSA_BLOCK
    n_kt = seq // tk
    blk_per_kt = tk // NSA_BLOCK

    @pl.when(qi == 0)
    def _():
        kcm[...] = jnp.sum(kc_ref[...].reshape(n_blk, NSA_BLOCK, NSA_HD), axis=1) * (1.0 / NSA_BLOCK)
        vcm[...] = jnp.sum(vc_ref[...].reshape(n_blk, NSA_BLOCK, NSA_HD), axis=1) * (1.0 / NSA_BLOCK)
        ksb[...] = ks_ref[...].astype(BF16)
        kwb[...] = kw_ref[...].astype(BF16)
        ones_rows = _ind(lax.broadcasted_iota(jnp.int32, (BF16_ROWS, tk), 0) == 0, BF16)
        for kt in range(n_kt):
            rows = slice(kt * tk, (kt + 1) * tk)
            vst[kt] = jnp.concatenate([vs_ref[rows, :].T.astype(BF16), ones_rows], axis=0)
            vwt[kt] = jnp.concatenate([vw_ref[rows, :].T.astype(BF16), ones_rows], axis=0)

    t0 = qi * tq
    q = q_ref[...] * (NSA_HD ** -0.5 * LOG2_E)
    qs_ref[...] = jnp.concatenate(
        [q[:, g * NSA_HD:(g + 1) * NSA_HD] for g in range(NSA_GROUP)], axis=0).astype(BF16)
    qs = qs_ref[...]
    t_row = t0 + lax.broadcasted_iota(jnp.int32, (1, tq), 1)
    t_row_g = jnp.concatenate([t_row] * NSA_GROUP, axis=1)

    kc_b = kcm[...].astype(BF16)
    blk_col = lax.broadcasted_iota(jnp.int32, (n_blk, 1), 0)
    p_t = _masked_softmax2(_dot_nt(kc_b, qs), (blk_col + 1) * NSA_BLOCK - 1 <= t_row_g, axis=0)
    o_c = _dot_tn(vcm[...], p_t)
    imp_t = sum(p_t[:, g * tq:(g + 1) * tq] for g in range(NSA_GROUP))
    sel_ref[...] = jnp.where(_select_blocks(imp_t, blk_col, t_row // NSA_BLOCK), 0.0, NEG_BIG)

    n_loc = lax.broadcasted_iota(jnp.int32, (tk, tq), 0)
    t_loc = lax.broadcasted_iota(jnp.int32, (tk, tq), 1)
    on_lanes = lambda a: jnp.concatenate([a] * NSA_GROUP, axis=1)
    causal = on_lanes(jnp.where(n_loc <= t_loc, 0.0, NEG_BIG))
    far = on_lanes(jnp.where(n_loc > t_loc, 0.0, NEG_BIG))

    def sel_bias(kt):
        return on_lanes(jnp.concatenate(
            [jnp.broadcast_to(sel_ref[pl.ds(kt * blk_per_kt + j, 1), :], (NSA_BLOCK, tq))
             for j in range(blk_per_kt)], axis=0))

    def reset():
        m_ref[...] = jnp.full(m_ref.shape, NEG_BIG, F32)
        acc_ref[...] = jnp.zeros(acc_ref.shape, F32)

    def step(k_ref, vt_ref, kt, bias):
        s = _dot_nt(k_ref[pl.ds(pl.multiple_of(kt * tk, tk), tk), :], qs_ref[...])
        if bias is not None:
            s = s + bias
        m_old = m_ref[...]
        m_new = jnp.maximum(m_old, jnp.max(s, axis=0, keepdims=True))
        acc_ref[...] = jnp.exp2(m_old - m_new) * acc_ref[...] + _dot(vt_ref[kt], jnp.exp2(s - m_new))
        m_ref[...] = m_new

    def result():
        acc = acc_ref[...]
        return acc[:NSA_HD] / jnp.maximum(acc[NSA_HD:NSA_HD + 1], TINY)

    reset()
    step(ksb, vst, qi, sel_bias(qi) + causal)

    def sel_body(i, carry):
        kt = qi - 1 - i
        step(ksb, vst, kt, sel_bias(kt))
        return carry

    lax.fori_loop(0, qi, sel_body, 0)
    os_ref[...] = result()

    reset()
    step(kwb, vwt, qi, causal)

    @pl.when(qi >= 1)
    def _():
        step(kwb, vwt, qi - 1, far)

    o_w = result()

    gates = jax.nn.sigmoid(gt_ref[...] + bg_ref[...])
    g_c, g_s, g_w = (jnp.concatenate([gates[3 * g + br:3 * g + br + 1, :] for g in range(NSA_GROUP)], axis=1)
                     for br in range(3))
    o_t = g_c * o_c + g_s * os_ref[...] + g_w * o_w
    for g in range(NSA_GROUP):
        o_ref[:, g * NSA_HD:(g + 1) * NSA_HD] = o_t[:, g * tq:(g + 1) * tq].T.astype(BF16)


def _nsa_prompt(main, gates, b_gate, batch, seq):
    tq = tk = NSA_WINDOW
    assert seq % tk == 0
    vt_rows = NSA_HD + BF16_ROWS
    nq = seq // tq
    q_lanes = NSA_GROUP * NSA_HD
    col0 = NSA_Q_W // NSA_HD

    def kv_spec(branch, part):
        off = col0 + (2 * branch + part) * NSA_KV
        return pl.BlockSpec((seq, NSA_HD), lambda b, kv, qi: (b, off + kv))

    g_w = 3 * NSA_GROUP
    g_rows = NSA_GROUP * tq
    return pl.pallas_call(
        functools.partial(_nsa_prompt_kernel, seq=seq, tq=tq, tk=tk),
        grid=(batch, NSA_KV, nq),
        in_specs=[pl.BlockSpec((tq, q_lanes), lambda b, kv, qi: (b * nq + qi, kv))]
        + [kv_spec(br, part) for br in range(3) for part in range(2)]
        + [pl.BlockSpec((None, g_w, tq), lambda b, kv, qi: (kv, 0, b * nq + qi)),
           pl.BlockSpec((None, g_w, 1), lambda b, kv, qi: (kv, 0, 0))],
        out_specs=pl.BlockSpec((tq, q_lanes), lambda b, kv, qi: (b * nq + qi, kv)),
        out_shape=jax.ShapeDtypeStruct((batch * seq, NSA_Q_W), BF16),
        scratch_shapes=[
            pltpu.VMEM((seq // NSA_BLOCK, NSA_HD), F32),
            pltpu.VMEM((seq // NSA_BLOCK, NSA_HD), F32),
            pltpu.VMEM((seq, NSA_HD), BF16),
            pltpu.VMEM((seq // tk, vt_rows, tk), BF16),
            pltpu.VMEM((seq, NSA_HD), BF16),
            pltpu.VMEM((seq // tk, vt_rows, tk), BF16),
            pltpu.VMEM((g_rows, NSA_HD), BF16),
            pltpu.VMEM((seq // NSA_BLOCK, tq), F32),
            pltpu.VMEM((1, g_rows), F32),
            pltpu.VMEM((vt_rows, g_rows), F32),
            pltpu.VMEM((NSA_HD, g_rows), F32),
        ],
        compiler_params=_params("parallel", "parallel", "arbitrary"),
        name="nsa_prompt",
    )(main, main, main, main, main, main, main, gates, b_gate)


def _kv_rows_out_kernel(c_ref, s_ref, w_ref, oc_ref, os_ref, ow_ref, *, rows, n_c):
    def spread(src, dst):
        for r in range(KV_ROWS):
            dst[pl.ds(r, rows, stride=KV_ROWS), :] = src[:, r * NSA_HD:(r + 1) * NSA_HD]

    spread(c_ref, oc_ref)
    spread(s_ref, os_ref)

    @pl.when(pl.program_id(1) == n_c - 1)
    def _():
        spread(w_ref, ow_ref)


def _kv_rows_out(main, batch, seq, keep):
    rows = keep
    n_c = seq // rows
    col0 = NSA_Q_W // NSA_KV_W
    tile = lambda br: pl.BlockSpec((rows, NSA_KV_W), lambda b, c: (b * n_c + c, col0 + br))
    out_tile = pl.BlockSpec((rows * KV_ROWS, NSA_HD), lambda b, c: (b * n_c + c, 0))
    return pl.pallas_call(
        functools.partial(_kv_rows_out_kernel, rows=rows, n_c=n_c),
        grid=(batch, n_c),
        in_specs=[tile(0), tile(1),
                  pl.BlockSpec((rows, NSA_KV_W), lambda b, c: (b * n_c + n_c - 1, col0 + 2))],
        out_specs=[out_tile, out_tile, pl.BlockSpec((rows * KV_ROWS, NSA_HD), lambda b, c: (b, 0))],
        out_shape=[jax.ShapeDtypeStruct((batch * seq * KV_ROWS, NSA_HD), F32)] * 2
        + [jax.ShapeDtypeStruct((batch * keep * KV_ROWS, NSA_HD), F32)],
        compiler_params=_params("arbitrary", "arbitrary"),
        name="nsa_kv_rows_out",
    )(main, main, main)


def _page_specs(n):
    return [pl.BlockSpec((1, PAGE * KV_ROWS, NSA_HD), lambda b, p, pt, i=i: (pt[b, p * n + i], 0, 0))
            for i in range(n)]


def _page_means_kernel(pt_ref, *refs):
    x_refs, o_ref = refs[:-1], refs[-1]
    for i, x_ref in enumerate(x_refs):
        x = x_ref[0].reshape(PAGE // NSA_BLOCK, NSA_BLOCK, KV_ROWS, NSA_HD)
        o_ref[0, i] = jnp.sum(x, axis=1) * (1.0 / NSA_BLOCK)


def _page_means(cache, page_table):
    seqs, n_pages = page_table.shape
    per_page = PAGE // NSA_BLOCK
    per_step = 8
    return pl.pallas_call(
        _page_means_kernel,
        grid_spec=pltpu.PrefetchScalarGridSpec(
            num_scalar_prefetch=1,
            grid=(seqs, n_pages // per_step),
            in_specs=_page_specs(per_step),
            out_specs=pl.BlockSpec((1, per_step, per_page, KV_ROWS, NSA_HD), lambda b, p, pt: (b, p, 0, 0, 0)),
        ),
        out_shape=jax.ShapeDtypeStruct((seqs, n_pages, per_page, KV_ROWS, NSA_HD), F32),
        compiler_params=_params("parallel", "arbitrary"),
        name="nsa_page_means",
    )(page_table, *([cache] * per_step))


def _nsa_sample_kernel(pt_ref, q_ref, cm_ref, *refs, past, n_new, n_steps, per_step):
    pg_refs = refs[:per_step]
    (kvn_ref, wb_ref, kwn_ref, gt_ref, bg_ref, o_ref,
     qr_ref, sel_ref, m_ref, l_ref, acc_ref, oc_ref, ow_ref) = refs[per_step:]
    p = pl.program_id(1)
    k_w = NSA_KV * NSA_HD
    cols = NSA_KV * n_new * NSA_GROUP
    per_kv = n_new * NSA_GROUP
    n_cmp = past // NSA_BLOCK
    n_sel = -(-(past + n_new) // NSA_BLOCK)
    n_sel_pad = sel_ref.shape[0]
    lane = lax.broadcasted_iota(jnp.int32, (1, cols), 1)
    q_pos = past + (lane // NSA_GROUP) % n_new

    def heads_on_lanes(ref, n, part):
        return jnp.concatenate(
            [ref[0, pl.ds(part * NSA_KV + kv, n, stride=KV_ROWS), :] for kv in range(NSA_KV)], axis=1)

    def scores(keys):
        return _dot_nt(keys, qr_ref[...])

    def own_head(x_t):
        out = jnp.zeros((NSA_HD, cols), F32)
        for kv in range(NSA_KV):
            mine = (lane // per_kv) == kv
            out = out + jnp.where(mine, x_t[kv * NSA_HD:(kv + 1) * NSA_HD, :], 0.0)
        return out

    def attend_once(keys, vals, mask):
        p_t = _masked_softmax2(scores(keys), mask, axis=0)
        return own_head(_dot_tn(vals, p_t))

    def online(keys, vals, bias):
        s = scores(keys) + bias
        m_old = m_ref[...]
        m_new = jnp.maximum(m_old, jnp.max(s, axis=0, keepdims=True))
        p_t = jnp.exp2(s - m_new)
        alpha = jnp.exp2(m_old - m_new)
        l_ref[...] = alpha * l_ref[...] + jnp.sum(p_t, axis=0, keepdims=True)
        acc_ref[...] = alpha * acc_ref[...] + own_head(_dot_tn(vals, p_t))
        m_ref[...] = m_new

    @pl.when(p == 0)
    def _():
        q = q_ref[0] * (NSA_HD ** -0.5 * LOG2_E)
        r_kv = lax.broadcasted_iota(jnp.int32, (cols, k_w), 0) // per_kv
        c_kv = lax.broadcasted_iota(jnp.int32, (cols, k_w), 1) // NSA_HD
        qr_ref[...] = jnp.where(r_kv == c_kv, jnp.tile(q, (1, NSA_KV)), 0.0).astype(BF16)
        cm = cm_ref[0]
        blk = lax.broadcasted_iota(jnp.int32, (n_cmp, 1), 0)
        p_t = _masked_softmax2(scores(cm[:, :k_w]), (blk + 1) * NSA_BLOCK - 1 <= q_pos, axis=0)
        oc_ref[...] = own_head(_dot_tn(cm[:, k_w:], p_t))
        same = _ind(lax.broadcasted_iota(jnp.int32, (cols, cols), 0) // NSA_GROUP
                    == lax.broadcasted_iota(jnp.int32, (cols, cols), 1) // NSA_GROUP, BF16)
        imp = sum(jnp.dot(t, same, preferred_element_type=F32) for t in _split3(p_t))
        imp = jnp.concatenate([imp, jnp.zeros((n_sel_pad - n_cmp, cols), F32)], axis=0)
        blk_s = lax.broadcasted_iota(jnp.int32, (n_sel_pad, 1), 0)
        cur = q_pos // NSA_BLOCK
        forced = (blk_s == 0) | ((blk_s >= cur - 1) & (blk_s <= cur))
        score = jnp.where((blk_s > cur) | (blk_s >= n_sel), -jnp.inf, jnp.where(forced, jnp.inf, imp))
        blk_f = blk_s.astype(F32)

        def pick(_, carry):
            left, bias = carry
            top = jnp.max(left, axis=0, keepdims=True)
            first = jnp.min(jnp.where(left == top, blk_f, float(n_sel_pad)), axis=0, keepdims=True)
            hit = (blk_f == first) & (top > -jnp.inf)
            return jnp.where(hit, -jnp.inf, left), jnp.where(hit, 0.0, bias)

        _, bias = lax.fori_loop(0, NSA_TOPK, pick, (score, jnp.full(score.shape, NEG_BIG, F32)))
        sel_ref[...] = bias
        n_wb = wb_ref.shape[1] // KV_ROWS
        kw_all = jnp.concatenate([heads_on_lanes(wb_ref, n_wb, 0), kwn_ref[0][:, :k_w]], axis=0)
        vw_all = jnp.concatenate([heads_on_lanes(wb_ref, n_wb, 1), kwn_ref[0][:, k_w:]], axis=0)
        n_w = kw_all.shape[0]
        w_row = lax.broadcasted_iota(jnp.int32, (n_w, 1), 0)
        w_pos = past - n_wb + w_row
        d = q_pos - w_pos
        ow_ref[...] = attend_once(kw_all, vw_all,
                                  (d >= 0) & (d < NSA_WINDOW) & (w_pos >= 0) & (w_row < n_wb + n_new))
        m_ref[...] = jnp.full(m_ref.shape, NEG_BIG, F32)
        l_ref[...] = jnp.zeros(l_ref.shape, F32)
        acc_ref[...] = jnp.zeros(acc_ref.shape, F32)

    def page_part(part):
        return jnp.concatenate([heads_on_lanes(pg_ref, PAGE, part) for pg_ref in pg_refs], axis=0)

    blk_per_step = per_step * PAGE // NSA_BLOCK
    online(page_part(0), page_part(1), jnp.concatenate(
        [jnp.broadcast_to(sel_ref[pl.ds(p * blk_per_step + j, 1), :], (NSA_BLOCK, cols))
         for j in range(blk_per_step)], axis=0))

    @pl.when(p == n_steps - 1)
    def _():
        kvn = kvn_ref[0]
        n_row = lax.broadcasted_iota(jnp.int32, (kvn.shape[0], 1), 0)
        bias_new = jnp.where((n_row < n_new) & (past + n_row <= q_pos), 0.0, NEG_BIG)
        online(kvn[:, :k_w], kvn[:, k_w:], bias_new + sel_ref[pl.ds(past // NSA_BLOCK, 1), :])
        o_s = acc_ref[...] / jnp.maximum(l_ref[...], TINY)
        gates = jax.nn.sigmoid(gt_ref[0] + bg_ref[...])
        o_t = gates[0:1, :] * oc_ref[...] + gates[1:2, :] * o_s + gates[2:3, :] * ow_ref[...]
        eye = _ind(lax.broadcasted_iota(jnp.int32, (NSA_HD, NSA_HD), 0)
                   == lax.broadcasted_iota(jnp.int32, (NSA_HD, NSA_HD), 1), BF16)
        o_ref[0] = sum(lax.dot_general(t, eye, (((0,), (0,)), ((), ())), preferred_element_type=F32)
                       for t in _split3(o_t))


def _nsa_sample(q_rows, cmp_means, cache_slc, page_table, kv_new, win_buf, kw_new, gates, b_gate,
                past, n_new):
    seqs, n_pages = page_table.shape
    cols = q_rows.shape[1]
    n_cmp = cmp_means.shape[1]
    n_sel_pad = -(-(-(-(past + n_new) // NSA_BLOCK)) // 8) * 8
    n_wb = win_buf.shape[1]
    pad_new = kv_new.shape[1]
    per_step = 8
    n_steps = n_pages // per_step
    seq_map = lambda b, p, pt: (b, 0, 0)
    return pl.pallas_call(
        functools.partial(_nsa_sample_kernel, past=past, n_new=n_new, n_steps=n_steps, per_step=per_step),
        grid_spec=pltpu.PrefetchScalarGridSpec(
            num_scalar_prefetch=1,
            grid=(seqs, n_steps),
            in_specs=[
                pl.BlockSpec((1, cols, NSA_HD), seq_map),
                pl.BlockSpec((1, n_cmp, NSA_KV_W), seq_map),
                *_page_specs(per_step),
                pl.BlockSpec((1, pad_new, NSA_KV_W), seq_map),
                pl.BlockSpec((1, n_wb, NSA_HD), seq_map),
                pl.BlockSpec((1, pad_new, NSA_KV_W), seq_map),
                pl.BlockSpec((1, 3, cols), seq_map),
                pl.BlockSpec((3, cols), lambda b, p, pt: (0, 0)),
            ],
            out_specs=pl.BlockSpec((1, cols, NSA_HD), seq_map),
            scratch_shapes=[
                pltpu.VMEM((cols, NSA_KV * NSA_HD), BF16),
                pltpu.VMEM((n_sel_pad, cols), F32),
                pltpu.VMEM((1, cols), F32),
                pltpu.VMEM((1, cols), F32),
                pltpu.VMEM((NSA_HD, cols), F32),
                pltpu.VMEM((NSA_HD, cols), F32),
                pltpu.VMEM((NSA_HD, cols), F32),
            ],
        ),
        out_shape=jax.ShapeDtypeStruct((seqs, cols, NSA_HD), F32),
        compiler_params=_params("parallel", "arbitrary"),
        name="nsa_sample",
    )(page_table, q_rows, cmp_means, *([cache_slc] * per_step), kv_new, win_buf, kw_new, gates, b_gate)


def kernel(x_prompt, x_sample, c_prompt, c_sample, state_gla, cache_cmp_kv, cache_slc_kv, cache_win_kv,
           page_table, ada_w, ada_b, gla_w_in, gla_w_alpha, gla_b_alpha, gla_norm_g, gla_w_out, nsa_w_in,
           nsa_b_gate, nsa_w_out, ln_mix_g, ln_mix_b, ffn_w_in, ffn_w_out, ln_ffn_g, ln_ffn_b):
    batch, seq, _ = x_prompt.shape
    seqs, n_new, _ = x_sample.shape
    n_pages = page_table.shape[1]
    past = n_pages * PAGE
    rows_s = seqs * n_new
    tm_p = seq
    w_out_mix = (gla_w_out.astype(BF16), nsa_w_out.astype(BF16))
    w_out_ffn = ffn_w_out.astype(BF16)

    pad_rows = 16 - (batch + seqs)
    c_all = jnp.concatenate([c_prompt, c_sample, jnp.zeros((pad_rows, D_MODEL), F32)], axis=0)
    mod = _adaln(c_all, ada_w, ada_b).reshape(DEPTH, 16, 6, D_MODEL)

    def mods(layer):
        mp = [mod[layer, :batch, i].reshape(batch, 1, D_MODEL) for i in range(6)]
        ms = [jnp.repeat(mod[layer, batch:batch + seqs, i], n_new, axis=0).reshape(1, rows_s, D_MODEL)
              for i in range(6)]
        return mp, ms

    def mixer_out(o, x, layer, m, tm, tag):
        return _out_ln(o, w_out_mix[layer % 2], layer // 2, x, m[2], ln_mix_g, ln_mix_b, layer, min(tm, 512), tag)

    def ffn(xp, xs, layer, mp, ms):
        act_p, act_s = _ffn_in(xp, mp[4], mp[3], xs, ms[4], ms[3], ffn_w_in, layer, tm_p, "ffn_in_%d" % layer)
        out = lambda act, x, m, tm, tag: _out_ln(act, w_out_ffn, layer, x, m[5], ln_ffn_g, ln_ffn_b, layer, tm,
                                                  "ffn_out_%s%d" % (tag, layer))
        return out(act_p, xp, mp, 256, "prompt"), out(act_s, xs, ms, rows_s, "sample")

    xp = x_prompt.reshape(batch * seq, D_MODEL)
    xs = x_sample.reshape(rows_s, D_MODEL)

    mp, ms = mods(0)
    main_p, tail_p, main_s, tail_s = _proj(xp, mp[1], mp[0], xs, ms[1], ms[0], gla_w_in.astype(BF16), 0,
                                           GLA_MAIN_W, tm_p, "gla_in")
    gated_p, gla_state_p = _gla(main_p, tail_p, gla_w_alpha[0], gla_b_alpha[0], gla_norm_g[0], None,
                                batch, seq, 128, 128)
    chunk_s = GLA_SUB

    def pad_steps(a):
        a = a.reshape(seqs, n_new, a.shape[-1])
        return jnp.pad(a, ((0, 0), (0, chunk_s - n_new), (0, 0))).reshape(seqs * chunk_s, a.shape[-1])

    gated_s, gla_state_s = _gla(pad_steps(main_s), pad_steps(tail_s), gla_w_alpha[0], gla_b_alpha[0],
                                gla_norm_g[0], state_gla[:, 0], seqs, chunk_s, chunk_s, n_new)
    gated_s = gated_s.reshape(seqs, chunk_s, -1)[:, :n_new].reshape(rows_s, -1)
    xp = mixer_out(gated_p, xp, 0, mp, tm_p, "gla_out_prompt")
    xs = mixer_out(gated_s, xs, 0, ms, rows_s, "gla_out_sample")
    xp, xs = ffn(xp, xs, 0, mp, ms)

    mp, ms = mods(1)
    main_p, tail_p, main_s, tail_s = _proj(xp, mp[1], mp[0], xs, ms[1], ms[0], nsa_w_in.astype(BF16), 0,
                                           NSA_MAIN_W, tm_p, "nsa_in")
    n_gate = 3 * NSA_HEADS
    g_w = 3 * NSA_GROUP
    gates_p = tail_p[:, :n_gate].reshape(batch * seq, NSA_KV, g_w).transpose(1, 2, 0)
    b_gate_p = nsa_b_gate[0].reshape(NSA_KV, g_w, 1)
    o_p = _nsa_prompt(main_p, gates_p, b_gate_p, batch, seq)

    kv_rows = lambda m, br: m[:, NSA_Q_W + br * NSA_KV_W:NSA_Q_W + (br + 1) * NSA_KV_W]
    kv_shape = (2, NSA_KV, NSA_HD)
    keep_p = min(NSA_WINDOW, seq)
    cmp_rows, slc_rows, win_rows = _kv_rows_out(main_p, batch, seq, keep_p)
    cmp_kv_p = cmp_rows.reshape(batch, 1, seq, *kv_shape)
    slc_kv_p = slc_rows.reshape(batch, 1, seq, *kv_shape)
    win_kv_p = win_rows.reshape(batch, 1, keep_p, *kv_shape)

    cmp_kv_s = kv_rows(main_s, 0).reshape(seqs, 1, n_new, *kv_shape)
    slc_kv_s = kv_rows(main_s, 1).reshape(seqs, 1, n_new, *kv_shape)
    kw_new = kv_rows(main_s, 2).reshape(seqs, n_new, NSA_KV_W)
    n_wb = cache_win_kv.shape[2]
    win_buf = cache_win_kv[:, 0].reshape(seqs, n_wb * KV_ROWS, NSA_HD)
    all_w = jnp.concatenate([win_buf, kw_new.reshape(seqs, n_new * KV_ROWS, NSA_HD)], axis=1)
    keep_s = min(NSA_WINDOW, n_wb + n_new)
    win_kv_s = all_w[:, (n_wb + n_new - keep_s) * KV_ROWS:].reshape(seqs, 1, keep_s, *kv_shape)

    n_phys = cache_cmp_kv.shape[0]
    page_rows = lambda cache: cache.reshape(n_phys, PAGE * KV_ROWS, NSA_HD)
    cmp_means = _page_means(page_rows(cache_cmp_kv), page_table)
    cmp_means = cmp_means.reshape(seqs, past // NSA_BLOCK, NSA_KV_W)
    pad_new = 8
    pad8 = lambda a: jnp.pad(a, ((0, 0), (0, pad_new - n_new), (0, 0)))
    cols = NSA_KV * n_new * NSA_GROUP
    q_rows = (main_s[:, :NSA_Q_W].reshape(seqs, n_new, NSA_KV, NSA_GROUP, NSA_HD)
              .transpose(0, 2, 1, 3, 4).reshape(seqs, cols, NSA_HD))
    gates_s = (tail_s[:, :n_gate].reshape(seqs, n_new, NSA_KV, NSA_GROUP, 3)
               .transpose(0, 4, 2, 1, 3).reshape(seqs, 3, cols))
    b_gate_s = jnp.broadcast_to(nsa_b_gate[0].reshape(1, NSA_KV, NSA_GROUP, 3), (n_new, NSA_KV, NSA_GROUP, 3))
    b_gate_s = b_gate_s.transpose(3, 1, 0, 2).reshape(3, cols)
    o_s = _nsa_sample(q_rows, cmp_means, page_rows(cache_slc_kv), page_table,
                      pad8(slc_kv_s.reshape(seqs, n_new, NSA_KV_W)), win_buf, pad8(kw_new), gates_s, b_gate_s,
                      past, n_new)
    o_s = (o_s.reshape(seqs, NSA_KV, n_new, NSA_GROUP, NSA_HD).transpose(0, 2, 1, 3, 4)
           .reshape(rows_s, NSA_Q_W).astype(BF16))

    xp = mixer_out(o_p, xp, 1, mp, tm_p, "nsa_out_prompt")
    xs = mixer_out(o_s, xs, 1, ms, rows_s, "nsa_out_sample")
    xp, xs = ffn(xp, xs, 1, mp, ms)

    return (xp.reshape(batch, seq, D_MODEL), xs.reshape(seqs, n_new, D_MODEL),
            gla_state_p[:, None], gla_state_s[:, None], cmp_kv_p, cmp_kv_s, slc_kv_p, slc_kv_s,
            win_kv_p, win_kv_s)
```

```python
import functools

import jax
import jax.numpy as jnp
from jax import lax
from jax.experimental import pallas as pl
from jax.experimental.pallas import tpu as pltpu

F32 = jnp.float32
BF16 = jnp.bfloat16

D_MODEL = 2048
DEPTH = 2
DN_ALPHA = (2 * DEPTH) ** 0.25
LN_EPS = 1e-5
D_FF = 5632
GLA_HEADS = 4
GLA_DK = 256
GLA_DV = 512
GLA_RANK = 16
GLA_TAU = 16.0
GLA_SUB = 32
GLA_MAIN_W = 2 * GLA_HEADS * GLA_DK + 2 * GLA_HEADS * GLA_DV
NSA_HEADS = 16
NSA_KV = 4
NSA_HD = 128
NSA_GROUP = NSA_HEADS // NSA_KV
NSA_BLOCK = 64
NSA_TOPK = 16
NSA_WINDOW = 512
NSA_Q_W = NSA_HEADS * NSA_HD
NSA_KV_W = 2 * NSA_KV * NSA_HD
NSA_MAIN_W = NSA_Q_W + 3 * NSA_KV_W
PAGE = 128
KV_ROWS = 2 * NSA_KV

LANES = 128
BF16_ROWS = 16
LOG2_E = 1.4426950408889634
V7X_VMEM_BYTES = 64 * 1024 * 1024
VMEM_LIMIT = V7X_VMEM_BYTES * 7 // 8

NEG_BIG = -1e30
TINY = float(jnp.finfo(jnp.float32).tiny)


def _params(*sem):
    return pltpu.CompilerParams(dimension_semantics=sem, vmem_limit_bytes=VMEM_LIMIT)


def _dot(a, b):
    return jnp.dot(a.astype(BF16), b.astype(BF16), preferred_element_type=F32)


def _dot_nt(a, b):
    return lax.dot_general(a.astype(BF16), b.astype(BF16), (((1,), (1,)), ((), ())),
                           preferred_element_type=F32)


def _dot_tn(a, b):
    return lax.dot_general(a.astype(BF16), b.astype(BF16), (((0,), (0,)), ((), ())),
                           preferred_element_type=F32)


def _split3(x):
    hi = x.astype(BF16)
    r1 = x - hi.astype(F32)
    mid = r1.astype(BF16)
    lo = (r1 - mid.astype(F32)).astype(BF16)
    return hi, mid, lo


def _silu(x):
    return x * jax.nn.sigmoid(x)


def _ind(cond, dtype=F32):
    return jnp.where(cond, 1.0, 0.0).astype(dtype)


def _masked_softmax2(s2, mask, axis):
    s2 = jnp.where(mask, s2, -jnp.inf)
    m = jnp.max(s2, axis=axis, keepdims=True)
    m = jnp.where(jnp.isfinite(m), m, 0.0)
    e = jnp.where(mask, jnp.exp2(s2 - m), 0.0)
    return e / jnp.maximum(jnp.sum(e, axis=axis, keepdims=True), TINY)


def _adaln_kernel(c_ref, w_ref, b_ref, o_ref):
    o_ref[...] = _dot(_silu(c_ref[...]), w_ref[...]) + b_ref[...]


def _adaln(c_all, ada_w, ada_b):
    rows = c_all.shape[0]
    n = ada_w.shape[2]
    tn = 1024
    return pl.pallas_call(
        _adaln_kernel,
        grid=(DEPTH, n // tn),
        in_specs=[
            pl.BlockSpec((rows, D_MODEL), lambda l, j: (0, 0)),
            pl.BlockSpec((None, D_MODEL, tn), lambda l, j: (l, 0, j)),
            pl.BlockSpec((None, 1, tn), lambda l, j: (l, 0, j)),
        ],
        out_specs=pl.BlockSpec((None, rows, tn), lambda l, j: (l, 0, j)),
        out_shape=jax.ShapeDtypeStruct((DEPTH, rows, n), F32),
        compiler_params=_params("parallel", "parallel"),
        name="adaln",
    )(c_all, ada_w, ada_b.reshape(DEPTH, 1, n))


def _modulate(x_ref, sc_ref, sh_ref):
    return (x_ref[...] * (1.0 + sc_ref[...]) + sh_ref[...]).astype(BF16)


def _proj_kernel(x_ref, sc_ref, sh_ref, xs_ref, scs_ref, shs_ref, w_ref, wt_ref,
                 o_ref, ot_ref, os_ref, ots_ref, h_ref, hs_ref, *, n_i):
    i, j = pl.program_id(0), pl.program_id(1)

    @pl.when(j == 0)
    def _():
        h_ref[...] = _modulate(x_ref, sc_ref, sh_ref)
        ot_ref[...] = _dot(h_ref[...], wt_ref[...])

    o_ref[...] = _dot(h_ref[...], w_ref[...])

    @pl.when(i == n_i - 1)
    def _():
        @pl.when(j == 0)
        def _():
            hs_ref[...] = _modulate(xs_ref, scs_ref, shs_ref)
            ots_ref[...] = _dot(hs_ref[...], wt_ref[...])

        os_ref[...] = _dot(hs_ref[...], w_ref[...])


def _proj(x, sc, sh, xs, scs, shs, w, layer, n_main, tm, name):
    m, ms = x.shape[0], xs.shape[0]
    n_tail = w.shape[2] - n_main
    w_tail = jnp.pad(w[layer, :, n_main:], ((0, 0), (0, LANES - n_tail)))
    tn = 512
    n_i = m // tm
    tiles_per_group = n_i // sc.shape[0]
    mod_spec = pl.BlockSpec((None, sc.shape[1], D_MODEL), lambda i, j: (i // tiles_per_group, 0, 0))
    const2 = lambda i, j: (0, 0)
    const3 = lambda i, j: (0, 0, 0)
    last_tile_col = lambda i, j: (0, jnp.where(i == n_i - 1, j, 0))
    return pl.pallas_call(
        functools.partial(_proj_kernel, n_i=n_i),
        grid=(n_i, n_main // tn),
        in_specs=[
            pl.BlockSpec((tm, D_MODEL), lambda i, j: (i, 0), pipeline_mode=pl.Buffered(1)),
            mod_spec,
            mod_spec,
            pl.BlockSpec((ms, D_MODEL), const2),
            pl.BlockSpec((None, ms, D_MODEL), const3),
            pl.BlockSpec((None, ms, D_MODEL), const3),
            pl.BlockSpec((None, D_MODEL, tn), lambda i, j: (layer, 0, j)),
            pl.BlockSpec((D_MODEL, LANES), const2),
        ],
        out_specs=[
            pl.BlockSpec((tm, tn), lambda i, j: (i, j)),
            pl.BlockSpec((tm, LANES), lambda i, j: (i, 0)),
            pl.BlockSpec((ms, tn), last_tile_col),
            pl.BlockSpec((ms, LANES), const2),
        ],
        out_shape=[jax.ShapeDtypeStruct((m, n_main), F32), jax.ShapeDtypeStruct((m, LANES), F32),
                   jax.ShapeDtypeStruct((ms, n_main), F32), jax.ShapeDtypeStruct((ms, LANES), F32)],
        scratch_shapes=[pltpu.VMEM((tm, D_MODEL), BF16), pltpu.VMEM((ms, D_MODEL), BF16)],
        compiler_params=_params("arbitrary", "arbitrary"),
        name=name,
    )(x, sc, sh, xs, scs, shs, w, w_tail)


def _out_ln_kernel(a_ref, w_ref, x_ref, gt_ref, g_ref, b_ref, o_ref):
    y = DN_ALPHA * x_ref[...] + gt_ref[...] * jnp.dot(a_ref[...], w_ref[...], preferred_element_type=F32)
    mu = jnp.mean(y, axis=-1, keepdims=True)
    yc = y - mu
    var = jnp.mean(yc * yc, axis=-1, keepdims=True)
    o_ref[...] = yc * lax.rsqrt(var + LN_EPS) * g_ref[...] + b_ref[...]


def _out_ln(a, w, w_layer, x, gate, ln_g, ln_b, ln_layer, tm, name):
    m, kdim = a.shape
    groups, mod_rows = gate.shape[0], gate.shape[1]
    tiles_per_group = m // tm // groups
    vec_spec = pl.BlockSpec((None, 1, D_MODEL), lambda i: (ln_layer, 0, 0))
    return pl.pallas_call(
        _out_ln_kernel,
        grid=(m // tm,),
        in_specs=[
            pl.BlockSpec((tm, kdim), lambda i: (i, 0)),
            pl.BlockSpec((None, kdim, D_MODEL), lambda i: (w_layer, 0, 0), pipeline_mode=pl.Buffered(1)),
            pl.BlockSpec((tm, D_MODEL), lambda i: (i, 0)),
            pl.BlockSpec((None, mod_rows, D_MODEL), lambda i: (i // tiles_per_group, 0, 0)),
            vec_spec,
            vec_spec,
        ],
        out_specs=pl.BlockSpec((tm, D_MODEL), lambda i: (i, 0)),
        out_shape=jax.ShapeDtypeStruct((m, D_MODEL), F32),
        compiler_params=_params("parallel"),
        name=name,
    )(a, w, x, gate, ln_g.reshape(-1, 1, D_MODEL), ln_b.reshape(-1, 1, D_MODEL))


def _ffn_in_kernel(x_ref, sc_ref, sh_ref, xs_ref, scs_ref, shs_ref, wa_ref, wu_ref,
                   o_ref, os_ref, h_ref, hs_ref, *, n_i):
    i, j = pl.program_id(0), pl.program_id(1)

    @pl.when(j == 0)
    def _():
        h_ref[...] = _modulate(x_ref, sc_ref, sh_ref)

    wa = wa_ref[...].astype(BF16)
    wu = wu_ref[...].astype(BF16)

    def swiglu_in(h):
        return (_silu(_dot(h, wa)) * _dot(h, wu)).astype(BF16)

    o_ref[...] = swiglu_in(h_ref[...])

    @pl.when(i == n_i - 1)
    def _():
        @pl.when(j == 0)
        def _():
            hs_ref[...] = _modulate(xs_ref, scs_ref, shs_ref)

        os_ref[...] = swiglu_in(hs_ref[...])


def _ffn_in(x, sc, sh, xs, scs, shs, w, layer, tm, name):
    m, ms = x.shape[0], xs.shape[0]
    tn = 256
    nj = D_FF // tn
    n_i = m // tm
    tiles_per_group = n_i // sc.shape[0]
    mod_spec = pl.BlockSpec((None, sc.shape[1], D_MODEL), lambda i, j: (i // tiles_per_group, 0, 0))
    const3 = lambda i, j: (0, 0, 0)
    return pl.pallas_call(
        functools.partial(_ffn_in_kernel, n_i=n_i),
        grid=(n_i, nj),
        in_specs=[
            pl.BlockSpec((tm, D_MODEL), lambda i, j: (i, 0), pipeline_mode=pl.Buffered(1)),
            mod_spec,
            mod_spec,
            pl.BlockSpec((ms, D_MODEL), lambda i, j: (0, 0)),
            pl.BlockSpec((None, ms, D_MODEL), const3),
            pl.BlockSpec((None, ms, D_MODEL), const3),
            pl.BlockSpec((None, D_MODEL, tn), lambda i, j: (layer, 0, j)),
            pl.BlockSpec((None, D_MODEL, tn), lambda i, j: (layer, 0, j + nj)),
        ],
        out_specs=[
            pl.BlockSpec((tm, tn), lambda i, j: (i, j)),
            pl.BlockSpec((ms, tn), lambda i, j: (0, jnp.where(i == n_i - 1, j, 0))),
        ],
        out_shape=[jax.ShapeDtypeStruct((m, D_FF), BF16), jax.ShapeDtypeStruct((ms, D_FF), BF16)],
        scratch_shapes=[pltpu.VMEM((tm, D_MODEL), BF16), pltpu.VMEM((ms, D_MODEL), BF16)],
        compiler_params=_params("arbitrary", "arbitrary"),
        name=name,
    )(x, sc, sh, xs, scs, shs, w, w)


def _gla_kernel(*refs, chunk, t_valid, has_s0):
    if has_s0:
        q_ref, k_ref, v_ref, r_ref, a_ref, wa_ref, ba_ref, g_ref, s0_ref, o_ref, st_ref = refs
    else:
        q_ref, k_ref, v_ref, r_ref, a_ref, wa_ref, ba_ref, g_ref, o_ref, st_ref = refs
    sub = GLA_SUB
    n_sub = chunk // sub
    anchor = sub // 2 - 1

    @pl.when(pl.program_id(1) == 0)
    def _():
        if has_s0:
            st_ref[...] = s0_ref[...]
        else:
            st_ref[...] = jnp.zeros(st_ref.shape, F32)

    row = lax.broadcasted_iota(jnp.int32, (chunk, 1), 0)
    tri = _ind(lax.broadcasted_iota(jnp.int32, (chunk, chunk), 1)
               <= lax.broadcasted_iota(jnp.int32, (chunk, chunk), 0), BF16)
    ones = jnp.ones((chunk, LANES), BF16)
    causal = (lax.broadcasted_iota(jnp.int32, (sub, sub), 1)
              <= lax.broadcasted_iota(jnp.int32, (sub, sub), 0))
    heads = range(GLA_HEADS)
    dk = [slice(h * GLA_DK, (h + 1) * GLA_DK) for h in heads]
    dv = [slice(h * GLA_DV, (h + 1) * GLA_DV) for h in heads]
    subs = [slice(s * sub, (s + 1) * sub) for s in range(n_sub)]

    z = _dot(a_ref[...], wa_ref[...]) + ba_ref[...]
    lb = (jnp.minimum(z, 0.0) - jnp.log1p(jnp.exp(-jnp.abs(z)))) * (1.0 / GLA_TAU)
    k = k_ref[...]
    if t_valid < chunk:
        lb = jnp.where(row < t_valid, lb, 0.0)
        k = jnp.where(row < t_valid, k, 0.0)
    q = q_ref[...] * (GLA_DK ** -0.5)
    vb = v_ref[...].astype(BF16)
    lb3 = _split3(lb)
    b = sum(jnp.dot(tri, t, preferred_element_type=F32) for t in lb3)
    dec = jnp.exp(sum(lax.dot_general(t, ones, (((0,), (0,)), ((), ())), preferred_element_type=F32)
                      for t in lb3))
    b_last = b[chunk - 1:chunk, :]
    q_in = (q * jnp.exp(b)).astype(BF16)
    k_out = (k * jnp.exp(b_last - b)).astype(BF16)
    q_diag, k_diag, q_off, k_off = [], [], [], []
    for s, rs in enumerate(subs):
        b_mid = b[s * sub + anchor:s * sub + anchor + 1, :]
        q_diag.append((q[rs] * jnp.exp(b[rs] - b_mid)).astype(BF16))
        k_diag.append((k[rs] * jnp.exp(b_mid - b[rs])).astype(BF16))
        if s > 0:
            prev = slice(0, s * sub)
            b_in = b[s * sub - 1:s * sub, :]
            q_off.append((q[rs] * jnp.exp(b[rs] - b_in)).astype(BF16))
            k_off.append((k[prev] * jnp.exp(b_in - b[prev])).astype(BF16))

    s_old = [st_ref[0, h] for h in heads]
    o_inter = [_dot(q_in[:, dk[h]], s_old[h]) for h in heads]
    a_diag = [[_dot_nt(q_diag[s][:, dk[h]], k_diag[s][:, dk[h]]) for h in heads] for s in range(n_sub)]
    a_off = [[_dot_nt(q_off[s][:, dk[h]], k_off[s][:, dk[h]]) for h in heads] for s in range(n_sub - 1)]
    a_diag = [[jnp.where(causal, a, 0.0).astype(BF16) for a in row_] for row_ in a_diag]
    a_off = [[a.astype(BF16) for a in row_] for row_ in a_off]
    o_intra = []
    for s, rs in enumerate(subs):
        o_s = [_dot(a_diag[s][h], vb[rs, dv[h]]) for h in heads]
        if s > 0:
            o_s = [o_s[h] + _dot(a_off[s - 1][h], vb[0:s * sub, dv[h]]) for h in heads]
        o_intra.append(o_s)
    kv = [_dot_tn(k_out[:, dk[h]], vb[:, dv[h]]) for h in heads]
    for h in heads:
        st_ref[0, h] = s_old[h] * jnp.tile(dec[dk[h], :], (1, GLA_DV // LANES)) + kv[h]
    for h in heads:
        o = o_inter[h] + jnp.concatenate([o_intra[s][h] for s in range(n_sub)], axis=0)
        o = o * lax.rsqrt(jnp.mean(o * o, axis=-1, keepdims=True) + LN_EPS)
        o_ref[:, dv[h]] = (o * g_ref[:, dv[h]] * _silu(r_ref[:, dv[h]])).astype(BF16)


def _gla(main, tail, w_alpha, b_alpha, norm_g, s0, batch, seq, chunk, t_valid):
    n_chunks = seq // chunk
    qk_w = GLA_HEADS * GLA_DK
    v_w = GLA_HEADS * GLA_DV
    wa = jnp.pad(w_alpha, ((0, LANES - GLA_RANK), (0, 0)))
    row_map = lambda b, c: (b * n_chunks + c, 0)
    const = lambda b, c: (0, 0)
    in_specs = [
        pl.BlockSpec((chunk, qk_w), row_map),
        pl.BlockSpec((chunk, qk_w), lambda b, c: (b * n_chunks + c, 1)),
        pl.BlockSpec((chunk, v_w), lambda b, c: (b * n_chunks + c, 1)),
        pl.BlockSpec((chunk, v_w), lambda b, c: (b * n_chunks + c, 2)),
        pl.BlockSpec((chunk, LANES), row_map),
        pl.BlockSpec((LANES, qk_w), const),
        pl.BlockSpec((1, qk_w), const),
        pl.BlockSpec((1, v_w), const),
    ]
    args = [main, main, main, main, tail, wa, b_alpha.reshape(1, qk_w), norm_g.reshape(1, v_w)]
    state_spec = pl.BlockSpec((1, GLA_HEADS, GLA_DK, GLA_DV), lambda b, c: (b, 0, 0, 0))
    if s0 is not None:
        in_specs.append(state_spec)
        args.append(s0)
    return pl.pallas_call(
        functools.partial(_gla_kernel, chunk=chunk, t_valid=t_valid, has_s0=s0 is not None),
        grid=(batch, n_chunks),
        in_specs=in_specs,
        out_specs=[pl.BlockSpec((chunk, v_w), row_map), state_spec],
        out_shape=[jax.ShapeDtypeStruct((batch * seq, v_w), BF16),
                   jax.ShapeDtypeStruct((batch, GLA_HEADS, GLA_DK, GLA_DV), F32)],
        compiler_params=_params("parallel", "arbitrary"),
        name="gla_prompt" if s0 is None else "gla_sample",
    )(*args)


def _select_blocks(imp_t, blk, cur):
    n = imp_t.shape[0]
    forced = (blk == 0) | ((blk >= cur - 1) & (blk <= cur))
    score = jnp.where(blk > cur, -jnp.inf, jnp.where(forced, jnp.inf, imp_t))
    rank = jnp.zeros(score.shape, F32)
    for i in range(n):
        si = score[i:i + 1, :]
        rank = rank + jnp.where(blk > i, _ind(si >= score), _ind(si > score))
    return (rank < NSA_TOPK) & (score > -jnp.inf)


def _nsa_prompt_kernel(pt_ref, q_ref, kc_ref, vc_ref, ks_ref, vs_ref, kw_ref, vw_ref, gt_ref, bg_ref, *refs,
                       seq, tq, tk, pages_per_step):
    page_refs = refs[:pages_per_step]
    (o_ref, means_ref, kcm, vcm, ksb, vst, kwb, vwt, qs_ref, sel_ref,
     ms_ref, accs_ref, mw_ref, accw_ref) = refs[pages_per_step:]
    qi = pl.program_id(2)
    n_blk = seq // NSA_BLOCK
    n_kt = seq // tk
    blk_per_kt = tk // NSA_BLOCK

    @pl.when(qi == 0)
    def _():
        kcm[...] = jnp.sum(kc_ref[...].reshape(n_blk, NSA_BLOCK, NSA_HD), axis=1) * (1.0 / NSA_BLOCK)
        vcm[...] = jnp.sum(vc_ref[...].reshape(n_blk, NSA_BLOCK, NSA_HD), axis=1) * (1.0 / NSA_BLOCK)
        ksb[...] = ks_ref[...].astype(BF16)
        kwb[...] = kw_ref[...].astype(BF16)
        ones_rows = _ind(lax.broadcasted_iota(jnp.int32, (BF16_ROWS, tk), 0) == 0, BF16)
        for kt in range(n_kt):
            rows = slice(kt * tk, (kt + 1) * tk)
            vst[kt] = jnp.concatenate([vs_ref[rows, :].T.astype(BF16), ones_rows], axis=0)
            vwt[kt] = jnp.concatenate([vw_ref[rows, :].T.astype(BF16), ones_rows], axis=0)

    t0 = qi * tq
    q = q_ref[...] * (NSA_HD ** -0.5 * LOG2_E)
    qs_ref[...] = jnp.concatenate(
        [q[:, g * NSA_HD:(g + 1) * NSA_HD] for g in range(NSA_GROUP)], axis=0).astype(BF16)
    qs = qs_ref[...]
    t_row = t0 + lax.broadcasted_iota(jnp.int32, (1, tq), 1)
    t_row_g = jnp.concatenate([t_row] * NSA_GROUP, axis=1)

    kc_b = kcm[...].astype(BF16)
    blk_col = lax.broadcasted_iota(jnp.int32, (n_blk, 1), 0)
    p_t = _masked_softmax2(_dot_nt(kc_b, qs), (blk_col + 1) * NSA_BLOCK - 1 <= t_row_g, axis=0)
    o_c = _dot_tn(vcm[...], p_t)
    imp_t = sum(p_t[:, g * tq:(g + 1) * tq] for g in range(NSA_GROUP))
    sel_ref[...] = jnp.where(_select_blocks(imp_t, blk_col, t_row // NSA_BLOCK), 0.0, NEG_BIG)

    n_loc = lax.broadcasted_iota(jnp.int32, (tk, tq), 0)
    t_loc = lax.broadcasted_iota(jnp.int32, (tk, tq), 1)
    on_lanes = lambda a: jnp.concatenate([a] * NSA_GROUP, axis=1)
    causal = on_lanes(jnp.where(n_loc <= t_loc, 0.0, NEG_BIG))
    far = on_lanes(jnp.where(n_loc > t_loc, 0.0, NEG_BIG))

    def sel_bias(kt):
        return on_lanes(jnp.concatenate(
            [jnp.broadcast_to(sel_ref[pl.ds(kt * blk_per_kt + j, 1), :], (NSA_BLOCK, tq))
             for j in range(blk_per_kt)], axis=0))

    sel_state, win_state = (ms_ref, accs_ref), (mw_ref, accw_ref)
    for m_ref, acc_ref in (sel_state, win_state):
        m_ref[...] = jnp.full(m_ref.shape, NEG_BIG, F32)
        acc_ref[...] = jnp.zeros(acc_ref.shape, F32)

    def steps(*work):
        s = [_dot_nt(k_ref[pl.ds(pl.multiple_of(kt * tk, tk), tk), :], qs_ref[...]) + bias
             for _, k_ref, _, kt, bias in work]
        m_old = [state[0][...] for state, *_ in work]
        m_new = [jnp.maximum(mo, jnp.max(si, axis=0, keepdims=True)) for mo, si in zip(m_old, s)]
        p = [jnp.exp2(si - mn).astype(BF16) for si, mn in zip(s, m_new)]
        pv = [_dot(vt_ref[kt], pi) for (_, _, vt_ref, kt, _), pi in zip(work, p)]
        for (state, *_), mo, mn, pvi in zip(work, m_old, m_new, pv):
            state[1][...] = jnp.exp2(mo - mn) * state[1][...] + pvi
            state[0][...] = mn

    def result(acc_ref):
        acc = acc_ref[...]
        return acc[:NSA_HD] / jnp.maximum(acc[NSA_HD:NSA_HD + 1], TINY)

    steps((sel_state, ksb, vst, qi, sel_bias(qi) + causal), (win_state, kwb, vwt, qi, causal))

    @pl.when(qi >= 1)
    def _():
        steps((sel_state, ksb, vst, qi - 1, sel_bias(qi - 1)), (win_state, kwb, vwt, qi - 1, far))

    def sel_body(i, carry):
        kt = qi - 2 - i
        steps((sel_state, ksb, vst, kt, sel_bias(kt)))
        return carry

    lax.fori_loop(0, qi - 1, sel_body, 0)

    gates = jax.nn.sigmoid(gt_ref[...] + bg_ref[...])
    g_c, g_s, g_w = (jnp.concatenate([gates[3 * g + br:3 * g + br + 1, :] for g in range(NSA_GROUP)], axis=1)
                     for br in range(3))
    o_t = g_c * o_c + g_s * result(accs_ref) + g_w * result(accw_ref)
    for g in range(NSA_GROUP):
        o_ref[:, g * NSA_HD:(g + 1) * NSA_HD] = o_t[:, g * tq:(g + 1) * tq].T.astype(BF16)

    for i, x_ref in enumerate(page_refs):
        x = x_ref[0].reshape(PAGE // NSA_BLOCK, NSA_BLOCK, KV_ROWS, NSA_HD)
        means_ref[0, i] = jnp.sum(x, axis=1) * (1.0 / NSA_BLOCK)


def _nsa_prompt(main, gates, b_gate, batch, seq, cache_cmp, page_table):
    tq = tk = NSA_WINDOW
    assert seq % tk == 0
    vt_rows = NSA_HD + BF16_ROWS
    nq = seq // tq
    q_lanes = NSA_GROUP * NSA_HD
    col0 = NSA_Q_W // NSA_HD
    seqs, n_pages = page_table.shape
    n_steps = batch * NSA_KV * nq
    pages_per_step = seqs * n_pages // n_steps
    groups = n_pages // pages_per_step
    assert pages_per_step * n_steps == seqs * n_pages and groups * pages_per_step == n_pages
    per_page = PAGE // NSA_BLOCK

    def kv_spec(branch, part):
        off = col0 + (2 * branch + part) * NSA_KV
        return pl.BlockSpec((seq, NSA_HD), lambda b, kv, qi, pt: (b, off + kv))

    def step_of(b, kv, qi):
        return (b * NSA_KV + kv) * nq + qi

    def page_spec(i):
        def index(b, kv, qi, pt):
            s = step_of(b, kv, qi)
            return (pt[s // groups, (s % groups) * pages_per_step + i], 0, 0)
        return pl.BlockSpec((1, PAGE * KV_ROWS, NSA_HD), index)

    def means_index(b, kv, qi, pt):
        s = step_of(b, kv, qi)
        return (s // groups, s % groups, 0, 0, 0)

    g_w = 3 * NSA_GROUP
    g_rows = NSA_GROUP * tq
    grid_spec = pltpu.PrefetchScalarGridSpec(
        num_scalar_prefetch=1,
        grid=(batch, NSA_KV, nq),
        in_specs=[pl.BlockSpec((tq, q_lanes), lambda b, kv, qi, pt: (b * nq + qi, kv))]
        + [kv_spec(br, part) for br in range(3) for part in range(2)]
        + [pl.BlockSpec((None, g_w, tq), lambda b, kv, qi, pt: (kv, 0, b * nq + qi)),
           pl.BlockSpec((None, g_w, 1), lambda b, kv, qi, pt: (kv, 0, 0))]
        + [page_spec(i) for i in range(pages_per_step)],
        out_specs=[pl.BlockSpec((tq, q_lanes), lambda b, kv, qi, pt: (b * nq + qi, kv)),
                   pl.BlockSpec((1, pages_per_step, per_page, KV_ROWS, NSA_HD), means_index)],
        scratch_shapes=[
            pltpu.VMEM((seq // NSA_BLOCK, NSA_HD), F32),
            pltpu.VMEM((seq // NSA_BLOCK, NSA_HD), F32),
            pltpu.VMEM((seq, NSA_HD), BF16),
            pltpu.VMEM((seq // tk, vt_rows, tk), BF16),
            pltpu.VMEM((seq, NSA_HD), BF16),
            pltpu.VMEM((seq // tk, vt_rows, tk), BF16),
            pltpu.VMEM((g_rows, NSA_HD), BF16),
            pltpu.VMEM((seq // NSA_BLOCK, tq), F32),
            pltpu.VMEM((1, g_rows), F32),
            pltpu.VMEM((vt_rows, g_rows), F32),
            pltpu.VMEM((1, g_rows), F32),
            pltpu.VMEM((vt_rows, g_rows), F32),
        ],
    )
    return pl.pallas_call(
        functools.partial(_nsa_prompt_kernel, seq=seq, tq=tq, tk=tk, pages_per_step=pages_per_step),
        grid_spec=grid_spec,
        out_shape=[jax.ShapeDtypeStruct((batch * seq, NSA_Q_W), BF16),
                   jax.ShapeDtypeStruct((seqs, n_pages, per_page, KV_ROWS, NSA_HD), F32)],
        compiler_params=_params("arbitrary", "arbitrary", "arbitrary"),
        name="nsa_prompt",
    )(page_table, main, main, main, main, main, main, main, gates, b_gate, *([cache_cmp] * pages_per_step))


def _kv_rows_out_kernel(c_ref, s_ref, w_ref, oc_ref, os_ref, ow_ref, *, rows, n_c):
    def spread(src, dst):
        for r in range(KV_ROWS):
            dst[pl.ds(r, rows, stride=KV_ROWS), :] = src[:, r * NSA_HD:(r + 1) * NSA_HD]

    spread(c_ref, oc_ref)
    spread(s_ref, os_ref)

    @pl.when(pl.program_id(1) == n_c - 1)
    def _():
        spread(w_ref, ow_ref)


def _kv_rows_out(main, batch, seq, keep):
    rows = keep
    n_c = seq // rows
    col0 = NSA_Q_W // NSA_KV_W
    tile = lambda br: pl.BlockSpec((rows, NSA_KV_W), lambda b, c: (b * n_c + c, col0 + br))
    out_tile = pl.BlockSpec((rows * KV_ROWS, NSA_HD), lambda b, c: (b * n_c + c, 0))
    return pl.pallas_call(
        functools.partial(_kv_rows_out_kernel, rows=rows, n_c=n_c),
        grid=(batch, n_c),
        in_specs=[tile(0), tile(1),
                  pl.BlockSpec((rows, NSA_KV_W), lambda b, c: (b * n_c + n_c - 1, col0 + 2))],
        out_specs=[out_tile, out_tile, pl.BlockSpec((rows * KV_ROWS, NSA_HD), lambda b, c: (b, 0))],
        out_shape=[jax.ShapeDtypeStruct((batch * seq * KV_ROWS, NSA_HD), F32)] * 2
        + [jax.ShapeDtypeStruct((batch * keep * KV_ROWS, NSA_HD), F32)],
        compiler_params=_params("arbitrary", "arbitrary"),
        name="nsa_kv_rows_out",
    )(main, main, main)


def _page_specs(n):
    return [pl.BlockSpec((1, PAGE * KV_ROWS, NSA_HD), lambda b, p, pt, i=i: (pt[b, p * n + i], 0, 0))
            for i in range(n)]


def _nsa_sample_kernel(pt_ref, q_ref, cm_ref, *refs, past, n_new, n_steps, per_step):
    pg_refs = refs[:per_step]
    (kvn_ref, wb_ref, kwn_ref, gt_ref, bg_ref, o_ref,
     qr_ref, sel_ref, m_ref, l_ref, acc_ref, oc_ref, ow_ref) = refs[per_step:]
    p = pl.program_id(1)
    k_w = NSA_KV * NSA_HD
    cols = NSA_KV * n_new * NSA_GROUP
    per_kv = n_new * NSA_GROUP
    n_cmp = past // NSA_BLOCK
    n_sel = -(-(past + n_new) // NSA_BLOCK)
    n_sel_pad = sel_ref.shape[0]
    lane = lax.broadcasted_iota(jnp.int32, (1, cols), 1)
    q_pos = past + (lane // NSA_GROUP) % n_new

    def heads_on_lanes(ref, n, part):
        return jnp.concatenate(
            [ref[0, pl.ds(part * NSA_KV + kv, n, stride=KV_ROWS), :] for kv in range(NSA_KV)], axis=1)

    def scores(keys):
        return _dot_nt(keys, qr_ref[...])

    def own_head(x_t):
        out = jnp.zeros((NSA_HD, cols), F32)
        for kv in range(NSA_KV):
            mine = (lane // per_kv) == kv
            out = out + jnp.where(mine, x_t[kv * NSA_HD:(kv + 1) * NSA_HD, :], 0.0)
        return out

    def attend_once(keys, vals, mask):
        p_t = _masked_softmax2(scores(keys), mask, axis=0)
        return own_head(_dot_tn(vals, p_t))

    def online(keys, vals, bias):
        s = scores(keys) + bias
        m_old = m_ref[...]
        m_new = jnp.maximum(m_old, jnp.max(s, axis=0, keepdims=True))
        p_t = jnp.exp2(s - m_new)
        alpha = jnp.exp2(m_old - m_new)
        l_ref[...] = alpha * l_ref[...] + jnp.sum(p_t, axis=0, keepdims=True)
        acc_ref[...] = alpha * acc_ref[...] + own_head(_dot_tn(vals, p_t))
        m_ref[...] = m_new

    @pl.when(p == 0)
    def _():
        q = q_ref[0] * (NSA_HD ** -0.5 * LOG2_E)
        r_kv = lax.broadcasted_iota(jnp.int32, (cols, k_w), 0) // per_kv
        c_kv = lax.broadcasted_iota(jnp.int32, (cols, k_w), 1) // NSA_HD
        qr_ref[...] = jnp.where(r_kv == c_kv, jnp.tile(q, (1, NSA_KV)), 0.0).astype(BF16)
        cm = cm_ref[0]
        blk = lax.broadcasted_iota(jnp.int32, (n_cmp, 1), 0)
        p_t = _masked_softmax2(scores(cm[:, :k_w]), (blk + 1) * NSA_BLOCK - 1 <= q_pos, axis=0)
        oc_ref[...] = own_head(_dot_tn(cm[:, k_w:], p_t))
        same = _ind(lax.broadcasted_iota(jnp.int32, (cols, cols), 0) // NSA_GROUP
                    == lax.broadcasted_iota(jnp.int32, (cols, cols), 1) // NSA_GROUP, BF16)
        imp = sum(jnp.dot(t, same, preferred_element_type=F32) for t in _split3(p_t))
        imp = jnp.concatenate([imp, jnp.zeros((n_sel_pad - n_cmp, cols), F32)], axis=0)
        blk_s = lax.broadcasted_iota(jnp.int32, (n_sel_pad, 1), 0)
        cur = q_pos // NSA_BLOCK
        forced = (blk_s == 0) | ((blk_s >= cur - 1) & (blk_s <= cur))
        score = jnp.where((blk_s > cur) | (blk_s >= n_sel), -jnp.inf, jnp.where(forced, jnp.inf, imp))
        blk_f = blk_s.astype(F32)

        def pick(_, carry):
            left, bias = carry
            top = jnp.max(left, axis=0, keepdims=True)
            first = jnp.min(jnp.where(left == top, blk_f, float(n_sel_pad)), axis=0, keepdims=True)
            hit = (blk_f == first) & (top > -jnp.inf)
            return jnp.where(hit, -jnp.inf, left), jnp.where(hit, 0.0, bias)

        _, bias = lax.fori_loop(0, NSA_TOPK, pick, (score, jnp.full(score.shape, NEG_BIG, F32)))
        sel_ref[...] = bias
        n_wb = wb_ref.shape[1] // KV_ROWS
        kw_all = jnp.concatenate([heads_on_lanes(wb_ref, n_wb, 0), kwn_ref[0][:, :k_w]], axis=0)
        vw_all = jnp.concatenate([heads_on_lanes(wb_ref, n_wb, 1), kwn_ref[0][:, k_w:]], axis=0)
        n_w = kw_all.shape[0]
        w_row = lax.broadcasted_iota(jnp.int32, (n_w, 1), 0)
        w_pos = past - n_wb + w_row
        d = q_pos - w_pos
        ow_ref[...] = attend_once(kw_all, vw_all,
                                  (d >= 0) & (d < NSA_WINDOW) & (w_pos >= 0) & (w_row < n_wb + n_new))
        m_ref[...] = jnp.full(m_ref.shape, NEG_BIG, F32)
        l_ref[...] = jnp.zeros(l_ref.shape, F32)
        acc_ref[...] = jnp.zeros(acc_ref.shape, F32)

    def page_part(part):
        return jnp.concatenate([heads_on_lanes(pg_ref, PAGE, part) for pg_ref in pg_refs], axis=0)

    blk_per_step = per_step * PAGE // NSA_BLOCK
    online(page_part(0), page_part(1), jnp.concatenate(
        [jnp.broadcast_to(sel_ref[pl.ds(p * blk_per_step + j, 1), :], (NSA_BLOCK, cols))
         for j in range(blk_per_step)], axis=0))

    @pl.when(p == n_steps - 1)
    def _():
        kvn = kvn_ref[0]
        n_row = lax.broadcasted_iota(jnp.int32, (kvn.shape[0], 1), 0)
        bias_new = jnp.where((n_row < n_new) & (past + n_row <= q_pos), 0.0, NEG_BIG)
        online(kvn[:, :k_w], kvn[:, k_w:], bias_new + sel_ref[pl.ds(past // NSA_BLOCK, 1), :])
        o_s = acc_ref[...] / jnp.maximum(l_ref[...], TINY)
        gates = jax.nn.sigmoid(gt_ref[0] + bg_ref[...])
        o_t = gates[0:1, :] * oc_ref[...] + gates[1:2, :] * o_s + gates[2:3, :] * ow_ref[...]
        eye = _ind(lax.broadcasted_iota(jnp.int32, (NSA_HD, NSA_HD), 0)
                   == lax.broadcasted_iota(jnp.int32, (NSA_HD, NSA_HD), 1), BF16)
        o_ref[0] = sum(lax.dot_general(t, eye, (((0,), (0,)), ((), ())), preferred_element_type=F32)
                       for t in _split3(o_t))


def _nsa_sample(q_rows, cmp_means, cache_slc, page_table, kv_new, win_buf, kw_new, gates, b_gate,
                past, n_new):
    seqs, n_pages = page_table.shape
    cols = q_rows.shape[1]
    n_cmp = cmp_means.shape[1]
    n_sel_pad = -(-(-(-(past + n_new) // NSA_BLOCK)) // 8) * 8
    n_wb = win_buf.shape[1]
    pad_new = kv_new.shape[1]
    per_step = 8
    n_steps = n_pages // per_step
    seq_map = lambda b, p, pt: (b, 0, 0)
    return pl.pallas_call(
        functools.partial(_nsa_sample_kernel, past=past, n_new=n_new, n_steps=n_steps, per_step=per_step),
        grid_spec=pltpu.PrefetchScalarGridSpec(
            num_scalar_prefetch=1,
            grid=(seqs, n_steps),
            in_specs=[
                pl.BlockSpec((1, cols, NSA_HD), seq_map),
                pl.BlockSpec((1, n_cmp, NSA_KV_W), seq_map),
                *_page_specs(per_step),
                pl.BlockSpec((1, pad_new, NSA_KV_W), seq_map),
                pl.BlockSpec((1, n_wb, NSA_HD), seq_map),
                pl.BlockSpec((1, pad_new, NSA_KV_W), seq_map),
                pl.BlockSpec((1, 3, cols), seq_map),
                pl.BlockSpec((3, cols), lambda b, p, pt: (0, 0)),
            ],
            out_specs=pl.BlockSpec((1, cols, NSA_HD), seq_map),
            scratch_shapes=[
                pltpu.VMEM((cols, NSA_KV * NSA_HD), BF16),
                pltpu.VMEM((n_sel_pad, cols), F32),
                pltpu.VMEM((1, cols), F32),
                pltpu.VMEM((1, cols), F32),
                pltpu.VMEM((NSA_HD, cols), F32),
                pltpu.VMEM((NSA_HD, cols), F32),
                pltpu.VMEM((NSA_HD, cols), F32),
            ],
        ),
        out_shape=jax.ShapeDtypeStruct((seqs, cols, NSA_HD), F32),
        compiler_params=_params("parallel", "arbitrary"),
        name="nsa_sample",
    )(page_table, q_rows, cmp_means, *([cache_slc] * per_step), kv_new, win_buf, kw_new, gates, b_gate)


def kernel(x_prompt, x_sample, c_prompt, c_sample, state_gla, cache_cmp_kv, cache_slc_kv, cache_win_kv,
           page_table, ada_w, ada_b, gla_w_in, gla_w_alpha, gla_b_alpha, gla_norm_g, gla_w_out, nsa_w_in,
           nsa_b_gate, nsa_w_out, ln_mix_g, ln_mix_b, ffn_w_in, ffn_w_out, ln_ffn_g, ln_ffn_b):
    batch, seq, _ = x_prompt.shape
    seqs, n_new, _ = x_sample.shape
    n_pages = page_table.shape[1]
    past = n_pages * PAGE
    rows_s = seqs * n_new
    tm_p = seq
    w_out_mix = (gla_w_out.astype(BF16), nsa_w_out.astype(BF16))
    w_out_ffn = ffn_w_out.astype(BF16)

    pad_rows = 16 - (batch + seqs)
    c_all = jnp.concatenate([c_prompt, c_sample, jnp.zeros((pad_rows, D_MODEL), F32)], axis=0)
    mod = _adaln(c_all, ada_w, ada_b).reshape(DEPTH, 16, 6, D_MODEL)

    def mods(layer):
        mp = [mod[layer, :batch, i].reshape(batch, 1, D_MODEL) for i in range(6)]
        ms = [jnp.repeat(mod[layer, batch:batch + seqs, i], n_new, axis=0).reshape(1, rows_s, D_MODEL)
              for i in range(6)]
        return mp, ms

    def mixer_out(o, x, layer, m, tm, tag):
        return _out_ln(o, w_out_mix[layer % 2], layer // 2, x, m[2], ln_mix_g, ln_mix_b, layer, min(tm, 512), tag)

    def ffn(xp, xs, layer, mp, ms):
        act_p, act_s = _ffn_in(xp, mp[4], mp[3], xs, ms[4], ms[3], ffn_w_in, layer, tm_p, "ffn_in_%d" % layer)
        out = lambda act, x, m, tm, tag: _out_ln(act, w_out_ffn, layer, x, m[5], ln_ffn_g, ln_ffn_b, layer, tm,
                                                  "ffn_out_%s%d" % (tag, layer))
        return out(act_p, xp, mp, 256, "prompt"), out(act_s, xs, ms, rows_s, "sample")

    xp = x_prompt.reshape(batch * seq, D_MODEL)
    xs = x_sample.reshape(rows_s, D_MODEL)

    mp, ms = mods(0)
    main_p, tail_p, main_s, tail_s = _proj(xp, mp[1], mp[0], xs, ms[1], ms[0], gla_w_in.astype(BF16), 0,
                                           GLA_MAIN_W, tm_p, "gla_in")
    gated_p, gla_state_p = _gla(main_p, tail_p, gla_w_alpha[0], gla_b_alpha[0], gla_norm_g[0], None,
                                batch, seq, 128, 128)
    chunk_s = GLA_SUB

    def pad_steps(a):
        a = a.reshape(seqs, n_new, a.shape[-1])
        return jnp.pad(a, ((0, 0), (0, chunk_s - n_new), (0, 0))).reshape(seqs * chunk_s, a.shape[-1])

    gated_s, gla_state_s = _gla(pad_steps(main_s), pad_steps(tail_s), gla_w_alpha[0], gla_b_alpha[0],
                                gla_norm_g[0], state_gla[:, 0], seqs, chunk_s, chunk_s, n_new)
    gated_s = gated_s.reshape(seqs, chunk_s, -1)[:, :n_new].reshape(rows_s, -1)
    xp = mixer_out(gated_p, xp, 0, mp, tm_p, "gla_out_prompt")
    xs = mixer_out(gated_s, xs, 0, ms, rows_s, "gla_out_sample")
    xp, xs = ffn(xp, xs, 0, mp, ms)

    mp, ms = mods(1)
    main_p, tail_p, main_s, tail_s = _proj(xp, mp[1], mp[0], xs, ms[1], ms[0], nsa_w_in.astype(BF16), 0,
                                           NSA_MAIN_W, tm_p, "nsa_in")
    n_gate = 3 * NSA_HEADS
    g_w = 3 * NSA_GROUP
    gates_p = tail_p[:, :n_gate].reshape(batch * seq, NSA_KV, g_w).transpose(1, 2, 0)
    b_gate_p = nsa_b_gate[0].reshape(NSA_KV, g_w, 1)
    n_phys = cache_cmp_kv.shape[0]
    page_rows = lambda cache: cache.reshape(n_phys, PAGE * KV_ROWS, NSA_HD)
    o_p, cmp_means = _nsa_prompt(main_p, gates_p, b_gate_p, batch, seq, page_rows(cache_cmp_kv), page_table)

    kv_rows = lambda m, br: m[:, NSA_Q_W + br * NSA_KV_W:NSA_Q_W + (br + 1) * NSA_KV_W]
    kv_shape = (2, NSA_KV, NSA_HD)
    keep_p = min(NSA_WINDOW, seq)
    cmp_rows, slc_rows, win_rows = _kv_rows_out(main_p, batch, seq, keep_p)
    cmp_kv_p = cmp_rows.reshape(batch, 1, seq, *kv_shape)
    slc_kv_p = slc_rows.reshape(batch, 1, seq, *kv_shape)
    win_kv_p = win_rows.reshape(batch, 1, keep_p, *kv_shape)

    cmp_kv_s = kv_rows(main_s, 0).reshape(seqs, 1, n_new, *kv_shape)
    slc_kv_s = kv_rows(main_s, 1).reshape(seqs, 1, n_new, *kv_shape)
    kw_new = kv_rows(main_s, 2).reshape(seqs, n_new, NSA_KV_W)
    n_wb = cache_win_kv.shape[2]
    win_buf = cache_win_kv[:, 0].reshape(seqs, n_wb * KV_ROWS, NSA_HD)
    all_w = jnp.concatenate([win_buf, kw_new.reshape(seqs, n_new * KV_ROWS, NSA_HD)], axis=1)
    keep_s = min(NSA_WINDOW, n_wb + n_new)
    win_kv_s = all_w[:, (n_wb + n_new - keep_s) * KV_ROWS:].reshape(seqs, 1, keep_s, *kv_shape)

    cmp_means = cmp_means.reshape(seqs, past // NSA_BLOCK, NSA_KV_W)
    pad_new = 8
    pad8 = lambda a: jnp.pad(a, ((0, 0), (0, pad_new - n_new), (0, 0)))
    cols = NSA_KV * n_new * NSA_GROUP
    q_rows = (main_s[:, :NSA_Q_W].reshape(seqs, n_new, NSA_KV, NSA_GROUP, NSA_HD)
              .transpose(0, 2, 1, 3, 4).reshape(seqs, cols, NSA_HD))
    gates_s = (tail_s[:, :n_gate].reshape(seqs, n_new, NSA_KV, NSA_GROUP, 3)
               .transpose(0, 4, 2, 1, 3).reshape(seqs, 3, cols))
    b_gate_s = jnp.broadcast_to(nsa_b_gate[0].reshape(1, NSA_KV, NSA_GROUP, 3), (n_new, NSA_KV, NSA_GROUP, 3))
    b_gate_s = b_gate_s.transpose(3, 1, 0, 2).reshape(3, cols)
    o_s = _nsa_sample(q_rows, cmp_means, page_rows(cache_slc_kv), page_table,
                      pad8(slc_kv_s.reshape(seqs, n_new, NSA_KV_W)), win_buf, pad8(kw_new), gates_s, b_gate_s,
                      past, n_new)
    o_s = (o_s.reshape(seqs, NSA_KV, n_new, NSA_GROUP, NSA_HD).transpose(0, 2, 1, 3, 4)
           .reshape(rows_s, NSA_Q_W).astype(BF16))

    xp = mixer_out(o_p, xp, 1, mp, tm_p, "nsa_out_prompt")
    xs = mixer_out(o_s, xs, 1, ms, rows_s, "nsa_out_sample")
    xp, xs = ffn(xp, xs, 1, mp, ms)

    return (xp.reshape(batch, seq, D_MODEL), xs.reshape(seqs, n_new, D_MODEL),
            gla_state_p[:, None], gla_state_s[:, None], cmp_kv_p, cmp_kv_s, slc_kv_p, slc_kv_s,
            win_kv_p, win_kv_s)
```

```python
import functools

import jax
import jax.numpy as jnp
from jax import lax
from jax.experimental import pallas as pl
from jax.experimental.pallas import tpu as pltpu

F32 = jnp.float32
BF16 = jnp.bfloat16

D_MODEL = 2048
DEPTH = 2
DN_ALPHA = (2 * DEPTH) ** 0.25
LN_EPS = 1e-5
D_FF = 5632
GLA_HEADS = 4
GLA_DK = 256
GLA_DV = 512
GLA_RANK = 16
GLA_TAU = 16.0
GLA_SUB = 32
GLA_MAIN_W = 2 * GLA_HEADS * GLA_DK + 2 * GLA_HEADS * GLA_DV
NSA_HEADS = 16
NSA_KV = 4
NSA_HD = 128
NSA_GROUP = NSA_HEADS // NSA_KV
NSA_BLOCK = 64
NSA_TOPK = 16
NSA_WINDOW = 512
NSA_Q_W = NSA_HEADS * NSA_HD
NSA_KV_W = 2 * NSA_KV * NSA_HD
NSA_MAIN_W = NSA_Q_W + 3 * NSA_KV_W
PAGE = 128
KV_ROWS = 2 * NSA_KV

LANES = 128
BF16_ROWS = 16
LOG2_E = 1.4426950408889634
V7X_VMEM_BYTES = 64 * 1024 * 1024
VMEM_LIMIT = V7X_VMEM_BYTES * 7 // 8

NEG_BIG = -1e30
TINY = float(jnp.finfo(jnp.float32).tiny)


def _params(*sem):
    return pltpu.CompilerParams(dimension_semantics=sem, vmem_limit_bytes=VMEM_LIMIT)


def _dot(a, b):
    return jnp.dot(a.astype(BF16), b.astype(BF16), preferred_element_type=F32)


def _dot_nt(a, b):
    return lax.dot_general(a.astype(BF16), b.astype(BF16), (((1,), (1,)), ((), ())),
                           preferred_element_type=F32)


def _dot_tn(a, b):
    return lax.dot_general(a.astype(BF16), b.astype(BF16), (((0,), (0,)), ((), ())),
                           preferred_element_type=F32)


def _split3(x):
    hi = x.astype(BF16)
    r1 = x - hi.astype(F32)
    mid = r1.astype(BF16)
    lo = (r1 - mid.astype(F32)).astype(BF16)
    return hi, mid, lo


def _silu(x):
    return x * jax.nn.sigmoid(x)


def _ind(cond, dtype=F32):
    return jnp.where(cond, 1.0, 0.0).astype(dtype)


def _masked_softmax2(s2, mask, axis):
    s2 = jnp.where(mask, s2, -jnp.inf)
    m = jnp.max(s2, axis=axis, keepdims=True)
    m = jnp.where(jnp.isfinite(m), m, 0.0)
    e = jnp.where(mask, jnp.exp2(s2 - m), 0.0)
    return e / jnp.maximum(jnp.sum(e, axis=axis, keepdims=True), TINY)


def _adaln_kernel(c_ref, w_ref, b_ref, o_ref):
    o_ref[...] = _dot(_silu(c_ref[...]), w_ref[...]) + b_ref[...]


def _adaln(c_all, ada_w, ada_b):
    rows = c_all.shape[0]
    n = ada_w.shape[2]
    tn = 1024
    return pl.pallas_call(
        _adaln_kernel,
        grid=(DEPTH, n // tn),
        in_specs=[
            pl.BlockSpec((rows, D_MODEL), lambda l, j: (0, 0)),
            pl.BlockSpec((None, D_MODEL, tn), lambda l, j: (l, 0, j)),
            pl.BlockSpec((None, 1, tn), lambda l, j: (l, 0, j)),
        ],
        out_specs=pl.BlockSpec((None, rows, tn), lambda l, j: (l, 0, j)),
        out_shape=jax.ShapeDtypeStruct((DEPTH, rows, n), F32),
        compiler_params=_params("parallel", "parallel"),
        name="adaln",
    )(c_all, ada_w, ada_b.reshape(DEPTH, 1, n))


def _modulate(x_ref, sc_ref, sh_ref):
    return (x_ref[...] * (1.0 + sc_ref[...]) + sh_ref[...]).astype(BF16)


def _proj_kernel(x_ref, sc_ref, sh_ref, xs_ref, scs_ref, shs_ref, w_ref, wt_ref,
                 o_ref, ot_ref, os_ref, ots_ref, h_ref, *, n_i):
    i, j = pl.program_id(0), pl.program_id(1)
    tm = x_ref.shape[0]

    @pl.when(j == 0)
    def _():
        h_ref[:tm] = _modulate(x_ref, sc_ref, sh_ref)
        ot_ref[...] = _dot(h_ref[:tm], wt_ref[...])

    @pl.when(i < n_i - 1)
    def _():
        o_ref[...] = _dot(h_ref[:tm], w_ref[...])

    @pl.when(i == n_i - 1)
    def _():
        @pl.when(j == 0)
        def _():
            h_ref[tm:] = _modulate(xs_ref, scs_ref, shs_ref)
            ots_ref[...] = _dot(h_ref[tm:], wt_ref[...])

        both = _dot(h_ref[...], w_ref[...])
        o_ref[...] = both[:tm]
        os_ref[...] = both[tm:]


def _proj(x, sc, sh, xs, scs, shs, w, layer, n_main, tm, name):
    m, ms = x.shape[0], xs.shape[0]
    n_tail = w.shape[2] - n_main
    w_tail = jnp.pad(w[layer, :, n_main:], ((0, 0), (0, LANES - n_tail)))
    tn = 512
    n_i = m // tm
    tiles_per_group = n_i // sc.shape[0]
    mod_spec = pl.BlockSpec((None, sc.shape[1], D_MODEL), lambda i, j: (i // tiles_per_group, 0, 0))
    const2 = lambda i, j: (0, 0)
    const3 = lambda i, j: (0, 0, 0)
    last_tile_col = lambda i, j: (0, jnp.where(i == n_i - 1, j, 0))
    return pl.pallas_call(
        functools.partial(_proj_kernel, n_i=n_i),
        grid=(n_i, n_main // tn),
        in_specs=[
            pl.BlockSpec((tm, D_MODEL), lambda i, j: (i, 0), pipeline_mode=pl.Buffered(1)),
            mod_spec,
            mod_spec,
            pl.BlockSpec((ms, D_MODEL), const2),
            pl.BlockSpec((None, ms, D_MODEL), const3),
            pl.BlockSpec((None, ms, D_MODEL), const3),
            pl.BlockSpec((None, D_MODEL, tn), lambda i, j: (layer, 0, j)),
            pl.BlockSpec((D_MODEL, LANES), const2),
        ],
        out_specs=[
            pl.BlockSpec((tm, tn), lambda i, j: (i, j)),
            pl.BlockSpec((tm, LANES), lambda i, j: (i, 0)),
            pl.BlockSpec((ms, tn), last_tile_col),
            pl.BlockSpec((ms, LANES), const2),
        ],
        out_shape=[jax.ShapeDtypeStruct((m, n_main), F32), jax.ShapeDtypeStruct((m, LANES), F32),
                   jax.ShapeDtypeStruct((ms, n_main), F32), jax.ShapeDtypeStruct((ms, LANES), F32)],
        scratch_shapes=[pltpu.VMEM((tm + ms, D_MODEL), BF16)],
        compiler_params=_params("arbitrary", "arbitrary"),
        name=name,
    )(x, sc, sh, xs, scs, shs, w, w_tail)


def _out_ln_kernel(a_ref, w_ref, x_ref, gt_ref, g_ref, b_ref, o_ref, *, n_sub):
    tm = a_ref.shape[0]
    rows = [slice(s * (tm // n_sub), (s + 1) * (tm // n_sub)) for s in range(n_sub)]
    proj = [jnp.dot(a_ref[rs, :], w_ref[...], preferred_element_type=F32) for rs in rows]
    for rs, m in zip(rows, proj):
        gate = gt_ref[...] if gt_ref.shape[0] == 1 else gt_ref[rs, :]
        y = DN_ALPHA * x_ref[rs, :] + gate * m
        mu = jnp.mean(y, axis=-1, keepdims=True)
        yc = y - mu
        var = jnp.mean(yc * yc, axis=-1, keepdims=True)
        o_ref[rs, :] = yc * lax.rsqrt(var + LN_EPS) * g_ref[...] + b_ref[...]


def _out_ln(a, w, w_layer, x, gate, ln_g, ln_b, ln_layer, tm, name):
    m, kdim = a.shape
    groups, mod_rows = gate.shape[0], gate.shape[1]
    tiles_per_group = m // tm // groups
    vec_spec = pl.BlockSpec((None, 1, D_MODEL), lambda i: (ln_layer, 0, 0))
    return pl.pallas_call(
        functools.partial(_out_ln_kernel, n_sub=2 if tm % 512 == 0 else 1),
        grid=(m // tm,),
        in_specs=[
            pl.BlockSpec((tm, kdim), lambda i: (i, 0)),
            pl.BlockSpec((None, kdim, D_MODEL), lambda i: (w_layer, 0, 0), pipeline_mode=pl.Buffered(1)),
            pl.BlockSpec((tm, D_MODEL), lambda i: (i, 0)),
            pl.BlockSpec((None, mod_rows, D_MODEL), lambda i: (i // tiles_per_group, 0, 0)),
            vec_spec,
            vec_spec,
        ],
        out_specs=pl.BlockSpec((tm, D_MODEL), lambda i: (i, 0)),
        out_shape=jax.ShapeDtypeStruct((m, D_MODEL), F32),
        compiler_params=_params("parallel"),
        name=name,
    )(a, w, x, gate, ln_g.reshape(-1, 1, D_MODEL), ln_b.reshape(-1, 1, D_MODEL))


def _ffn_in_kernel(x_ref, sc_ref, sh_ref, xs_ref, scs_ref, shs_ref, wa_ref, wu_ref,
                   o_ref, os_ref, h_ref, *, n_i):
    i, j = pl.program_id(0), pl.program_id(1)
    tm = x_ref.shape[0]

    @pl.when(j == 0)
    def _():
        h_ref[:tm] = _modulate(x_ref, sc_ref, sh_ref)

    wa = wa_ref[...].astype(BF16)
    wu = wu_ref[...].astype(BF16)

    def swiglu_in(h):
        return (_silu(_dot(h, wa)) * _dot(h, wu)).astype(BF16)

    @pl.when(i < n_i - 1)
    def _():
        o_ref[...] = swiglu_in(h_ref[:tm])

    @pl.when(i == n_i - 1)
    def _():
        @pl.when(j == 0)
        def _():
            h_ref[tm:] = _modulate(xs_ref, scs_ref, shs_ref)

        both = swiglu_in(h_ref[...])
        o_ref[...] = both[:tm]
        os_ref[...] = both[tm:]


def _ffn_in(x, sc, sh, xs, scs, shs, w, layer, tm, name):
    m, ms = x.shape[0], xs.shape[0]
    tn = 256
    nj = D_FF // tn
    n_i = m // tm
    tiles_per_group = n_i // sc.shape[0]
    mod_spec = pl.BlockSpec((None, sc.shape[1], D_MODEL), lambda i, j: (i // tiles_per_group, 0, 0))
    const3 = lambda i, j: (0, 0, 0)
    return pl.pallas_call(
        functools.partial(_ffn_in_kernel, n_i=n_i),
        grid=(n_i, nj),
        in_specs=[
            pl.BlockSpec((tm, D_MODEL), lambda i, j: (i, 0), pipeline_mode=pl.Buffered(1)),
            mod_spec,
            mod_spec,
            pl.BlockSpec((ms, D_MODEL), lambda i, j: (0, 0)),
            pl.BlockSpec((None, ms, D_MODEL), const3),
            pl.BlockSpec((None, ms, D_MODEL), const3),
            pl.BlockSpec((None, D_MODEL, tn), lambda i, j: (layer, 0, j)),
            pl.BlockSpec((None, D_MODEL, tn), lambda i, j: (layer, 0, j + nj)),
        ],
        out_specs=[
            pl.BlockSpec((tm, tn), lambda i, j: (i, j)),
            pl.BlockSpec((ms, tn), lambda i, j: (0, jnp.where(i == n_i - 1, j, 0))),
        ],
        out_shape=[jax.ShapeDtypeStruct((m, D_FF), BF16), jax.ShapeDtypeStruct((ms, D_FF), BF16)],
        scratch_shapes=[pltpu.VMEM((tm + ms, D_MODEL), BF16)],
        compiler_params=_params("arbitrary", "arbitrary"),
        name=name,
    )(x, sc, sh, xs, scs, shs, w, w)


def _gla_kernel(*refs, chunk, t_valid, has_s0):
    if has_s0:
        q_ref, k_ref, v_ref, r_ref, a_ref, wa_ref, ba_ref, g_ref, s0_ref, o_ref, st_ref = refs
    else:
        q_ref, k_ref, v_ref, r_ref, a_ref, wa_ref, ba_ref, g_ref, o_ref, st_ref = refs
    sub = GLA_SUB
    n_sub = chunk // sub
    anchor = sub // 2 - 1

    @pl.when(pl.program_id(1) == 0)
    def _():
        if has_s0:
            st_ref[...] = s0_ref[...]
        else:
            st_ref[...] = jnp.zeros(st_ref.shape, F32)

    row = lax.broadcasted_iota(jnp.int32, (chunk, 1), 0)
    tri = _ind(lax.broadcasted_iota(jnp.int32, (chunk, chunk), 1)
               <= lax.broadcasted_iota(jnp.int32, (chunk, chunk), 0), BF16)
    ones = jnp.ones((chunk, LANES), BF16)
    causal = (lax.broadcasted_iota(jnp.int32, (sub, sub), 1)
              <= lax.broadcasted_iota(jnp.int32, (sub, sub), 0))
    heads = range(GLA_HEADS)
    dk = [slice(h * GLA_DK, (h + 1) * GLA_DK) for h in heads]
    dv = [slice(h * GLA_DV, (h + 1) * GLA_DV) for h in heads]
    subs = [slice(s * sub, (s + 1) * sub) for s in range(n_sub)]

    z = _dot(a_ref[...], wa_ref[...]) + ba_ref[...]
    lb = (jnp.minimum(z, 0.0) - jnp.log1p(jnp.exp(-jnp.abs(z)))) * (1.0 / GLA_TAU)
    k = k_ref[...]
    if t_valid < chunk:
        lb = jnp.where(row < t_valid, lb, 0.0)
        k = jnp.where(row < t_valid, k, 0.0)
    q = q_ref[...] * (GLA_DK ** -0.5)
    vb = v_ref[...].astype(BF16)
    lb3 = _split3(lb)
    b = sum(jnp.dot(tri, t, preferred_element_type=F32) for t in lb3)
    dec = jnp.exp(sum(lax.dot_general(t, ones, (((0,), (0,)), ((), ())), preferred_element_type=F32)
                      for t in lb3))
    b_last = b[chunk - 1:chunk, :]
    q_in = (q * jnp.exp(b)).astype(BF16)
    k_out = (k * jnp.exp(b_last - b)).astype(BF16)
    q_diag, k_diag, q_off, k_off = [], [], [], []
    for s, rs in enumerate(subs):
        b_mid = b[s * sub + anchor:s * sub + anchor + 1, :]
        q_diag.append((q[rs] * jnp.exp(b[rs] - b_mid)).astype(BF16))
        k_diag.append((k[rs] * jnp.exp(b_mid - b[rs])).astype(BF16))
        if s > 0:
            prev = slice(0, s * sub)
            b_in = b[s * sub - 1:s * sub, :]
            q_off.append((q[rs] * jnp.exp(b[rs] - b_in)).astype(BF16))
            k_off.append((k[prev] * jnp.exp(b_in - b[prev])).astype(BF16))

    s_old = [st_ref[0, h] for h in heads]
    o_inter = [_dot(q_in[:, dk[h]], s_old[h]) for h in heads]
    a_diag = [[_dot_nt(q_diag[s][:, dk[h]], k_diag[s][:, dk[h]]) for h in heads] for s in range(n_sub)]
    a_off = [[_dot_nt(q_off[s][:, dk[h]], k_off[s][:, dk[h]]) for h in heads] for s in range(n_sub - 1)]
    a_diag = [[jnp.where(causal, a, 0.0).astype(BF16) for a in row_] for row_ in a_diag]
    a_off = [[a.astype(BF16) for a in row_] for row_ in a_off]
    o_intra = []
    for s, rs in enumerate(subs):
        o_s = [_dot(a_diag[s][h], vb[rs, dv[h]]) for h in heads]
        if s > 0:
            o_s = [o_s[h] + _dot(a_off[s - 1][h], vb[0:s * sub, dv[h]]) for h in heads]
        o_intra.append(o_s)
    kv = [_dot_tn(k_out[:, dk[h]], vb[:, dv[h]]) for h in heads]
    for h in heads:
        st_ref[0, h] = s_old[h] * jnp.tile(dec[dk[h], :], (1, GLA_DV // LANES)) + kv[h]
    for h in heads:
        o = o_inter[h] + jnp.concatenate([o_intra[s][h] for s in range(n_sub)], axis=0)
        o = o * lax.rsqrt(jnp.mean(o * o, axis=-1, keepdims=True) + LN_EPS)
        o_ref[:, dv[h]] = (o * g_ref[:, dv[h]] * _silu(r_ref[:, dv[h]])).astype(BF16)


def _gla(main, tail, w_alpha, b_alpha, norm_g, s0, batch, seq, chunk, t_valid):
    n_chunks = seq // chunk
    qk_w = GLA_HEADS * GLA_DK
    v_w = GLA_HEADS * GLA_DV
    wa = jnp.pad(w_alpha, ((0, LANES - GLA_RANK), (0, 0)))
    row_map = lambda b, c: (b * n_chunks + c, 0)
    const = lambda b, c: (0, 0)
    in_specs = [
        pl.BlockSpec((chunk, qk_w), row_map),
        pl.BlockSpec((chunk, qk_w), lambda b, c: (b * n_chunks + c, 1)),
        pl.BlockSpec((chunk, v_w), lambda b, c: (b * n_chunks + c, 1)),
        pl.BlockSpec((chunk, v_w), lambda b, c: (b * n_chunks + c, 2)),
        pl.BlockSpec((chunk, LANES), row_map),
        pl.BlockSpec((LANES, qk_w), const),
        pl.BlockSpec((1, qk_w), const),
        pl.BlockSpec((1, v_w), const),
    ]
    args = [main, main, main, main, tail, wa, b_alpha.reshape(1, qk_w), norm_g.reshape(1, v_w)]
    state_spec = pl.BlockSpec((1, GLA_HEADS, GLA_DK, GLA_DV), lambda b, c: (b, 0, 0, 0))
    if s0 is not None:
        in_specs.append(state_spec)
        args.append(s0)
    return pl.pallas_call(
        functools.partial(_gla_kernel, chunk=chunk, t_valid=t_valid, has_s0=s0 is not None),
        grid=(batch, n_chunks),
        in_specs=in_specs,
        out_specs=[pl.BlockSpec((chunk, v_w), row_map), state_spec],
        out_shape=[jax.ShapeDtypeStruct((batch * seq, v_w), BF16),
                   jax.ShapeDtypeStruct((batch, GLA_HEADS, GLA_DK, GLA_DV), F32)],
        compiler_params=_params("parallel", "arbitrary"),
        name="gla_prompt" if s0 is None else "gla_sample",
    )(*args)


def _select_blocks(imp_t, blk, cur):
    n = imp_t.shape[0]
    forced = (blk == 0) | ((blk >= cur - 1) & (blk <= cur))
    score = jnp.where(blk > cur, -jnp.inf, jnp.where(forced, jnp.inf, imp_t))
    rank = jnp.zeros(score.shape, F32)
    for i in range(n):
        si = score[i:i + 1, :]
        rank = rank + jnp.where(blk > i, _ind(si >= score), _ind(si > score))
    return (rank < NSA_TOPK) & (score > -jnp.inf)


def _nsa_prompt_kernel(pt_ref, q_ref, kc_ref, vc_ref, ks_ref, vs_ref, kw_ref, vw_ref, gt_ref, bg_ref, *refs,
                       seq, tq, tk, pages_per_step):
    page_refs = refs[:pages_per_step]
    (o_ref, means_ref, kcm, vcm, ksb, vst, kwb, vwt, qs_ref, sel_ref,
     ms_ref, accs_ref, mw_ref, accw_ref) = refs[pages_per_step:]
    qi = pl.program_id(2)
    n_blk = seq // NSA_BLOCK
    n_kt = seq // tk
    blk_per_kt = tk // NSA_BLOCK

    @pl.when(qi == 0)
    def _():
        kcm[...] = jnp.sum(kc_ref[...].reshape(n_blk, NSA_BLOCK, NSA_HD), axis=1) * (1.0 / NSA_BLOCK)
        vcm[...] = jnp.sum(vc_ref[...].reshape(n_blk, NSA_BLOCK, NSA_HD), axis=1) * (1.0 / NSA_BLOCK)
        ksb[...] = ks_ref[...].astype(BF16)
        kwb[...] = kw_ref[...].astype(BF16)
        ones_rows = _ind(lax.broadcasted_iota(jnp.int32, (BF16_ROWS, tk), 0) == 0, BF16)
        for kt in range(n_kt):
            rows = slice(kt * tk, (kt + 1) * tk)
            vst[kt] = jnp.concatenate([vs_ref[rows, :].T.astype(BF16), ones_rows], axis=0)
            vwt[kt] = jnp.concatenate([vw_ref[rows, :].T.astype(BF16), ones_rows], axis=0)

    t0 = qi * tq
    q = q_ref[...] * (NSA_HD ** -0.5 * LOG2_E)
    qs_ref[...] = jnp.concatenate(
        [q[:, g * NSA_HD:(g + 1) * NSA_HD] for g in range(NSA_GROUP)], axis=0).astype(BF16)
    qs = qs_ref[...]
    t_row = t0 + lax.broadcasted_iota(jnp.int32, (1, tq), 1)
    t_row_g = jnp.concatenate([t_row] * NSA_GROUP, axis=1)

    kc_b = kcm[...].astype(BF16)
    blk_col = lax.broadcasted_iota(jnp.int32, (n_blk, 1), 0)
    p_t = _masked_softmax2(_dot_nt(kc_b, qs), (blk_col + 1) * NSA_BLOCK - 1 <= t_row_g, axis=0)
    o_c = _dot_tn(vcm[...], p_t)
    imp_t = sum(p_t[:, g * tq:(g + 1) * tq] for g in range(NSA_GROUP))
    sel_ref[...] = jnp.where(_select_blocks(imp_t, blk_col, t_row // NSA_BLOCK), 0.0, NEG_BIG)

    n_loc = lax.broadcasted_iota(jnp.int32, (tk, tq), 0)
    t_loc = lax.broadcasted_iota(jnp.int32, (tk, tq), 1)
    on_lanes = lambda a: jnp.concatenate([a] * NSA_GROUP, axis=1)
    causal = on_lanes(jnp.where(n_loc <= t_loc, 0.0, NEG_BIG))
    far = on_lanes(jnp.where(n_loc > t_loc, 0.0, NEG_BIG))

    def sel_bias(kt):
        return on_lanes(jnp.concatenate(
            [jnp.broadcast_to(sel_ref[pl.ds(kt * blk_per_kt + j, 1), :], (NSA_BLOCK, tq))
             for j in range(blk_per_kt)], axis=0))

    sel_state, win_state = (ms_ref, accs_ref), (mw_ref, accw_ref)
    for m_ref, acc_ref in (sel_state, win_state):
        m_ref[...] = jnp.full(m_ref.shape, NEG_BIG, F32)
        acc_ref[...] = jnp.zeros(acc_ref.shape, F32)

    def steps(*work):
        s = [_dot_nt(k_ref[pl.ds(pl.multiple_of(kt * tk, tk), tk), :], qs_ref[...]) + bias
             for _, k_ref, _, kt, bias in work]
        m_old = [state[0][...] for state, *_ in work]
        m_new = [jnp.maximum(mo, jnp.max(si, axis=0, keepdims=True)) for mo, si in zip(m_old, s)]
        p = [jnp.exp2(si - mn).astype(BF16) for si, mn in zip(s, m_new)]
        pv = [_dot(vt_ref[kt], pi) for (_, _, vt_ref, kt, _), pi in zip(work, p)]
        for (state, *_), mo, mn, pvi in zip(work, m_old, m_new, pv):
            state[1][...] = jnp.exp2(mo - mn) * state[1][...] + pvi
            state[0][...] = mn

    def result(acc_ref):
        acc = acc_ref[...]
        return acc[:NSA_HD] / jnp.maximum(acc[NSA_HD:NSA_HD + 1], TINY)

    steps((sel_state, ksb, vst, qi, sel_bias(qi) + causal), (win_state, kwb, vwt, qi, causal))

    @pl.when(qi >= 1)
    def _():
        steps((sel_state, ksb, vst, qi - 1, sel_bias(qi - 1)), (win_state, kwb, vwt, qi - 1, far))

    def sel_body(i, carry):
        kt = qi - 2 - i
        steps((sel_state, ksb, vst, kt, sel_bias(kt)))
        return carry

    lax.fori_loop(0, qi - 1, sel_body, 0)

    gates = jax.nn.sigmoid(gt_ref[...] + bg_ref[...])
    g_c, g_s, g_w = (jnp.concatenate([gates[3 * g + br:3 * g + br + 1, :] for g in range(NSA_GROUP)], axis=1)
                     for br in range(3))
    o_t = g_c * o_c + g_s * result(accs_ref) + g_w * result(accw_ref)
    for g in range(NSA_GROUP):
        o_ref[:, g * NSA_HD:(g + 1) * NSA_HD] = o_t[:, g * tq:(g + 1) * tq].T.astype(BF16)

    for i, x_ref in enumerate(page_refs):
        x = x_ref[0].reshape(PAGE // NSA_BLOCK, NSA_BLOCK, KV_ROWS, NSA_HD)
        means_ref[0, i] = jnp.sum(x, axis=1) * (1.0 / NSA_BLOCK)


def _nsa_prompt(main, gates, b_gate, batch, seq, cache_cmp, page_table):
    tq = tk = NSA_WINDOW
    assert seq % tk == 0
    vt_rows = NSA_HD + BF16_ROWS
    nq = seq // tq
    q_lanes = NSA_GROUP * NSA_HD
    col0 = NSA_Q_W // NSA_HD
    seqs, n_pages = page_table.shape
    n_steps = batch * NSA_KV * nq
    pages_per_step = seqs * n_pages // n_steps
    groups = n_pages // pages_per_step
    assert pages_per_step * n_steps == seqs * n_pages and groups * pages_per_step == n_pages
    per_page = PAGE // NSA_BLOCK

    def kv_spec(branch, part):
        off = col0 + (2 * branch + part) * NSA_KV
        return pl.BlockSpec((seq, NSA_HD), lambda b, kv, qi, pt: (b, off + kv))

    def step_of(b, kv, qi):
        return (b * NSA_KV + kv) * nq + qi

    def page_spec(i):
        def index(b, kv, qi, pt):
            s = step_of(b, kv, qi)
            return (pt[s // groups, (s % groups) * pages_per_step + i], 0, 0)
        return pl.BlockSpec((1, PAGE * KV_ROWS, NSA_HD), index)

    def means_index(b, kv, qi, pt):
        s = step_of(b, kv, qi)
        return (s // groups, s % groups, 0, 0, 0)

    g_w = 3 * NSA_GROUP
    g_rows = NSA_GROUP * tq
    grid_spec = pltpu.PrefetchScalarGridSpec(
        num_scalar_prefetch=1,
        grid=(batch, NSA_KV, nq),
        in_specs=[pl.BlockSpec((tq, q_lanes), lambda b, kv, qi, pt: (b * nq + qi, kv))]
        + [kv_spec(br, part) for br in range(3) for part in range(2)]
        + [pl.BlockSpec((None, g_w, tq), lambda b, kv, qi, pt: (kv, 0, b * nq + qi)),
           pl.BlockSpec((None, g_w, 1), lambda b, kv, qi, pt: (kv, 0, 0))]
        + [page_spec(i) for i in range(pages_per_step)],
        out_specs=[pl.BlockSpec((tq, q_lanes), lambda b, kv, qi, pt: (b * nq + qi, kv)),
                   pl.BlockSpec((1, pages_per_step, per_page, KV_ROWS, NSA_HD), means_index)],
        scratch_shapes=[
            pltpu.VMEM((seq // NSA_BLOCK, NSA_HD), F32),
            pltpu.VMEM((seq // NSA_BLOCK, NSA_HD), F32),
            pltpu.VMEM((seq, NSA_HD), BF16),
            pltpu.VMEM((seq // tk, vt_rows, tk), BF16),
            pltpu.VMEM((seq, NSA_HD), BF16),
            pltpu.VMEM((seq // tk, vt_rows, tk), BF16),
            pltpu.VMEM((g_rows, NSA_HD), BF16),
            pltpu.VMEM((seq // NSA_BLOCK, tq), F32),
            pltpu.VMEM((1, g_rows), F32),
            pltpu.VMEM((vt_rows, g_rows), F32),
            pltpu.VMEM((1, g_rows), F32),
            pltpu.VMEM((vt_rows, g_rows), F32),
        ],
    )
    return pl.pallas_call(
        functools.partial(_nsa_prompt_kernel, seq=seq, tq=tq, tk=tk, pages_per_step=pages_per_step),
        grid_spec=grid_spec,
        out_shape=[jax.ShapeDtypeStruct((batch * seq, NSA_Q_W), BF16),
                   jax.ShapeDtypeStruct((seqs, n_pages, per_page, KV_ROWS, NSA_HD), F32)],
        compiler_params=_params("arbitrary", "arbitrary", "arbitrary"),
        name="nsa_prompt",
    )(page_table, main, main, main, main, main, main, main, gates, b_gate, *([cache_cmp] * pages_per_step))


def _kv_rows_out_kernel(c_ref, s_ref, w_ref, oc_ref, os_ref, ow_ref, *, rows, n_c):
    def spread(src, dst):
        for r in range(KV_ROWS):
            dst[pl.ds(r, rows, stride=KV_ROWS), :] = src[:, r * NSA_HD:(r + 1) * NSA_HD]

    spread(c_ref, oc_ref)
    spread(s_ref, os_ref)

    @pl.when(pl.program_id(1) == n_c - 1)
    def _():
        spread(w_ref, ow_ref)


def _kv_rows_out(main, batch, seq, keep):
    rows = keep
    n_c = seq // rows
    col0 = NSA_Q_W // NSA_KV_W
    tile = lambda br: pl.BlockSpec((rows, NSA_KV_W), lambda b, c: (b * n_c + c, col0 + br))
    out_tile = pl.BlockSpec((rows * KV_ROWS, NSA_HD), lambda b, c: (b * n_c + c, 0))
    return pl.pallas_call(
        functools.partial(_kv_rows_out_kernel, rows=rows, n_c=n_c),
        grid=(batch, n_c),
        in_specs=[tile(0), tile(1),
                  pl.BlockSpec((rows, NSA_KV_W), lambda b, c: (b * n_c + n_c - 1, col0 + 2))],
        out_specs=[out_tile, out_tile, pl.BlockSpec((rows * KV_ROWS, NSA_HD), lambda b, c: (b, 0))],
        out_shape=[jax.ShapeDtypeStruct((batch * seq * KV_ROWS, NSA_HD), F32)] * 2
        + [jax.ShapeDtypeStruct((batch * keep * KV_ROWS, NSA_HD), F32)],
        compiler_params=_params("arbitrary", "arbitrary"),
        name="nsa_kv_rows_out",
    )(main, main, main)


def _page_specs(n):
    return [pl.BlockSpec((1, PAGE * KV_ROWS, NSA_HD), lambda b, p, pt, i=i: (pt[b, p * n + i], 0, 0))
            for i in range(n)]


def _nsa_sample_kernel(pt_ref, q_ref, cm_ref, *refs, past, n_new, n_steps, per_step):
    pg_refs = refs[:per_step]
    (kvn_ref, wb_ref, kwn_ref, gt_ref, bg_ref, o_ref,
     qr_ref, sel_ref, m_ref, l_ref, acc_ref, oc_ref, ow_ref) = refs[per_step:]
    p = pl.program_id(1)
    k_w = NSA_KV * NSA_HD
    cols = NSA_KV * n_new * NSA_GROUP
    per_kv = n_new * NSA_GROUP
    n_cmp = past // NSA_BLOCK
    n_sel = -(-(past + n_new) // NSA_BLOCK)
    n_sel_pad = sel_ref.shape[0]
    lane = lax.broadcasted_iota(jnp.int32, (1, cols), 1)
    q_pos = past + (lane // NSA_GROUP) % n_new

    def heads_on_lanes(ref, n, part):
        return jnp.concatenate(
            [ref[0, pl.ds(part * NSA_KV + kv, n, stride=KV_ROWS), :] for kv in range(NSA_KV)], axis=1)

    def scores(keys):
        return _dot_nt(keys, qr_ref[...])

    def own_head(x_t):
        out = jnp.zeros((NSA_HD, cols), F32)
        for kv in range(NSA_KV):
            mine = (lane // per_kv) == kv
            out = out + jnp.where(mine, x_t[kv * NSA_HD:(kv + 1) * NSA_HD, :], 0.0)
        return out

    def attend_once(keys, vals, mask):
        p_t = _masked_softmax2(scores(keys), mask, axis=0)
        return own_head(_dot_tn(vals, p_t))

    def online(keys, vals, bias):
        s = scores(keys) + bias
        m_old = m_ref[...]
        m_new = jnp.maximum(m_old, jnp.max(s, axis=0, keepdims=True))
        p_t = jnp.exp2(s - m_new)
        alpha = jnp.exp2(m_old - m_new)
        l_ref[...] = alpha * l_ref[...] + jnp.sum(p_t, axis=0, keepdims=True)
        acc_ref[...] = alpha * acc_ref[...] + own_head(_dot_tn(vals, p_t))
        m_ref[...] = m_new

    @pl.when(p == 0)
    def _():
        q = q_ref[0] * (NSA_HD ** -0.5 * LOG2_E)
        r_kv = lax.broadcasted_iota(jnp.int32, (cols, k_w), 0) // per_kv
        c_kv = lax.broadcasted_iota(jnp.int32, (cols, k_w), 1) // NSA_HD
        qr_ref[...] = jnp.where(r_kv == c_kv, jnp.tile(q, (1, NSA_KV)), 0.0).astype(BF16)
        cm = cm_ref[0]
        blk = lax.broadcasted_iota(jnp.int32, (n_cmp, 1), 0)
        p_t = _masked_softmax2(scores(cm[:, :k_w]), (blk + 1) * NSA_BLOCK - 1 <= q_pos, axis=0)
        oc_ref[...] = own_head(_dot_tn(cm[:, k_w:], p_t))
        same = _ind(lax.broadcasted_iota(jnp.int32, (cols, cols), 0) // NSA_GROUP
                    == lax.broadcasted_iota(jnp.int32, (cols, cols), 1) // NSA_GROUP, BF16)
        imp = sum(jnp.dot(t, same, preferred_element_type=F32) for t in _split3(p_t))
        imp = jnp.concatenate([imp, jnp.zeros((n_sel_pad - n_cmp, cols), F32)], axis=0)
        blk_s = lax.broadcasted_iota(jnp.int32, (n_sel_pad, 1), 0)
        cur = q_pos // NSA_BLOCK
        forced = (blk_s == 0) | ((blk_s >= cur - 1) & (blk_s <= cur))
        score = jnp.where((blk_s > cur) | (blk_s >= n_sel), -jnp.inf, jnp.where(forced, jnp.inf, imp))
        blk_f = blk_s.astype(F32)

        def pick(_, carry):
            left, bias = carry
            top = jnp.max(left, axis=0, keepdims=True)
            first = jnp.min(jnp.where(left == top, blk_f, float(n_sel_pad)), axis=0, keepdims=True)
            hit = (blk_f == first) & (top > -jnp.inf)
            return jnp.where(hit, -jnp.inf, left), jnp.where(hit, 0.0, bias)

        _, bias = lax.fori_loop(0, NSA_TOPK, pick, (score, jnp.full(score.shape, NEG_BIG, F32)))
        sel_ref[...] = bias
        n_wb = wb_ref.shape[1] // KV_ROWS
        kw_all = jnp.concatenate([heads_on_lanes(wb_ref, n_wb, 0), kwn_ref[0][:, :k_w]], axis=0)
        vw_all = jnp.concatenate([heads_on_lanes(wb_ref, n_wb, 1), kwn_ref[0][:, k_w:]], axis=0)
        n_w = kw_all.shape[0]
        w_row = lax.broadcasted_iota(jnp.int32, (n_w, 1), 0)
        w_pos = past - n_wb + w_row
        d = q_pos - w_pos
        ow_ref[...] = attend_once(kw_all, vw_all,
                                  (d >= 0) & (d < NSA_WINDOW) & (w_pos >= 0) & (w_row < n_wb + n_new))
        m_ref[...] = jnp.full(m_ref.shape, NEG_BIG, F32)
        l_ref[...] = jnp.zeros(l_ref.shape, F32)
        acc_ref[...] = jnp.zeros(acc_ref.shape, F32)

    def page_part(part):
        return jnp.concatenate([heads_on_lanes(pg_ref, PAGE, part) for pg_ref in pg_refs], axis=0)

    blk_per_step = per_step * PAGE // NSA_BLOCK
    online(page_part(0), page_part(1), jnp.concatenate(
        [jnp.broadcast_to(sel_ref[pl.ds(p * blk_per_step + j, 1), :], (NSA_BLOCK, cols))
         for j in range(blk_per_step)], axis=0))

    @pl.when(p == n_steps - 1)
    def _():
        kvn = kvn_ref[0]
        n_row = lax.broadcasted_iota(jnp.int32, (kvn.shape[0], 1), 0)
        bias_new = jnp.where((n_row < n_new) & (past + n_row <= q_pos), 0.0, NEG_BIG)
        online(kvn[:, :k_w], kvn[:, k_w:], bias_new + sel_ref[pl.ds(past // NSA_BLOCK, 1), :])
        o_s = acc_ref[...] / jnp.maximum(l_ref[...], TINY)
        gates = jax.nn.sigmoid(gt_ref[0] + bg_ref[...])
        o_t = gates[0:1, :] * oc_ref[...] + gates[1:2, :] * o_s + gates[2:3, :] * ow_ref[...]
        eye = _ind(lax.broadcasted_iota(jnp.int32, (NSA_HD, NSA_HD), 0)
                   == lax.broadcasted_iota(jnp.int32, (NSA_HD, NSA_HD), 1), BF16)
        o_ref[0] = sum(lax.dot_general(t, eye, (((0,), (0,)), ((), ())), preferred_element_type=F32)
                       for t in _split3(o_t))


def _nsa_sample(q_rows, cmp_means, cache_slc, page_table, kv_new, win_buf, kw_new, gates, b_gate,
                past, n_new):
    seqs, n_pages = page_table.shape
    cols = q_rows.shape[1]
    n_cmp = cmp_means.shape[1]
    n_sel_pad = -(-(-(-(past + n_new) // NSA_BLOCK)) // 8) * 8
    n_wb = win_buf.shape[1]
    pad_new = kv_new.shape[1]
    per_step = 8
    n_steps = n_pages // per_step
    seq_map = lambda b, p, pt: (b, 0, 0)
    return pl.pallas_call(
        functools.partial(_nsa_sample_kernel, past=past, n_new=n_new, n_steps=n_steps, per_step=per_step),
        grid_spec=pltpu.PrefetchScalarGridSpec(
            num_scalar_prefetch=1,
            grid=(seqs, n_steps),
            in_specs=[
                pl.BlockSpec((1, cols, NSA_HD), seq_map),
                pl.BlockSpec((1, n_cmp, NSA_KV_W), seq_map),
                *_page_specs(per_step),
                pl.BlockSpec((1, pad_new, NSA_KV_W), seq_map),
                pl.BlockSpec((1, n_wb, NSA_HD), seq_map),
                pl.BlockSpec((1, pad_new, NSA_KV_W), seq_map),
                pl.BlockSpec((1, 3, cols), seq_map),
                pl.BlockSpec((3, cols), lambda b, p, pt: (0, 0)),
            ],
            out_specs=pl.BlockSpec((1, cols, NSA_HD), seq_map),
            scratch_shapes=[
                pltpu.VMEM((cols, NSA_KV * NSA_HD), BF16),
                pltpu.VMEM((n_sel_pad, cols), F32),
                pltpu.VMEM((1, cols), F32),
                pltpu.VMEM((1, cols), F32),
                pltpu.VMEM((NSA_HD, cols), F32),
                pltpu.VMEM((NSA_HD, cols), F32),
                pltpu.VMEM((NSA_HD, cols), F32),
            ],
        ),
        out_shape=jax.ShapeDtypeStruct((seqs, cols, NSA_HD), F32),
        compiler_params=_params("parallel", "arbitrary"),
        name="nsa_sample",
    )(page_table, q_rows, cmp_means, *([cache_slc] * per_step), kv_new, win_buf, kw_new, gates, b_gate)


def kernel(x_prompt, x_sample, c_prompt, c_sample, state_gla, cache_cmp_kv, cache_slc_kv, cache_win_kv,
           page_table, ada_w, ada_b, gla_w_in, gla_w_alpha, gla_b_alpha, gla_norm_g, gla_w_out, nsa_w_in,
           nsa_b_gate, nsa_w_out, ln_mix_g, ln_mix_b, ffn_w_in, ffn_w_out, ln_ffn_g, ln_ffn_b):
    batch, seq, _ = x_prompt.shape
    seqs, n_new, _ = x_sample.shape
    n_pages = page_table.shape[1]
    past = n_pages * PAGE
    rows_s = seqs * n_new
    tm_p = seq
    w_out_mix = (gla_w_out.astype(BF16), nsa_w_out.astype(BF16))
    w_out_ffn = ffn_w_out.astype(BF16)

    pad_rows = 16 - (batch + seqs)
    c_all = jnp.concatenate([c_prompt, c_sample, jnp.zeros((pad_rows, D_MODEL), F32)], axis=0)
    mod = _adaln(c_all, ada_w, ada_b).reshape(DEPTH, 16, 6, D_MODEL)

    def mods(layer):
        mp = [mod[layer, :batch, i].reshape(batch, 1, D_MODEL) for i in range(6)]
        ms = [jnp.repeat(mod[layer, batch:batch + seqs, i], n_new, axis=0).reshape(1, rows_s, D_MODEL)
              for i in range(6)]
        return mp, ms

    def mixer_out(o, x, layer, m, tm, tag):
        return _out_ln(o, w_out_mix[layer % 2], layer // 2, x, m[2], ln_mix_g, ln_mix_b, layer, min(tm, 512), tag)

    def ffn(xp, xs, layer, mp, ms):
        act_p, act_s = _ffn_in(xp, mp[4], mp[3], xs, ms[4], ms[3], ffn_w_in, layer, tm_p, "ffn_in_%d" % layer)
        out = lambda act, x, m, tm, tag: _out_ln(act, w_out_ffn, layer, x, m[5], ln_ffn_g, ln_ffn_b, layer, tm,
                                                  "ffn_out_%s%d" % (tag, layer))
        return out(act_p, xp, mp, 256, "prompt"), out(act_s, xs, ms, rows_s, "sample")

    xp = x_prompt.reshape(batch * seq, D_MODEL)
    xs = x_sample.reshape(rows_s, D_MODEL)

    mp, ms = mods(0)
    main_p, tail_p, main_s, tail_s = _proj(xp, mp[1], mp[0], xs, ms[1], ms[0], gla_w_in.astype(BF16), 0,
                                           GLA_MAIN_W, tm_p, "gla_in")
    gated_p, gla_state_p = _gla(main_p, tail_p, gla_w_alpha[0], gla_b_alpha[0], gla_norm_g[0], None,
                                batch, seq, 128, 128)
    chunk_s = GLA_SUB

    def pad_steps(a):
        a = a.reshape(seqs, n_new, a.shape[-1])
        return jnp.pad(a, ((0, 0), (0, chunk_s - n_new), (0, 0))).reshape(seqs * chunk_s, a.shape[-1])

    gated_s, gla_state_s = _gla(pad_steps(main_s), pad_steps(tail_s), gla_w_alpha[0], gla_b_alpha[0],
                                gla_norm_g[0], state_gla[:, 0], seqs, chunk_s, chunk_s, n_new)
    gated_s = gated_s.reshape(seqs, chunk_s, -1)[:, :n_new].reshape(rows_s, -1)
    xp = mixer_out(gated_p, xp, 0, mp, tm_p, "gla_out_prompt")
    xs = mixer_out(gated_s, xs, 0, ms, rows_s, "gla_out_sample")
    xp, xs = ffn(xp, xs, 0, mp, ms)

    mp, ms = mods(1)
    main_p, tail_p, main_s, tail_s = _proj(xp, mp[1], mp[0], xs, ms[1], ms[0], nsa_w_in.astype(BF16), 0,
                                           NSA_MAIN_W, tm_p, "nsa_in")
    n_gate = 3 * NSA_HEADS
    g_w = 3 * NSA_GROUP
    gates_p = tail_p[:, :n_gate].reshape(batch * seq, NSA_KV, g_w).transpose(1, 2, 0)
    b_gate_p = nsa_b_gate[0].reshape(NSA_KV, g_w, 1)
    n_phys = cache_cmp_kv.shape[0]
    page_rows = lambda cache: cache.reshape(n_phys, PAGE * KV_ROWS, NSA_HD)
    o_p, cmp_means = _nsa_prompt(main_p, gates_p, b_gate_p, batch, seq, page_rows(cache_cmp_kv), page_table)

    kv_rows = lambda m, br: m[:, NSA_Q_W + br * NSA_KV_W:NSA_Q_W + (br + 1) * NSA_KV_W]
    kv_shape = (2, NSA_KV, NSA_HD)
    keep_p = min(NSA_WINDOW, seq)
    cmp_rows, slc_rows, win_rows = _kv_rows_out(main_p, batch, seq, keep_p)
    cmp_kv_p = cmp_rows.reshape(batch, 1, seq, *kv_shape)
    slc_kv_p = slc_rows.reshape(batch, 1, seq, *kv_shape)
    win_kv_p = win_rows.reshape(batch, 1, keep_p, *kv_shape)

    cmp_kv_s = kv_rows(main_s, 0).reshape(seqs, 1, n_new, *kv_shape)
    slc_kv_s = kv_rows(main_s, 1).reshape(seqs, 1, n_new, *kv_shape)
    kw_new = kv_rows(main_s, 2).reshape(seqs, n_new, NSA_KV_W)
    n_wb = cache_win_kv.shape[2]
    win_buf = cache_win_kv[:, 0].reshape(seqs, n_wb * KV_ROWS, NSA_HD)
    all_w = jnp.concatenate([win_buf, kw_new.reshape(seqs, n_new * KV_ROWS, NSA_HD)], axis=1)
    keep_s = min(NSA_WINDOW, n_wb + n_new)
    win_kv_s = all_w[:, (n_wb + n_new - keep_s) * KV_ROWS:].reshape(seqs, 1, keep_s, *kv_shape)

    cmp_means = cmp_means.reshape(seqs, past // NSA_BLOCK, NSA_KV_W)
    pad_new = 8
    pad8 = lambda a: jnp.pad(a, ((0, 0), (0, pad_new - n_new), (0, 0)))
    cols = NSA_KV * n_new * NSA_GROUP
    q_rows = (main_s[:, :NSA_Q_W].reshape(seqs, n_new, NSA_KV, NSA_GROUP, NSA_HD)
              .transpose(0, 2, 1, 3, 4).reshape(seqs, cols, NSA_HD))
    gates_s = (tail_s[:, :n_gate].reshape(seqs, n_new, NSA_KV, NSA_GROUP, 3)
               .transpose(0, 4, 2, 1, 3).reshape(seqs, 3, cols))
    b_gate_s = jnp.broadcast_to(nsa_b_gate[0].reshape(1, NSA_KV, NSA_GROUP, 3), (n_new, NSA_KV, NSA_GROUP, 3))
    b_gate_s = b_gate_s.transpose(3, 1, 0, 2).reshape(3, cols)
    o_s = _nsa_sample(q_rows, cmp_means, page_rows(cache_slc_kv), page_table,
                      pad8(slc_kv_s.reshape(seqs, n_new, NSA_KV_W)), win_buf, pad8(kw_new), gates_s, b_gate_s,
                      past, n_new)
    o_s = (o_s.reshape(seqs, NSA_KV, n_new, NSA_GROUP, NSA_HD).transpose(0, 2, 1, 3, 4)
           .reshape(rows_s, NSA_Q_W).astype(BF16))

    xp = mixer_out(o_p, xp, 1, mp, tm_p, "nsa_out_prompt")
    xs = mixer_out(o_s, xs, 1, ms, rows_s, "nsa_out_sample")
    xp, xs = ffn(xp, xs, 1, mp, ms)

    return (xp.reshape(batch, seq, D_MODEL), xs.reshape(seqs, n_new, D_MODEL),
            gla_state_p[:, None], gla_state_s[:, None], cmp_kv_p, cmp_kv_s, slc_kv_p, slc_kv_s,
            win_kv_p, win_kv_s)
```

```python
import functools

import jax
import jax.numpy as jnp
from jax import lax
from jax.experimental import pallas as pl
from jax.experimental.pallas import tpu as pltpu

F32 = jnp.float32
BF16 = jnp.bfloat16

D_MODEL = 2048
DEPTH = 2
DN_ALPHA = (2 * DEPTH) ** 0.25
LN_EPS = 1e-5
D_FF = 5632
GLA_HEADS = 4
GLA_DK = 256
GLA_DV = 512
GLA_RANK = 16
GLA_TAU = 16.0
GLA_SUB = 32
GLA_MAIN_W = 2 * GLA_HEADS * GLA_DK + 2 * GLA_HEADS * GLA_DV
NSA_HEADS = 16
NSA_KV = 4
NSA_HD = 128
NSA_GROUP = NSA_HEADS // NSA_KV
NSA_BLOCK = 64
NSA_TOPK = 16
NSA_WINDOW = 512
NSA_Q_W = NSA_HEADS * NSA_HD
NSA_KV_W = 2 * NSA_KV * NSA_HD
NSA_MAIN_W = NSA_Q_W + 3 * NSA_KV_W
PAGE = 128
KV_ROWS = 2 * NSA_KV

LANES = 128
BF16_ROWS = 16
LOG2_E = 1.4426950408889634
V7X_VMEM_BYTES = 64 * 1024 * 1024
VMEM_LIMIT = V7X_VMEM_BYTES * 7 // 8

NEG_BIG = -1e30
TINY = float(jnp.finfo(jnp.float32).tiny)


def _params(*sem):
    return pltpu.CompilerParams(dimension_semantics=sem, vmem_limit_bytes=VMEM_LIMIT)


def _dot(a, b):
    return jnp.dot(a.astype(BF16), b.astype(BF16), preferred_element_type=F32)


def _dot_nt(a, b):
    return lax.dot_general(a.astype(BF16), b.astype(BF16), (((1,), (1,)), ((), ())),
                           preferred_element_type=F32)


def _dot_tn(a, b):
    return lax.dot_general(a.astype(BF16), b.astype(BF16), (((0,), (0,)), ((), ())),
                           preferred_element_type=F32)


def _split3(x):
    hi = x.astype(BF16)
    r1 = x - hi.astype(F32)
    mid = r1.astype(BF16)
    lo = (r1 - mid.astype(F32)).astype(BF16)
    return hi, mid, lo


def _silu(x):
    return x * jax.nn.sigmoid(x)


def _ind(cond, dtype=F32):
    return jnp.where(cond, 1.0, 0.0).astype(dtype)


def _masked_softmax2(s2, mask, axis):
    s2 = jnp.where(mask, s2, -jnp.inf)
    m = jnp.max(s2, axis=axis, keepdims=True)
    m = jnp.where(jnp.isfinite(m), m, 0.0)
    e = jnp.where(mask, jnp.exp2(s2 - m), 0.0)
    return e / jnp.maximum(jnp.sum(e, axis=axis, keepdims=True), TINY)


def _adaln_kernel(c_ref, w_ref, b_ref, o_ref):
    o_ref[...] = _dot(_silu(c_ref[...]), w_ref[...]) + b_ref[...]


def _adaln(c_all, ada_w, ada_b):
    rows = c_all.shape[0]
    n = ada_w.shape[2]
    tn = 1024
    return pl.pallas_call(
        _adaln_kernel,
        grid=(DEPTH, n // tn),
        in_specs=[
            pl.BlockSpec((rows, D_MODEL), lambda l, j: (0, 0)),
            pl.BlockSpec((None, D_MODEL, tn), lambda l, j: (l, 0, j)),
            pl.BlockSpec((None, 1, tn), lambda l, j: (l, 0, j)),
        ],
        out_specs=pl.BlockSpec((None, rows, tn), lambda l, j: (l, 0, j)),
        out_shape=jax.ShapeDtypeStruct((DEPTH, rows, n), F32),
        compiler_params=_params("parallel", "parallel"),
        name="adaln",
    )(c_all, ada_w, ada_b.reshape(DEPTH, 1, n))


def _modulate(x_ref, sc_ref, sh_ref):
    return (x_ref[...] * (1.0 + sc_ref[...]) + sh_ref[...]).astype(BF16)


def _proj_kernel(x_ref, sc_ref, sh_ref, xs_ref, scs_ref, shs_ref, w_ref, wt_ref,
                 o_ref, ot_ref, os_ref, ots_ref, h_ref, *, n_i):
    i, j = pl.program_id(0), pl.program_id(1)
    tm = x_ref.shape[0]

    @pl.when(j == 0)
    def _():
        h_ref[:tm] = _modulate(x_ref, sc_ref, sh_ref)
        ot_ref[...] = _dot(h_ref[:tm], wt_ref[...])

    @pl.when(i < n_i - 1)
    def _():
        o_ref[...] = _dot(h_ref[:tm], w_ref[...])

    @pl.when(i == n_i - 1)
    def _():
        @pl.when(j == 0)
        def _():
            h_ref[tm:] = _modulate(xs_ref, scs_ref, shs_ref)
            ots_ref[...] = _dot(h_ref[tm:], wt_ref[...])

        both = _dot(h_ref[...], w_ref[...])
        o_ref[...] = both[:tm]
        os_ref[...] = both[tm:]


def _proj(x, sc, sh, xs, scs, shs, w, layer, n_main, tm, name):
    m, ms = x.shape[0], xs.shape[0]
    n_tail = w.shape[2] - n_main
    w_tail = jnp.pad(w[layer, :, n_main:], ((0, 0), (0, LANES - n_tail))).astype(BF16)
    tn = 512
    w_tiles = w[layer, :, :n_main].astype(BF16).reshape(D_MODEL, n_main // tn, tn).transpose(1, 0, 2)
    n_i = m // tm
    tiles_per_group = n_i // sc.shape[0]
    mod_spec = pl.BlockSpec((None, sc.shape[1], D_MODEL), lambda i, j: (i // tiles_per_group, 0, 0))
    const2 = lambda i, j: (0, 0)
    const3 = lambda i, j: (0, 0, 0)
    last_tile_col = lambda i, j: (0, jnp.where(i == n_i - 1, j, 0))
    return pl.pallas_call(
        functools.partial(_proj_kernel, n_i=n_i),
        grid=(n_i, n_main // tn),
        in_specs=[
            pl.BlockSpec((tm, D_MODEL), lambda i, j: (i, 0), pipeline_mode=pl.Buffered(1)),
            mod_spec,
            mod_spec,
            pl.BlockSpec((ms, D_MODEL), const2),
            pl.BlockSpec((None, ms, D_MODEL), const3),
            pl.BlockSpec((None, ms, D_MODEL), const3),
            pl.BlockSpec((None, D_MODEL, tn), lambda i, j: (j, 0, 0)),
            pl.BlockSpec((D_MODEL, LANES), const2),
        ],
        out_specs=[
            pl.BlockSpec((tm, tn), lambda i, j: (i, j)),
            pl.BlockSpec((tm, LANES), lambda i, j: (i, 0)),
            pl.BlockSpec((ms, tn), last_tile_col),
            pl.BlockSpec((ms, LANES), const2),
        ],
        out_shape=[jax.ShapeDtypeStruct((m, n_main), F32), jax.ShapeDtypeStruct((m, LANES), F32),
                   jax.ShapeDtypeStruct((ms, n_main), F32), jax.ShapeDtypeStruct((ms, LANES), F32)],
        scratch_shapes=[pltpu.VMEM((tm + ms, D_MODEL), BF16)],
        compiler_params=_params("arbitrary", "arbitrary"),
        name=name,
    )(x, sc, sh, xs, scs, shs, w_tiles, w_tail)


def _out_ln_kernel(a_ref, w_ref, x_ref, gt_ref, g_ref, b_ref, o_ref, *, n_sub):
    tm = a_ref.shape[0]
    rows = [slice(s * (tm // n_sub), (s + 1) * (tm // n_sub)) for s in range(n_sub)]
    proj = [jnp.dot(a_ref[rs, :], w_ref[...], preferred_element_type=F32) for rs in rows]
    for rs, m in zip(rows, proj):
        gate = gt_ref[...] if gt_ref.shape[0] == 1 else gt_ref[rs, :]
        y = DN_ALPHA * x_ref[rs, :] + gate * m
        mu = jnp.mean(y, axis=-1, keepdims=True)
        yc = y - mu
        var = jnp.mean(yc * yc, axis=-1, keepdims=True)
        o_ref[rs, :] = yc * lax.rsqrt(var + LN_EPS) * g_ref[...] + b_ref[...]


def _out_ln(a, w, w_layer, x, gate, ln_g, ln_b, ln_layer, tm, name):
    m, kdim = a.shape
    groups, mod_rows = gate.shape[0], gate.shape[1]
    tiles_per_group = m // tm // groups
    vec_spec = pl.BlockSpec((None, 1, D_MODEL), lambda i: (ln_layer, 0, 0))
    return pl.pallas_call(
        functools.partial(_out_ln_kernel, n_sub=2 if tm % 512 == 0 else 1),
        grid=(m // tm,),
        in_specs=[
            pl.BlockSpec((tm, kdim), lambda i: (i, 0)),
            pl.BlockSpec((None, kdim, D_MODEL), lambda i: (w_layer, 0, 0), pipeline_mode=pl.Buffered(1)),
            pl.BlockSpec((tm, D_MODEL), lambda i: (i, 0)),
            pl.BlockSpec((None, mod_rows, D_MODEL), lambda i: (i // tiles_per_group, 0, 0)),
            vec_spec,
            vec_spec,
        ],
        out_specs=pl.BlockSpec((tm, D_MODEL), lambda i: (i, 0)),
        out_shape=jax.ShapeDtypeStruct((m, D_MODEL), F32),
        compiler_params=_params("parallel"),
        name=name,
    )(a, w, x, gate, ln_g.reshape(-1, 1, D_MODEL), ln_b.reshape(-1, 1, D_MODEL))


def _ffn_in_kernel(x_ref, sc_ref, sh_ref, xs_ref, scs_ref, shs_ref, wa_ref, wu_ref,
                   o_ref, os_ref, h_ref, *, n_i):
    i, j = pl.program_id(0), pl.program_id(1)
    tm = x_ref.shape[0]

    @pl.when(j == 0)
    def _():
        h_ref[:tm] = _modulate(x_ref, sc_ref, sh_ref)

    wa = wa_ref[...].astype(BF16)
    wu = wu_ref[...].astype(BF16)

    def swiglu_in(h):
        return (_silu(_dot(h, wa)) * _dot(h, wu)).astype(BF16)

    @pl.when(i < n_i - 1)
    def _():
        o_ref[...] = swiglu_in(h_ref[:tm])

    @pl.when(i == n_i - 1)
    def _():
        @pl.when(j == 0)
        def _():
            h_ref[tm:] = _modulate(xs_ref, scs_ref, shs_ref)

        both = swiglu_in(h_ref[...])
        o_ref[...] = both[:tm]
        os_ref[...] = both[tm:]


def _ffn_in(x, sc, sh, xs, scs, shs, w, layer, tm, name):
    m, ms = x.shape[0], xs.shape[0]
    tn = 256
    nj = D_FF // tn
    n_i = m // tm
    tiles_per_group = n_i // sc.shape[0]
    mod_spec = pl.BlockSpec((None, sc.shape[1], D_MODEL), lambda i, j: (i // tiles_per_group, 0, 0))
    const3 = lambda i, j: (0, 0, 0)
    return pl.pallas_call(
        functools.partial(_ffn_in_kernel, n_i=n_i),
        grid=(n_i, nj),
        in_specs=[
            pl.BlockSpec((tm, D_MODEL), lambda i, j: (i, 0), pipeline_mode=pl.Buffered(1)),
            mod_spec,
            mod_spec,
            pl.BlockSpec((ms, D_MODEL), lambda i, j: (0, 0)),
            pl.BlockSpec((None, ms, D_MODEL), const3),
            pl.BlockSpec((None, ms, D_MODEL), const3),
            pl.BlockSpec((None, D_MODEL, tn), lambda i, j: (layer, 0, j)),
            pl.BlockSpec((None, D_MODEL, tn), lambda i, j: (layer, 0, j + nj)),
        ],
        out_specs=[
            pl.BlockSpec((tm, tn), lambda i, j: (i, j)),
            pl.BlockSpec((ms, tn), lambda i, j: (0, jnp.where(i == n_i - 1, j, 0))),
        ],
        out_shape=[jax.ShapeDtypeStruct((m, D_FF), BF16), jax.ShapeDtypeStruct((ms, D_FF), BF16)],
        scratch_shapes=[pltpu.VMEM((tm + ms, D_MODEL), BF16)],
        compiler_params=_params("arbitrary", "arbitrary"),
        name=name,
    )(x, sc, sh, xs, scs, shs, w, w)


def _gla_kernel(*refs, chunk, t_valid, has_s0):
    if has_s0:
        q_ref, k_ref, v_ref, r_ref, a_ref, wa_ref, ba_ref, g_ref, s0_ref, o_ref, st_ref = refs
    else:
        q_ref, k_ref, v_ref, r_ref, a_ref, wa_ref, ba_ref, g_ref, o_ref, st_ref = refs
    sub = GLA_SUB
    n_sub = chunk // sub
    anchor = sub // 2 - 1

    @pl.when(pl.program_id(1) == 0)
    def _():
        if has_s0:
            st_ref[...] = s0_ref[...]
        else:
            st_ref[...] = jnp.zeros(st_ref.shape, F32)

    row = lax.broadcasted_iota(jnp.int32, (chunk, 1), 0)
    tri = _ind(lax.broadcasted_iota(jnp.int32, (chunk, chunk), 1)
               <= lax.broadcasted_iota(jnp.int32, (chunk, chunk), 0), BF16)
    ones = jnp.ones((chunk, LANES), BF16)
    causal = (lax.broadcasted_iota(jnp.int32, (sub, sub), 1)
              <= lax.broadcasted_iota(jnp.int32, (sub, sub), 0))
    heads = range(GLA_HEADS)
    dk = [slice(h * GLA_DK, (h + 1) * GLA_DK) for h in heads]
    dv = [slice(h * GLA_DV, (h + 1) * GLA_DV) for h in heads]
    subs = [slice(s * sub, (s + 1) * sub) for s in range(n_sub)]

    z = _dot(a_ref[...], wa_ref[...]) + ba_ref[...]
    lb = (jnp.minimum(z, 0.0) - jnp.log1p(jnp.exp(-jnp.abs(z)))) * (1.0 / GLA_TAU)
    k = k_ref[...]
    if t_valid < chunk:
        lb = jnp.where(row < t_valid, lb, 0.0)
        k = jnp.where(row < t_valid, k, 0.0)
    q = q_ref[...] * (GLA_DK ** -0.5)
    vb = v_ref[...].astype(BF16)
    lb3 = _split3(lb)
    b = sum(jnp.dot(tri, t, preferred_element_type=F32) for t in lb3)
    dec = jnp.exp(sum(lax.dot_general(t, ones, (((0,), (0,)), ((), ())), preferred_element_type=F32)
                      for t in lb3))
    b_last = b[chunk - 1:chunk, :]
    q_in = (q * jnp.exp(b)).astype(BF16)
    k_out = (k * jnp.exp(b_last - b)).astype(BF16)
    q_diag, k_diag, q_off, k_off = [], [], [], []
    for s, rs in enumerate(subs):
        b_mid = b[s * sub + anchor:s * sub + anchor + 1, :]
        q_diag.append((q[rs] * jnp.exp(b[rs] - b_mid)).astype(BF16))
        k_diag.append((k[rs] * jnp.exp(b_mid - b[rs])).astype(BF16))
        if s > 0:
            prev = slice(0, s * sub)
            b_in = b[s * sub - 1:s * sub, :]
            q_off.append((q[rs] * jnp.exp(b[rs] - b_in)).astype(BF16))
            k_off.append((k[prev] * jnp.exp(b_in - b[prev])).astype(BF16))

    s_old = [st_ref[0, h] for h in heads]
    o_inter = [_dot(q_in[:, dk[h]], s_old[h]) for h in heads]
    a_diag = [[_dot_nt(q_diag[s][:, dk[h]], k_diag[s][:, dk[h]]) for h in heads] for s in range(n_sub)]
    a_off = [[_dot_nt(q_off[s][:, dk[h]], k_off[s][:, dk[h]]) for h in heads] for s in range(n_sub - 1)]
    a_diag = [[jnp.where(causal, a, 0.0).astype(BF16) for a in row_] for row_ in a_diag]
    a_off = [[a.astype(BF16) for a in row_] for row_ in a_off]
    o_intra = []
    for s, rs in enumerate(subs):
        o_s = [_dot(a_diag[s][h], vb[rs, dv[h]]) for h in heads]
        if s > 0:
            o_s = [o_s[h] + _dot(a_off[s - 1][h], vb[0:s * sub, dv[h]]) for h in heads]
        o_intra.append(o_s)
    kv = [_dot_tn(k_out[:, dk[h]], vb[:, dv[h]]) for h in heads]
    for h in heads:
        st_ref[0, h] = s_old[h] * jnp.tile(dec[dk[h], :], (1, GLA_DV // LANES)) + kv[h]
    for h in heads:
        o = o_inter[h] + jnp.concatenate([o_intra[s][h] for s in range(n_sub)], axis=0)
        o = o * lax.rsqrt(jnp.mean(o * o, axis=-1, keepdims=True) + LN_EPS)
        o_ref[:, dv[h]] = (o * g_ref[:, dv[h]] * _silu(r_ref[:, dv[h]])).astype(BF16)


def _gla(main, tail, w_alpha, b_alpha, norm_g, s0, batch, seq, chunk, t_valid):
    n_chunks = seq // chunk
    qk_w = GLA_HEADS * GLA_DK
    v_w = GLA_HEADS * GLA_DV
    wa = jnp.pad(w_alpha, ((0, LANES - GLA_RANK), (0, 0)))
    row_map = lambda b, c: (b * n_chunks + c, 0)
    const = lambda b, c: (0, 0)
    in_specs = [
        pl.BlockSpec((chunk, qk_w), row_map),
        pl.BlockSpec((chunk, qk_w), lambda b, c: (b * n_chunks + c, 1)),
        pl.BlockSpec((chunk, v_w), lambda b, c: (b * n_chunks + c, 1)),
        pl.BlockSpec((chunk, v_w), lambda b, c: (b * n_chunks + c, 2)),
        pl.BlockSpec((chunk, LANES), row_map),
        pl.BlockSpec((LANES, qk_w), const),
        pl.BlockSpec((1, qk_w), const),
        pl.BlockSpec((1, v_w), const),
    ]
    args = [main, main, main, main, tail, wa, b_alpha.reshape(1, qk_w), norm_g.reshape(1, v_w)]
    state_spec = pl.BlockSpec((1, GLA_HEADS, GLA_DK, GLA_DV), lambda b, c: (b, 0, 0, 0))
    if s0 is not None:
        in_specs.append(state_spec)
        args.append(s0)
    return pl.pallas_call(
        functools.partial(_gla_kernel, chunk=chunk, t_valid=t_valid, has_s0=s0 is not None),
        grid=(batch, n_chunks),
        in_specs=in_specs,
        out_specs=[pl.BlockSpec((chunk, v_w), row_map), state_spec],
        out_shape=[jax.ShapeDtypeStruct((batch * seq, v_w), BF16),
                   jax.ShapeDtypeStruct((batch, GLA_HEADS, GLA_DK, GLA_DV), F32)],
        compiler_params=_params("parallel", "arbitrary"),
        name="gla_prompt" if s0 is None else "gla_sample",
    )(*args)


def _select_blocks(imp_t, blk, cur):
    n = imp_t.shape[0]
    forced = (blk == 0) | ((blk >= cur - 1) & (blk <= cur))
    score = jnp.where(blk > cur, -jnp.inf, jnp.where(forced, jnp.inf, imp_t))
    rank = jnp.zeros(score.shape, F32)
    for i in range(n):
        si = score[i:i + 1, :]
        rank = rank + jnp.where(blk > i, _ind(si >= score), _ind(si > score))
    return (rank < NSA_TOPK) & (score > -jnp.inf)


def _nsa_prompt_kernel(pt_ref, q_ref, kc_ref, vc_ref, ks_ref, vs_ref, kw_ref, vw_ref, gt_ref, bg_ref, *refs,
                       seq, tq, tk, pages_per_step):
    page_refs = refs[:pages_per_step]
    (o_ref, means_ref, kcm, vcm, ksb, vst, kwb, vwt, qs_ref, sel_ref,
     ms_ref, accs_ref, mw_ref, accw_ref) = refs[pages_per_step:]
    qi = pl.program_id(2)
    n_blk = seq // NSA_BLOCK
    n_kt = seq // tk
    blk_per_kt = tk // NSA_BLOCK

    @pl.when(qi == 0)
    def _():
        kcm[...] = jnp.sum(kc_ref[...].reshape(n_blk, NSA_BLOCK, NSA_HD), axis=1) * (1.0 / NSA_BLOCK)
        vcm[...] = jnp.sum(vc_ref[...].reshape(n_blk, NSA_BLOCK, NSA_HD), axis=1) * (1.0 / NSA_BLOCK)
        ksb[...] = ks_ref[...].astype(BF16)
        kwb[...] = kw_ref[...].astype(BF16)
        ones_rows = _ind(lax.broadcasted_iota(jnp.int32, (BF16_ROWS, tk), 0) == 0, BF16)
        for kt in range(n_kt):
            rows = slice(kt * tk, (kt + 1) * tk)
            vst[kt] = jnp.concatenate([vs_ref[rows, :].T.astype(BF16), ones_rows], axis=0)
            vwt[kt] = jnp.concatenate([vw_ref[rows, :].T.astype(BF16), ones_rows], axis=0)

    t0 = qi * tq
    q = q_ref[...] * (NSA_HD ** -0.5 * LOG2_E)
    qs_ref[...] = jnp.concatenate(
        [q[:, g * NSA_HD:(g + 1) * NSA_HD] for g in range(NSA_GROUP)], axis=0).astype(BF16)
    qs = qs_ref[...]
    t_row = t0 + lax.broadcasted_iota(jnp.int32, (1, tq), 1)
    t_row_g = jnp.concatenate([t_row] * NSA_GROUP, axis=1)

    kc_b = kcm[...].astype(BF16)
    blk_col = lax.broadcasted_iota(jnp.int32, (n_blk, 1), 0)
    p_t = _masked_softmax2(_dot_nt(kc_b, qs), (blk_col + 1) * NSA_BLOCK - 1 <= t_row_g, axis=0)
    o_c = _dot_tn(vcm[...], p_t)
    imp_t = sum(p_t[:, g * tq:(g + 1) * tq] for g in range(NSA_GROUP))
    sel_ref[...] = jnp.where(_select_blocks(imp_t, blk_col, t_row // NSA_BLOCK), 0.0, NEG_BIG)

    n_loc = lax.broadcasted_iota(jnp.int32, (tk, tq), 0)
    t_loc = lax.broadcasted_iota(jnp.int32, (tk, tq), 1)
    on_lanes = lambda a: jnp.concatenate([a] * NSA_GROUP, axis=1)
    causal = on_lanes(jnp.where(n_loc <= t_loc, 0.0, NEG_BIG))
    far = on_lanes(jnp.where(n_loc > t_loc, 0.0, NEG_BIG))

    def sel_bias(kt):
        return on_lanes(jnp.concatenate(
            [jnp.broadcast_to(sel_ref[pl.ds(kt * blk_per_kt + j, 1), :], (NSA_BLOCK, tq))
             for j in range(blk_per_kt)], axis=0))

    sel_state, win_state = (ms_ref, accs_ref), (mw_ref, accw_ref)
    for m_ref, acc_ref in (sel_state, win_state):
        m_ref[...] = jnp.full(m_ref.shape, NEG_BIG, F32)
        acc_ref[...] = jnp.zeros(acc_ref.shape, F32)

    def steps(*work):
        s = [_dot_nt(k_ref[pl.ds(pl.multiple_of(kt * tk, tk), tk), :], qs_ref[...]) + bias
             for _, k_ref, _, kt, bias in work]
        m_old = [state[0][...] for state, *_ in work]
        m_new = [jnp.maximum(mo, jnp.max(si, axis=0, keepdims=True)) for mo, si in zip(m_old, s)]
        p = [jnp.exp2(si - mn).astype(BF16) for si, mn in zip(s, m_new)]
        pv = [_dot(vt_ref[kt], pi) for (_, _, vt_ref, kt, _), pi in zip(work, p)]
        for (state, *_), mo, mn, pvi in zip(work, m_old, m_new, pv):
            state[1][...] = jnp.exp2(mo - mn) * state[1][...] + pvi
            state[0][...] = mn

    def result(acc_ref):
        acc = acc_ref[...]
        return acc[:NSA_HD] / jnp.maximum(acc[NSA_HD:NSA_HD + 1], TINY)

    steps((sel_state, ksb, vst, qi, sel_bias(qi) + causal), (win_state, kwb, vwt, qi, causal))

    @pl.when(qi >= 1)
    def _():
        steps((sel_state, ksb, vst, qi - 1, sel_bias(qi - 1)), (win_state, kwb, vwt, qi - 1, far))

    def sel_body(i, carry):
        kt = qi - 2 - i
        steps((sel_state, ksb, vst, kt, sel_bias(kt)))
        return carry

    lax.fori_loop(0, qi - 1, sel_body, 0)

    gates = jax.nn.sigmoid(gt_ref[...] + bg_ref[...])
    g_c, g_s, g_w = (jnp.concatenate([gates[3 * g + br:3 * g + br + 1, :] for g in range(NSA_GROUP)], axis=1)
                     for br in range(3))
    o_t = g_c * o_c + g_s * result(accs_ref) + g_w * result(accw_ref)
    for g in range(NSA_GROUP):
        o_ref[:, g * NSA_HD:(g + 1) * NSA_HD] = o_t[:, g * tq:(g + 1) * tq].T.astype(BF16)

    for i, x_ref in enumerate(page_refs):
        x = x_ref[0].reshape(PAGE // NSA_BLOCK, NSA_BLOCK, KV_ROWS, NSA_HD)
        means_ref[0, i] = jnp.sum(x, axis=1) * (1.0 / NSA_BLOCK)


def _nsa_prompt(main, gates, b_gate, batch, seq, cache_cmp, page_table):
    tq = tk = NSA_WINDOW
    assert seq % tk == 0
    vt_rows = NSA_HD + BF16_ROWS
    nq = seq // tq
    q_lanes = NSA_GROUP * NSA_HD
    col0 = NSA_Q_W // NSA_HD
    seqs, n_pages = page_table.shape
    n_steps = batch * NSA_KV * nq
    pages_per_step = seqs * n_pages // n_steps
    groups = n_pages // pages_per_step
    assert pages_per_step * n_steps == seqs * n_pages and groups * pages_per_step == n_pages
    per_page = PAGE // NSA_BLOCK

    def kv_spec(branch, part):
        off = col0 + (2 * branch + part) * NSA_KV
        return pl.BlockSpec((seq, NSA_HD), lambda b, kv, qi, pt: (b, off + kv))

    def step_of(b, kv, qi):
        return (b * NSA_KV + kv) * nq + qi

    def page_spec(i):
        def index(b, kv, qi, pt):
            s = step_of(b, kv, qi)
            return (pt[s // groups, (s % groups) * pages_per_step + i], 0, 0)
        return pl.BlockSpec((1, PAGE * KV_ROWS, NSA_HD), index)

    def means_index(b, kv, qi, pt):
        s = step_of(b, kv, qi)
        return (s // groups, s % groups, 0, 0, 0)

    g_w = 3 * NSA_GROUP
    g_rows = NSA_GROUP * tq
    grid_spec = pltpu.PrefetchScalarGridSpec(
        num_scalar_prefetch=1,
        grid=(batch, NSA_KV, nq),
        in_specs=[pl.BlockSpec((tq, q_lanes), lambda b, kv, qi, pt: (b * nq + qi, kv))]
        + [kv_spec(br, part) for br in range(3) for part in range(2)]
        + [pl.BlockSpec((None, g_w, tq), lambda b, kv, qi, pt: (kv, 0, b * nq + qi)),
           pl.BlockSpec((None, g_w, 1), lambda b, kv, qi, pt: (kv, 0, 0))]
        + [page_spec(i) for i in range(pages_per_step)],
        out_specs=[pl.BlockSpec((tq, q_lanes), lambda b, kv, qi, pt: (b * nq + qi, kv)),
                   pl.BlockSpec((1, pages_per_step, per_page, KV_ROWS, NSA_HD), means_index)],
        scratch_shapes=[
            pltpu.VMEM((seq // NSA_BLOCK, NSA_HD), F32),
            pltpu.VMEM((seq // NSA_BLOCK, NSA_HD), F32),
            pltpu.VMEM((seq, NSA_HD), BF16),
            pltpu.VMEM((seq // tk, vt_rows, tk), BF16),
            pltpu.VMEM((seq, NSA_HD), BF16),
            pltpu.VMEM((seq // tk, vt_rows, tk), BF16),
            pltpu.VMEM((g_rows, NSA_HD), BF16),
            pltpu.VMEM((seq // NSA_BLOCK, tq), F32),
            pltpu.VMEM((1, g_rows), F32),
            pltpu.VMEM((vt_rows, g_rows), F32),
            pltpu.VMEM((1, g_rows), F32),
            pltpu.VMEM((vt_rows, g_rows), F32),
        ],
    )
    return pl.pallas_call(
        functools.partial(_nsa_prompt_kernel, seq=seq, tq=tq, tk=tk, pages_per_step=pages_per_step),
        grid_spec=grid_spec,
        out_shape=[jax.ShapeDtypeStruct((batch * seq, NSA_Q_W), BF16),
                   jax.ShapeDtypeStruct((seqs, n_pages, per_page, KV_ROWS, NSA_HD), F32)],
        compiler_params=_params("arbitrary", "arbitrary", "arbitrary"),
        name="nsa_prompt",
    )(page_table, main, main, main, main, main, main, main, gates, b_gate, *([cache_cmp] * pages_per_step))


def _kv_rows_out_kernel(c_ref, s_ref, w_ref, oc_ref, os_ref, ow_ref, *, rows, n_c):
    def spread(src, dst):
        for r in range(KV_ROWS):
            dst[pl.ds(r, rows, stride=KV_ROWS), :] = src[:, r * NSA_HD:(r + 1) * NSA_HD]

    spread(c_ref, oc_ref)
    spread(s_ref, os_ref)

    @pl.when(pl.program_id(1) == n_c - 1)
    def _():
        spread(w_ref, ow_ref)


def _kv_rows_out(main, batch, seq, keep):
    rows = keep
    n_c = seq // rows
    col0 = NSA_Q_W // NSA_KV_W
    tile = lambda br: pl.BlockSpec((rows, NSA_KV_W), lambda b, c: (b * n_c + c, col0 + br))
    out_tile = pl.BlockSpec((rows * KV_ROWS, NSA_HD), lambda b, c: (b * n_c + c, 0))
    return pl.pallas_call(
        functools.partial(_kv_rows_out_kernel, rows=rows, n_c=n_c),
        grid=(batch, n_c),
        in_specs=[tile(0), tile(1),
                  pl.BlockSpec((rows, NSA_KV_W), lambda b, c: (b * n_c + n_c - 1, col0 + 2))],
        out_specs=[out_tile, out_tile, pl.BlockSpec((rows * KV_ROWS, NSA_HD), lambda b, c: (b, 0))],
        out_shape=[jax.ShapeDtypeStruct((batch * seq * KV_ROWS, NSA_HD), F32)] * 2
        + [jax.ShapeDtypeStruct((batch * keep * KV_ROWS, NSA_HD), F32)],
        compiler_params=_params("arbitrary", "arbitrary"),
        name="nsa_kv_rows_out",
    )(main, main, main)


def _page_specs(n):
    return [pl.BlockSpec((1, PAGE * KV_ROWS, NSA_HD), lambda b, p, pt, i=i: (pt[b, p * n + i], 0, 0))
            for i in range(n)]


def _nsa_sample_kernel(pt_ref, q_ref, cm_ref, *refs, past, n_new, n_steps, per_step):
    pg_refs = refs[:per_step]
    (kvn_ref, wb_ref, kwn_ref, gt_ref, bg_ref, o_ref,
     qr_ref, sel_ref, m_ref, l_ref, acc_ref, oc_ref, ow_ref) = refs[per_step:]
    p = pl.program_id(1)
    k_w = NSA_KV * NSA_HD
    cols = NSA_KV * n_new * NSA_GROUP
    per_kv = n_new * NSA_GROUP
    n_cmp = past // NSA_BLOCK
    n_sel = -(-(past + n_new) // NSA_BLOCK)
    n_sel_pad = sel_ref.shape[0]
    lane = lax.broadcasted_iota(jnp.int32, (1, cols), 1)
    q_pos = past + (lane // NSA_GROUP) % n_new

    def heads_on_lanes(ref, n, part):
        return jnp.concatenate(
            [ref[0, pl.ds(part * NSA_KV + kv, n, stride=KV_ROWS), :] for kv in range(NSA_KV)], axis=1)

    def scores(keys):
        return _dot_nt(keys, qr_ref[...])

    def own_head(x_t):
        out = jnp.zeros((NSA_HD, cols), F32)
        for kv in range(NSA_KV):
            mine = (lane // per_kv) == kv
            out = out + jnp.where(mine, x_t[kv * NSA_HD:(kv + 1) * NSA_HD, :], 0.0)
        return out

    def attend_once(keys, vals, mask):
        p_t = _masked_softmax2(scores(keys), mask, axis=0)
        return own_head(_dot_tn(vals, p_t))

    def online(keys, vals, bias):
        s = scores(keys) + bias
        m_old = m_ref[...]
        m_new = jnp.maximum(m_old, jnp.max(s, axis=0, keepdims=True))
        p_t = jnp.exp2(s - m_new)
        alpha = jnp.exp2(m_old - m_new)
        l_ref[...] = alpha * l_ref[...] + jnp.sum(p_t, axis=0, keepdims=True)
        acc_ref[...] = alpha * acc_ref[...] + own_head(_dot_tn(vals, p_t))
        m_ref[...] = m_new

    @pl.when(p == 0)
    def _():
        q = q_ref[0] * (NSA_HD ** -0.5 * LOG2_E)
        r_kv = lax.broadcasted_iota(jnp.int32, (cols, k_w), 0) // per_kv
        c_kv = lax.broadcasted_iota(jnp.int32, (cols, k_w), 1) // NSA_HD
        qr_ref[...] = jnp.where(r_kv == c_kv, jnp.tile(q, (1, NSA_KV)), 0.0).astype(BF16)
        cm = cm_ref[0]
        blk = lax.broadcasted_iota(jnp.int32, (n_cmp, 1), 0)
        p_t = _masked_softmax2(scores(cm[:, :k_w]), (blk + 1) * NSA_BLOCK - 1 <= q_pos, axis=0)
        oc_ref[...] = own_head(_dot_tn(cm[:, k_w:], p_t))
        same = _ind(lax.broadcasted_iota(jnp.int32, (cols, cols), 0) // NSA_GROUP
                    == lax.broadcasted_iota(jnp.int32, (cols, cols), 1) // NSA_GROUP, BF16)
        imp = sum(jnp.dot(t, same, preferred_element_type=F32) for t in _split3(p_t))
        imp = jnp.concatenate([imp, jnp.zeros((n_sel_pad - n_cmp, cols), F32)], axis=0)
        blk_s = lax.broadcasted_iota(jnp.int32, (n_sel_pad, 1), 0)
        cur = q_pos // NSA_BLOCK
        forced = (blk_s == 0) | ((blk_s >= cur - 1) & (blk_s <= cur))
        score = jnp.where((blk_s > cur) | (blk_s >= n_sel), -jnp.inf, jnp.where(forced, jnp.inf, imp))
        blk_f = blk_s.astype(F32)

        def pick(_, carry):
            left, bias = carry
            top = jnp.max(left, axis=0, keepdims=True)
            first = jnp.min(jnp.where(left == top, blk_f, float(n_sel_pad)), axis=0, keepdims=True)
            hit = (blk_f == first) & (top > -jnp.inf)
            return jnp.where(hit, -jnp.inf, left), jnp.where(hit, 0.0, bias)

        _, bias = lax.fori_loop(0, NSA_TOPK, pick, (score, jnp.full(score.shape, NEG_BIG, F32)))
        sel_ref[...] = bias
        n_wb = wb_ref.shape[1] // KV_ROWS
        kw_all = jnp.concatenate([heads_on_lanes(wb_ref, n_wb, 0), kwn_ref[0][:, :k_w]], axis=0)
        vw_all = jnp.concatenate([heads_on_lanes(wb_ref, n_wb, 1), kwn_ref[0][:, k_w:]], axis=0)
        n_w = kw_all.shape[0]
        w_row = lax.broadcasted_iota(jnp.int32, (n_w, 1), 0)
        w_pos = past - n_wb + w_row
        d = q_pos - w_pos
        ow_ref[...] = attend_once(kw_all, vw_all,
                                  (d >= 0) & (d < NSA_WINDOW) & (w_pos >= 0) & (w_row < n_wb + n_new))
        m_ref[...] = jnp.full(m_ref.shape, NEG_BIG, F32)
        l_ref[...] = jnp.zeros(l_ref.shape, F32)
        acc_ref[...] = jnp.zeros(acc_ref.shape, F32)

    def page_part(part):
        return jnp.concatenate([heads_on_lanes(pg_ref, PAGE, part) for pg_ref in pg_refs], axis=0)

    blk_per_step = per_step * PAGE // NSA_BLOCK
    online(page_part(0), page_part(1), jnp.concatenate(
        [jnp.broadcast_to(sel_ref[pl.ds(p * blk_per_step + j, 1), :], (NSA_BLOCK, cols))
         for j in range(blk_per_step)], axis=0))

    @pl.when(p == n_steps - 1)
    def _():
        kvn = kvn_ref[0]
        n_row = lax.broadcasted_iota(jnp.int32, (kvn.shape[0], 1), 0)
        bias_new = jnp.where((n_row < n_new) & (past + n_row <= q_pos), 0.0, NEG_BIG)
        online(kvn[:, :k_w], kvn[:, k_w:], bias_new + sel_ref[pl.ds(past // NSA_BLOCK, 1), :])
        o_s = acc_ref[...] / jnp.maximum(l_ref[...], TINY)
        gates = jax.nn.sigmoid(gt_ref[0] + bg_ref[...])
        o_t = gates[0:1, :] * oc_ref[...] + gates[1:2, :] * o_s + gates[2:3, :] * ow_ref[...]
        eye = _ind(lax.broadcasted_iota(jnp.int32, (NSA_HD, NSA_HD), 0)
                   == lax.broadcasted_iota(jnp.int32, (NSA_HD, NSA_HD), 1), BF16)
        o_ref[0] = sum(lax.dot_general(t, eye, (((0,), (0,)), ((), ())), preferred_element_type=F32)
                       for t in _split3(o_t))


def _nsa_sample(q_rows, cmp_means, cache_slc, page_table, kv_new, win_buf, kw_new, gates, b_gate,
                past, n_new):
    seqs, n_pages = page_table.shape
    cols = q_rows.shape[1]
    n_cmp = cmp_means.shape[1]
    n_sel_pad = -(-(-(-(past + n_new) // NSA_BLOCK)) // 8) * 8
    n_wb = win_buf.shape[1]
    pad_new = kv_new.shape[1]
    per_step = 8
    n_steps = n_pages // per_step
    seq_map = lambda b, p, pt: (b, 0, 0)
    return pl.pallas_call(
        functools.partial(_nsa_sample_kernel, past=past, n_new=n_new, n_steps=n_steps, per_step=per_step),
        grid_spec=pltpu.PrefetchScalarGridSpec(
            num_scalar_prefetch=1,
            grid=(seqs, n_steps),
            in_specs=[
                pl.BlockSpec((1, cols, NSA_HD), seq_map),
                pl.BlockSpec((1, n_cmp, NSA_KV_W), seq_map),
                *_page_specs(per_step),
                pl.BlockSpec((1, pad_new, NSA_KV_W), seq_map),
                pl.BlockSpec((1, n_wb, NSA_HD), seq_map),
                pl.BlockSpec((1, pad_new, NSA_KV_W), seq_map),
                pl.BlockSpec((1, 3, cols), seq_map),
                pl.BlockSpec((3, cols), lambda b, p, pt: (0, 0)),
            ],
            out_specs=pl.BlockSpec((1, cols, NSA_HD), seq_map),
            scratch_shapes=[
                pltpu.VMEM((cols, NSA_KV * NSA_HD), BF16),
                pltpu.VMEM((n_sel_pad, cols), F32),
                pltpu.VMEM((1, cols), F32),
                pltpu.VMEM((1, cols), F32),
                pltpu.VMEM((NSA_HD, cols), F32),
                pltpu.VMEM((NSA_HD, cols), F32),
                pltpu.VMEM((NSA_HD, cols), F32),
            ],
        ),
        out_shape=jax.ShapeDtypeStruct((seqs, cols, NSA_HD), F32),
        compiler_params=_params("parallel", "arbitrary"),
        name="nsa_sample",
    )(page_table, q_rows, cmp_means, *([cache_slc] * per_step), kv_new, win_buf, kw_new, gates, b_gate)


def kernel(x_prompt, x_sample, c_prompt, c_sample, state_gla, cache_cmp_kv, cache_slc_kv, cache_win_kv,
           page_table, ada_w, ada_b, gla_w_in, gla_w_alpha, gla_b_alpha, gla_norm_g, gla_w_out, nsa_w_in,
           nsa_b_gate, nsa_w_out, ln_mix_g, ln_mix_b, ffn_w_in, ffn_w_out, ln_ffn_g, ln_ffn_b):
    batch, seq, _ = x_prompt.shape
    seqs, n_new, _ = x_sample.shape
    n_pages = page_table.shape[1]
    past = n_pages * PAGE
    rows_s = seqs * n_new
    tm_p = seq
    w_out_mix = (gla_w_out.astype(BF16), nsa_w_out.astype(BF16))
    w_out_ffn = ffn_w_out.astype(BF16)

    pad_rows = 16 - (batch + seqs)
    c_all = jnp.concatenate([c_prompt, c_sample, jnp.zeros((pad_rows, D_MODEL), F32)], axis=0)
    mod = _adaln(c_all, ada_w, ada_b).reshape(DEPTH, 16, 6, D_MODEL)

    def mods(layer):
        mp = [mod[layer, :batch, i].reshape(batch, 1, D_MODEL) for i in range(6)]
        ms = [jnp.repeat(mod[layer, batch:batch + seqs, i], n_new, axis=0).reshape(1, rows_s, D_MODEL)
              for i in range(6)]
        return mp, ms

    def mixer_out(o, x, layer, m, tm, tag):
        return _out_ln(o, w_out_mix[layer % 2], layer // 2, x, m[2], ln_mix_g, ln_mix_b, layer, min(tm, 512), tag)

    def ffn(xp, xs, layer, mp, ms):
        act_p, act_s = _ffn_in(xp, mp[4], mp[3], xs, ms[4], ms[3], ffn_w_in, layer, tm_p, "ffn_in_%d" % layer)
        out = lambda act, x, m, tm, tag: _out_ln(act, w_out_ffn, layer, x, m[5], ln_ffn_g, ln_ffn_b, layer, tm,
                                                  "ffn_out_%s%d" % (tag, layer))
        return out(act_p, xp, mp, 256, "prompt"), out(act_s, xs, ms, rows_s, "sample")

    xp = x_prompt.reshape(batch * seq, D_MODEL)
    xs = x_sample.reshape(rows_s, D_MODEL)

    mp, ms = mods(0)
    main_p, tail_p, main_s, tail_s = _proj(xp, mp[1], mp[0], xs, ms[1], ms[0], gla_w_in, 0,
                                           GLA_MAIN_W, tm_p, "gla_in")
    gated_p, gla_state_p = _gla(main_p, tail_p, gla_w_alpha[0], gla_b_alpha[0], gla_norm_g[0], None,
                                batch, seq, 128, 128)
    chunk_s = GLA_SUB

    def pad_steps(a):
        a = a.reshape(seqs, n_new, a.shape[-1])
        return jnp.pad(a, ((0, 0), (0, chunk_s - n_new), (0, 0))).reshape(seqs * chunk_s, a.shape[-1])

    gated_s, gla_state_s = _gla(pad_steps(main_s), pad_steps(tail_s), gla_w_alpha[0], gla_b_alpha[0],
                                gla_norm_g[0], state_gla[:, 0], seqs, chunk_s, chunk_s, n_new)
    gated_s = gated_s.reshape(seqs, chunk_s, -1)[:, :n_new].reshape(rows_s, -1)
    xp = mixer_out(gated_p, xp, 0, mp, tm_p, "gla_out_prompt")
    xs = mixer_out(gated_s, xs, 0, ms, rows_s, "gla_out_sample")
    xp, xs = ffn(xp, xs, 0, mp, ms)

    mp, ms = mods(1)
    main_p, tail_p, main_s, tail_s = _proj(xp, mp[1], mp[0], xs, ms[1], ms[0], nsa_w_in, 0,
                                           NSA_MAIN_W, tm_p, "nsa_in")
    n_gate = 3 * NSA_HEADS
    g_w = 3 * NSA_GROUP
    gates_p = tail_p[:, :n_gate].reshape(batch * seq, NSA_KV, g_w).transpose(1, 2, 0)
    b_gate_p = nsa_b_gate[0].reshape(NSA_KV, g_w, 1)
    n_phys = cache_cmp_kv.shape[0]
    page_rows = lambda cache: cache.reshape(n_phys, PAGE * KV_ROWS, NSA_HD)
    o_p, cmp_means = _nsa_prompt(main_p, gates_p, b_gate_p, batch, seq, page_rows(cache_cmp_kv), page_table)

    kv_rows = lambda m, br: m[:, NSA_Q_W + br * NSA_KV_W:NSA_Q_W + (br + 1) * NSA_KV_W]
    kv_shape = (2, NSA_KV, NSA_HD)
    keep_p = min(NSA_WINDOW, seq)
    cmp_rows, slc_rows, win_rows = _kv_rows_out(main_p, batch, seq, keep_p)
    cmp_kv_p = cmp_rows.reshape(batch, 1, seq, *kv_shape)
    slc_kv_p = slc_rows.reshape(batch, 1, seq, *kv_shape)
    win_kv_p = win_rows.reshape(batch, 1, keep_p, *kv_shape)

    cmp_kv_s = kv_rows(main_s, 0).reshape(seqs, 1, n_new, *kv_shape)
    slc_kv_s = kv_rows(main_s, 1).reshape(seqs, 1, n_new, *kv_shape)
    kw_new = kv_rows(main_s, 2).reshape(seqs, n_new, NSA_KV_W)
    n_wb = cache_win_kv.shape[2]
    win_buf = cache_win_kv[:, 0].reshape(seqs, n_wb * KV_ROWS, NSA_HD)
    all_w = jnp.concatenate([win_buf, kw_new.reshape(seqs, n_new * KV_ROWS, NSA_HD)], axis=1)
    keep_s = min(NSA_WINDOW, n_wb + n_new)
    win_kv_s = all_w[:, (n_wb + n_new - keep_s) * KV_ROWS:].reshape(seqs, 1, keep_s, *kv_shape)

    cmp_means = cmp_means.reshape(seqs, past // NSA_BLOCK, NSA_KV_W)
    pad_new = 8
    pad8 = lambda a: jnp.pad(a, ((0, 0), (0, pad_new - n_new), (0, 0)))
    cols = NSA_KV * n_new * NSA_GROUP
    q_rows = (main_s[:, :NSA_Q_W].reshape(seqs, n_new, NSA_KV, NSA_GROUP, NSA_HD)
              .transpose(0, 2, 1, 3, 4).reshape(seqs, cols, NSA_HD))
    gates_s = (tail_s[:, :n_gate].reshape(seqs, n_new, NSA_KV, NSA_GROUP, 3)
               .transpose(0, 4, 2, 1, 3).reshape(seqs, 3, cols))
    b_gate_s = jnp.broadcast_to(nsa_b_gate[0].reshape(1, NSA_KV, NSA_GROUP, 3), (n_new, NSA_KV, NSA_GROUP, 3))
    b_gate_s = b_gate_s.transpose(3, 1, 0, 2).reshape(3, cols)
    o_s = _nsa_sample(q_rows, cmp_means, page_rows(cache_slc_kv), page_table,
                      pad8(slc_kv_s.reshape(seqs, n_new, NSA_KV_W)), win_buf, pad8(kw_new), gates_s, b_gate_s,
                      past, n_new)
    o_s = (o_s.reshape(seqs, NSA_KV, n_new, NSA_GROUP, NSA_HD).transpose(0, 2, 1, 3, 4)
           .reshape(rows_s, NSA_Q_W).astype(BF16))

    xp = mixer_out(o_p, xp, 1, mp, tm_p, "nsa_out_prompt")
    xs = mixer_out(o_s, xs, 1, ms, rows_s, "nsa_out_sample")
    xp, xs = ffn(xp, xs, 1, mp, ms)

    return (xp.reshape(batch, seq, D_MODEL), xs.reshape(seqs, n_new, D_MODEL),
            gla_state_p[:, None], gla_state_s[:, None], cmp_kv_p, cmp_kv_s, slc_kv_p, slc_kv_s,
            win_kv_p, win_kv_s)
```

```python
import functools

import jax
import jax.numpy as jnp
from jax import lax
from jax.experimental import pallas as pl
from jax.experimental.pallas import tpu as pltpu

F32 = jnp.float32
BF16 = jnp.bfloat16

D_MODEL = 2048
DEPTH = 2
DN_ALPHA = (2 * DEPTH) ** 0.25
LN_EPS = 1e-5
D_FF = 5632
GLA_HEADS = 4
GLA_DK = 256
GLA_DV = 512
GLA_RANK = 16
GLA_TAU = 16.0
GLA_SUB = 32
GLA_MAIN_W = 2 * GLA_HEADS * GLA_DK + 2 * GLA_HEADS * GLA_DV
NSA_HEADS = 16
NSA_KV = 4
NSA_HD = 128
NSA_GROUP = NSA_HEADS // NSA_KV
NSA_BLOCK = 64
NSA_TOPK = 16
NSA_WINDOW = 512
NSA_Q_W = NSA_HEADS * NSA_HD
NSA_KV_W = 2 * NSA_KV * NSA_HD
NSA_MAIN_W = NSA_Q_W + 3 * NSA_KV_W
PAGE = 128
KV_ROWS = 2 * NSA_KV

LANES = 128
BF16_ROWS = 16
LOG2_E = 1.4426950408889634
V7X_VMEM_BYTES = 64 * 1024 * 1024
VMEM_LIMIT = V7X_VMEM_BYTES * 7 // 8

NEG_BIG = -1e30
TINY = float(jnp.finfo(jnp.float32).tiny)


def _params(*sem):
    return pltpu.CompilerParams(dimension_semantics=sem, vmem_limit_bytes=VMEM_LIMIT)


def _dot(a, b):
    return jnp.dot(a.astype(BF16), b.astype(BF16), preferred_element_type=F32)


def _dot_nt(a, b):
    return lax.dot_general(a.astype(BF16), b.astype(BF16), (((1,), (1,)), ((), ())),
                           preferred_element_type=F32)


def _dot_tn(a, b):
    return lax.dot_general(a.astype(BF16), b.astype(BF16), (((0,), (0,)), ((), ())),
                           preferred_element_type=F32)


def _split3(x):
    hi = x.astype(BF16)
    r1 = x - hi.astype(F32)
    mid = r1.astype(BF16)
    lo = (r1 - mid.astype(F32)).astype(BF16)
    return hi, mid, lo


def _silu(x):
    return x * jax.nn.sigmoid(x)


def _ind(cond, dtype=F32):
    return jnp.where(cond, 1.0, 0.0).astype(dtype)


def _masked_softmax2(s2, mask, axis):
    s2 = jnp.where(mask, s2, -jnp.inf)
    m = jnp.max(s2, axis=axis, keepdims=True)
    m = jnp.where(jnp.isfinite(m), m, 0.0)
    e = jnp.where(mask, jnp.exp2(s2 - m), 0.0)
    return e / jnp.maximum(jnp.sum(e, axis=axis, keepdims=True), TINY)


def _adaln_kernel(c_ref, w_ref, b_ref, o_ref):
    o_ref[...] = _dot(_silu(c_ref[...]), w_ref[...]) + b_ref[...]


def _adaln(c_all, ada_w, ada_b):
    rows = c_all.shape[0]
    n = ada_w.shape[2]
    tn = 1024
    return pl.pallas_call(
        _adaln_kernel,
        grid=(DEPTH, n // tn),
        in_specs=[
            pl.BlockSpec((rows, D_MODEL), lambda l, j: (0, 0)),
            pl.BlockSpec((None, D_MODEL, tn), lambda l, j: (l, 0, j)),
            pl.BlockSpec((None, 1, tn), lambda l, j: (l, 0, j)),
        ],
        out_specs=pl.BlockSpec((None, rows, tn), lambda l, j: (l, 0, j)),
        out_shape=jax.ShapeDtypeStruct((DEPTH, rows, n), F32),
        compiler_params=_params("parallel", "parallel"),
        name="adaln",
    )(c_all, ada_w, ada_b.reshape(DEPTH, 1, n))


def _modulate(x_ref, sc_ref, sh_ref):
    return (x_ref[...] * (1.0 + sc_ref[...]) + sh_ref[...]).astype(BF16)


def _proj_kernel(x_ref, sc_ref, sh_ref, xs_ref, scs_ref, shs_ref, w_ref, wt_ref,
                 o_ref, ot_ref, os_ref, ots_ref, h_ref, *, n_i):
    i, j = pl.program_id(0), pl.program_id(1)
    tm = x_ref.shape[0]

    @pl.when(j == 0)
    def _():
        h_ref[:tm] = _modulate(x_ref, sc_ref, sh_ref)
        ot_ref[...] = _dot(h_ref[:tm], wt_ref[...])

    @pl.when(i < n_i - 1)
    def _():
        o_ref[...] = _dot(h_ref[:tm], w_ref[...])

    @pl.when(i == n_i - 1)
    def _():
        @pl.when(j == 0)
        def _():
            h_ref[tm:] = _modulate(xs_ref, scs_ref, shs_ref)
            ots_ref[...] = _dot(h_ref[tm:], wt_ref[...])

        both = _dot(h_ref[...], w_ref[...])
        o_ref[...] = both[:tm]
        os_ref[...] = both[tm:]


def _proj(x, sc, sh, xs, scs, shs, w, layer, n_main, tm, name):
    m, ms = x.shape[0], xs.shape[0]
    n_tail = w.shape[2] - n_main
    w_tail = jnp.pad(w[layer, :, n_main:], ((0, 0), (0, LANES - n_tail)))
    tn = 512
    n_i = m // tm
    tiles_per_group = n_i // sc.shape[0]
    mod_spec = pl.BlockSpec((None, sc.shape[1], D_MODEL), lambda i, j: (i // tiles_per_group, 0, 0))
    const2 = lambda i, j: (0, 0)
    const3 = lambda i, j: (0, 0, 0)
    last_tile_col = lambda i, j: (0, jnp.where(i == n_i - 1, j, 0))
    return pl.pallas_call(
        functools.partial(_proj_kernel, n_i=n_i),
        grid=(n_i, n_main // tn),
        in_specs=[
            pl.BlockSpec((tm, D_MODEL), lambda i, j: (i, 0), pipeline_mode=pl.Buffered(1)),
            mod_spec,
            mod_spec,
            pl.BlockSpec((ms, D_MODEL), const2),
            pl.BlockSpec((None, ms, D_MODEL), const3),
            pl.BlockSpec((None, ms, D_MODEL), const3),
            pl.BlockSpec((None, D_MODEL, tn), lambda i, j: (layer, 0, j)),
            pl.BlockSpec((D_MODEL, LANES), const2),
        ],
        out_specs=[
            pl.BlockSpec((tm, tn), lambda i, j: (i, j)),
            pl.BlockSpec((tm, LANES), lambda i, j: (i, 0)),
            pl.BlockSpec((ms, tn), last_tile_col),
            pl.BlockSpec((ms, LANES), const2),
        ],
        out_shape=[jax.ShapeDtypeStruct((m, n_main), F32), jax.ShapeDtypeStruct((m, LANES), F32),
                   jax.ShapeDtypeStruct((ms, n_main), F32), jax.ShapeDtypeStruct((ms, LANES), F32)],
        scratch_shapes=[pltpu.VMEM((tm + ms, D_MODEL), BF16)],
        compiler_params=_params("arbitrary", "arbitrary"),
        name=name,
    )(x, sc, sh, xs, scs, shs, w, w_tail)


def _out_ln_kernel(a_ref, w_ref, x_ref, gt_ref, as_ref, xs_ref, gts_ref, g_ref, b_ref, o_ref, os_ref,
                   *, n_sub, n_i):
    def deep_norm(x, gate, m):
        y = DN_ALPHA * x + gate * m
        mu = jnp.mean(y, axis=-1, keepdims=True)
        yc = y - mu
        var = jnp.mean(yc * yc, axis=-1, keepdims=True)
        return yc * lax.rsqrt(var + LN_EPS) * g_ref[...] + b_ref[...]

    tm = a_ref.shape[0]
    rows = [slice(s * (tm // n_sub), (s + 1) * (tm // n_sub)) for s in range(n_sub)]
    proj = [jnp.dot(a_ref[rs, :], w_ref[...], preferred_element_type=F32) for rs in rows]
    for rs, m in zip(rows, proj):
        o_ref[rs, :] = deep_norm(x_ref[rs, :], gt_ref[...], m)

    @pl.when(pl.program_id(0) == n_i - 1)
    def _():
        os_ref[...] = deep_norm(xs_ref[...], gts_ref[...],
                                jnp.dot(as_ref[...], w_ref[...], preferred_element_type=F32))


def _out_ln(a, x, gate, a_s, x_s, gate_s, w, w_layer, ln_g, ln_b, ln_layer, tm, name):
    m, kdim = a.shape
    ms = a_s.shape[0]
    n_i = m // tm
    tiles_per_group = n_i // gate.shape[0]
    vec_spec = pl.BlockSpec((None, 1, D_MODEL), lambda i: (ln_layer, 0, 0))
    return pl.pallas_call(
        functools.partial(_out_ln_kernel, n_sub=2 if tm % 512 == 0 else 1, n_i=n_i),
        grid=(n_i,),
        in_specs=[
            pl.BlockSpec((tm, kdim), lambda i: (i, 0)),
            pl.BlockSpec((None, kdim, D_MODEL), lambda i: (w_layer, 0, 0), pipeline_mode=pl.Buffered(1)),
            pl.BlockSpec((tm, D_MODEL), lambda i: (i, 0)),
            pl.BlockSpec((None, 1, D_MODEL), lambda i: (i // tiles_per_group, 0, 0)),
            pl.BlockSpec((ms, kdim), lambda i: (0, 0)),
            pl.BlockSpec((ms, D_MODEL), lambda i: (0, 0)),
            pl.BlockSpec((None, ms, D_MODEL), lambda i: (0, 0, 0)),
            vec_spec,
            vec_spec,
        ],
        out_specs=[pl.BlockSpec((tm, D_MODEL), lambda i: (i, 0)),
                   pl.BlockSpec((ms, D_MODEL), lambda i: (0, 0))],
        out_shape=[jax.ShapeDtypeStruct((m, D_MODEL), F32), jax.ShapeDtypeStruct((ms, D_MODEL), F32)],
        compiler_params=_params("arbitrary"),
        name=name,
    )(a, w, x, gate, a_s, x_s, gate_s, ln_g.reshape(-1, 1, D_MODEL), ln_b.reshape(-1, 1, D_MODEL))


def _ffn_in_kernel(x_ref, sc_ref, sh_ref, xs_ref, scs_ref, shs_ref, wa_ref, wu_ref,
                   o_ref, os_ref, h_ref, *, n_i):
    i, j = pl.program_id(0), pl.program_id(1)
    tm = x_ref.shape[0]

    @pl.when(j == 0)
    def _():
        h_ref[:tm] = _modulate(x_ref, sc_ref, sh_ref)

    wa = wa_ref[...].astype(BF16)
    wu = wu_ref[...].astype(BF16)

    def swiglu_in(h):
        return (_silu(_dot(h, wa)) * _dot(h, wu)).astype(BF16)

    @pl.when(i < n_i - 1)
    def _():
        o_ref[...] = swiglu_in(h_ref[:tm])

    @pl.when(i == n_i - 1)
    def _():
        @pl.when(j == 0)
        def _():
            h_ref[tm:] = _modulate(xs_ref, scs_ref, shs_ref)

        both = swiglu_in(h_ref[...])
        o_ref[...] = both[:tm]
        os_ref[...] = both[tm:]


def _ffn_in(x, sc, sh, xs, scs, shs, w, layer, tm, name):
    m, ms = x.shape[0], xs.shape[0]
    tn = 256
    nj = D_FF // tn
    n_i = m // tm
    tiles_per_group = n_i // sc.shape[0]
    mod_spec = pl.BlockSpec((None, sc.shape[1], D_MODEL), lambda i, j: (i // tiles_per_group, 0, 0))
    const3 = lambda i, j: (0, 0, 0)
    return pl.pallas_call(
        functools.partial(_ffn_in_kernel, n_i=n_i),
        grid=(n_i, nj),
        in_specs=[
            pl.BlockSpec((tm, D_MODEL), lambda i, j: (i, 0), pipeline_mode=pl.Buffered(1)),
            mod_spec,
            mod_spec,
            pl.BlockSpec((ms, D_MODEL), lambda i, j: (0, 0)),
            pl.BlockSpec((None, ms, D_MODEL), const3),
            pl.BlockSpec((None, ms, D_MODEL), const3),
            pl.BlockSpec((None, D_MODEL, tn), lambda i, j: (layer, 0, j)),
            pl.BlockSpec((None, D_MODEL, tn), lambda i, j: (layer, 0, j + nj)),
        ],
        out_specs=[
            pl.BlockSpec((tm, tn), lambda i, j: (i, j)),
            pl.BlockSpec((ms, tn), lambda i, j: (0, jnp.where(i == n_i - 1, j, 0))),
        ],
        out_shape=[jax.ShapeDtypeStruct((m, D_FF), BF16), jax.ShapeDtypeStruct((ms, D_FF), BF16)],
        scratch_shapes=[pltpu.VMEM((tm + ms, D_MODEL), BF16)],
        compiler_params=_params("arbitrary", "arbitrary"),
        name=name,
    )(x, sc, sh, xs, scs, shs, w, w)


def _gla_kernel(*refs, chunk, t_valid, has_s0):
    if has_s0:
        q_ref, k_ref, v_ref, r_ref, a_ref, wa_ref, ba_ref, g_ref, s0_ref, o_ref, st_ref = refs
    else:
        q_ref, k_ref, v_ref, r_ref, a_ref, wa_ref, ba_ref, g_ref, o_ref, st_ref = refs
    sub = GLA_SUB
    n_sub = chunk // sub
    anchor = sub // 2 - 1

    @pl.when(pl.program_id(1) == 0)
    def _():
        if has_s0:
            st_ref[...] = s0_ref[...]
        else:
            st_ref[...] = jnp.zeros(st_ref.shape, F32)

    row = lax.broadcasted_iota(jnp.int32, (chunk, 1), 0)
    tri = _ind(lax.broadcasted_iota(jnp.int32, (chunk, chunk), 1)
               <= lax.broadcasted_iota(jnp.int32, (chunk, chunk), 0), BF16)
    ones = jnp.ones((chunk, LANES), BF16)
    causal = (lax.broadcasted_iota(jnp.int32, (sub, sub), 1)
              <= lax.broadcasted_iota(jnp.int32, (sub, sub), 0))
    heads = range(GLA_HEADS)
    dk = [slice(h * GLA_DK, (h + 1) * GLA_DK) for h in heads]
    dv = [slice(h * GLA_DV, (h + 1) * GLA_DV) for h in heads]
    subs = [slice(s * sub, (s + 1) * sub) for s in range(n_sub)]

    z = _dot(a_ref[...], wa_ref[...]) + ba_ref[...]
    lb = (jnp.minimum(z, 0.0) - jnp.log1p(jnp.exp(-jnp.abs(z)))) * (1.0 / GLA_TAU)
    k = k_ref[...]
    if t_valid < chunk:
        lb = jnp.where(row < t_valid, lb, 0.0)
        k = jnp.where(row < t_valid, k, 0.0)
    q = q_ref[...] * (GLA_DK ** -0.5)
    vb = v_ref[...].astype(BF16)
    lb3 = _split3(lb)
    b = sum(jnp.dot(tri, t, preferred_element_type=F32) for t in lb3)
    dec = jnp.exp(sum(lax.dot_general(t, ones, (((0,), (0,)), ((), ())), preferred_element_type=F32)
                      for t in lb3))
    b_last = b[chunk - 1:chunk, :]
    q_in = (q * jnp.exp(b)).astype(BF16)
    k_out = (k * jnp.exp(b_last - b)).astype(BF16)
    q_diag, k_diag, q_off, k_off = [], [], [], []
    for s, rs in enumerate(subs):
        b_mid = b[s * sub + anchor:s * sub + anchor + 1, :]
        q_diag.append((q[rs] * jnp.exp(b[rs] - b_mid)).astype(BF16))
        k_diag.append((k[rs] * jnp.exp(b_mid - b[rs])).astype(BF16))
        if s > 0:
            prev = slice(0, s * sub)
            b_in = b[s * sub - 1:s * sub, :]
            q_off.append((q[rs] * jnp.exp(b[rs] - b_in)).astype(BF16))
            k_off.append((k[prev] * jnp.exp(b_in - b[prev])).astype(BF16))

    s_old = [st_ref[0, h] for h in heads]
    o_inter = [_dot(q_in[:, dk[h]], s_old[h]) for h in heads]
    a_diag = [[_dot_nt(q_diag[s][:, dk[h]], k_diag[s][:, dk[h]]) for h in heads] for s in range(n_sub)]
    a_off = [[_dot_nt(q_off[s][:, dk[h]], k_off[s][:, dk[h]]) for h in heads] for s in range(n_sub - 1)]
    a_diag = [[jnp.where(causal, a, 0.0).astype(BF16) for a in row_] for row_ in a_diag]
    a_off = [[a.astype(BF16) for a in row_] for row_ in a_off]
    o_intra = []
    for s, rs in enumerate(subs):
        o_s = [_dot(a_diag[s][h], vb[rs, dv[h]]) for h in heads]
        if s > 0:
            o_s = [o_s[h] + _dot(a_off[s - 1][h], vb[0:s * sub, dv[h]]) for h in heads]
        o_intra.append(o_s)
    kv = [_dot_tn(k_out[:, dk[h]], vb[:, dv[h]]) for h in heads]
    for h in heads:
        st_ref[0, h] = s_old[h] * jnp.tile(dec[dk[h], :], (1, GLA_DV // LANES)) + kv[h]
    for h in heads:
        o = o_inter[h] + jnp.concatenate([o_intra[s][h] for s in range(n_sub)], axis=0)
        o = o * lax.rsqrt(jnp.mean(o * o, axis=-1, keepdims=True) + LN_EPS)
        o_ref[:, dv[h]] = (o * g_ref[:, dv[h]] * _silu(r_ref[:, dv[h]])).astype(BF16)


def _gla(main, tail, w_alpha, b_alpha, norm_g, s0, batch, seq, chunk, t_valid):
    n_chunks = seq // chunk
    qk_w = GLA_HEADS * GLA_DK
    v_w = GLA_HEADS * GLA_DV
    wa = jnp.pad(w_alpha, ((0, LANES - GLA_RANK), (0, 0)))
    row_map = lambda b, c: (b * n_chunks + c, 0)
    const = lambda b, c: (0, 0)
    in_specs = [
        pl.BlockSpec((chunk, qk_w), row_map),
        pl.BlockSpec((chunk, qk_w), lambda b, c: (b * n_chunks + c, 1)),
        pl.BlockSpec((chunk, v_w), lambda b, c: (b * n_chunks + c, 1)),
        pl.BlockSpec((chunk, v_w), lambda b, c: (b * n_chunks + c, 2)),
        pl.BlockSpec((chunk, LANES), row_map),
        pl.BlockSpec((LANES, qk_w), const),
        pl.BlockSpec((1, qk_w), const),
        pl.BlockSpec((1, v_w), const),
    ]
    args = [main, main, main, main, tail, wa, b_alpha.reshape(1, qk_w), norm_g.reshape(1, v_w)]
    state_spec = pl.BlockSpec((1, GLA_HEADS, GLA_DK, GLA_DV), lambda b, c: (b, 0, 0, 0))
    if s0 is not None:
        in_specs.append(state_spec)
        args.append(s0)
    return pl.pallas_call(
        functools.partial(_gla_kernel, chunk=chunk, t_valid=t_valid, has_s0=s0 is not None),
        grid=(batch, n_chunks),
        in_specs=in_specs,
        out_specs=[pl.BlockSpec((chunk, v_w), row_map), state_spec],
        out_shape=[jax.ShapeDtypeStruct((batch * seq, v_w), BF16),
                   jax.ShapeDtypeStruct((batch, GLA_HEADS, GLA_DK, GLA_DV), F32)],
        compiler_params=_params("parallel", "arbitrary"),
        name="gla_prompt" if s0 is None else "gla_sample",
    )(*args)


def _select_blocks(imp_t, blk, cur):
    n = imp_t.shape[0]
    forced = (blk == 0) | ((blk >= cur - 1) & (blk <= cur))
    score = jnp.where(blk > cur, -jnp.inf, jnp.where(forced, jnp.inf, imp_t))
    rank = jnp.zeros(score.shape, F32)
    for i in range(n):
        si = score[i:i + 1, :]
        rank = rank + jnp.where(blk > i, _ind(si >= score), _ind(si > score))
    return (rank < NSA_TOPK) & (score > -jnp.inf)


def _nsa_prompt_kernel(pt_ref, q_ref, kc_ref, vc_ref, ks_ref, vs_ref, kw_ref, vw_ref, gt_ref, bg_ref, *refs,
                       seq, tq, tk, pages_per_step):
    page_refs = refs[:pages_per_step]
    (o_ref, means_ref, kcm, vcm, ksb, vst, kwb, vwt, qs_ref, sel_ref,
     ms_ref, accs_ref, mw_ref, accw_ref) = refs[pages_per_step:]
    qi = pl.program_id(2)
    n_blk = seq // NSA_BLOCK
    n_kt = seq // tk
    blk_per_kt = tk // NSA_BLOCK

    @pl.when(qi == 0)
    def _():
        kcm[...] = jnp.sum(kc_ref[...].reshape(n_blk, NSA_BLOCK, NSA_HD), axis=1) * (1.0 / NSA_BLOCK)
        vcm[...] = jnp.sum(vc_ref[...].reshape(n_blk, NSA_BLOCK, NSA_HD), axis=1) * (1.0 / NSA_BLOCK)
        ksb[...] = ks_ref[...].astype(BF16)
        kwb[...] = kw_ref[...].astype(BF16)
        ones_rows = _ind(lax.broadcasted_iota(jnp.int32, (BF16_ROWS, tk), 0) == 0, BF16)
        for kt in range(n_kt):
            rows = slice(kt * tk, (kt + 1) * tk)
            vst[kt] = jnp.concatenate([vs_ref[rows, :].T.astype(BF16), ones_rows], axis=0)
            vwt[kt] = jnp.concatenate([vw_ref[rows, :].T.astype(BF16), ones_rows], axis=0)

    t0 = qi * tq
    q = q_ref[...] * (NSA_HD ** -0.5 * LOG2_E)
    qs_ref[...] = jnp.concatenate(
        [q[:, g * NSA_HD:(g + 1) * NSA_HD] for g in range(NSA_GROUP)], axis=0).astype(BF16)
    qs = qs_ref[...]
    t_row = t0 + lax.broadcasted_iota(jnp.int32, (1, tq), 1)
    t_row_g = jnp.concatenate([t_row] * NSA_GROUP, axis=1)

    kc_b = kcm[...].astype(BF16)
    blk_col = lax.broadcasted_iota(jnp.int32, (n_blk, 1), 0)
    p_t = _masked_softmax2(_dot_nt(kc_b, qs), (blk_col + 1) * NSA_BLOCK - 1 <= t_row_g, axis=0)
    o_c = _dot_tn(vcm[...], p_t)
    imp_t = sum(p_t[:, g * tq:(g + 1) * tq] for g in range(NSA_GROUP))
    sel_ref[...] = jnp.where(_select_blocks(imp_t, blk_col, t_row // NSA_BLOCK), 0.0, NEG_BIG)

    n_loc = lax.broadcasted_iota(jnp.int32, (tk, tq), 0)
    t_loc = lax.broadcasted_iota(jnp.int32, (tk, tq), 1)
    on_lanes = lambda a: jnp.concatenate([a] * NSA_GROUP, axis=1)
    causal = on_lanes(jnp.where(n_loc <= t_loc, 0.0, NEG_BIG))
    far = on_lanes(jnp.where(n_loc > t_loc, 0.0, NEG_BIG))

    def sel_bias(kt):
        return on_lanes(jnp.concatenate(
            [jnp.broadcast_to(sel_ref[pl.ds(kt * blk_per_kt + j, 1), :], (NSA_BLOCK, tq))
             for j in range(blk_per_kt)], axis=0))

    sel_state, win_state = (ms_ref, accs_ref), (mw_ref, accw_ref)
    for m_ref, acc_ref in (sel_state, win_state):
        m_ref[...] = jnp.full(m_ref.shape, NEG_BIG, F32)
        acc_ref[...] = jnp.zeros(acc_ref.shape, F32)

    def steps(*work):
        s = [_dot_nt(k_ref[pl.ds(pl.multiple_of(kt * tk, tk), tk), :], qs_ref[...]) + bias
             for _, k_ref, _, kt, bias in work]
        m_old = [state[0][...] for state, *_ in work]
        m_new = [jnp.maximum(mo, jnp.max(si, axis=0, keepdims=True)) for mo, si in zip(m_old, s)]
        p = [jnp.exp2(si - mn).astype(BF16) for si, mn in zip(s, m_new)]
        pv = [_dot(vt_ref[kt], pi) for (_, _, vt_ref, kt, _), pi in zip(work, p)]
        for (state, *_), mo, mn, pvi in zip(work, m_old, m_new, pv):
            state[1][...] = jnp.exp2(mo - mn) * state[1][...] + pvi
            state[0][...] = mn

    def result(acc_ref):
        acc = acc_ref[...]
        return acc[:NSA_HD] / jnp.maximum(acc[NSA_HD:NSA_HD + 1], TINY)

    steps((sel_state, ksb, vst, qi, sel_bias(qi) + causal), (win_state, kwb, vwt, qi, causal))

    @pl.when(qi >= 1)
    def _():
        steps((sel_state, ksb, vst, qi - 1, sel_bias(qi - 1)), (win_state, kwb, vwt, qi - 1, far))

    def sel_body(i, carry):
        kt = qi - 2 - i
        steps((sel_state, ksb, vst, kt, sel_bias(kt)))
        return carry

    lax.fori_loop(0, qi - 1, sel_body, 0)

    gates = jax.nn.sigmoid(gt_ref[...] + bg_ref[...])
    g_c, g_s, g_w = (jnp.concatenate([gates[3 * g + br:3 * g + br + 1, :] for g in range(NSA_GROUP)], axis=1)
                     for br in range(3))
    o_t = g_c * o_c + g_s * result(accs_ref) + g_w * result(accw_ref)
    for g in range(NSA_GROUP):
        o_ref[:, g * NSA_HD:(g + 1) * NSA_HD] = o_t[:, g * tq:(g + 1) * tq].T.astype(BF16)

    for i, x_ref in enumerate(page_refs):
        x = x_ref[0].reshape(PAGE // NSA_BLOCK, NSA_BLOCK, KV_ROWS, NSA_HD)
        means_ref[0, i] = jnp.sum(x, axis=1) * (1.0 / NSA_BLOCK)


def _nsa_prompt(main, gates, b_gate, batch, seq, cache_cmp, page_table):
    tq = tk = NSA_WINDOW
    assert seq % tk == 0
    vt_rows = NSA_HD + BF16_ROWS
    nq = seq // tq
    q_lanes = NSA_GROUP * NSA_HD
    col0 = NSA_Q_W // NSA_HD
    seqs, n_pages = page_table.shape
    n_steps = batch * NSA_KV * nq
    pages_per_step = seqs * n_pages // n_steps
    groups = n_pages // pages_per_step
    assert pages_per_step * n_steps == seqs * n_pages and groups * pages_per_step == n_pages
    per_page = PAGE // NSA_BLOCK

    def kv_spec(branch, part):
        off = col0 + (2 * branch + part) * NSA_KV
        return pl.BlockSpec((seq, NSA_HD), lambda b, kv, qi, pt: (b, off + kv))

    def step_of(b, kv, qi):
        return (b * NSA_KV + kv) * nq + qi

    def page_spec(i):
        def index(b, kv, qi, pt):
            s = step_of(b, kv, qi)
            return (pt[s // groups, (s % groups) * pages_per_step + i], 0, 0)
        return pl.BlockSpec((1, PAGE * KV_ROWS, NSA_HD), index)

    def means_index(b, kv, qi, pt):
        s = step_of(b, kv, qi)
        return (s // groups, s % groups, 0, 0, 0)

    g_w = 3 * NSA_GROUP
    g_rows = NSA_GROUP * tq
    grid_spec = pltpu.PrefetchScalarGridSpec(
        num_scalar_prefetch=1,
        grid=(batch, NSA_KV, nq),
        in_specs=[pl.BlockSpec((tq, q_lanes), lambda b, kv, qi, pt: (b * nq + qi, kv))]
        + [kv_spec(br, part) for br in range(3) for part in range(2)]
        + [pl.BlockSpec((None, g_w, tq), lambda b, kv, qi, pt: (kv, 0, b * nq + qi)),
           pl.BlockSpec((None, g_w, 1), lambda b, kv, qi, pt: (kv, 0, 0))]
        + [page_spec(i) for i in range(pages_per_step)],
        out_specs=[pl.BlockSpec((tq, q_lanes), lambda b, kv, qi, pt: (b * nq + qi, kv)),
                   pl.BlockSpec((1, pages_per_step, per_page, KV_ROWS, NSA_HD), means_index)],
        scratch_shapes=[
            pltpu.VMEM((seq // NSA_BLOCK, NSA_HD), F32),
            pltpu.VMEM((seq // NSA_BLOCK, NSA_HD), F32),
            pltpu.VMEM((seq, NSA_HD), BF16),
            pltpu.VMEM((seq // tk, vt_rows, tk), BF16),
            pltpu.VMEM((seq, NSA_HD), BF16),
            pltpu.VMEM((seq // tk, vt_rows, tk), BF16),
            pltpu.VMEM((g_rows, NSA_HD), BF16),
            pltpu.VMEM((seq // NSA_BLOCK, tq), F32),
            pltpu.VMEM((1, g_rows), F32),
            pltpu.VMEM((vt_rows, g_rows), F32),
            pltpu.VMEM((1, g_rows), F32),
            pltpu.VMEM((vt_rows, g_rows), F32),
        ],
    )
    return pl.pallas_call(
        functools.partial(_nsa_prompt_kernel, seq=seq, tq=tq, tk=tk, pages_per_step=pages_per_step),
        grid_spec=grid_spec,
        out_shape=[jax.ShapeDtypeStruct((batch * seq, NSA_Q_W), BF16),
                   jax.ShapeDtypeStruct((seqs, n_pages, per_page, KV_ROWS, NSA_HD), F32)],
        compiler_params=_params("arbitrary", "arbitrary", "arbitrary"),
        name="nsa_prompt",
    )(page_table, main, main, main, main, main, main, main, gates, b_gate, *([cache_cmp] * pages_per_step))


def _kv_rows_out_kernel(c_ref, s_ref, w_ref, oc_ref, os_ref, ow_ref, *, rows, n_c):
    def spread(src, dst):
        for r in range(KV_ROWS):
            dst[pl.ds(r, rows, stride=KV_ROWS), :] = src[:, r * NSA_HD:(r + 1) * NSA_HD]

    spread(c_ref, oc_ref)
    spread(s_ref, os_ref)

    @pl.when(pl.program_id(1) == n_c - 1)
    def _():
        spread(w_ref, ow_ref)


def _kv_rows_out(main, batch, seq, keep):
    rows = keep
    n_c = seq // rows
    col0 = NSA_Q_W // NSA_KV_W
    tile = lambda br: pl.BlockSpec((rows, NSA_KV_W), lambda b, c: (b * n_c + c, col0 + br))
    out_tile = pl.BlockSpec((rows * KV_ROWS, NSA_HD), lambda b, c: (b * n_c + c, 0))
    return pl.pallas_call(
        functools.partial(_kv_rows_out_kernel, rows=rows, n_c=n_c),
        grid=(batch, n_c),
        in_specs=[tile(0), tile(1),
                  pl.BlockSpec((rows, NSA_KV_W), lambda b, c: (b * n_c + n_c - 1, col0 + 2))],
        out_specs=[out_tile, out_tile, pl.BlockSpec((rows * KV_ROWS, NSA_HD), lambda b, c: (b, 0))],
        out_shape=[jax.ShapeDtypeStruct((batch * seq * KV_ROWS, NSA_HD), F32)] * 2
        + [jax.ShapeDtypeStruct((batch * keep * KV_ROWS, NSA_HD), F32)],
        compiler_params=_params("arbitrary", "arbitrary"),
        name="nsa_kv_rows_out",
    )(main, main, main)


def _page_specs(n):
    return [pl.BlockSpec((1, PAGE * KV_ROWS, NSA_HD), lambda b, p, pt, i=i: (pt[b, p * n + i], 0, 0))
            for i in range(n)]


def _nsa_sample_kernel(pt_ref, q_ref, cm_ref, *refs, past, n_new, n_steps, per_step):
    pg_refs = refs[:per_step]
    (kvn_ref, wb_ref, kwn_ref, gt_ref, bg_ref, o_ref,
     qr_ref, sel_ref, m_ref, l_ref, acc_ref, oc_ref, ow_ref) = refs[per_step:]
    p = pl.program_id(1)
    k_w = NSA_KV * NSA_HD
    cols = NSA_KV * n_new * NSA_GROUP
    per_kv = n_new * NSA_GROUP
    n_cmp = past // NSA_BLOCK
    n_sel = -(-(past + n_new) // NSA_BLOCK)
    n_sel_pad = sel_ref.shape[0]
    lane = lax.broadcasted_iota(jnp.int32, (1, cols), 1)
    q_pos = past + (lane // NSA_GROUP) % n_new

    def heads_on_lanes(ref, n, part):
        return jnp.concatenate(
            [ref[0, pl.ds(part * NSA_KV + kv, n, stride=KV_ROWS), :] for kv in range(NSA_KV)], axis=1)

    def scores(keys):
        return _dot_nt(keys, qr_ref[...])

    def own_head(x_t):
        out = jnp.zeros((NSA_HD, cols), F32)
        for kv in range(NSA_KV):
            mine = (lane // per_kv) == kv
            out = out + jnp.where(mine, x_t[kv * NSA_HD:(kv + 1) * NSA_HD, :], 0.0)
        return out

    def attend_once(keys, vals, mask):
        p_t = _masked_softmax2(scores(keys), mask, axis=0)
        return own_head(_dot_tn(vals, p_t))

    def online(keys, vals, bias):
        s = scores(keys) + bias
        m_old = m_ref[...]
        m_new = jnp.maximum(m_old, jnp.max(s, axis=0, keepdims=True))
        p_t = jnp.exp2(s - m_new)
        alpha = jnp.exp2(m_old - m_new)
        l_ref[...] = alpha * l_ref[...] + jnp.sum(p_t, axis=0, keepdims=True)
        acc_ref[...] = alpha * acc_ref[...] + own_head(_dot_tn(vals, p_t))
        m_ref[...] = m_new

    @pl.when(p == 0)
    def _():
        q = q_ref[0] * (NSA_HD ** -0.5 * LOG2_E)
        r_kv = lax.broadcasted_iota(jnp.int32, (cols, k_w), 0) // per_kv
        c_kv = lax.broadcasted_iota(jnp.int32, (cols, k_w), 1) // NSA_HD
        qr_ref[...] = jnp.where(r_kv == c_kv, jnp.tile(q, (1, NSA_KV)), 0.0).astype(BF16)
        cm = cm_ref[0]
        blk = lax.broadcasted_iota(jnp.int32, (n_cmp, 1), 0)
        p_t = _masked_softmax2(scores(cm[:, :k_w]), (blk + 1) * NSA_BLOCK - 1 <= q_pos, axis=0)
        oc_ref[...] = own_head(_dot_tn(cm[:, k_w:], p_t))
        same = _ind(lax.broadcasted_iota(jnp.int32, (cols, cols), 0) // NSA_GROUP
                    == lax.broadcasted_iota(jnp.int32, (cols, cols), 1) // NSA_GROUP, BF16)
        imp = sum(jnp.dot(t, same, preferred_element_type=F32) for t in _split3(p_t))
        imp = jnp.concatenate([imp, jnp.zeros((n_sel_pad - n_cmp, cols), F32)], axis=0)
        blk_s = lax.broadcasted_iota(jnp.int32, (n_sel_pad, 1), 0)
        cur = q_pos // NSA_BLOCK
        forced = (blk_s == 0) | ((blk_s >= cur - 1) & (blk_s <= cur))
        score = jnp.where((blk_s > cur) | (blk_s >= n_sel), -jnp.inf, jnp.where(forced, jnp.inf, imp))
        blk_f = blk_s.astype(F32)

        def pick(_, carry):
            left, bias = carry
            top = jnp.max(left, axis=0, keepdims=True)
            first = jnp.min(jnp.where(left == top, blk_f, float(n_sel_pad)), axis=0, keepdims=True)
            hit = (blk_f == first) & (top > -jnp.inf)
            return jnp.where(hit, -jnp.inf, left), jnp.where(hit, 0.0, bias)

        _, bias = lax.fori_loop(0, NSA_TOPK, pick, (score, jnp.full(score.shape, NEG_BIG, F32)))
        sel_ref[...] = bias
        n_wb = wb_ref.shape[1] // KV_ROWS
        kw_all = jnp.concatenate([heads_on_lanes(wb_ref, n_wb, 0), kwn_ref[0][:, :k_w]], axis=0)
        vw_all = jnp.concatenate([heads_on_lanes(wb_ref, n_wb, 1), kwn_ref[0][:, k_w:]], axis=0)
        n_w = kw_all.shape[0]
        w_row = lax.broadcasted_iota(jnp.int32, (n_w, 1), 0)
        w_pos = past - n_wb + w_row
        d = q_pos - w_pos
        ow_ref[...] = attend_once(kw_all, vw_all,
                                  (d >= 0) & (d < NSA_WINDOW) & (w_pos >= 0) & (w_row < n_wb + n_new))
        m_ref[...] = jnp.full(m_ref.shape, NEG_BIG, F32)
        l_ref[...] = jnp.zeros(l_ref.shape, F32)
        acc_ref[...] = jnp.zeros(acc_ref.shape, F32)

    def page_part(part):
        return jnp.concatenate([heads_on_lanes(pg_ref, PAGE, part) for pg_ref in pg_refs], axis=0)

    blk_per_step = per_step * PAGE // NSA_BLOCK
    online(page_part(0), page_part(1), jnp.concatenate(
        [jnp.broadcast_to(sel_ref[pl.ds(p * blk_per_step + j, 1), :], (NSA_BLOCK, cols))
         for j in range(blk_per_step)], axis=0))

    @pl.when(p == n_steps - 1)
    def _():
        kvn = kvn_ref[0]
        n_row = lax.broadcasted_iota(jnp.int32, (kvn.shape[0], 1), 0)
        bias_new = jnp.where((n_row < n_new) & (past + n_row <= q_pos), 0.0, NEG_BIG)
        online(kvn[:, :k_w], kvn[:, k_w:], bias_new + sel_ref[pl.ds(past // NSA_BLOCK, 1), :])
        o_s = acc_ref[...] / jnp.maximum(l_ref[...], TINY)
        gates = jax.nn.sigmoid(gt_ref[0] + bg_ref[...])
        o_t = gates[0:1, :] * oc_ref[...] + gates[1:2, :] * o_s + gates[2:3, :] * ow_ref[...]
        eye = _ind(lax.broadcasted_iota(jnp.int32, (NSA_HD, NSA_HD), 0)
                   == lax.broadcasted_iota(jnp.int32, (NSA_HD, NSA_HD), 1), BF16)
        o_ref[0] = sum(lax.dot_general(t, eye, (((0,), (0,)), ((), ())), preferred_element_type=F32)
                       for t in _split3(o_t))


def _nsa_sample(q_rows, cmp_means, cache_slc, page_table, kv_new, win_buf, kw_new, gates, b_gate,
                past, n_new):
    seqs, n_pages = page_table.shape
    cols = q_rows.shape[1]
    n_cmp = cmp_means.shape[1]
    n_sel_pad = -(-(-(-(past + n_new) // NSA_BLOCK)) // 8) * 8
    n_wb = win_buf.shape[1]
    pad_new = kv_new.shape[1]
    per_step = 8
    n_steps = n_pages // per_step
    seq_map = lambda b, p, pt: (b, 0, 0)
    return pl.pallas_call(
        functools.partial(_nsa_sample_kernel, past=past, n_new=n_new, n_steps=n_steps, per_step=per_step),
        grid_spec=pltpu.PrefetchScalarGridSpec(
            num_scalar_prefetch=1,
            grid=(seqs, n_steps),
            in_specs=[
                pl.BlockSpec((1, cols, NSA_HD), seq_map),
                pl.BlockSpec((1, n_cmp, NSA_KV_W), seq_map),
                *_page_specs(per_step),
                pl.BlockSpec((1, pad_new, NSA_KV_W), seq_map),
                pl.BlockSpec((1, n_wb, NSA_HD), seq_map),
                pl.BlockSpec((1, pad_new, NSA_KV_W), seq_map),
                pl.BlockSpec((1, 3, cols), seq_map),
                pl.BlockSpec((3, cols), lambda b, p, pt: (0, 0)),
            ],
            out_specs=pl.BlockSpec((1, cols, NSA_HD), seq_map),
            scratch_shapes=[
                pltpu.VMEM((cols, NSA_KV * NSA_HD), BF16),
                pltpu.VMEM((n_sel_pad, cols), F32),
                pltpu.VMEM((1, cols), F32),
                pltpu.VMEM((1, cols), F32),
                pltpu.VMEM((NSA_HD, cols), F32),
                pltpu.VMEM((NSA_HD, cols), F32),
                pltpu.VMEM((NSA_HD, cols), F32),
            ],
        ),
        out_shape=jax.ShapeDtypeStruct((seqs, cols, NSA_HD), F32),
        compiler_params=_params("parallel", "arbitrary"),
        name="nsa_sample",
    )(page_table, q_rows, cmp_means, *([cache_slc] * per_step), kv_new, win_buf, kw_new, gates, b_gate)


def kernel(x_prompt, x_sample, c_prompt, c_sample, state_gla, cache_cmp_kv, cache_slc_kv, cache_win_kv,
           page_table, ada_w, ada_b, gla_w_in, gla_w_alpha, gla_b_alpha, gla_norm_g, gla_w_out, nsa_w_in,
           nsa_b_gate, nsa_w_out, ln_mix_g, ln_mix_b, ffn_w_in, ffn_w_out, ln_ffn_g, ln_ffn_b):
    batch, seq, _ = x_prompt.shape
    seqs, n_new, _ = x_sample.shape
    n_pages = page_table.shape[1]
    past = n_pages * PAGE
    rows_s = seqs * n_new
    tm_p = seq
    w_out_mix = (gla_w_out.astype(BF16), nsa_w_out.astype(BF16))
    w_out_ffn = ffn_w_out.astype(BF16)

    pad_rows = 16 - (batch + seqs)
    c_all = jnp.concatenate([c_prompt, c_sample, jnp.zeros((pad_rows, D_MODEL), F32)], axis=0)
    mod = _adaln(c_all, ada_w, ada_b).reshape(DEPTH, 16, 6, D_MODEL)

    def mods(layer):
        mp = [mod[layer, :batch, i].reshape(batch, 1, D_MODEL) for i in range(6)]
        ms = [jnp.repeat(mod[layer, batch:batch + seqs, i], n_new, axis=0).reshape(1, rows_s, D_MODEL)
              for i in range(6)]
        return mp, ms

    def mixer_out(o_p, o_s, xp, xs, layer, mp, ms, tag):
        return _out_ln(o_p, xp, mp[2], o_s, xs, ms[2], w_out_mix[layer % 2], layer // 2, ln_mix_g, ln_mix_b,
                       layer, 512, tag)

    def ffn(xp, xs, layer, mp, ms):
        act_p, act_s = _ffn_in(xp, mp[4], mp[3], xs, ms[4], ms[3], ffn_w_in, layer, tm_p, "ffn_in_%d" % layer)
        return _out_ln(act_p, xp, mp[5], act_s, xs, ms[5], w_out_ffn, layer, ln_ffn_g, ln_ffn_b, layer, 256,
                       "ffn_out_%d" % layer)

    xp = x_prompt.reshape(batch * seq, D_MODEL)
    xs = x_sample.reshape(rows_s, D_MODEL)

    mp, ms = mods(0)
    main_p, tail_p, main_s, tail_s = _proj(xp, mp[1], mp[0], xs, ms[1], ms[0], gla_w_in.astype(BF16), 0,
                                           GLA_MAIN_W, tm_p, "gla_in")
    gated_p, gla_state_p = _gla(main_p, tail_p, gla_w_alpha[0], gla_b_alpha[0], gla_norm_g[0], None,
                                batch, seq, 256, 256)
    chunk_s = GLA_SUB

    def pad_steps(a):
        a = a.reshape(seqs, n_new, a.shape[-1])
        return jnp.pad(a, ((0, 0), (0, chunk_s - n_new), (0, 0))).reshape(seqs * chunk_s, a.shape[-1])

    gated_s, gla_state_s = _gla(pad_steps(main_s), pad_steps(tail_s), gla_w_alpha[0], gla_b_alpha[0],
                                gla_norm_g[0], state_gla[:, 0], seqs, chunk_s, chunk_s, n_new)
    gated_s = gated_s.reshape(seqs, chunk_s, -1)[:, :n_new].reshape(rows_s, -1)
    xp, xs = mixer_out(gated_p, gated_s, xp, xs, 0, mp, ms, "gla_out")
    xp, xs = ffn(xp, xs, 0, mp, ms)

    mp, ms = mods(1)
    main_p, tail_p, main_s, tail_s = _proj(xp, mp[1], mp[0], xs, ms[1], ms[0], nsa_w_in.astype(BF16), 0,
                                           NSA_MAIN_W, tm_p, "nsa_in")
    n_gate = 3 * NSA_HEADS
    g_w = 3 * NSA_GROUP
    gates_p = tail_p[:, :n_gate].reshape(batch * seq, NSA_KV, g_w).transpose(1, 2, 0)
    b_gate_p = nsa_b_gate[0].reshape(NSA_KV, g_w, 1)
    n_phys = cache_cmp_kv.shape[0]
    page_rows = lambda cache: cache.reshape(n_phys, PAGE * KV_ROWS, NSA_HD)
    o_p, cmp_means = _nsa_prompt(main_p, gates_p, b_gate_p, batch, seq, page_rows(cache_cmp_kv), page_table)

    kv_rows = lambda m, br: m[:, NSA_Q_W + br * NSA_KV_W:NSA_Q_W + (br + 1) * NSA_KV_W]
    kv_shape = (2, NSA_KV, NSA_HD)
    keep_p = min(NSA_WINDOW, seq)
    cmp_rows, slc_rows, win_rows = _kv_rows_out(main_p, batch, seq, keep_p)
    cmp_kv_p = cmp_rows.reshape(batch, 1, seq, *kv_shape)
    slc_kv_p = slc_rows.reshape(batch, 1, seq, *kv_shape)
    win_kv_p = win_rows.reshape(batch, 1, keep_p, *kv_shape)

    cmp_kv_s = kv_rows(main_s, 0).reshape(seqs, 1, n_new, *kv_shape)
    slc_kv_s = kv_rows(main_s, 1).reshape(seqs, 1, n_new, *kv_shape)
    kw_new = kv_rows(main_s, 2).reshape(seqs, n_new, NSA_KV_W)
    n_wb = cache_win_kv.shape[2]
    win_buf = cache_win_kv[:, 0].reshape(seqs, n_wb * KV_ROWS, NSA_HD)
    all_w = jnp.concatenate([win_buf, kw_new.reshape(seqs, n_new * KV_ROWS, NSA_HD)], axis=1)
    keep_s = min(NSA_WINDOW, n_wb + n_new)
    win_kv_s = all_w[:, (n_wb + n_new - keep_s) * KV_ROWS:].reshape(seqs, 1, keep_s, *kv_shape)

    cmp_means = cmp_means.reshape(seqs, past // NSA_BLOCK, NSA_KV_W)
    pad_new = 8
    pad8 = lambda a: jnp.pad(a, ((0, 0), (0, pad_new - n_new), (0, 0)))
    cols = NSA_KV * n_new * NSA_GROUP
    q_rows = (main_s[:, :NSA_Q_W].reshape(seqs, n_new, NSA_KV, NSA_GROUP, NSA_HD)
              .transpose(0, 2, 1, 3, 4).reshape(seqs, cols, NSA_HD))
    gates_s = (tail_s[:, :n_gate].reshape(seqs, n_new, NSA_KV, NSA_GROUP, 3)
               .transpose(0, 4, 2, 1, 3).reshape(seqs, 3, cols))
    b_gate_s = jnp.broadcast_to(nsa_b_gate[0].reshape(1, NSA_KV, NSA_GROUP, 3), (n_new, NSA_KV, NSA_GROUP, 3))
    b_gate_s = b_gate_s.transpose(3, 1, 0, 2).reshape(3, cols)
    o_s = _nsa_sample(q_rows, cmp_means, page_rows(cache_slc_kv), page_table,
                      pad8(slc_kv_s.reshape(seqs, n_new, NSA_KV_W)), win_buf, pad8(kw_new), gates_s, b_gate_s,
                      past, n_new)
    o_s = (o_s.reshape(seqs, NSA_KV, n_new, NSA_GROUP, NSA_HD).transpose(0, 2, 1, 3, 4)
           .reshape(rows_s, NSA_Q_W).astype(BF16))

    xp, xs = mixer_out(o_p, o_s, xp, xs, 1, mp, ms, "nsa_out")
    xp, xs = ffn(xp, xs, 1, mp, ms)

    return (xp.reshape(batch, seq, D_MODEL), xs.reshape(seqs, n_new, D_MODEL),
            gla_state_p[:, None], gla_state_s[:, None], cmp_kv_p, cmp_kv_s, slc_kv_p, slc_kv_s,
            win_kv_p, win_kv_s)
```

```python
import functools

import jax
import jax.numpy as jnp
from jax import lax
from jax.experimental import pallas as pl
from jax.experimental.pallas import tpu as pltpu

F32 = jnp.float32
BF16 = jnp.bfloat16

D_MODEL = 2048
DEPTH = 2
DN_ALPHA = (2 * DEPTH) ** 0.25
LN_EPS = 1e-5
D_FF = 5632
GLA_HEADS = 4
GLA_DK = 256
GLA_DV = 512
GLA_RANK = 16
GLA_TAU = 16.0
GLA_SUB = 32
GLA_MAIN_W = 2 * GLA_HEADS * GLA_DK + 2 * GLA_HEADS * GLA_DV
NSA_HEADS = 16
NSA_KV = 4
NSA_HD = 128
NSA_GROUP = NSA_HEADS // NSA_KV
NSA_BLOCK = 64
NSA_TOPK = 16
NSA_WINDOW = 512
NSA_Q_W = NSA_HEADS * NSA_HD
NSA_KV_W = 2 * NSA_KV * NSA_HD
NSA_MAIN_W = NSA_Q_W + 3 * NSA_KV_W
PAGE = 128
KV_ROWS = 2 * NSA_KV

LANES = 128
BF16_ROWS = 16
LOG2_E = 1.4426950408889634
V7X_VMEM_BYTES = 64 * 1024 * 1024
VMEM_LIMIT = V7X_VMEM_BYTES * 7 // 8

NEG_BIG = -1e30
TINY = float(jnp.finfo(jnp.float32).tiny)


def _params(*sem):
    return pltpu.CompilerParams(dimension_semantics=sem, vmem_limit_bytes=VMEM_LIMIT)


def _dot(a, b):
    return jnp.dot(a.astype(BF16), b.astype(BF16), preferred_element_type=F32)


def _dot_nt(a, b):
    return lax.dot_general(a.astype(BF16), b.astype(BF16), (((1,), (1,)), ((), ())),
                           preferred_element_type=F32)


def _dot_tn(a, b):
    return lax.dot_general(a.astype(BF16), b.astype(BF16), (((0,), (0,)), ((), ())),
                           preferred_element_type=F32)


def _split3(x):
    hi = x.astype(BF16)
    r1 = x - hi.astype(F32)
    mid = r1.astype(BF16)
    lo = (r1 - mid.astype(F32)).astype(BF16)
    return hi, mid, lo


def _silu(x):
    return x * jax.nn.sigmoid(x)


def _ind(cond, dtype=F32):
    return jnp.where(cond, 1.0, 0.0).astype(dtype)


def _masked_softmax2(s2, mask, axis):
    s2 = jnp.where(mask, s2, -jnp.inf)
    m = jnp.max(s2, axis=axis, keepdims=True)
    m = jnp.where(jnp.isfinite(m), m, 0.0)
    e = jnp.where(mask, jnp.exp2(s2 - m), 0.0)
    return e / jnp.maximum(jnp.sum(e, axis=axis, keepdims=True), TINY)


def _adaln_kernel(c_ref, w_ref, b_ref, o_ref):
    o_ref[...] = _dot(_silu(c_ref[...]), w_ref[...]) + b_ref[...]


def _adaln(c_all, ada_w, ada_b):
    rows = c_all.shape[0]
    n = ada_w.shape[2]
    tn = 1024
    return pl.pallas_call(
        _adaln_kernel,
        grid=(DEPTH, n // tn),
        in_specs=[
            pl.BlockSpec((rows, D_MODEL), lambda l, j: (0, 0)),
            pl.BlockSpec((None, D_MODEL, tn), lambda l, j: (l, 0, j)),
            pl.BlockSpec((None, 1, tn), lambda l, j: (l, 0, j)),
        ],
        out_specs=pl.BlockSpec((None, rows, tn), lambda l, j: (l, 0, j)),
        out_shape=jax.ShapeDtypeStruct((DEPTH, rows, n), F32),
        compiler_params=_params("parallel", "parallel"),
        name="adaln",
    )(c_all, ada_w, ada_b.reshape(DEPTH, 1, n))


def _modulate(x_ref, sc_ref, sh_ref):
    return (x_ref[...] * (1.0 + sc_ref[...]) + sh_ref[...]).astype(BF16)


def _proj_kernel(x_ref, sc_ref, sh_ref, xs_ref, scs_ref, shs_ref, w_ref, wt_ref,
                 o_ref, ot_ref, os_ref, ots_ref, h_ref, *, n_i):
    i, j = pl.program_id(0), pl.program_id(1)
    tm = x_ref.shape[0]

    @pl.when(j == 0)
    def _():
        h_ref[:tm] = _modulate(x_ref, sc_ref, sh_ref)
        ot_ref[...] = _dot_nt(h_ref[:tm], wt_ref[...])

    @pl.when(i < n_i - 1)
    def _():
        o_ref[...] = _dot_nt(h_ref[:tm], w_ref[...])

    @pl.when(i == n_i - 1)
    def _():
        @pl.when(j == 0)
        def _():
            h_ref[tm:] = _modulate(xs_ref, scs_ref, shs_ref)
            ots_ref[...] = _dot_nt(h_ref[tm:], wt_ref[...])

        both = _dot_nt(h_ref[...], w_ref[...])
        o_ref[...] = both[:tm]
        os_ref[...] = both[tm:]


def _proj(x, sc, sh, xs, scs, shs, w_t, layer, n_main, tm, name):
    m, ms = x.shape[0], xs.shape[0]
    n_tail = w_t.shape[1] - n_main
    w_tail = jnp.pad(w_t[layer, n_main:, :], ((0, LANES - n_tail), (0, 0)))
    tn = 512
    n_i = m // tm
    tiles_per_group = n_i // sc.shape[0]
    mod_spec = pl.BlockSpec((None, sc.shape[1], D_MODEL), lambda i, j: (i // tiles_per_group, 0, 0))
    const2 = lambda i, j: (0, 0)
    const3 = lambda i, j: (0, 0, 0)
    last_tile_col = lambda i, j: (0, jnp.where(i == n_i - 1, j, 0))
    return pl.pallas_call(
        functools.partial(_proj_kernel, n_i=n_i),
        grid=(n_i, n_main // tn),
        in_specs=[
            pl.BlockSpec((tm, D_MODEL), lambda i, j: (i, 0), pipeline_mode=pl.Buffered(1)),
            mod_spec,
            mod_spec,
            pl.BlockSpec((ms, D_MODEL), const2),
            pl.BlockSpec((None, ms, D_MODEL), const3),
            pl.BlockSpec((None, ms, D_MODEL), const3),
            pl.BlockSpec((None, tn, D_MODEL), lambda i, j: (layer, j, 0)),
            pl.BlockSpec((LANES, D_MODEL), const2),
        ],
        out_specs=[
            pl.BlockSpec((tm, tn), lambda i, j: (i, j)),
            pl.BlockSpec((tm, LANES), lambda i, j: (i, 0)),
            pl.BlockSpec((ms, tn), last_tile_col),
            pl.BlockSpec((ms, LANES), const2),
        ],
        out_shape=[jax.ShapeDtypeStruct((m, n_main), F32), jax.ShapeDtypeStruct((m, LANES), F32),
                   jax.ShapeDtypeStruct((ms, n_main), F32), jax.ShapeDtypeStruct((ms, LANES), F32)],
        scratch_shapes=[pltpu.VMEM((tm + ms, D_MODEL), BF16)],
        compiler_params=_params("arbitrary", "arbitrary"),
        name=name,
    )(x, sc, sh, xs, scs, shs, w_t, w_tail)


def _out_ln_kernel(a_ref, w_ref, x_ref, gt_ref, as_ref, xs_ref, gts_ref, g_ref, b_ref, o_ref, os_ref,
                   *, n_sub, n_i):
    def deep_norm(x, gate, m):
        y = DN_ALPHA * x + gate * m
        mu = jnp.mean(y, axis=-1, keepdims=True)
        yc = y - mu
        var = jnp.mean(yc * yc, axis=-1, keepdims=True)
        return yc * lax.rsqrt(var + LN_EPS) * g_ref[...] + b_ref[...]

    tm = a_ref.shape[0]
    rows = [slice(s * (tm // n_sub), (s + 1) * (tm // n_sub)) for s in range(n_sub)]
    proj = [jnp.dot(a_ref[rs, :], w_ref[...], preferred_element_type=F32) for rs in rows]
    for rs, m in zip(rows, proj):
        o_ref[rs, :] = deep_norm(x_ref[rs, :], gt_ref[...], m)

    @pl.when(pl.program_id(0) == n_i - 1)
    def _():
        os_ref[...] = deep_norm(xs_ref[...], gts_ref[...],
                                jnp.dot(as_ref[...], w_ref[...], preferred_element_type=F32))


def _out_ln(a, x, gate, a_s, x_s, gate_s, w, w_layer, ln_g, ln_b, ln_layer, tm, name):
    m, kdim = a.shape
    ms = a_s.shape[0]
    n_i = m // tm
    tiles_per_group = n_i // gate.shape[0]
    vec_spec = pl.BlockSpec((None, 1, D_MODEL), lambda i: (ln_layer, 0, 0))
    return pl.pallas_call(
        functools.partial(_out_ln_kernel, n_sub=2 if tm % 512 == 0 else 1, n_i=n_i),
        grid=(n_i,),
        in_specs=[
            pl.BlockSpec((tm, kdim), lambda i: (i, 0)),
            pl.BlockSpec((None, kdim, D_MODEL), lambda i: (w_layer, 0, 0), pipeline_mode=pl.Buffered(1)),
            pl.BlockSpec((tm, D_MODEL), lambda i: (i, 0)),
            pl.BlockSpec((None, 1, D_MODEL), lambda i: (i // tiles_per_group, 0, 0)),
            pl.BlockSpec((ms, kdim), lambda i: (0, 0)),
            pl.BlockSpec((ms, D_MODEL), lambda i: (0, 0)),
            pl.BlockSpec((None, ms, D_MODEL), lambda i: (0, 0, 0)),
            vec_spec,
            vec_spec,
        ],
        out_specs=[pl.BlockSpec((tm, D_MODEL), lambda i: (i, 0)),
                   pl.BlockSpec((ms, D_MODEL), lambda i: (0, 0))],
        out_shape=[jax.ShapeDtypeStruct((m, D_MODEL), F32), jax.ShapeDtypeStruct((ms, D_MODEL), F32)],
        compiler_params=_params("arbitrary"),
        name=name,
    )(a, w, x, gate, a_s, x_s, gate_s, ln_g.reshape(-1, 1, D_MODEL), ln_b.reshape(-1, 1, D_MODEL))


def _ffn_in_kernel(x_ref, sc_ref, sh_ref, xs_ref, scs_ref, shs_ref, wa_ref, wu_ref,
                   o_ref, os_ref, h_ref, *, n_i):
    i, j = pl.program_id(0), pl.program_id(1)
    tm = x_ref.shape[0]

    @pl.when(j == 0)
    def _():
        h_ref[:tm] = _modulate(x_ref, sc_ref, sh_ref)

    wa = wa_ref[...].astype(BF16)
    wu = wu_ref[...].astype(BF16)

    def swiglu_in(h):
        return (_silu(_dot(h, wa)) * _dot(h, wu)).astype(BF16)

    @pl.when(i < n_i - 1)
    def _():
        o_ref[...] = swiglu_in(h_ref[:tm])

    @pl.when(i == n_i - 1)
    def _():
        @pl.when(j == 0)
        def _():
            h_ref[tm:] = _modulate(xs_ref, scs_ref, shs_ref)

        both = swiglu_in(h_ref[...])
        o_ref[...] = both[:tm]
        os_ref[...] = both[tm:]


def _ffn_in(x, sc, sh, xs, scs, shs, w, layer, tm, name):
    m, ms = x.shape[0], xs.shape[0]
    tn = 256
    nj = D_FF // tn
    n_i = m // tm
    tiles_per_group = n_i // sc.shape[0]
    mod_spec = pl.BlockSpec((None, sc.shape[1], D_MODEL), lambda i, j: (i // tiles_per_group, 0, 0))
    const3 = lambda i, j: (0, 0, 0)
    return pl.pallas_call(
        functools.partial(_ffn_in_kernel, n_i=n_i),
        grid=(n_i, nj),
        in_specs=[
            pl.BlockSpec((tm, D_MODEL), lambda i, j: (i, 0), pipeline_mode=pl.Buffered(1)),
            mod_spec,
            mod_spec,
            pl.BlockSpec((ms, D_MODEL), lambda i, j: (0, 0)),
            pl.BlockSpec((None, ms, D_MODEL), const3),
            pl.BlockSpec((None, ms, D_MODEL), const3),
            pl.BlockSpec((None, D_MODEL, tn), lambda i, j: (layer, 0, j)),
            pl.BlockSpec((None, D_MODEL, tn), lambda i, j: (layer, 0, j + nj)),
        ],
        out_specs=[
            pl.BlockSpec((tm, tn), lambda i, j: (i, j)),
            pl.BlockSpec((ms, tn), lambda i, j: (0, jnp.where(i == n_i - 1, j, 0))),
        ],
        out_shape=[jax.ShapeDtypeStruct((m, D_FF), BF16), jax.ShapeDtypeStruct((ms, D_FF), BF16)],
        scratch_shapes=[pltpu.VMEM((tm + ms, D_MODEL), BF16)],
        compiler_params=_params("arbitrary", "arbitrary"),
        name=name,
    )(x, sc, sh, xs, scs, shs, w, w)


def _gla_kernel(*refs, chunk, t_valid, has_s0):
    if has_s0:
        q_ref, k_ref, v_ref, r_ref, a_ref, wa_ref, ba_ref, g_ref, s0_ref, o_ref, st_ref = refs
    else:
        q_ref, k_ref, v_ref, r_ref, a_ref, wa_ref, ba_ref, g_ref, o_ref, st_ref = refs
    sub = GLA_SUB
    n_sub = chunk // sub
    anchor = sub // 2 - 1

    @pl.when(pl.program_id(1) == 0)
    def _():
        if has_s0:
            st_ref[...] = s0_ref[...]
        else:
            st_ref[...] = jnp.zeros(st_ref.shape, F32)

    row = lax.broadcasted_iota(jnp.int32, (chunk, 1), 0)
    tri = _ind(lax.broadcasted_iota(jnp.int32, (chunk, chunk), 1)
               <= lax.broadcasted_iota(jnp.int32, (chunk, chunk), 0), BF16)
    ones = jnp.ones((chunk, LANES), BF16)
    causal = (lax.broadcasted_iota(jnp.int32, (sub, sub), 1)
              <= lax.broadcasted_iota(jnp.int32, (sub, sub), 0))
    heads = range(GLA_HEADS)
    dk = [slice(h * GLA_DK, (h + 1) * GLA_DK) for h in heads]
    dv = [slice(h * GLA_DV, (h + 1) * GLA_DV) for h in heads]
    subs = [slice(s * sub, (s + 1) * sub) for s in range(n_sub)]

    z = _dot(a_ref[...], wa_ref[...]) + ba_ref[...]
    lb = (jnp.minimum(z, 0.0) - jnp.log1p(jnp.exp(-jnp.abs(z)))) * (1.0 / GLA_TAU)
    k = k_ref[...]
    if t_valid < chunk:
        lb = jnp.where(row < t_valid, lb, 0.0)
        k = jnp.where(row < t_valid, k, 0.0)
    q = q_ref[...] * (GLA_DK ** -0.5)
    vb = v_ref[...].astype(BF16)
    lb3 = _split3(lb)
    b = sum(jnp.dot(tri, t, preferred_element_type=F32) for t in lb3)
    dec = jnp.exp(sum(lax.dot_general(t, ones, (((0,), (0,)), ((), ())), preferred_element_type=F32)
                      for t in lb3))
    b_last = b[chunk - 1:chunk, :]
    q_in = (q * jnp.exp(b)).astype(BF16)
    k_out = (k * jnp.exp(b_last - b)).astype(BF16)
    q_diag, k_diag, q_off, k_off = [], [], [], []
    for s, rs in enumerate(subs):
        b_mid = b[s * sub + anchor:s * sub + anchor + 1, :]
        q_diag.append((q[rs] * jnp.exp(b[rs] - b_mid)).astype(BF16))
        k_diag.append((k[rs] * jnp.exp(b_mid - b[rs])).astype(BF16))
        if s > 0:
            prev = slice(0, s * sub)
            b_in = b[s * sub - 1:s * sub, :]
            q_off.append((q[rs] * jnp.exp(b[rs] - b_in)).astype(BF16))
            k_off.append((k[prev] * jnp.exp(b_in - b[prev])).astype(BF16))

    s_old = [st_ref[0, h] for h in heads]
    o_inter = [_dot(q_in[:, dk[h]], s_old[h]) for h in heads]
    a_diag = [[_dot_nt(q_diag[s][:, dk[h]], k_diag[s][:, dk[h]]) for h in heads] for s in range(n_sub)]
    a_off = [[_dot_nt(q_off[s][:, dk[h]], k_off[s][:, dk[h]]) for h in heads] for s in range(n_sub - 1)]
    a_diag = [[jnp.where(causal, a, 0.0).astype(BF16) for a in row_] for row_ in a_diag]
    a_off = [[a.astype(BF16) for a in row_] for row_ in a_off]
    o_intra = []
    for s, rs in enumerate(subs):
        o_s = [_dot(a_diag[s][h], vb[rs, dv[h]]) for h in heads]
        if s > 0:
            o_s = [o_s[h] + _dot(a_off[s - 1][h], vb[0:s * sub, dv[h]]) for h in heads]
        o_intra.append(o_s)
    kv = [_dot_tn(k_out[:, dk[h]], vb[:, dv[h]]) for h in heads]
    for h in heads:
        st_ref[0, h] = s_old[h] * jnp.tile(dec[dk[h], :], (1, GLA_DV // LANES)) + kv[h]
    for h in heads:
        o = o_inter[h] + jnp.concatenate([o_intra[s][h] for s in range(n_sub)], axis=0)
        o = o * lax.rsqrt(jnp.mean(o * o, axis=-1, keepdims=True) + LN_EPS)
        o_ref[:, dv[h]] = (o * g_ref[:, dv[h]] * _silu(r_ref[:, dv[h]])).astype(BF16)


def _gla(main, tail, w_alpha, b_alpha, norm_g, s0, batch, seq, chunk, t_valid):
    n_chunks = seq // chunk
    qk_w = GLA_HEADS * GLA_DK
    v_w = GLA_HEADS * GLA_DV
    wa = jnp.pad(w_alpha, ((0, LANES - GLA_RANK), (0, 0)))
    row_map = lambda b, c: (b * n_chunks + c, 0)
    const = lambda b, c: (0, 0)
    in_specs = [
        pl.BlockSpec((chunk, qk_w), row_map),
        pl.BlockSpec((chunk, qk_w), lambda b, c: (b * n_chunks + c, 1)),
        pl.BlockSpec((chunk, v_w), lambda b, c: (b * n_chunks + c, 1)),
        pl.BlockSpec((chunk, v_w), lambda b, c: (b * n_chunks + c, 2)),
        pl.BlockSpec((chunk, LANES), row_map),
        pl.BlockSpec((LANES, qk_w), const),
        pl.BlockSpec((1, qk_w), const),
        pl.BlockSpec((1, v_w), const),
    ]
    args = [main, main, main, main, tail, wa, b_alpha.reshape(1, qk_w), norm_g.reshape(1, v_w)]
    state_spec = pl.BlockSpec((1, GLA_HEADS, GLA_DK, GLA_DV), lambda b, c: (b, 0, 0, 0))
    if s0 is not None:
        in_specs.append(state_spec)
        args.append(s0)
    return pl.pallas_call(
        functools.partial(_gla_kernel, chunk=chunk, t_valid=t_valid, has_s0=s0 is not None),
        grid=(batch, n_chunks),
        in_specs=in_specs,
        out_specs=[pl.BlockSpec((chunk, v_w), row_map), state_spec],
        out_shape=[jax.ShapeDtypeStruct((batch * seq, v_w), BF16),
                   jax.ShapeDtypeStruct((batch, GLA_HEADS, GLA_DK, GLA_DV), F32)],
        compiler_params=_params("parallel", "arbitrary"),
        name="gla_prompt" if s0 is None else "gla_sample",
    )(*args)


def _select_blocks(imp_t, blk, cur):
    n = imp_t.shape[0]
    forced = (blk == 0) | ((blk >= cur - 1) & (blk <= cur))
    score = jnp.where(blk > cur, -jnp.inf, jnp.where(forced, jnp.inf, imp_t))
    rank = jnp.zeros(score.shape, F32)
    for i in range(n):
        si = score[i:i + 1, :]
        rank = rank + jnp.where(blk > i, _ind(si >= score), _ind(si > score))
    return (rank < NSA_TOPK) & (score > -jnp.inf)


def _nsa_prompt_kernel(pt_ref, q_ref, kc_ref, vc_ref, ks_ref, vs_ref, kw_ref, vw_ref, gt_ref, bg_ref, *refs,
                       seq, tq, tk, pages_per_step):
    page_refs = refs[:pages_per_step]
    (o_ref, means_ref, kcm, vcm, ksb, vst, kwb, vwt, qs_ref, sel_ref,
     ms_ref, accs_ref, mw_ref, accw_ref) = refs[pages_per_step:]
    qi = pl.program_id(2)
    n_blk = seq // NSA_BLOCK
    n_kt = seq // tk
    blk_per_kt = tk // NSA_BLOCK

    @pl.when(qi == 0)
    def _():
        kcm[...] = jnp.sum(kc_ref[...].reshape(n_blk, NSA_BLOCK, NSA_HD), axis=1) * (1.0 / NSA_BLOCK)
        vcm[...] = jnp.sum(vc_ref[...].reshape(n_blk, NSA_BLOCK, NSA_HD), axis=1) * (1.0 / NSA_BLOCK)
        ksb[...] = ks_ref[...].astype(BF16)
        kwb[...] = kw_ref[...].astype(BF16)
        ones_rows = _ind(lax.broadcasted_iota(jnp.int32, (BF16_ROWS, tk), 0) == 0, BF16)
        for kt in range(n_kt):
            rows = slice(kt * tk, (kt + 1) * tk)
            vst[kt] = jnp.concatenate([vs_ref[rows, :].T.astype(BF16), ones_rows], axis=0)
            vwt[kt] = jnp.concatenate([vw_ref[rows, :].T.astype(BF16), ones_rows], axis=0)

    t0 = qi * tq
    q = q_ref[...] * (NSA_HD ** -0.5 * LOG2_E)
    qs_ref[...] = jnp.concatenate(
        [q[:, g * NSA_HD:(g + 1) * NSA_HD] for g in range(NSA_GROUP)], axis=0).astype(BF16)
    qs = qs_ref[...]
    t_row = t0 + lax.broadcasted_iota(jnp.int32, (1, tq), 1)
    t_row_g = jnp.concatenate([t_row] * NSA_GROUP, axis=1)

    kc_b = kcm[...].astype(BF16)
    blk_col = lax.broadcasted_iota(jnp.int32, (n_blk, 1), 0)
    p_t = _masked_softmax2(_dot_nt(kc_b, qs), (blk_col + 1) * NSA_BLOCK - 1 <= t_row_g, axis=0)
    o_c = _dot_tn(vcm[...], p_t)
    imp_t = sum(p_t[:, g * tq:(g + 1) * tq] for g in range(NSA_GROUP))
    sel_ref[...] = jnp.where(_select_blocks(imp_t, blk_col, t_row // NSA_BLOCK), 0.0, NEG_BIG)

    n_loc = lax.broadcasted_iota(jnp.int32, (tk, tq), 0)
    t_loc = lax.broadcasted_iota(jnp.int32, (tk, tq), 1)
    on_lanes = lambda a: jnp.concatenate([a] * NSA_GROUP, axis=1)
    causal = on_lanes(jnp.where(n_loc <= t_loc, 0.0, NEG_BIG))
    far = on_lanes(jnp.where(n_loc > t_loc, 0.0, NEG_BIG))

    def sel_bias(kt):
        return on_lanes(jnp.concatenate(
            [jnp.broadcast_to(sel_ref[pl.ds(kt * blk_per_kt + j, 1), :], (NSA_BLOCK, tq))
             for j in range(blk_per_kt)], axis=0))

    sel_state, win_state = (ms_ref, accs_ref), (mw_ref, accw_ref)
    for m_ref, acc_ref in (sel_state, win_state):
        m_ref[...] = jnp.full(m_ref.shape, NEG_BIG, F32)
        acc_ref[...] = jnp.zeros(acc_ref.shape, F32)

    def steps(*work):
        s = [_dot_nt(k_ref[pl.ds(pl.multiple_of(kt * tk, tk), tk), :], qs_ref[...]) + bias
             for _, k_ref, _, kt, bias in work]
        m_old = [state[0][...] for state, *_ in work]
        m_new = [jnp.maximum(mo, jnp.max(si, axis=0, keepdims=True)) for mo, si in zip(m_old, s)]
        p = [jnp.exp2(si - mn).astype(BF16) for si, mn in zip(s, m_new)]
        pv = [_dot(vt_ref[kt], pi) for (_, _, vt_ref, kt, _), pi in zip(work, p)]
        for (state, *_), mo, mn, pvi in zip(work, m_old, m_new, pv):
            state[1][...] = jnp.exp2(mo - mn) * state[1][...] + pvi
            state[0][...] = mn

    def result(acc_ref):
        acc = acc_ref[...]
        return acc[:NSA_HD] / jnp.maximum(acc[NSA_HD:NSA_HD + 1], TINY)

    steps((sel_state, ksb, vst, qi, sel_bias(qi) + causal), (win_state, kwb, vwt, qi, causal))

    @pl.when(qi >= 1)
    def _():
        steps((sel_state, ksb, vst, qi - 1, sel_bias(qi - 1)), (win_state, kwb, vwt, qi - 1, far))

    def sel_body(i, carry):
        kt = qi - 2 - i
        steps((sel_state, ksb, vst, kt, sel_bias(kt)))
        return carry

    lax.fori_loop(0, qi - 1, sel_body, 0)

    gates = jax.nn.sigmoid(gt_ref[...] + bg_ref[...])
    g_c, g_s, g_w = (jnp.concatenate([gates[3 * g + br:3 * g + br + 1, :] for g in range(NSA_GROUP)], axis=1)
                     for br in range(3))
    o_t = g_c * o_c + g_s * result(accs_ref) + g_w * result(accw_ref)
    for g in range(NSA_GROUP):
        o_ref[:, g * NSA_HD:(g + 1) * NSA_HD] = o_t[:, g * tq:(g + 1) * tq].T.astype(BF16)

    for i, x_ref in enumerate(page_refs):
        x = x_ref[0].reshape(PAGE // NSA_BLOCK, NSA_BLOCK, KV_ROWS, NSA_HD)
        means_ref[0, i] = jnp.sum(x, axis=1) * (1.0 / NSA_BLOCK)


def _nsa_prompt(main, gates, b_gate, batch, seq, cache_cmp, page_table):
    tq = tk = NSA_WINDOW
    assert seq % tk == 0
    vt_rows = NSA_HD + BF16_ROWS
    nq = seq // tq
    q_lanes = NSA_GROUP * NSA_HD
    col0 = NSA_Q_W // NSA_HD
    seqs, n_pages = page_table.shape
    n_steps = batch * NSA_KV * nq
    pages_per_step = seqs * n_pages // n_steps
    groups = n_pages // pages_per_step
    assert pages_per_step * n_steps == seqs * n_pages and groups * pages_per_step == n_pages
    per_page = PAGE // NSA_BLOCK

    def kv_spec(branch, part):
        off = col0 + (2 * branch + part) * NSA_KV
        return pl.BlockSpec((seq, NSA_HD), lambda b, kv, qi, pt: (b, off + kv))

    def step_of(b, kv, qi):
        return (b * NSA_KV + kv) * nq + qi

    def page_spec(i):
        def index(b, kv, qi, pt):
            s = step_of(b, kv, qi)
            return (pt[s // groups, (s % groups) * pages_per_step + i], 0, 0)
        return pl.BlockSpec((1, PAGE * KV_ROWS, NSA_HD), index)

    def means_index(b, kv, qi, pt):
        s = step_of(b, kv, qi)
        return (s // groups, s % groups, 0, 0, 0)

    g_w = 3 * NSA_GROUP
    g_rows = NSA_GROUP * tq
    grid_spec = pltpu.PrefetchScalarGridSpec(
        num_scalar_prefetch=1,
        grid=(batch, NSA_KV, nq),
        in_specs=[pl.BlockSpec((tq, q_lanes), lambda b, kv, qi, pt: (b * nq + qi, kv))]
        + [kv_spec(br, part) for br in range(3) for part in range(2)]
        + [pl.BlockSpec((None, g_w, tq), lambda b, kv, qi, pt: (kv, 0, b * nq + qi)),
           pl.BlockSpec((None, g_w, 1), lambda b, kv, qi, pt: (kv, 0, 0))]
        + [page_spec(i) for i in range(pages_per_step)],
        out_specs=[pl.BlockSpec((tq, q_lanes), lambda b, kv, qi, pt: (b * nq + qi, kv)),
                   pl.BlockSpec((1, pages_per_step, per_page, KV_ROWS, NSA_HD), means_index)],
        scratch_shapes=[
            pltpu.VMEM((seq // NSA_BLOCK, NSA_HD), F32),
            pltpu.VMEM((seq // NSA_BLOCK, NSA_HD), F32),
            pltpu.VMEM((seq, NSA_HD), BF16),
            pltpu.VMEM((seq // tk, vt_rows, tk), BF16),
            pltpu.VMEM((seq, NSA_HD), BF16),
            pltpu.VMEM((seq // tk, vt_rows, tk), BF16),
            pltpu.VMEM((g_rows, NSA_HD), BF16),
            pltpu.VMEM((seq // NSA_BLOCK, tq), F32),
            pltpu.VMEM((1, g_rows), F32),
            pltpu.VMEM((vt_rows, g_rows), F32),
            pltpu.VMEM((1, g_rows), F32),
            pltpu.VMEM((vt_rows, g_rows), F32),
        ],
    )
    return pl.pallas_call(
        functools.partial(_nsa_prompt_kernel, seq=seq, tq=tq, tk=tk, pages_per_step=pages_per_step),
        grid_spec=grid_spec,
        out_shape=[jax.ShapeDtypeStruct((batch * seq, NSA_Q_W), BF16),
                   jax.ShapeDtypeStruct((seqs, n_pages, per_page, KV_ROWS, NSA_HD), F32)],
        compiler_params=_params("arbitrary", "arbitrary", "arbitrary"),
        name="nsa_prompt",
    )(page_table, main, main, main, main, main, main, main, gates, b_gate, *([cache_cmp] * pages_per_step))


def _kv_rows_out_kernel(c_ref, s_ref, w_ref, oc_ref, os_ref, ow_ref, *, rows, n_c):
    def spread(src, dst):
        for r in range(KV_ROWS):
            dst[pl.ds(r, rows, stride=KV_ROWS), :] = src[:, r * NSA_HD:(r + 1) * NSA_HD]

    spread(c_ref, oc_ref)
    spread(s_ref, os_ref)

    @pl.when(pl.program_id(1) == n_c - 1)
    def _():
        spread(w_ref, ow_ref)


def _kv_rows_out(main, batch, seq, keep):
    rows = keep
    n_c = seq // rows
    col0 = NSA_Q_W // NSA_KV_W
    tile = lambda br: pl.BlockSpec((rows, NSA_KV_W), lambda b, c: (b * n_c + c, col0 + br))
    out_tile = pl.BlockSpec((rows * KV_ROWS, NSA_HD), lambda b, c: (b * n_c + c, 0))
    return pl.pallas_call(
        functools.partial(_kv_rows_out_kernel, rows=rows, n_c=n_c),
        grid=(batch, n_c),
        in_specs=[tile(0), tile(1),
                  pl.BlockSpec((rows, NSA_KV_W), lambda b, c: (b * n_c + n_c - 1, col0 + 2))],
        out_specs=[out_tile, out_tile, pl.BlockSpec((rows * KV_ROWS, NSA_HD), lambda b, c: (b, 0))],
        out_shape=[jax.ShapeDtypeStruct((batch * seq * KV_ROWS, NSA_HD), F32)] * 2
        + [jax.ShapeDtypeStruct((batch * keep * KV_ROWS, NSA_HD), F32)],
        compiler_params=_params("arbitrary", "arbitrary"),
        name="nsa_kv_rows_out",
    )(main, main, main)


def _page_specs(n):
    return [pl.BlockSpec((1, PAGE * KV_ROWS, NSA_HD), lambda b, p, pt, i=i: (pt[b, p * n + i], 0, 0))
            for i in range(n)]


def _nsa_sample_kernel(pt_ref, q_ref, cm_ref, *refs, past, n_new, n_steps, per_step):
    pg_refs = refs[:per_step]
    (kvn_ref, wb_ref, kwn_ref, gt_ref, bg_ref, o_ref,
     qr_ref, sel_ref, m_ref, l_ref, acc_ref, oc_ref, ow_ref) = refs[per_step:]
    p = pl.program_id(1)
    k_w = NSA_KV * NSA_HD
    cols = NSA_KV * n_new * NSA_GROUP
    per_kv = n_new * NSA_GROUP
    n_cmp = past // NSA_BLOCK
    n_sel = -(-(past + n_new) // NSA_BLOCK)
    n_sel_pad = sel_ref.shape[0]
    lane = lax.broadcasted_iota(jnp.int32, (1, cols), 1)
    q_pos = past + (lane // NSA_GROUP) % n_new

    def heads_on_lanes(ref, n, part):
        return jnp.concatenate(
            [ref[0, pl.ds(part * NSA_KV + kv, n, stride=KV_ROWS), :] for kv in range(NSA_KV)], axis=1)

    def scores(keys):
        return _dot_nt(keys, qr_ref[...])

    def own_head(x_t):
        out = jnp.zeros((NSA_HD, cols), F32)
        for kv in range(NSA_KV):
            mine = (lane // per_kv) == kv
            out = out + jnp.where(mine, x_t[kv * NSA_HD:(kv + 1) * NSA_HD, :], 0.0)
        return out

    def attend_once(keys, vals, mask):
        p_t = _masked_softmax2(scores(keys), mask, axis=0)
        return own_head(_dot_tn(vals, p_t))

    def online(keys, vals, bias):
        s = scores(keys) + bias
        m_old = m_ref[...]
        m_new = jnp.maximum(m_old, jnp.max(s, axis=0, keepdims=True))
        p_t = jnp.exp2(s - m_new)
        alpha = jnp.exp2(m_old - m_new)
        l_ref[...] = alpha * l_ref[...] + jnp.sum(p_t, axis=0, keepdims=True)
        acc_ref[...] = alpha * acc_ref[...] + own_head(_dot_tn(vals, p_t))
        m_ref[...] = m_new

    @pl.when(p == 0)
    def _():
        q = q_ref[0] * (NSA_HD ** -0.5 * LOG2_E)
        r_kv = lax.broadcasted_iota(jnp.int32, (cols, k_w), 0) // per_kv
        c_kv = lax.broadcasted_iota(jnp.int32, (cols, k_w), 1) // NSA_HD
        qr_ref[...] = jnp.where(r_kv == c_kv, jnp.tile(q, (1, NSA_KV)), 0.0).astype(BF16)
        cm = cm_ref[0]
        blk = lax.broadcasted_iota(jnp.int32, (n_cmp, 1), 0)
        p_t = _masked_softmax2(scores(cm[:, :k_w]), (blk + 1) * NSA_BLOCK - 1 <= q_pos, axis=0)
        oc_ref[...] = own_head(_dot_tn(cm[:, k_w:], p_t))
        same = _ind(lax.broadcasted_iota(jnp.int32, (cols, cols), 0) // NSA_GROUP
                    == lax.broadcasted_iota(jnp.int32, (cols, cols), 1) // NSA_GROUP, BF16)
        imp = sum(jnp.dot(t, same, preferred_element_type=F32) for t in _split3(p_t))
        imp = jnp.concatenate([imp, jnp.zeros((n_sel_pad - n_cmp, cols), F32)], axis=0)
        blk_s = lax.broadcasted_iota(jnp.int32, (n_sel_pad, 1), 0)
        cur = q_pos // NSA_BLOCK
        forced = (blk_s == 0) | ((blk_s >= cur - 1) & (blk_s <= cur))
        score = jnp.where((blk_s > cur) | (blk_s >= n_sel), -jnp.inf, jnp.where(forced, jnp.inf, imp))
        blk_f = blk_s.astype(F32)

        def pick(_, carry):
            left, bias = carry
            top = jnp.max(left, axis=0, keepdims=True)
            first = jnp.min(jnp.where(left == top, blk_f, float(n_sel_pad)), axis=0, keepdims=True)
            hit = (blk_f == first) & (top > -jnp.inf)
            return jnp.where(hit, -jnp.inf, left), jnp.where(hit, 0.0, bias)

        _, bias = lax.fori_loop(0, NSA_TOPK, pick, (score, jnp.full(score.shape, NEG_BIG, F32)))
        sel_ref[...] = bias
        n_wb = wb_ref.shape[1] // KV_ROWS
        kw_all = jnp.concatenate([heads_on_lanes(wb_ref, n_wb, 0), kwn_ref[0][:, :k_w]], axis=0)
        vw_all = jnp.concatenate([heads_on_lanes(wb_ref, n_wb, 1), kwn_ref[0][:, k_w:]], axis=0)
        n_w = kw_all.shape[0]
        w_row = lax.broadcasted_iota(jnp.int32, (n_w, 1), 0)
        w_pos = past - n_wb + w_row
        d = q_pos - w_pos
        ow_ref[...] = attend_once(kw_all, vw_all,
                                  (d >= 0) & (d < NSA_WINDOW) & (w_pos >= 0) & (w_row < n_wb + n_new))
        m_ref[...] = jnp.full(m_ref.shape, NEG_BIG, F32)
        l_ref[...] = jnp.zeros(l_ref.shape, F32)
        acc_ref[...] = jnp.zeros(acc_ref.shape, F32)

    def page_part(part):
        return jnp.concatenate([heads_on_lanes(pg_ref, PAGE, part) for pg_ref in pg_refs], axis=0)

    blk_per_step = per_step * PAGE // NSA_BLOCK
    online(page_part(0), page_part(1), jnp.concatenate(
        [jnp.broadcast_to(sel_ref[pl.ds(p * blk_per_step + j, 1), :], (NSA_BLOCK, cols))
         for j in range(blk_per_step)], axis=0))

    @pl.when(p == n_steps - 1)
    def _():
        kvn = kvn_ref[0]
        n_row = lax.broadcasted_iota(jnp.int32, (kvn.shape[0], 1), 0)
        bias_new = jnp.where((n_row < n_new) & (past + n_row <= q_pos), 0.0, NEG_BIG)
        online(kvn[:, :k_w], kvn[:, k_w:], bias_new + sel_ref[pl.ds(past // NSA_BLOCK, 1), :])
        o_s = acc_ref[...] / jnp.maximum(l_ref[...], TINY)
        gates = jax.nn.sigmoid(gt_ref[0] + bg_ref[...])
        o_t = gates[0:1, :] * oc_ref[...] + gates[1:2, :] * o_s + gates[2:3, :] * ow_ref[...]
        eye = _ind(lax.broadcasted_iota(jnp.int32, (NSA_HD, NSA_HD), 0)
                   == lax.broadcasted_iota(jnp.int32, (NSA_HD, NSA_HD), 1), BF16)
        o_ref[0] = sum(lax.dot_general(t, eye, (((0,), (0,)), ((), ())), preferred_element_type=F32)
                       for t in _split3(o_t))


def _nsa_sample(q_rows, cmp_means, cache_slc, page_table, kv_new, win_buf, kw_new, gates, b_gate,
                past, n_new):
    seqs, n_pages = page_table.shape
    cols = q_rows.shape[1]
    n_cmp = cmp_means.shape[1]
    n_sel_pad = -(-(-(-(past + n_new) // NSA_BLOCK)) // 8) * 8
    n_wb = win_buf.shape[1]
    pad_new = kv_new.shape[1]
    per_step = 8
    n_steps = n_pages // per_step
    seq_map = lambda b, p, pt: (b, 0, 0)
    return pl.pallas_call(
        functools.partial(_nsa_sample_kernel, past=past, n_new=n_new, n_steps=n_steps, per_step=per_step),
        grid_spec=pltpu.PrefetchScalarGridSpec(
            num_scalar_prefetch=1,
            grid=(seqs, n_steps),
            in_specs=[
                pl.BlockSpec((1, cols, NSA_HD), seq_map),
                pl.BlockSpec((1, n_cmp, NSA_KV_W), seq_map),
                *_page_specs(per_step),
                pl.BlockSpec((1, pad_new, NSA_KV_W), seq_map),
                pl.BlockSpec((1, n_wb, NSA_HD), seq_map),
                pl.BlockSpec((1, pad_new, NSA_KV_W), seq_map),
                pl.BlockSpec((1, 3, cols), seq_map),
                pl.BlockSpec((3, cols), lambda b, p, pt: (0, 0)),
            ],
            out_specs=pl.BlockSpec((1, cols, NSA_HD), seq_map),
            scratch_shapes=[
                pltpu.VMEM((cols, NSA_KV * NSA_HD), BF16),
                pltpu.VMEM((n_sel_pad, cols), F32),
                pltpu.VMEM((1, cols), F32),
                pltpu.VMEM((1, cols), F32),
                pltpu.VMEM((NSA_HD, cols), F32),
                pltpu.VMEM((NSA_HD, cols), F32),
                pltpu.VMEM((NSA_HD, cols), F32),
            ],
        ),
        out_shape=jax.ShapeDtypeStruct((seqs, cols, NSA_HD), F32),
        compiler_params=_params("parallel", "arbitrary"),
        name="nsa_sample",
    )(page_table, q_rows, cmp_means, *([cache_slc] * per_step), kv_new, win_buf, kw_new, gates, b_gate)


def kernel(x_prompt, x_sample, c_prompt, c_sample, state_gla, cache_cmp_kv, cache_slc_kv, cache_win_kv,
           page_table, ada_w, ada_b, gla_w_in, gla_w_alpha, gla_b_alpha, gla_norm_g, gla_w_out, nsa_w_in,
           nsa_b_gate, nsa_w_out, ln_mix_g, ln_mix_b, ffn_w_in, ffn_w_out, ln_ffn_g, ln_ffn_b):
    batch, seq, _ = x_prompt.shape
    seqs, n_new, _ = x_sample.shape
    n_pages = page_table.shape[1]
    past = n_pages * PAGE
    rows_s = seqs * n_new
    tm_p = seq
    w_out_mix = (gla_w_out.astype(BF16), nsa_w_out.astype(BF16))
    w_out_ffn = ffn_w_out.astype(BF16)

    pad_rows = 16 - (batch + seqs)
    c_all = jnp.concatenate([c_prompt, c_sample, jnp.zeros((pad_rows, D_MODEL), F32)], axis=0)
    mod = _adaln(c_all, ada_w, ada_b).reshape(DEPTH, 16, 6, D_MODEL)

    def mods(layer):
        mp = [mod[layer, :batch, i].reshape(batch, 1, D_MODEL) for i in range(6)]
        ms = [jnp.repeat(mod[layer, batch:batch + seqs, i], n_new, axis=0).reshape(1, rows_s, D_MODEL)
              for i in range(6)]
        return mp, ms

    def mixer_out(o_p, o_s, xp, xs, layer, mp, ms, tag):
        return _out_ln(o_p, xp, mp[2], o_s, xs, ms[2], w_out_mix[layer % 2], layer // 2, ln_mix_g, ln_mix_b,
                       layer, 512, tag)

    def ffn(xp, xs, layer, mp, ms):
        act_p, act_s = _ffn_in(xp, mp[4], mp[3], xs, ms[4], ms[3], ffn_w_in, layer, tm_p, "ffn_in_%d" % layer)
        return _out_ln(act_p, xp, mp[5], act_s, xs, ms[5], w_out_ffn, layer, ln_ffn_g, ln_ffn_b, layer, 256,
                       "ffn_out_%d" % layer)

    xp = x_prompt.reshape(batch * seq, D_MODEL)
    xs = x_sample.reshape(rows_s, D_MODEL)

    mp, ms = mods(0)
    main_p, tail_p, main_s, tail_s = _proj(xp, mp[1], mp[0], xs, ms[1], ms[0], jnp.swapaxes(gla_w_in, 1, 2), 0,
                                           GLA_MAIN_W, tm_p, "gla_in")
    gated_p, gla_state_p = _gla(main_p, tail_p, gla_w_alpha[0], gla_b_alpha[0], gla_norm_g[0], None,
                                batch, seq, 256, 256)
    chunk_s = GLA_SUB

    def pad_steps(a):
        a = a.reshape(seqs, n_new, a.shape[-1])
        return jnp.pad(a, ((0, 0), (0, chunk_s - n_new), (0, 0))).reshape(seqs * chunk_s, a.shape[-1])

    gated_s, gla_state_s = _gla(pad_steps(main_s), pad_steps(tail_s), gla_w_alpha[0], gla_b_alpha[0],
                                gla_norm_g[0], state_gla[:, 0], seqs, chunk_s, chunk_s, n_new)
    gated_s = gated_s.reshape(seqs, chunk_s, -1)[:, :n_new].reshape(rows_s, -1)
    xp, xs = mixer_out(gated_p, gated_s, xp, xs, 0, mp, ms, "gla_out")
    xp, xs = ffn(xp, xs, 0, mp, ms)

    mp, ms = mods(1)
    main_p, tail_p, main_s, tail_s = _proj(xp, mp[1], mp[0], xs, ms[1], ms[0], jnp.swapaxes(nsa_w_in, 1, 2), 0,
                                           NSA_MAIN_W, tm_p, "nsa_in")
    n_gate = 3 * NSA_HEADS
    g_w = 3 * NSA_GROUP
    gates_p = tail_p[:, :n_gate].reshape(batch * seq, NSA_KV, g_w).transpose(1, 2, 0)
    b_gate_p = nsa_b_gate[0].reshape(NSA_KV, g_w, 1)
    n_phys = cache_cmp_kv.shape[0]
    page_rows = lambda cache: cache.reshape(n_phys, PAGE * KV_ROWS, NSA_HD)
    o_p, cmp_means = _nsa_prompt(main_p, gates_p, b_gate_p, batch, seq, page_rows(cache_cmp_kv), page_table)

    kv_rows = lambda m, br: m[:, NSA_Q_W + br * NSA_KV_W:NSA_Q_W + (br + 1) * NSA_KV_W]
    kv_shape = (2, NSA_KV, NSA_HD)
    keep_p = min(NSA_WINDOW, seq)
    cmp_rows, slc_rows, win_rows = _kv_rows_out(main_p, batch, seq, keep_p)
    cmp_kv_p = cmp_rows.reshape(batch, 1, seq, *kv_shape)
    slc_kv_p = slc_rows.reshape(batch, 1, seq, *kv_shape)
    win_kv_p = win_rows.reshape(batch, 1, keep_p, *kv_shape)

    cmp_kv_s = kv_rows(main_s, 0).reshape(seqs, 1, n_new, *kv_shape)
    slc_kv_s = kv_rows(main_s, 1).reshape(seqs, 1, n_new, *kv_shape)
    kw_new = kv_rows(main_s, 2).reshape(seqs, n_new, NSA_KV_W)
    n_wb = cache_win_kv.shape[2]
    win_buf = cache_win_kv[:, 0].reshape(seqs, n_wb * KV_ROWS, NSA_HD)
    all_w = jnp.concatenate([win_buf, kw_new.reshape(seqs, n_new * KV_ROWS, NSA_HD)], axis=1)
    keep_s = min(NSA_WINDOW, n_wb + n_new)
    win_kv_s = all_w[:, (n_wb + n_new - keep_s) * KV_ROWS:].reshape(seqs, 1, keep_s, *kv_shape)

    cmp_means = cmp_means.reshape(seqs, past // NSA_BLOCK, NSA_KV_W)
    pad_new = 8
    pad8 = lambda a: jnp.pad(a, ((0, 0), (0, pad_new - n_new), (0, 0)))
    cols = NSA_KV * n_new * NSA_GROUP
    q_rows = (main_s[:, :NSA_Q_W].reshape(seqs, n_new, NSA_KV, NSA_GROUP, NSA_HD)
              .transpose(0, 2, 1, 3, 4).reshape(seqs, cols, NSA_HD))
    gates_s = (tail_s[:, :n_gate].reshape(seqs, n_new, NSA_KV, NSA_GROUP, 3)
               .transpose(0, 4, 2, 1, 3).reshape(seqs, 3, cols))
    b_gate_s = jnp.broadcast_to(nsa_b_gate[0].reshape(1, NSA_KV, NSA_GROUP, 3), (n_new, NSA_KV, NSA_GROUP, 3))
    b_gate_s = b_gate_s.transpose(3, 1, 0, 2).reshape(3, cols)
    o_s = _nsa_sample(q_rows, cmp_means, page_rows(cache_slc_kv), page_table,
                      pad8(slc_kv_s.reshape(seqs, n_new, NSA_KV_W)), win_buf, pad8(kw_new), gates_s, b_gate_s,
                      past, n_new)
    o_s = (o_s.reshape(seqs, NSA_KV, n_new, NSA_GROUP, NSA_HD).transpose(0, 2, 1, 3, 4)
           .reshape(rows_s, NSA_Q_W).astype(BF16))

    xp, xs = mixer_out(o_p, o_s, xp, xs, 1, mp, ms, "nsa_out")
    xp, xs = ffn(xp, xs, 1, mp, ms)

    return (xp.reshape(batch, seq, D_MODEL), xs.reshape(seqs, n_new, D_MODEL),
            gla_state_p[:, None], gla_state_s[:, None], cmp_kv_p, cmp_kv_s, slc_kv_p, slc_kv_s,
            win_kv_p, win_kv_s)
```

```python
import functools

import jax
import jax.numpy as jnp
from jax import lax
from jax.experimental import pallas as pl
from jax.experimental.pallas import tpu as pltpu

F32 = jnp.float32
BF16 = jnp.bfloat16

D_MODEL = 2048
DEPTH = 2
DN_ALPHA = (2 * DEPTH) ** 0.25
LN_EPS = 1e-5
D_FF = 5632
GLA_HEADS = 4
GLA_DK = 256
GLA_DV = 512
GLA_RANK = 16
GLA_TAU = 16.0
GLA_SUB = 32
GLA_MAIN_W = 2 * GLA_HEADS * GLA_DK + 2 * GLA_HEADS * GLA_DV
NSA_HEADS = 16
NSA_KV = 4
NSA_HD = 128
NSA_GROUP = NSA_HEADS // NSA_KV
NSA_BLOCK = 64
NSA_TOPK = 16
NSA_WINDOW = 512
NSA_Q_W = NSA_HEADS * NSA_HD
NSA_KV_W = 2 * NSA_KV * NSA_HD
NSA_MAIN_W = NSA_Q_W + 3 * NSA_KV_W
PAGE = 128
KV_ROWS = 2 * NSA_KV

LANES = 128
BF16_ROWS = 16
LOG2_E = 1.4426950408889634
V7X_VMEM_BYTES = 64 * 1024 * 1024
VMEM_LIMIT = V7X_VMEM_BYTES * 7 // 8

NEG_BIG = -1e30
TINY = float(jnp.finfo(jnp.float32).tiny)


def _params(*sem):
    return pltpu.CompilerParams(dimension_semantics=sem, vmem_limit_bytes=VMEM_LIMIT)


def _dot(a, b):
    return jnp.dot(a.astype(BF16), b.astype(BF16), preferred_element_type=F32)


def _dot_nt(a, b):
    return lax.dot_general(a.astype(BF16), b.astype(BF16), (((1,), (1,)), ((), ())),
                           preferred_element_type=F32)


def _dot_tn(a, b):
    return lax.dot_general(a.astype(BF16), b.astype(BF16), (((0,), (0,)), ((), ())),
                           preferred_element_type=F32)


def _split3(x):
    hi = x.astype(BF16)
    r1 = x - hi.astype(F32)
    mid = r1.astype(BF16)
    lo = (r1 - mid.astype(F32)).astype(BF16)
    return hi, mid, lo


def _silu(x):
    return x * jax.nn.sigmoid(x)


def _ind(cond, dtype=F32):
    return jnp.where(cond, 1.0, 0.0).astype(dtype)


def _masked_softmax2(s2, mask, axis):
    s2 = jnp.where(mask, s2, -jnp.inf)
    m = jnp.max(s2, axis=axis, keepdims=True)
    m = jnp.where(jnp.isfinite(m), m, 0.0)
    e = jnp.where(mask, jnp.exp2(s2 - m), 0.0)
    return e / jnp.maximum(jnp.sum(e, axis=axis, keepdims=True), TINY)


def _adaln_kernel(c_ref, w_ref, b_ref, o_ref):
    o_ref[...] = _dot(_silu(c_ref[...]), w_ref[...]) + b_ref[...]


def _adaln(c_all, ada_w, ada_b):
    rows = c_all.shape[0]
    n = ada_w.shape[2]
    tn = 1024
    return pl.pallas_call(
        _adaln_kernel,
        grid=(DEPTH, n // tn),
        in_specs=[
            pl.BlockSpec((rows, D_MODEL), lambda l, j: (0, 0)),
            pl.BlockSpec((None, D_MODEL, tn), lambda l, j: (l, 0, j)),
            pl.BlockSpec((None, 1, tn), lambda l, j: (l, 0, j)),
        ],
        out_specs=pl.BlockSpec((None, rows, tn), lambda l, j: (l, 0, j)),
        out_shape=jax.ShapeDtypeStruct((DEPTH, rows, n), F32),
        compiler_params=_params("parallel", "parallel"),
        name="adaln",
    )(c_all, ada_w, ada_b.reshape(DEPTH, 1, n))


def _modulate(x_ref, sc_ref, sh_ref):
    return (x_ref[...] * (1.0 + sc_ref[...]) + sh_ref[...]).astype(BF16)


def _proj_kernel(x_ref, sc_ref, sh_ref, xs_ref, scs_ref, shs_ref, w_ref, wt_ref,
                 o_ref, ot_ref, os_ref, ots_ref, h_ref, *, n_i):
    i, j = pl.program_id(0), pl.program_id(1)
    tm = x_ref.shape[0]

    @pl.when(j == 0)
    def _():
        h_ref[:tm] = _modulate(x_ref, sc_ref, sh_ref)
        ot_ref[...] = _dot_nt(h_ref[:tm], wt_ref[...])

    @pl.when(i < n_i - 1)
    def _():
        o_ref[...] = _dot_nt(h_ref[:tm], w_ref[...])

    @pl.when(i == n_i - 1)
    def _():
        @pl.when(j == 0)
        def _():
            h_ref[tm:] = _modulate(xs_ref, scs_ref, shs_ref)
            ots_ref[...] = _dot_nt(h_ref[tm:], wt_ref[...])

        both = _dot_nt(h_ref[...], w_ref[...])
        o_ref[...] = both[:tm]
        os_ref[...] = both[tm:]


def _proj(x, sc, sh, xs, scs, shs, w_t, layer, n_main, tm, name):
    m, ms = x.shape[0], xs.shape[0]
    n_tail = w_t.shape[1] - n_main
    w_tail = jnp.pad(w_t[layer, n_main:, :], ((0, LANES - n_tail), (0, 0)))
    tn = 512
    n_i = m // tm
    tiles_per_group = n_i // sc.shape[0]
    mod_spec = pl.BlockSpec((None, sc.shape[1], D_MODEL), lambda i, j: (i // tiles_per_group, 0, 0))
    const2 = lambda i, j: (0, 0)
    const3 = lambda i, j: (0, 0, 0)
    last_tile_col = lambda i, j: (0, jnp.where(i == n_i - 1, j, 0))
    return pl.pallas_call(
        functools.partial(_proj_kernel, n_i=n_i),
        grid=(n_i, n_main // tn),
        in_specs=[
            pl.BlockSpec((tm, D_MODEL), lambda i, j: (i, 0), pipeline_mode=pl.Buffered(1)),
            mod_spec,
            mod_spec,
            pl.BlockSpec((ms, D_MODEL), const2),
            pl.BlockSpec((None, ms, D_MODEL), const3),
            pl.BlockSpec((None, ms, D_MODEL), const3),
            pl.BlockSpec((None, tn, D_MODEL), lambda i, j: (layer, j, 0)),
            pl.BlockSpec((LANES, D_MODEL), const2),
        ],
        out_specs=[
            pl.BlockSpec((tm, tn), lambda i, j: (i, j)),
            pl.BlockSpec((tm, LANES), lambda i, j: (i, 0)),
            pl.BlockSpec((ms, tn), last_tile_col),
            pl.BlockSpec((ms, LANES), const2),
        ],
        out_shape=[jax.ShapeDtypeStruct((m, n_main), F32), jax.ShapeDtypeStruct((m, LANES), F32),
                   jax.ShapeDtypeStruct((ms, n_main), F32), jax.ShapeDtypeStruct((ms, LANES), F32)],
        scratch_shapes=[pltpu.VMEM((tm + ms, D_MODEL), BF16)],
        compiler_params=_params("arbitrary", "arbitrary"),
        name=name,
    )(x, sc, sh, xs, scs, shs, w_t, w_tail)


def _out_ln_kernel(a_ref, w_ref, x_ref, gt_ref, as_ref, xs_ref, gts_ref, g_ref, b_ref, o_ref, os_ref,
                   *, n_sub, n_i):
    def deep_norm(x, gate, m):
        y = DN_ALPHA * x + gate * m
        mu = jnp.mean(y, axis=-1, keepdims=True)
        yc = y - mu
        var = jnp.mean(yc * yc, axis=-1, keepdims=True)
        return yc * lax.rsqrt(var + LN_EPS) * g_ref[...] + b_ref[...]

    tm = a_ref.shape[0]
    rows = [slice(s * (tm // n_sub), (s + 1) * (tm // n_sub)) for s in range(n_sub)]
    proj = [jnp.dot(a_ref[rs, :], w_ref[...], preferred_element_type=F32) for rs in rows]
    for rs, m in zip(rows, proj):
        o_ref[rs, :] = deep_norm(x_ref[rs, :], gt_ref[...], m)

    @pl.when(pl.program_id(0) == n_i - 1)
    def _():
        os_ref[...] = deep_norm(xs_ref[...], gts_ref[...],
                                jnp.dot(as_ref[...], w_ref[...], preferred_element_type=F32))


def _out_ln(a, x, gate, a_s, x_s, gate_s, w, w_layer, ln_g, ln_b, ln_layer, tm, name):
    m, kdim = a.shape
    ms = a_s.shape[0]
    n_i = m // tm
    tiles_per_group = n_i // gate.shape[0]
    vec_spec = pl.BlockSpec((None, 1, D_MODEL), lambda i: (ln_layer, 0, 0))
    return pl.pallas_call(
        functools.partial(_out_ln_kernel, n_sub=2 if tm % 512 == 0 else 1, n_i=n_i),
        grid=(n_i,),
        in_specs=[
            pl.BlockSpec((tm, kdim), lambda i: (i, 0)),
            pl.BlockSpec((None, kdim, D_MODEL), lambda i: (w_layer, 0, 0), pipeline_mode=pl.Buffered(1)),
            pl.BlockSpec((tm, D_MODEL), lambda i: (i, 0)),
            pl.BlockSpec((None, 1, D_MODEL), lambda i: (i // tiles_per_group, 0, 0)),
            pl.BlockSpec((ms, kdim), lambda i: (0, 0)),
            pl.BlockSpec((ms, D_MODEL), lambda i: (0, 0)),
            pl.BlockSpec((None, ms, D_MODEL), lambda i: (0, 0, 0)),
            vec_spec,
            vec_spec,
        ],
        out_specs=[pl.BlockSpec((tm, D_MODEL), lambda i: (i, 0)),
                   pl.BlockSpec((ms, D_MODEL), lambda i: (0, 0))],
        out_shape=[jax.ShapeDtypeStruct((m, D_MODEL), F32), jax.ShapeDtypeStruct((ms, D_MODEL), F32)],
        compiler_params=_params("arbitrary"),
        name=name,
    )(a, w, x, gate, a_s, x_s, gate_s, ln_g.reshape(-1, 1, D_MODEL), ln_b.reshape(-1, 1, D_MODEL))


def _ffn_in_kernel(x_ref, sc_ref, sh_ref, xs_ref, scs_ref, shs_ref, wa_ref, wu_ref,
                   o_ref, os_ref, h_ref, *, n_i):
    i, j = pl.program_id(0), pl.program_id(1)
    tm = x_ref.shape[0]

    @pl.when(j == 0)
    def _():
        h_ref[:tm] = _modulate(x_ref, sc_ref, sh_ref)

    wa = wa_ref[...].astype(BF16)
    wu = wu_ref[...].astype(BF16)

    def swiglu_in(h):
        return (_silu(_dot(h, wa)) * _dot(h, wu)).astype(BF16)

    @pl.when(i < n_i - 1)
    def _():
        o_ref[...] = swiglu_in(h_ref[:tm])

    @pl.when(i == n_i - 1)
    def _():
        @pl.when(j == 0)
        def _():
            h_ref[tm:] = _modulate(xs_ref, scs_ref, shs_ref)

        both = swiglu_in(h_ref[...])
        o_ref[...] = both[:tm]
        os_ref[...] = both[tm:]


def _ffn_in(x, sc, sh, xs, scs, shs, w, layer, tm, name):
    m, ms = x.shape[0], xs.shape[0]
    tn = 256
    nj = D_FF // tn
    n_i = m // tm
    tiles_per_group = n_i // sc.shape[0]
    mod_spec = pl.BlockSpec((None, sc.shape[1], D_MODEL), lambda i, j: (i // tiles_per_group, 0, 0))
    const3 = lambda i, j: (0, 0, 0)
    return pl.pallas_call(
        functools.partial(_ffn_in_kernel, n_i=n_i),
        grid=(n_i, nj),
        in_specs=[
            pl.BlockSpec((tm, D_MODEL), lambda i, j: (i, 0), pipeline_mode=pl.Buffered(1)),
            mod_spec,
            mod_spec,
            pl.BlockSpec((ms, D_MODEL), lambda i, j: (0, 0)),
            pl.BlockSpec((None, ms, D_MODEL), const3),
            pl.BlockSpec((None, ms, D_MODEL), const3),
            pl.BlockSpec((None, D_MODEL, tn), lambda i, j: (layer, 0, j)),
            pl.BlockSpec((None, D_MODEL, tn), lambda i, j: (layer, 0, j + nj)),
        ],
        out_specs=[
            pl.BlockSpec((tm, tn), lambda i, j: (i, j)),
            pl.BlockSpec((ms, tn), lambda i, j: (0, jnp.where(i == n_i - 1, j, 0))),
        ],
        out_shape=[jax.ShapeDtypeStruct((m, D_FF), BF16), jax.ShapeDtypeStruct((ms, D_FF), BF16)],
        scratch_shapes=[pltpu.VMEM((tm + ms, D_MODEL), BF16)],
        compiler_params=_params("arbitrary", "arbitrary"),
        name=name,
    )(x, sc, sh, xs, scs, shs, w, w)


def _gla_kernel(*refs, chunk, t_valid, has_s0, n_cast):
    n_in = 8 + has_s0
    q_ref, k_ref, v_ref, r_ref, a_ref, wa_ref, ba_ref, g_ref = refs[:8]
    s0_ref = refs[8] if has_s0 else None
    o_ref, st_ref = refs[n_in + n_cast:n_in + n_cast + 2]
    for src, dst in zip(refs[n_in:n_in + n_cast], refs[n_in + n_cast + 2:]):
        dst[...] = src[...].astype(BF16)
    sub = GLA_SUB
    n_sub = chunk // sub
    anchor = sub // 2 - 1

    @pl.when(pl.program_id(1) == 0)
    def _():
        if has_s0:
            st_ref[...] = s0_ref[...]
        else:
            st_ref[...] = jnp.zeros(st_ref.shape, F32)

    row = lax.broadcasted_iota(jnp.int32, (chunk, 1), 0)
    tri = _ind(lax.broadcasted_iota(jnp.int32, (chunk, chunk), 1)
               <= lax.broadcasted_iota(jnp.int32, (chunk, chunk), 0), BF16)
    ones = jnp.ones((chunk, LANES), BF16)
    causal = (lax.broadcasted_iota(jnp.int32, (sub, sub), 1)
              <= lax.broadcasted_iota(jnp.int32, (sub, sub), 0))
    heads = range(GLA_HEADS)
    dk = [slice(h * GLA_DK, (h + 1) * GLA_DK) for h in heads]
    dv = [slice(h * GLA_DV, (h + 1) * GLA_DV) for h in heads]
    subs = [slice(s * sub, (s + 1) * sub) for s in range(n_sub)]

    z = _dot(a_ref[...], wa_ref[...]) + ba_ref[...]
    lb = (jnp.minimum(z, 0.0) - jnp.log1p(jnp.exp(-jnp.abs(z)))) * (1.0 / GLA_TAU)
    k = k_ref[...]
    if t_valid < chunk:
        lb = jnp.where(row < t_valid, lb, 0.0)
        k = jnp.where(row < t_valid, k, 0.0)
    q = q_ref[...] * (GLA_DK ** -0.5)
    vb = v_ref[...].astype(BF16)
    lb3 = _split3(lb)
    b = sum(jnp.dot(tri, t, preferred_element_type=F32) for t in lb3)
    dec = jnp.exp(sum(lax.dot_general(t, ones, (((0,), (0,)), ((), ())), preferred_element_type=F32)
                      for t in lb3))
    b_last = b[chunk - 1:chunk, :]
    q_in = (q * jnp.exp(b)).astype(BF16)
    k_out = (k * jnp.exp(b_last - b)).astype(BF16)
    q_diag, k_diag, q_off, k_off = [], [], [], []
    for s, rs in enumerate(subs):
        b_mid = b[s * sub + anchor:s * sub + anchor + 1, :]
        q_diag.append((q[rs] * jnp.exp(b[rs] - b_mid)).astype(BF16))
        k_diag.append((k[rs] * jnp.exp(b_mid - b[rs])).astype(BF16))
        if s > 0:
            prev = slice(0, s * sub)
            b_in = b[s * sub - 1:s * sub, :]
            q_off.append((q[rs] * jnp.exp(b[rs] - b_in)).astype(BF16))
            k_off.append((k[prev] * jnp.exp(b_in - b[prev])).astype(BF16))

    s_old = [st_ref[0, h] for h in heads]
    o_inter = [_dot(q_in[:, dk[h]], s_old[h]) for h in heads]
    a_diag = [[_dot_nt(q_diag[s][:, dk[h]], k_diag[s][:, dk[h]]) for h in heads] for s in range(n_sub)]
    a_off = [[_dot_nt(q_off[s][:, dk[h]], k_off[s][:, dk[h]]) for h in heads] for s in range(n_sub - 1)]
    a_diag = [[jnp.where(causal, a, 0.0).astype(BF16) for a in row_] for row_ in a_diag]
    a_off = [[a.astype(BF16) for a in row_] for row_ in a_off]
    o_intra = []
    for s, rs in enumerate(subs):
        o_s = [_dot(a_diag[s][h], vb[rs, dv[h]]) for h in heads]
        if s > 0:
            o_s = [o_s[h] + _dot(a_off[s - 1][h], vb[0:s * sub, dv[h]]) for h in heads]
        o_intra.append(o_s)
    kv = [_dot_tn(k_out[:, dk[h]], vb[:, dv[h]]) for h in heads]
    for h in heads:
        st_ref[0, h] = s_old[h] * jnp.tile(dec[dk[h], :], (1, GLA_DV // LANES)) + kv[h]
    for h in heads:
        o = o_inter[h] + jnp.concatenate([o_intra[s][h] for s in range(n_sub)], axis=0)
        o = o * lax.rsqrt(jnp.mean(o * o, axis=-1, keepdims=True) + LN_EPS)
        o_ref[:, dv[h]] = (o * g_ref[:, dv[h]] * _silu(r_ref[:, dv[h]])).astype(BF16)


def _gla(main, tail, w_alpha, b_alpha, norm_g, s0, batch, seq, chunk, t_valid, to_bf16=()):
    n_chunks = seq // chunk
    qk_w = GLA_HEADS * GLA_DK
    v_w = GLA_HEADS * GLA_DV
    wa = jnp.pad(w_alpha, ((0, LANES - GLA_RANK), (0, 0)))
    row_map = lambda b, c: (b * n_chunks + c, 0)
    const = lambda b, c: (0, 0)
    in_specs = [
        pl.BlockSpec((chunk, qk_w), row_map),
        pl.BlockSpec((chunk, qk_w), lambda b, c: (b * n_chunks + c, 1)),
        pl.BlockSpec((chunk, v_w), lambda b, c: (b * n_chunks + c, 1)),
        pl.BlockSpec((chunk, v_w), lambda b, c: (b * n_chunks + c, 2)),
        pl.BlockSpec((chunk, LANES), row_map),
        pl.BlockSpec((LANES, qk_w), const),
        pl.BlockSpec((1, qk_w), const),
        pl.BlockSpec((1, v_w), const),
    ]
    args = [main, main, main, main, tail, wa, b_alpha.reshape(1, qk_w), norm_g.reshape(1, v_w)]
    state_spec = pl.BlockSpec((1, GLA_HEADS, GLA_DK, GLA_DV), lambda b, c: (b, 0, 0, 0))
    if s0 is not None:
        in_specs.append(state_spec)
        args.append(s0)
    n_steps = batch * n_chunks
    cast_specs = [pl.BlockSpec((a.shape[0] // n_steps, a.shape[1]), row_map) for a in to_bf16]
    assert all(a.shape[0] % (n_steps * BF16_ROWS) == 0 for a in to_bf16)
    return pl.pallas_call(
        functools.partial(_gla_kernel, chunk=chunk, t_valid=t_valid, has_s0=s0 is not None,
                          n_cast=len(to_bf16)),
        grid=(batch, n_chunks),
        in_specs=in_specs + cast_specs,
        out_specs=[pl.BlockSpec((chunk, v_w), row_map), state_spec] + cast_specs,
        out_shape=[jax.ShapeDtypeStruct((batch * seq, v_w), BF16),
                   jax.ShapeDtypeStruct((batch, GLA_HEADS, GLA_DK, GLA_DV), F32)]
        + [jax.ShapeDtypeStruct(a.shape, BF16) for a in to_bf16],
        compiler_params=_params("parallel", "arbitrary"),
        name="gla_prompt" if s0 is None else "gla_sample",
    )(*args, *to_bf16)


def _select_blocks(imp_t, blk, cur):
    n = imp_t.shape[0]
    forced = (blk == 0) | ((blk >= cur - 1) & (blk <= cur))
    score = jnp.where(blk > cur, -jnp.inf, jnp.where(forced, jnp.inf, imp_t))
    rank = jnp.zeros(score.shape, F32)
    for i in range(n):
        si = score[i:i + 1, :]
        rank = rank + jnp.where(blk > i, _ind(si >= score), _ind(si > score))
    return (rank < NSA_TOPK) & (score > -jnp.inf)


def _nsa_prompt_kernel(pt_ref, q_ref, kc_ref, vc_ref, ks_ref, vs_ref, kw_ref, vw_ref, gt_ref, bg_ref, *refs,
                       seq, tq, tk, pages_per_step):
    page_refs = refs[:pages_per_step]
    (o_ref, means_ref, kcm, vcm, ksb, vst, kwb, vwt, qs_ref, sel_ref,
     ms_ref, accs_ref, mw_ref, accw_ref) = refs[pages_per_step:]
    qi = pl.program_id(2)
    n_blk = seq // NSA_BLOCK
    n_kt = seq // tk
    blk_per_kt = tk // NSA_BLOCK

    @pl.when(qi == 0)
    def _():
        kcm[...] = jnp.sum(kc_ref[...].reshape(n_blk, NSA_BLOCK, NSA_HD), axis=1) * (1.0 / NSA_BLOCK)
        vcm[...] = jnp.sum(vc_ref[...].reshape(n_blk, NSA_BLOCK, NSA_HD), axis=1) * (1.0 / NSA_BLOCK)
        ksb[...] = ks_ref[...].astype(BF16)
        kwb[...] = kw_ref[...].astype(BF16)
        ones_rows = _ind(lax.broadcasted_iota(jnp.int32, (BF16_ROWS, tk), 0) == 0, BF16)
        for kt in range(n_kt):
            rows = slice(kt * tk, (kt + 1) * tk)
            vst[kt] = jnp.concatenate([vs_ref[rows, :].T.astype(BF16), ones_rows], axis=0)
            vwt[kt] = jnp.concatenate([vw_ref[rows, :].T.astype(BF16), ones_rows], axis=0)

    t0 = qi * tq
    q = q_ref[...] * (NSA_HD ** -0.5 * LOG2_E)
    qs_ref[...] = jnp.concatenate(
        [q[:, g * NSA_HD:(g + 1) * NSA_HD] for g in range(NSA_GROUP)], axis=0).astype(BF16)
    qs = qs_ref[...]
    t_row = t0 + lax.broadcasted_iota(jnp.int32, (1, tq), 1)
    t_row_g = jnp.concatenate([t_row] * NSA_GROUP, axis=1)

    kc_b = kcm[...].astype(BF16)
    blk_col = lax.broadcasted_iota(jnp.int32, (n_blk, 1), 0)
    p_t = _masked_softmax2(_dot_nt(kc_b, qs), (blk_col + 1) * NSA_BLOCK - 1 <= t_row_g, axis=0)
    o_c = _dot_tn(vcm[...], p_t)
    imp_t = sum(p_t[:, g * tq:(g + 1) * tq] for g in range(NSA_GROUP))
    sel_ref[...] = jnp.where(_select_blocks(imp_t, blk_col, t_row // NSA_BLOCK), 0.0, NEG_BIG)

    n_loc = lax.broadcasted_iota(jnp.int32, (tk, tq), 0)
    t_loc = lax.broadcasted_iota(jnp.int32, (tk, tq), 1)
    on_lanes = lambda a: jnp.concatenate([a] * NSA_GROUP, axis=1)
    causal = on_lanes(jnp.where(n_loc <= t_loc, 0.0, NEG_BIG))
    far = on_lanes(jnp.where(n_loc > t_loc, 0.0, NEG_BIG))

    def sel_bias(kt):
        return on_lanes(jnp.concatenate(
            [jnp.broadcast_to(sel_ref[pl.ds(kt * blk_per_kt + j, 1), :], (NSA_BLOCK, tq))
             for j in range(blk_per_kt)], axis=0))

    sel_state, win_state = (ms_ref, accs_ref), (mw_ref, accw_ref)
    for m_ref, acc_ref in (sel_state, win_state):
        m_ref[...] = jnp.full(m_ref.shape, NEG_BIG, F32)
        acc_ref[...] = jnp.zeros(acc_ref.shape, F32)

    def steps(*work):
        s = [_dot_nt(k_ref[pl.ds(pl.multiple_of(kt * tk, tk), tk), :], qs_ref[...]) + bias
             for _, k_ref, _, kt, bias in work]
        m_old = [state[0][...] for state, *_ in work]
        m_new = [jnp.maximum(mo, jnp.max(si, axis=0, keepdims=True)) for mo, si in zip(m_old, s)]
        p = [jnp.exp2(si - mn).astype(BF16) for si, mn in zip(s, m_new)]
        pv = [_dot(vt_ref[kt], pi) for (_, _, vt_ref, kt, _), pi in zip(work, p)]
        for (state, *_), mo, mn, pvi in zip(work, m_old, m_new, pv):
            state[1][...] = jnp.exp2(mo - mn) * state[1][...] + pvi
            state[0][...] = mn

    def result(acc_ref):
        acc = acc_ref[...]
        return acc[:NSA_HD] / jnp.maximum(acc[NSA_HD:NSA_HD + 1], TINY)

    steps((sel_state, ksb, vst, qi, sel_bias(qi) + causal), (win_state, kwb, vwt, qi, causal))

    @pl.when(qi >= 1)
    def _():
        steps((sel_state, ksb, vst, qi - 1, sel_bias(qi - 1)), (win_state, kwb, vwt, qi - 1, far))

    def sel_body(i, carry):
        kt = qi - 2 - i
        steps((sel_state, ksb, vst, kt, sel_bias(kt)))
        return carry

    lax.fori_loop(0, qi - 1, sel_body, 0)

    gates = jax.nn.sigmoid(gt_ref[...] + bg_ref[...])
    g_c, g_s, g_w = (jnp.concatenate([gates[3 * g + br:3 * g + br + 1, :] for g in range(NSA_GROUP)], axis=1)
                     for br in range(3))
    o_t = g_c * o_c + g_s * result(accs_ref) + g_w * result(accw_ref)
    for g in range(NSA_GROUP):
        o_ref[:, g * NSA_HD:(g + 1) * NSA_HD] = o_t[:, g * tq:(g + 1) * tq].T.astype(BF16)

    for i, x_ref in enumerate(page_refs):
        x = x_ref[0].reshape(PAGE // NSA_BLOCK, NSA_BLOCK, KV_ROWS, NSA_HD)
        means_ref[0, i] = jnp.sum(x, axis=1) * (1.0 / NSA_BLOCK)


def _nsa_prompt(main, gates, b_gate, batch, seq, cache_cmp, page_table):
    tq = tk = NSA_WINDOW
    assert seq % tk == 0
    vt_rows = NSA_HD + BF16_ROWS
    nq = seq // tq
    q_lanes = NSA_GROUP * NSA_HD
    col0 = NSA_Q_W // NSA_HD
    seqs, n_pages = page_table.shape
    n_steps = batch * NSA_KV * nq
    pages_per_step = seqs * n_pages // n_steps
    groups = n_pages // pages_per_step
    assert pages_per_step * n_steps == seqs * n_pages and groups * pages_per_step == n_pages
    per_page = PAGE // NSA_BLOCK

    def kv_spec(branch, part):
        off = col0 + (2 * branch + part) * NSA_KV
        return pl.BlockSpec((seq, NSA_HD), lambda b, kv, qi, pt: (b, off + kv))

    def step_of(b, kv, qi):
        return (b * NSA_KV + kv) * nq + qi

    def page_spec(i):
        def index(b, kv, qi, pt):
            s = step_of(b, kv, qi)
            return (pt[s // groups, (s % groups) * pages_per_step + i], 0, 0)
        return pl.BlockSpec((1, PAGE * KV_ROWS, NSA_HD), index)

    def means_index(b, kv, qi, pt):
        s = step_of(b, kv, qi)
        return (s // groups, s % groups, 0, 0, 0)

    g_w = 3 * NSA_GROUP
    g_rows = NSA_GROUP * tq
    grid_spec = pltpu.PrefetchScalarGridSpec(
        num_scalar_prefetch=1,
        grid=(batch, NSA_KV, nq),
        in_specs=[pl.BlockSpec((tq, q_lanes), lambda b, kv, qi, pt: (b * nq + qi, kv))]
        + [kv_spec(br, part) for br in range(3) for part in range(2)]
        + [pl.BlockSpec((None, g_w, tq), lambda b, kv, qi, pt: (kv, 0, b * nq + qi)),
           pl.BlockSpec((None, g_w, 1), lambda b, kv, qi, pt: (kv, 0, 0))]
        + [page_spec(i) for i in range(pages_per_step)],
        out_specs=[pl.BlockSpec((tq, q_lanes), lambda b, kv, qi, pt: (b * nq + qi, kv)),
                   pl.BlockSpec((1, pages_per_step, per_page, KV_ROWS, NSA_HD), means_index)],
        scratch_shapes=[
            pltpu.VMEM((seq // NSA_BLOCK, NSA_HD), F32),
            pltpu.VMEM((seq // NSA_BLOCK, NSA_HD), F32),
            pltpu.VMEM((seq, NSA_HD), BF16),
            pltpu.VMEM((seq // tk, vt_rows, tk), BF16),
            pltpu.VMEM((seq, NSA_HD), BF16),
            pltpu.VMEM((seq // tk, vt_rows, tk), BF16),
            pltpu.VMEM((g_rows, NSA_HD), BF16),
            pltpu.VMEM((seq // NSA_BLOCK, tq), F32),
            pltpu.VMEM((1, g_rows), F32),
            pltpu.VMEM((vt_rows, g_rows), F32),
            pltpu.VMEM((1, g_rows), F32),
            pltpu.VMEM((vt_rows, g_rows), F32),
        ],
    )
    return pl.pallas_call(
        functools.partial(_nsa_prompt_kernel, seq=seq, tq=tq, tk=tk, pages_per_step=pages_per_step),
        grid_spec=grid_spec,
        out_shape=[jax.ShapeDtypeStruct((batch * seq, NSA_Q_W), BF16),
                   jax.ShapeDtypeStruct((seqs, n_pages, per_page, KV_ROWS, NSA_HD), F32)],
        compiler_params=_params("arbitrary", "arbitrary", "arbitrary"),
        name="nsa_prompt",
    )(page_table, main, main, main, main, main, main, main, gates, b_gate, *([cache_cmp] * pages_per_step))


def _kv_rows_out_kernel(c_ref, s_ref, w_ref, oc_ref, os_ref, ow_ref, *, rows, n_c):
    def spread(src, dst):
        for r in range(KV_ROWS):
            dst[pl.ds(r, rows, stride=KV_ROWS), :] = src[:, r * NSA_HD:(r + 1) * NSA_HD]

    spread(c_ref, oc_ref)
    spread(s_ref, os_ref)

    @pl.when(pl.program_id(1) == n_c - 1)
    def _():
        spread(w_ref, ow_ref)


def _kv_rows_out(main, batch, seq, keep):
    rows = keep
    n_c = seq // rows
    col0 = NSA_Q_W // NSA_KV_W
    tile = lambda br: pl.BlockSpec((rows, NSA_KV_W), lambda b, c: (b * n_c + c, col0 + br))
    out_tile = pl.BlockSpec((rows * KV_ROWS, NSA_HD), lambda b, c: (b * n_c + c, 0))
    return pl.pallas_call(
        functools.partial(_kv_rows_out_kernel, rows=rows, n_c=n_c),
        grid=(batch, n_c),
        in_specs=[tile(0), tile(1),
                  pl.BlockSpec((rows, NSA_KV_W), lambda b, c: (b * n_c + n_c - 1, col0 + 2))],
        out_specs=[out_tile, out_tile, pl.BlockSpec((rows * KV_ROWS, NSA_HD), lambda b, c: (b, 0))],
        out_shape=[jax.ShapeDtypeStruct((batch * seq * KV_ROWS, NSA_HD), F32)] * 2
        + [jax.ShapeDtypeStruct((batch * keep * KV_ROWS, NSA_HD), F32)],
        compiler_params=_params("arbitrary", "arbitrary"),
        name="nsa_kv_rows_out",
    )(main, main, main)


def _page_specs(n):
    return [pl.BlockSpec((1, PAGE * KV_ROWS, NSA_HD), lambda b, p, pt, i=i: (pt[b, p * n + i], 0, 0))
            for i in range(n)]


def _nsa_sample_kernel(pt_ref, q_ref, cm_ref, *refs, past, n_new, n_steps, per_step):
    pg_refs = refs[:per_step]
    (kvn_ref, wb_ref, kwn_ref, gt_ref, bg_ref, o_ref,
     qr_ref, sel_ref, m_ref, l_ref, acc_ref, oc_ref, ow_ref) = refs[per_step:]
    p = pl.program_id(1)
    k_w = NSA_KV * NSA_HD
    cols = NSA_KV * n_new * NSA_GROUP
    per_kv = n_new * NSA_GROUP
    n_cmp = past // NSA_BLOCK
    n_sel = -(-(past + n_new) // NSA_BLOCK)
    n_sel_pad = sel_ref.shape[0]
    lane = lax.broadcasted_iota(jnp.int32, (1, cols), 1)
    q_pos = past + (lane // NSA_GROUP) % n_new

    def heads_on_lanes(ref, n, part):
        return jnp.concatenate(
            [ref[0, pl.ds(part * NSA_KV + kv, n, stride=KV_ROWS), :] for kv in range(NSA_KV)], axis=1)

    def scores(keys):
        return _dot_nt(keys, qr_ref[...])

    def own_head(x_t):
        out = jnp.zeros((NSA_HD, cols), F32)
        for kv in range(NSA_KV):
            mine = (lane // per_kv) == kv
            out = out + jnp.where(mine, x_t[kv * NSA_HD:(kv + 1) * NSA_HD, :], 0.0)
        return out

    def attend_once(keys, vals, mask):
        p_t = _masked_softmax2(scores(keys), mask, axis=0)
        return own_head(_dot_tn(vals, p_t))

    def online(keys, vals, bias):
        s = scores(keys) + bias
        m_old = m_ref[...]
        m_new = jnp.maximum(m_old, jnp.max(s, axis=0, keepdims=True))
        p_t = jnp.exp2(s - m_new)
        alpha = jnp.exp2(m_old - m_new)
        l_ref[...] = alpha * l_ref[...] + jnp.sum(p_t, axis=0, keepdims=True)
        acc_ref[...] = alpha * acc_ref[...] + own_head(_dot_tn(vals, p_t))
        m_ref[...] = m_new

    @pl.when(p == 0)
    def _():
        q = q_ref[0] * (NSA_HD ** -0.5 * LOG2_E)
        r_kv = lax.broadcasted_iota(jnp.int32, (cols, k_w), 0) // per_kv
        c_kv = lax.broadcasted_iota(jnp.int32, (cols, k_w), 1) // NSA_HD
        qr_ref[...] = jnp.where(r_kv == c_kv, jnp.tile(q, (1, NSA_KV)), 0.0).astype(BF16)
        cm = cm_ref[0]
        blk = lax.broadcasted_iota(jnp.int32, (n_cmp, 1), 0)
        p_t = _masked_softmax2(scores(cm[:, :k_w]), (blk + 1) * NSA_BLOCK - 1 <= q_pos, axis=0)
        oc_ref[...] = own_head(_dot_tn(cm[:, k_w:], p_t))
        same = _ind(lax.broadcasted_iota(jnp.int32, (cols, cols), 0) // NSA_GROUP
                    == lax.broadcasted_iota(jnp.int32, (cols, cols), 1) // NSA_GROUP, BF16)
        imp = sum(jnp.dot(t, same, preferred_element_type=F32) for t in _split3(p_t))
        imp = jnp.concatenate([imp, jnp.zeros((n_sel_pad - n_cmp, cols), F32)], axis=0)
        blk_s = lax.broadcasted_iota(jnp.int32, (n_sel_pad, 1), 0)
        cur = q_pos // NSA_BLOCK
        forced = (blk_s == 0) | ((blk_s >= cur - 1) & (blk_s <= cur))
        score = jnp.where((blk_s > cur) | (blk_s >= n_sel), -jnp.inf, jnp.where(forced, jnp.inf, imp))
        blk_f = blk_s.astype(F32)

        def pick(_, carry):
            left, bias = carry
            top = jnp.max(left, axis=0, keepdims=True)
            first = jnp.min(jnp.where(left == top, blk_f, float(n_sel_pad)), axis=0, keepdims=True)
            hit = (blk_f == first) & (top > -jnp.inf)
            return jnp.where(hit, -jnp.inf, left), jnp.where(hit, 0.0, bias)

        _, bias = lax.fori_loop(0, NSA_TOPK, pick, (score, jnp.full(score.shape, NEG_BIG, F32)))
        sel_ref[...] = bias
        n_wb = wb_ref.shape[1] // KV_ROWS
        kw_all = jnp.concatenate([heads_on_lanes(wb_ref, n_wb, 0), kwn_ref[0][:, :k_w]], axis=0)
        vw_all = jnp.concatenate([heads_on_lanes(wb_ref, n_wb, 1), kwn_ref[0][:, k_w:]], axis=0)
        n_w = kw_all.shape[0]
        w_row = lax.broadcasted_iota(jnp.int32, (n_w, 1), 0)
        w_pos = past - n_wb + w_row
        d = q_pos - w_pos
        ow_ref[...] = attend_once(kw_all, vw_all,
                                  (d >= 0) & (d < NSA_WINDOW) & (w_pos >= 0) & (w_row < n_wb + n_new))
        m_ref[...] = jnp.full(m_ref.shape, NEG_BIG, F32)
        l_ref[...] = jnp.zeros(l_ref.shape, F32)
        acc_ref[...] = jnp.zeros(acc_ref.shape, F32)

    def page_part(part):
        return jnp.concatenate([heads_on_lanes(pg_ref, PAGE, part) for pg_ref in pg_refs], axis=0)

    blk_per_step = per_step * PAGE // NSA_BLOCK
    online(page_part(0), page_part(1), jnp.concatenate(
        [jnp.broadcast_to(sel_ref[pl.ds(p * blk_per_step + j, 1), :], (NSA_BLOCK, cols))
         for j in range(blk_per_step)], axis=0))

    @pl.when(p == n_steps - 1)
    def _():
        kvn = kvn_ref[0]
        n_row = lax.broadcasted_iota(jnp.int32, (kvn.shape[0], 1), 0)
        bias_new = jnp.where((n_row < n_new) & (past + n_row <= q_pos), 0.0, NEG_BIG)
        online(kvn[:, :k_w], kvn[:, k_w:], bias_new + sel_ref[pl.ds(past // NSA_BLOCK, 1), :])
        o_s = acc_ref[...] / jnp.maximum(l_ref[...], TINY)
        gates = jax.nn.sigmoid(gt_ref[0] + bg_ref[...])
        o_t = gates[0:1, :] * oc_ref[...] + gates[1:2, :] * o_s + gates[2:3, :] * ow_ref[...]
        eye = _ind(lax.broadcasted_iota(jnp.int32, (NSA_HD, NSA_HD), 0)
                   == lax.broadcasted_iota(jnp.int32, (NSA_HD, NSA_HD), 1), BF16)
        o_ref[0] = sum(lax.dot_general(t, eye, (((0,), (0,)), ((), ())), preferred_element_type=F32)
                       for t in _split3(o_t))


def _nsa_sample(q_rows, cmp_means, cache_slc, page_table, kv_new, win_buf, kw_new, gates, b_gate,
                past, n_new):
    seqs, n_pages = page_table.shape
    cols = q_rows.shape[1]
    n_cmp = cmp_means.shape[1]
    n_sel_pad = -(-(-(-(past + n_new) // NSA_BLOCK)) // 8) * 8
    n_wb = win_buf.shape[1]
    pad_new = kv_new.shape[1]
    per_step = 8
    n_steps = n_pages // per_step
    seq_map = lambda b, p, pt: (b, 0, 0)
    return pl.pallas_call(
        functools.partial(_nsa_sample_kernel, past=past, n_new=n_new, n_steps=n_steps, per_step=per_step),
        grid_spec=pltpu.PrefetchScalarGridSpec(
            num_scalar_prefetch=1,
            grid=(seqs, n_steps),
            in_specs=[
                pl.BlockSpec((1, cols, NSA_HD), seq_map),
                pl.BlockSpec((1, n_cmp, NSA_KV_W), seq_map),
                *_page_specs(per_step),
                pl.BlockSpec((1, pad_new, NSA_KV_W), seq_map),
                pl.BlockSpec((1, n_wb, NSA_HD), seq_map),
                pl.BlockSpec((1, pad_new, NSA_KV_W), seq_map),
                pl.BlockSpec((1, 3, cols), seq_map),
                pl.BlockSpec((3, cols), lambda b, p, pt: (0, 0)),
            ],
            out_specs=pl.BlockSpec((1, cols, NSA_HD), seq_map),
            scratch_shapes=[
                pltpu.VMEM((cols, NSA_KV * NSA_HD), BF16),
                pltpu.VMEM((n_sel_pad, cols), F32),
                pltpu.VMEM((1, cols), F32),
                pltpu.VMEM((1, cols), F32),
                pltpu.VMEM((NSA_HD, cols), F32),
                pltpu.VMEM((NSA_HD, cols), F32),
                pltpu.VMEM((NSA_HD, cols), F32),
            ],
        ),
        out_shape=jax.ShapeDtypeStruct((seqs, cols, NSA_HD), F32),
        compiler_params=_params("parallel", "arbitrary"),
        name="nsa_sample",
    )(page_table, q_rows, cmp_means, *([cache_slc] * per_step), kv_new, win_buf, kw_new, gates, b_gate)


def kernel(x_prompt, x_sample, c_prompt, c_sample, state_gla, cache_cmp_kv, cache_slc_kv, cache_win_kv,
           page_table, ada_w, ada_b, gla_w_in, gla_w_alpha, gla_b_alpha, gla_norm_g, gla_w_out, nsa_w_in,
           nsa_b_gate, nsa_w_out, ln_mix_g, ln_mix_b, ffn_w_in, ffn_w_out, ln_ffn_g, ln_ffn_b):
    batch, seq, _ = x_prompt.shape
    seqs, n_new, _ = x_sample.shape
    n_pages = page_table.shape[1]
    past = n_pages * PAGE
    rows_s = seqs * n_new
    tm_p = seq

    pad_rows = 16 - (batch + seqs)
    c_all = jnp.concatenate([c_prompt, c_sample, jnp.zeros((pad_rows, D_MODEL), F32)], axis=0)
    mod = _adaln(c_all, ada_w, ada_b).reshape(DEPTH, 16, 6, D_MODEL)

    def mods(layer):
        mp = [mod[layer, :batch, i].reshape(batch, 1, D_MODEL) for i in range(6)]
        ms = [jnp.repeat(mod[layer, batch:batch + seqs, i], n_new, axis=0).reshape(1, rows_s, D_MODEL)
              for i in range(6)]
        return mp, ms

    def mixer_out(o_p, o_s, xp, xs, layer, mp, ms, tag):
        return _out_ln(o_p, xp, mp[2], o_s, xs, ms[2], w_out_mix[layer % 2], layer // 2, ln_mix_g, ln_mix_b,
                       layer, 512, tag)

    def ffn(xp, xs, layer, mp, ms):
        act_p, act_s = _ffn_in(xp, mp[4], mp[3], xs, ms[4], ms[3], ffn_w_in, layer, tm_p, "ffn_in_%d" % layer)
        return _out_ln(act_p, xp, mp[5], act_s, xs, ms[5], w_out_ffn, layer, ln_ffn_g, ln_ffn_b, layer, 256,
                       "ffn_out_%d" % layer)

    xp = x_prompt.reshape(batch * seq, D_MODEL)
    xs = x_sample.reshape(rows_s, D_MODEL)

    mp, ms = mods(0)
    main_p, tail_p, main_s, tail_s = _proj(xp, mp[1], mp[0], xs, ms[1], ms[0], jnp.swapaxes(gla_w_in, 1, 2), 0,
                                           GLA_MAIN_W, tm_p, "gla_in")
    rows2d = lambda w: w.reshape(-1, w.shape[-1])
    gated_p, gla_state_p, *w_bf16 = _gla(main_p, tail_p, gla_w_alpha[0], gla_b_alpha[0], gla_norm_g[0], None,
                                         batch, seq, 256, 256,
                                         to_bf16=(rows2d(gla_w_out), rows2d(nsa_w_out), rows2d(ffn_w_out)))
    w_out_mix = (w_bf16[0].reshape(gla_w_out.shape), w_bf16[1].reshape(nsa_w_out.shape))
    w_out_ffn = w_bf16[2].reshape(ffn_w_out.shape)
    chunk_s = GLA_SUB

    def pad_steps(a):
        a = a.reshape(seqs, n_new, a.shape[-1])
        return jnp.pad(a, ((0, 0), (0, chunk_s - n_new), (0, 0))).reshape(seqs * chunk_s, a.shape[-1])

    gated_s, gla_state_s = _gla(pad_steps(main_s), pad_steps(tail_s), gla_w_alpha[0], gla_b_alpha[0],
                                gla_norm_g[0], state_gla[:, 0], seqs, chunk_s, chunk_s, n_new)
    gated_s = gated_s.reshape(seqs, chunk_s, -1)[:, :n_new].reshape(rows_s, -1)
    xp, xs = mixer_out(gated_p, gated_s, xp, xs, 0, mp, ms, "gla_out")
    xp, xs = ffn(xp, xs, 0, mp, ms)

    mp, ms = mods(1)
    main_p, tail_p, main_s, tail_s = _proj(xp, mp[1], mp[0], xs, ms[1], ms[0], jnp.swapaxes(nsa_w_in, 1, 2), 0,
                                           NSA_MAIN_W, tm_p, "nsa_in")
    n_gate = 3 * NSA_HEADS
    g_w = 3 * NSA_GROUP
    gates_p = tail_p[:, :n_gate].reshape(batch * seq, NSA_KV, g_w).transpose(1, 2, 0)
    b_gate_p = nsa_b_gate[0].reshape(NSA_KV, g_w, 1)
    n_phys = cache_cmp_kv.shape[0]
    page_rows = lambda cache: cache.reshape(n_phys, PAGE * KV_ROWS, NSA_HD)
    o_p, cmp_means = _nsa_prompt(main_p, gates_p, b_gate_p, batch, seq, page_rows(cache_cmp_kv), page_table)

    kv_rows = lambda m, br: m[:, NSA_Q_W + br * NSA_KV_W:NSA_Q_W + (br + 1) * NSA_KV_W]
    kv_shape = (2, NSA_KV, NSA_HD)
    keep_p = min(NSA_WINDOW, seq)
    cmp_rows, slc_rows, win_rows = _kv_rows_out(main_p, batch, seq, keep_p)
    cmp_kv_p = cmp_rows.reshape(batch, 1, seq, *kv_shape)
    slc_kv_p = slc_rows.reshape(batch, 1, seq, *kv_shape)
    win_kv_p = win_rows.reshape(batch, 1, keep_p, *kv_shape)

    cmp_kv_s = kv_rows(main_s, 0).reshape(seqs, 1, n_new, *kv_shape)
    slc_kv_s = kv_rows(main_s, 1).reshape(seqs, 1, n_new, *kv_shape)
    kw_new = kv_rows(main_s, 2).reshape(seqs, n_new, NSA_KV_W)
    n_wb = cache_win_kv.shape[2]
    win_buf = cache_win_kv[:, 0].reshape(seqs, n_wb * KV_ROWS, NSA_HD)
    all_w = jnp.concatenate([win_buf, kw_new.reshape(seqs, n_new * KV_ROWS, NSA_HD)], axis=1)
    keep_s = min(NSA_WINDOW, n_wb + n_new)
    win_kv_s = all_w[:, (n_wb + n_new - keep_s) * KV_ROWS:].reshape(seqs, 1, keep_s, *kv_shape)

    cmp_means = cmp_means.reshape(seqs, past // NSA_BLOCK, NSA_KV_W)
    pad_new = 8
    pad8 = lambda a: jnp.pad(a, ((0, 0), (0, pad_new - n_new), (0, 0)))
    cols = NSA_KV * n_new * NSA_GROUP
    q_rows = (main_s[:, :NSA_Q_W].reshape(seqs, n_new, NSA_KV, NSA_GROUP, NSA_HD)
              .transpose(0, 2, 1, 3, 4).reshape(seqs, cols, NSA_HD))
    gates_s = (tail_s[:, :n_gate].reshape(seqs, n_new, NSA_KV, NSA_GROUP, 3)
               .transpose(0, 4, 2, 1, 3).reshape(seqs, 3, cols))
    b_gate_s = jnp.broadcast_to(nsa_b_gate[0].reshape(1, NSA_KV, NSA_GROUP, 3), (n_new, NSA_KV, NSA_GROUP, 3))
    b_gate_s = b_gate_s.transpose(3, 1, 0, 2).reshape(3, cols)
    o_s = _nsa_sample(q_rows, cmp_means, page_rows(cache_slc_kv), page_table,
                      pad8(slc_kv_s.reshape(seqs, n_new, NSA_KV_W)), win_buf, pad8(kw_new), gates_s, b_gate_s,
                      past, n_new)
    o_s = (o_s.reshape(seqs, NSA_KV, n_new, NSA_GROUP, NSA_HD).transpose(0, 2, 1, 3, 4)
           .reshape(rows_s, NSA_Q_W).astype(BF16))

    xp, xs = mixer_out(o_p, o_s, xp, xs, 1, mp, ms, "nsa_out")
    xp, xs = ffn(xp, xs, 1, mp, ms)

    return (xp.reshape(batch, seq, D_MODEL), xs.reshape(seqs, n_new, D_MODEL),
            gla_state_p[:, None], gla_state_s[:, None], cmp_kv_p, cmp_kv_s, slc_kv_p, slc_kv_s,
            win_kv_p, win_kv_s)
```

```python
import functools

import jax
import jax.numpy as jnp
from jax import lax
from jax.experimental import pallas as pl
from jax.experimental.pallas import tpu as pltpu

F32 = jnp.float32
BF16 = jnp.bfloat16

D_MODEL = 2048
DEPTH = 2
DN_ALPHA = (2 * DEPTH) ** 0.25
LN_EPS = 1e-5
D_FF = 5632
GLA_HEADS = 4
GLA_DK = 256
GLA_DV = 512
GLA_RANK = 16
GLA_TAU = 16.0
GLA_SUB = 32
GLA_MAIN_W = 2 * GLA_HEADS * GLA_DK + 2 * GLA_HEADS * GLA_DV
NSA_HEADS = 16
NSA_KV = 4
NSA_HD = 128
NSA_GROUP = NSA_HEADS // NSA_KV
NSA_BLOCK = 64
NSA_TOPK = 16
NSA_WINDOW = 512
NSA_Q_W = NSA_HEADS * NSA_HD
NSA_KV_W = 2 * NSA_KV * NSA_HD
NSA_MAIN_W = NSA_Q_W + 3 * NSA_KV_W
PAGE = 128
KV_ROWS = 2 * NSA_KV

LANES = 128
BF16_ROWS = 16
LOG2_E = 1.4426950408889634
V7X_VMEM_BYTES = 64 * 1024 * 1024
VMEM_LIMIT = V7X_VMEM_BYTES * 7 // 8

NEG_BIG = -1e30
TINY = float(jnp.finfo(jnp.float32).tiny)


def _params(*sem):
    return pltpu.CompilerParams(dimension_semantics=sem, vmem_limit_bytes=VMEM_LIMIT)


def _dot(a, b):
    return jnp.dot(a.astype(BF16), b.astype(BF16), preferred_element_type=F32)


def _dot_nt(a, b):
    return lax.dot_general(a.astype(BF16), b.astype(BF16), (((1,), (1,)), ((), ())),
                           preferred_element_type=F32)


def _dot_tn(a, b):
    return lax.dot_general(a.astype(BF16), b.astype(BF16), (((0,), (0,)), ((), ())),
                           preferred_element_type=F32)


def _split3(x):
    hi = x.astype(BF16)
    r1 = x - hi.astype(F32)
    mid = r1.astype(BF16)
    lo = (r1 - mid.astype(F32)).astype(BF16)
    return hi, mid, lo


def _silu(x):
    return x * jax.nn.sigmoid(x)


def _ind(cond, dtype=F32):
    return jnp.where(cond, 1.0, 0.0).astype(dtype)


def _masked_softmax2(s2, mask, axis):
    s2 = jnp.where(mask, s2, -jnp.inf)
    m = jnp.max(s2, axis=axis, keepdims=True)
    m = jnp.where(jnp.isfinite(m), m, 0.0)
    e = jnp.where(mask, jnp.exp2(s2 - m), 0.0)
    return e / jnp.maximum(jnp.sum(e, axis=axis, keepdims=True), TINY)


def _adaln_kernel(c_ref, w_ref, b_ref, o_ref):
    o_ref[...] = _dot(_silu(c_ref[...]), w_ref[...]) + b_ref[...]


def _adaln(c_all, ada_w, ada_b):
    rows = c_all.shape[0]
    n = ada_w.shape[2]
    tn = 1024
    return pl.pallas_call(
        _adaln_kernel,
        grid=(DEPTH, n // tn),
        in_specs=[
            pl.BlockSpec((rows, D_MODEL), lambda l, j: (0, 0)),
            pl.BlockSpec((None, D_MODEL, tn), lambda l, j: (l, 0, j)),
            pl.BlockSpec((None, 1, tn), lambda l, j: (l, 0, j)),
        ],
        out_specs=pl.BlockSpec((None, rows, tn), lambda l, j: (l, 0, j)),
        out_shape=jax.ShapeDtypeStruct((DEPTH, rows, n), F32),
        compiler_params=_params("parallel", "parallel"),
        name="adaln",
    )(c_all, ada_w, ada_b.reshape(DEPTH, 1, n))


def _modulate(x_ref, sc_ref, sh_ref):
    return (x_ref[...] * (1.0 + sc_ref[...]) + sh_ref[...]).astype(BF16)


def _proj_kernel(*refs, n_i, modulated):
    if modulated:
        h_ref, hs_ref, w_ref, wt_ref, o_ref, ot_ref, os_ref, ots_ref = refs
    else:
        (x_ref, sc_ref, sh_ref, xs_ref, scs_ref, shs_ref, w_ref, wt_ref,
         o_ref, ot_ref, os_ref, ots_ref, h_ref, hs_ref) = refs
    i, j = pl.program_id(0), pl.program_id(1)

    @pl.when(j == 0)
    def _():
        if not modulated:
            h_ref[...] = _modulate(x_ref, sc_ref, sh_ref)
        ot_ref[...] = _dot_nt(h_ref[...], wt_ref[...])

    o_ref[...] = _dot_nt(h_ref[...], w_ref[...])

    @pl.when(i == n_i - 1)
    def _():
        @pl.when(j == 0)
        def _():
            if not modulated:
                hs_ref[...] = _modulate(xs_ref, scs_ref, shs_ref)
            ots_ref[...] = _dot_nt(hs_ref[...], wt_ref[...])

        os_ref[...] = _dot_nt(hs_ref[...], w_ref[...])


def _proj(prompt, sample, w_t, layer, n_main, tm, name):
    modulated = not isinstance(prompt, tuple)
    m = (prompt if modulated else prompt[0]).shape[0]
    ms = (sample if modulated else sample[0]).shape[0]
    n_tail = w_t.shape[1] - n_main
    w_tail = jnp.pad(w_t[layer, n_main:, :], ((0, LANES - n_tail), (0, 0)))
    tn = 512
    n_i = m // tm
    const2 = lambda i, j: (0, 0)
    const3 = lambda i, j: (0, 0, 0)
    last_tile_col = lambda i, j: (0, jnp.where(i == n_i - 1, j, 0))
    if modulated:
        row_specs = [pl.BlockSpec((tm, D_MODEL), lambda i, j: (i, 0)), pl.BlockSpec((ms, D_MODEL), const2)]
        rows, scratch = (prompt, sample), []
    else:
        tiles_per_group = n_i // prompt[1].shape[0]
        mod_spec = pl.BlockSpec((None, 1, D_MODEL), lambda i, j: (i // tiles_per_group, 0, 0))
        row_specs = [pl.BlockSpec((tm, D_MODEL), lambda i, j: (i, 0), pipeline_mode=pl.Buffered(1)),
                     mod_spec, mod_spec, pl.BlockSpec((ms, D_MODEL), const2),
                     pl.BlockSpec((None, ms, D_MODEL), const3), pl.BlockSpec((None, ms, D_MODEL), const3)]
        rows = (*prompt, *sample)
        scratch = [pltpu.VMEM((tm, D_MODEL), BF16), pltpu.VMEM((ms, D_MODEL), BF16)]
    return pl.pallas_call(
        functools.partial(_proj_kernel, n_i=n_i, modulated=modulated),
        grid=(n_i, n_main // tn),
        in_specs=row_specs + [
            pl.BlockSpec((None, tn, D_MODEL), lambda i, j: (layer, j, 0)),
            pl.BlockSpec((LANES, D_MODEL), const2),
        ],
        out_specs=[
            pl.BlockSpec((tm, tn), lambda i, j: (i, j)),
            pl.BlockSpec((tm, LANES), lambda i, j: (i, 0)),
            pl.BlockSpec((ms, tn), last_tile_col),
            pl.BlockSpec((ms, LANES), const2),
        ],
        out_shape=[jax.ShapeDtypeStruct((m, n_main), F32), jax.ShapeDtypeStruct((m, LANES), F32),
                   jax.ShapeDtypeStruct((ms, n_main), F32), jax.ShapeDtypeStruct((ms, LANES), F32)],
        scratch_shapes=scratch,
        compiler_params=_params("arbitrary", "arbitrary"),
        name=name,
    )(*rows, w_t, w_tail)


def _out_ln_kernel(*refs, n_i, with_next):
    if with_next:
        (a_ref, w_ref, x_ref, gt_ref, sc_ref, sh_ref, as_ref, xs_ref, gts_ref, scs_ref, shs_ref, g_ref, b_ref,
         o_ref, os_ref, h_ref, hs_ref) = refs
    else:
        a_ref, w_ref, x_ref, gt_ref, as_ref, xs_ref, gts_ref, g_ref, b_ref, o_ref, os_ref = refs

    def deep_norm(a, x, gate):
        y = DN_ALPHA * x + gate * jnp.dot(a, w_ref[...], preferred_element_type=F32)
        mu = jnp.mean(y, axis=-1, keepdims=True)
        yc = y - mu
        var = jnp.mean(yc * yc, axis=-1, keepdims=True)
        return yc * lax.rsqrt(var + LN_EPS) * g_ref[...] + b_ref[...]

    o_ref[...] = deep_norm(a_ref[...], x_ref[...], gt_ref[...])
    if with_next:
        h_ref[...] = _modulate(o_ref, sc_ref, sh_ref)

    @pl.when(pl.program_id(0) == n_i - 1)
    def _():
        os_ref[...] = deep_norm(as_ref[...], xs_ref[...], gts_ref[...])
        if with_next:
            hs_ref[...] = _modulate(os_ref, scs_ref, shs_ref)


def _out_ln(prompt, sample, w, w_layer, ln_g, ln_b, ln_layer, tm, name):
    with_next = len(prompt) == 5
    a, x = prompt[:2]
    m, kdim = a.shape
    ms = sample[0].shape[0]
    n_i = m // tm
    tiles_per_group = n_i // prompt[2].shape[0]
    rows_p = lambda width: pl.BlockSpec((tm, width), lambda i: (i, 0))
    rows_s = lambda width: pl.BlockSpec((ms, width), lambda i: (0, 0))
    mod_p = pl.BlockSpec((None, 1, D_MODEL), lambda i: (i // tiles_per_group, 0, 0))
    mod_s = pl.BlockSpec((None, ms, D_MODEL), lambda i: (0, 0, 0))
    vec_spec = pl.BlockSpec((None, 1, D_MODEL), lambda i: (ln_layer, 0, 0))
    n_mod = len(prompt) - 2
    out_specs = [rows_p(D_MODEL), rows_s(D_MODEL)]
    out_shape = [jax.ShapeDtypeStruct((m, D_MODEL), F32), jax.ShapeDtypeStruct((ms, D_MODEL), F32)]
    if with_next:
        out_specs += [rows_p(D_MODEL), rows_s(D_MODEL)]
        out_shape += [jax.ShapeDtypeStruct((m, D_MODEL), BF16), jax.ShapeDtypeStruct((ms, D_MODEL), BF16)]
    return pl.pallas_call(
        functools.partial(_out_ln_kernel, n_i=n_i, with_next=with_next),
        grid=(n_i,),
        in_specs=[rows_p(kdim),
                  pl.BlockSpec((None, kdim, D_MODEL), lambda i: (w_layer, 0, 0), pipeline_mode=pl.Buffered(1)),
                  rows_p(D_MODEL)] + [mod_p] * n_mod
        + [rows_s(kdim), rows_s(D_MODEL)] + [mod_s] * n_mod + [vec_spec, vec_spec],
        out_specs=out_specs,
        out_shape=out_shape,
        compiler_params=_params("arbitrary"),
        name=name,
    )(a, w, x, *prompt[2:], *sample, ln_g.reshape(-1, 1, D_MODEL), ln_b.reshape(-1, 1, D_MODEL))


def _ffn_in_kernel(h_ref, hs_ref, wa_ref, wu_ref, o_ref, os_ref, *, n_i):
    wa = wa_ref[...].astype(BF16)
    wu = wu_ref[...].astype(BF16)

    def swiglu_in(h):
        return (_silu(_dot(h, wa)) * _dot(h, wu)).astype(BF16)

    o_ref[...] = swiglu_in(h_ref[...])

    @pl.when(pl.program_id(0) == n_i - 1)
    def _():
        os_ref[...] = swiglu_in(hs_ref[...])


def _ffn_in(h, hs, w, layer, tm, name):
    m, ms = h.shape[0], hs.shape[0]
    tn = 512
    nj = D_FF // tn
    n_i = m // tm
    return pl.pallas_call(
        functools.partial(_ffn_in_kernel, n_i=n_i),
        grid=(n_i, nj),
        in_specs=[
            pl.BlockSpec((tm, D_MODEL), lambda i, j: (i, 0)),
            pl.BlockSpec((ms, D_MODEL), lambda i, j: (0, 0)),
            pl.BlockSpec((None, D_MODEL, tn), lambda i, j: (layer, 0, j)),
            pl.BlockSpec((None, D_MODEL, tn), lambda i, j: (layer, 0, j + nj)),
        ],
        out_specs=[
            pl.BlockSpec((tm, tn), lambda i, j: (i, j)),
            pl.BlockSpec((ms, tn), lambda i, j: (0, jnp.where(i == n_i - 1, j, 0))),
        ],
        out_shape=[jax.ShapeDtypeStruct((m, D_FF), BF16), jax.ShapeDtypeStruct((ms, D_FF), BF16)],
        compiler_params=_params("arbitrary", "arbitrary"),
        name=name,
    )(h, hs, w, w)


def _gla_kernel(*refs, chunk, t_valid, has_s0, n_cast):
    n_in = 8 + has_s0
    q_ref, k_ref, v_ref, r_ref, a_ref, wa_ref, ba_ref, g_ref = refs[:8]
    s0_ref = refs[8] if has_s0 else None
    o_ref, st_ref = refs[n_in + n_cast:n_in + n_cast + 2]
    for src, dst in zip(refs[n_in:n_in + n_cast], refs[n_in + n_cast + 2:]):
        dst[...] = src[...].astype(BF16)
    sub = GLA_SUB
    n_sub = chunk // sub
    anchor = sub // 2 - 1

    @pl.when(pl.program_id(1) == 0)
    def _():
        if has_s0:
            st_ref[...] = s0_ref[...]
        else:
            st_ref[...] = jnp.zeros(st_ref.shape, F32)

    row = lax.broadcasted_iota(jnp.int32, (chunk, 1), 0)
    tri = _ind(lax.broadcasted_iota(jnp.int32, (chunk, chunk), 1)
               <= lax.broadcasted_iota(jnp.int32, (chunk, chunk), 0), BF16)
    ones = jnp.ones((chunk, LANES), BF16)
    causal = (lax.broadcasted_iota(jnp.int32, (sub, sub), 1)
              <= lax.broadcasted_iota(jnp.int32, (sub, sub), 0))
    heads = range(GLA_HEADS)
    dk = [slice(h * GLA_DK, (h + 1) * GLA_DK) for h in heads]
    dv = [slice(h * GLA_DV, (h + 1) * GLA_DV) for h in heads]
    subs = [slice(s * sub, (s + 1) * sub) for s in range(n_sub)]

    z = _dot(a_ref[...], wa_ref[...]) + ba_ref[...]
    lb = (jnp.minimum(z, 0.0) - jnp.log1p(jnp.exp(-jnp.abs(z)))) * (1.0 / GLA_TAU)
    k = k_ref[...]
    if t_valid < chunk:
        lb = jnp.where(row < t_valid, lb, 0.0)
        k = jnp.where(row < t_valid, k, 0.0)
    q = q_ref[...] * (GLA_DK ** -0.5)
    vb = v_ref[...].astype(BF16)
    lb3 = _split3(lb)
    b = sum(jnp.dot(tri, t, preferred_element_type=F32) for t in lb3)
    dec = jnp.exp(sum(lax.dot_general(t, ones, (((0,), (0,)), ((), ())), preferred_element_type=F32)
                      for t in lb3))
    b_last = b[chunk - 1:chunk, :]
    q_in = (q * jnp.exp(b)).astype(BF16)
    k_out = (k * jnp.exp(b_last - b)).astype(BF16)
    q_diag, k_diag, q_off, k_off = [], [], [], []
    for s, rs in enumerate(subs):
        b_mid = b[s * sub + anchor:s * sub + anchor + 1, :]
        q_diag.append((q[rs] * jnp.exp(b[rs] - b_mid)).astype(BF16))
        k_diag.append((k[rs] * jnp.exp(b_mid - b[rs])).astype(BF16))
        if s > 0:
            prev = slice(0, s * sub)
            b_in = b[s * sub - 1:s * sub, :]
            q_off.append((q[rs] * jnp.exp(b[rs] - b_in)).astype(BF16))
            k_off.append((k[prev] * jnp.exp(b_in - b[prev])).astype(BF16))

    s_old = [st_ref[0, h] for h in heads]
    o_inter = [_dot(q_in[:, dk[h]], s_old[h]) for h in heads]
    a_diag = [[_dot_nt(q_diag[s][:, dk[h]], k_diag[s][:, dk[h]]) for h in heads] for s in range(n_sub)]
    a_off = [[_dot_nt(q_off[s][:, dk[h]], k_off[s][:, dk[h]]) for h in heads] for s in range(n_sub - 1)]
    a_diag = [[jnp.where(causal, a, 0.0).astype(BF16) for a in row_] for row_ in a_diag]
    a_off = [[a.astype(BF16) for a in row_] for row_ in a_off]
    o_intra = []
    for s, rs in enumerate(subs):
        o_s = [_dot(a_diag[s][h], vb[rs, dv[h]]) for h in heads]
        if s > 0:
            o_s = [o_s[h] + _dot(a_off[s - 1][h], vb[0:s * sub, dv[h]]) for h in heads]
        o_intra.append(o_s)
    kv = [_dot_tn(k_out[:, dk[h]], vb[:, dv[h]]) for h in heads]
    for h in heads:
        st_ref[0, h] = s_old[h] * jnp.tile(dec[dk[h], :], (1, GLA_DV // LANES)) + kv[h]
    for h in heads:
        o = o_inter[h] + jnp.concatenate([o_intra[s][h] for s in range(n_sub)], axis=0)
        o = o * lax.rsqrt(jnp.mean(o * o, axis=-1, keepdims=True) + LN_EPS)
        o_ref[:, dv[h]] = (o * g_ref[:, dv[h]] * _silu(r_ref[:, dv[h]])).astype(BF16)


def _gla(main, tail, w_alpha, b_alpha, norm_g, s0, batch, seq, chunk, t_valid, to_bf16=()):
    n_chunks = seq // chunk
    qk_w = GLA_HEADS * GLA_DK
    v_w = GLA_HEADS * GLA_DV
    wa = jnp.pad(w_alpha, ((0, LANES - GLA_RANK), (0, 0)))
    row_map = lambda b, c: (b * n_chunks + c, 0)
    const = lambda b, c: (0, 0)
    in_specs = [
        pl.BlockSpec((chunk, qk_w), row_map),
        pl.BlockSpec((chunk, qk_w), lambda b, c: (b * n_chunks + c, 1)),
        pl.BlockSpec((chunk, v_w), lambda b, c: (b * n_chunks + c, 1)),
        pl.BlockSpec((chunk, v_w), lambda b, c: (b * n_chunks + c, 2)),
        pl.BlockSpec((chunk, LANES), row_map),
        pl.BlockSpec((LANES, qk_w), const),
        pl.BlockSpec((1, qk_w), const),
        pl.BlockSpec((1, v_w), const),
    ]
    args = [main, main, main, main, tail, wa, b_alpha.reshape(1, qk_w), norm_g.reshape(1, v_w)]
    state_spec = pl.BlockSpec((1, GLA_HEADS, GLA_DK, GLA_DV), lambda b, c: (b, 0, 0, 0))
    if s0 is not None:
        in_specs.append(state_spec)
        args.append(s0)
    n_steps = batch * n_chunks
    cast_specs = [pl.BlockSpec((a.shape[0] // n_steps, a.shape[1]), row_map) for a in to_bf16]
    assert all(a.shape[0] % (n_steps * BF16_ROWS) == 0 for a in to_bf16)
    return pl.pallas_call(
        functools.partial(_gla_kernel, chunk=chunk, t_valid=t_valid, has_s0=s0 is not None,
                          n_cast=len(to_bf16)),
        grid=(batch, n_chunks),
        in_specs=in_specs + cast_specs,
        out_specs=[pl.BlockSpec((chunk, v_w), row_map), state_spec] + cast_specs,
        out_shape=[jax.ShapeDtypeStruct((batch * seq, v_w), BF16),
                   jax.ShapeDtypeStruct((batch, GLA_HEADS, GLA_DK, GLA_DV), F32)]
        + [jax.ShapeDtypeStruct(a.shape, BF16) for a in to_bf16],
        compiler_params=_params("parallel", "arbitrary"),
        name="gla_prompt" if s0 is None else "gla_sample",
    )(*args, *to_bf16)


def _select_blocks(imp_t, blk, cur):
    n = imp_t.shape[0]
    forced = (blk == 0) | ((blk >= cur - 1) & (blk <= cur))
    score = jnp.where(blk > cur, -jnp.inf, jnp.where(forced, jnp.inf, imp_t))
    rank = jnp.zeros(score.shape, F32)
    for i in range(n):
        si = score[i:i + 1, :]
        rank = rank + jnp.where(blk > i, _ind(si >= score), _ind(si > score))
    return (rank < NSA_TOPK) & (score > -jnp.inf)


def _nsa_prompt_kernel(pt_ref, q_ref, kc_ref, vc_ref, ks_ref, vs_ref, kw_ref, vw_ref, gt_ref, bg_ref, *refs,
                       seq, tq, tk, pages_per_step):
    page_refs = refs[:pages_per_step]
    (o_ref, means_ref, kcm, vcm, ksb, vst, kwb, vwt, qs_ref, sel_ref,
     ms_ref, accs_ref, mw_ref, accw_ref) = refs[pages_per_step:]
    qi = pl.program_id(2)
    n_blk = seq // NSA_BLOCK
    n_kt = seq // tk
    blk_per_kt = tk // NSA_BLOCK

    @pl.when(qi == 0)
    def _():
        kcm[...] = jnp.sum(kc_ref[...].reshape(n_blk, NSA_BLOCK, NSA_HD), axis=1) * (1.0 / NSA_BLOCK)
        vcm[...] = jnp.sum(vc_ref[...].reshape(n_blk, NSA_BLOCK, NSA_HD), axis=1) * (1.0 / NSA_BLOCK)
        ksb[...] = ks_ref[...].astype(BF16)
        kwb[...] = kw_ref[...].astype(BF16)
        ones_rows = _ind(lax.broadcasted_iota(jnp.int32, (BF16_ROWS, tk), 0) == 0, BF16)
        for kt in range(n_kt):
            rows = slice(kt * tk, (kt + 1) * tk)
            vst[kt] = jnp.concatenate([vs_ref[rows, :].T.astype(BF16), ones_rows], axis=0)
            vwt[kt] = jnp.concatenate([vw_ref[rows, :].T.astype(BF16), ones_rows], axis=0)

    t0 = qi * tq
    q = q_ref[...] * (NSA_HD ** -0.5 * LOG2_E)
    qs_ref[...] = jnp.concatenate(
        [q[:, g * NSA_HD:(g + 1) * NSA_HD] for g in range(NSA_GROUP)], axis=0).astype(BF16)
    qs = qs_ref[...]
    t_row = t0 + lax.broadcasted_iota(jnp.int32, (1, tq), 1)
    t_row_g = jnp.concatenate([t_row] * NSA_GROUP, axis=1)

    kc_b = kcm[...].astype(BF16)
    blk_col = lax.broadcasted_iota(jnp.int32, (n_blk, 1), 0)
    p_t = _masked_softmax2(_dot_nt(kc_b, qs), (blk_col + 1) * NSA_BLOCK - 1 <= t_row_g, axis=0)
    o_c = _dot_tn(vcm[...], p_t)
    imp_t = sum(p_t[:, g * tq:(g + 1) * tq] for g in range(NSA_GROUP))
    sel_ref[...] = jnp.where(_select_blocks(imp_t, blk_col, t_row // NSA_BLOCK), 0.0, NEG_BIG)

    n_loc = lax.broadcasted_iota(jnp.int32, (tk, tq), 0)
    t_loc = lax.broadcasted_iota(jnp.int32, (tk, tq), 1)
    on_lanes = lambda a: jnp.concatenate([a] * NSA_GROUP, axis=1)
    causal = on_lanes(jnp.where(n_loc <= t_loc, 0.0, NEG_BIG))
    far = on_lanes(jnp.where(n_loc > t_loc, 0.0, NEG_BIG))

    def sel_bias(kt):
        return on_lanes(jnp.concatenate(
            [jnp.broadcast_to(sel_ref[pl.ds(kt * blk_per_kt + j, 1), :], (NSA_BLOCK, tq))
             for j in range(blk_per_kt)], axis=0))

    sel_state, win_state = (ms_ref, accs_ref), (mw_ref, accw_ref)
    for m_ref, acc_ref in (sel_state, win_state):
        m_ref[...] = jnp.full(m_ref.shape, NEG_BIG, F32)
        acc_ref[...] = jnp.zeros(acc_ref.shape, F32)

    def steps(*work):
        s = [_dot_nt(k_ref[pl.ds(pl.multiple_of(kt * tk, tk), tk), :], qs_ref[...]) + bias
             for _, k_ref, _, kt, bias in work]
        m_old = [state[0][...] for state, *_ in work]
        m_new = [jnp.maximum(mo, jnp.max(si, axis=0, keepdims=True)) for mo, si in zip(m_old, s)]
        p = [jnp.exp2(si - mn).astype(BF16) for si, mn in zip(s, m_new)]
        pv = [_dot(vt_ref[kt], pi) for (_, _, vt_ref, kt, _), pi in zip(work, p)]
        for (state, *_), mo, mn, pvi in zip(work, m_old, m_new, pv):
            state[1][...] = jnp.exp2(mo - mn) * state[1][...] + pvi
            state[0][...] = mn

    def result(acc_ref):
        acc = acc_ref[...]
        return acc[:NSA_HD] / jnp.maximum(acc[NSA_HD:NSA_HD + 1], TINY)

    steps((sel_state, ksb, vst, qi, sel_bias(qi) + causal), (win_state, kwb, vwt, qi, causal))

    @pl.when(qi >= 1)
    def _():
        steps((sel_state, ksb, vst, qi - 1, sel_bias(qi - 1)), (win_state, kwb, vwt, qi - 1, far))

    def sel_body(i, carry):
        kt = qi - 2 - i
        steps((sel_state, ksb, vst, kt, sel_bias(kt)))
        return carry

    lax.fori_loop(0, qi - 1, sel_body, 0)

    gates = jax.nn.sigmoid(gt_ref[...] + bg_ref[...])
    g_c, g_s, g_w = (jnp.concatenate([gates[3 * g + br:3 * g + br + 1, :] for g in range(NSA_GROUP)], axis=1)
                     for br in range(3))
    o_t = g_c * o_c + g_s * result(accs_ref) + g_w * result(accw_ref)
    for g in range(NSA_GROUP):
        o_ref[:, g * NSA_HD:(g + 1) * NSA_HD] = o_t[:, g * tq:(g + 1) * tq].T.astype(BF16)

    for i, x_ref in enumerate(page_refs):
        x = x_ref[0].reshape(PAGE // NSA_BLOCK, NSA_BLOCK, KV_ROWS, NSA_HD)
        means_ref[0, i] = jnp.sum(x, axis=1) * (1.0 / NSA_BLOCK)


def _nsa_prompt(main, gates, b_gate, batch, seq, cache_cmp, page_table):
    tq = tk = NSA_WINDOW
    assert seq % tk == 0
    vt_rows = NSA_HD + BF16_ROWS
    nq = seq // tq
    q_lanes = NSA_GROUP * NSA_HD
    col0 = NSA_Q_W // NSA_HD
    seqs, n_pages = page_table.shape
    n_steps = batch * NSA_KV * nq
    pages_per_step = seqs * n_pages // n_steps
    groups = n_pages // pages_per_step
    assert pages_per_step * n_steps == seqs * n_pages and groups * pages_per_step == n_pages
    per_page = PAGE // NSA_BLOCK

    def kv_spec(branch, part):
        off = col0 + (2 * branch + part) * NSA_KV
        return pl.BlockSpec((seq, NSA_HD), lambda b, kv, qi, pt: (b, off + kv))

    def step_of(b, kv, qi):
        return (b * NSA_KV + kv) * nq + qi

    def page_spec(i):
        def index(b, kv, qi, pt):
            s = step_of(b, kv, qi)
            return (pt[s // groups, (s % groups) * pages_per_step + i], 0, 0)
        return pl.BlockSpec((1, PAGE * KV_ROWS, NSA_HD), index)

    def means_index(b, kv, qi, pt):
        s = step_of(b, kv, qi)
        return (s // groups, s % groups, 0, 0, 0)

    g_w = 3 * NSA_GROUP
    g_rows = NSA_GROUP * tq
    grid_spec = pltpu.PrefetchScalarGridSpec(
        num_scalar_prefetch=1,
        grid=(batch, NSA_KV, nq),
        in_specs=[pl.BlockSpec((tq, q_lanes), lambda b, kv, qi, pt: (b * nq + qi, kv))]
        + [kv_spec(br, part) for br in range(3) for part in range(2)]
        + [pl.BlockSpec((None, g_w, tq), lambda b, kv, qi, pt: (kv, 0, b * nq + qi)),
           pl.BlockSpec((None, g_w, 1), lambda b, kv, qi, pt: (kv, 0, 0))]
        + [page_spec(i) for i in range(pages_per_step)],
        out_specs=[pl.BlockSpec((tq, q_lanes), lambda b, kv, qi, pt: (b * nq + qi, kv)),
                   pl.BlockSpec((1, pages_per_step, per_page, KV_ROWS, NSA_HD), means_index)],
        scratch_shapes=[
            pltpu.VMEM((seq // NSA_BLOCK, NSA_HD), F32),
            pltpu.VMEM((seq // NSA_BLOCK, NSA_HD), F32),
            pltpu.VMEM((seq, NSA_HD), BF16),
            pltpu.VMEM((seq // tk, vt_rows, tk), BF16),
            pltpu.VMEM((seq, NSA_HD), BF16),
            pltpu.VMEM((seq // tk, vt_rows, tk), BF16),
            pltpu.VMEM((g_rows, NSA_HD), BF16),
            pltpu.VMEM((seq // NSA_BLOCK, tq), F32),
            pltpu.VMEM((1, g_rows), F32),
            pltpu.VMEM((vt_rows, g_rows), F32),
            pltpu.VMEM((1, g_rows), F32),
            pltpu.VMEM((vt_rows, g_rows), F32),
        ],
    )
    return pl.pallas_call(
        functools.partial(_nsa_prompt_kernel, seq=seq, tq=tq, tk=tk, pages_per_step=pages_per_step),
        grid_spec=grid_spec,
        out_shape=[jax.ShapeDtypeStruct((batch * seq, NSA_Q_W), BF16),
                   jax.ShapeDtypeStruct((seqs, n_pages, per_page, KV_ROWS, NSA_HD), F32)],
        compiler_params=_params("arbitrary", "arbitrary", "arbitrary"),
        name="nsa_prompt",
    )(page_table, main, main, main, main, main, main, main, gates, b_gate, *([cache_cmp] * pages_per_step))


def _kv_rows_out_kernel(c_ref, s_ref, w_ref, oc_ref, os_ref, ow_ref, *, rows, n_c):
    def spread(src, dst):
        for r in range(KV_ROWS):
            dst[pl.ds(r, rows, stride=KV_ROWS), :] = src[:, r * NSA_HD:(r + 1) * NSA_HD]

    spread(c_ref, oc_ref)
    spread(s_ref, os_ref)

    @pl.when(pl.program_id(1) == n_c - 1)
    def _():
        spread(w_ref, ow_ref)


def _kv_rows_out(main, batch, seq, keep):
    rows = keep
    n_c = seq // rows
    col0 = NSA_Q_W // NSA_KV_W
    tile = lambda br: pl.BlockSpec((rows, NSA_KV_W), lambda b, c: (b * n_c + c, col0 + br))
    out_tile = pl.BlockSpec((rows * KV_ROWS, NSA_HD), lambda b, c: (b * n_c + c, 0))
    return pl.pallas_call(
        functools.partial(_kv_rows_out_kernel, rows=rows, n_c=n_c),
        grid=(batch, n_c),
        in_specs=[tile(0), tile(1),
                  pl.BlockSpec((rows, NSA_KV_W), lambda b, c: (b * n_c + n_c - 1, col0 + 2))],
        out_specs=[out_tile, out_tile, pl.BlockSpec((rows * KV_ROWS, NSA_HD), lambda b, c: (b, 0))],
        out_shape=[jax.ShapeDtypeStruct((batch * seq * KV_ROWS, NSA_HD), F32)] * 2
        + [jax.ShapeDtypeStruct((batch * keep * KV_ROWS, NSA_HD), F32)],
        compiler_params=_params("arbitrary", "arbitrary"),
        name="nsa_kv_rows_out",
    )(main, main, main)


def _page_specs(n):
    return [pl.BlockSpec((1, PAGE * KV_ROWS, NSA_HD), lambda b, p, pt, i=i: (pt[b, p * n + i], 0, 0))
            for i in range(n)]


def _nsa_sample_kernel(pt_ref, q_ref, cm_ref, *refs, past, n_new, n_steps, per_step):
    pg_refs = refs[:per_step]
    (kvn_ref, wb_ref, kwn_ref, gt_ref, bg_ref, o_ref,
     qr_ref, sel_ref, m_ref, l_ref, acc_ref, oc_ref, ow_ref) = refs[per_step:]
    p = pl.program_id(1)
    k_w = NSA_KV * NSA_HD
    cols = NSA_KV * n_new * NSA_GROUP
    per_kv = n_new * NSA_GROUP
    n_cmp = past // NSA_BLOCK
    n_sel = -(-(past + n_new) // NSA_BLOCK)
    n_sel_pad = sel_ref.shape[0]
    lane = lax.broadcasted_iota(jnp.int32, (1, cols), 1)
    q_pos = past + (lane // NSA_GROUP) % n_new

    def heads_on_lanes(ref, n, part):
        return jnp.concatenate(
            [ref[0, pl.ds(part * NSA_KV + kv, n, stride=KV_ROWS), :] for kv in range(NSA_KV)], axis=1)

    def scores(keys):
        return _dot_nt(keys, qr_ref[...])

    def own_head(x_t):
        out = jnp.zeros((NSA_HD, cols), F32)
        for kv in range(NSA_KV):
            mine = (lane // per_kv) == kv
            out = out + jnp.where(mine, x_t[kv * NSA_HD:(kv + 1) * NSA_HD, :], 0.0)
        return out

    def attend_once(keys, vals, mask):
        p_t = _masked_softmax2(scores(keys), mask, axis=0)
        return own_head(_dot_tn(vals, p_t))

    def online(keys, vals, bias):
        s = scores(keys) + bias
        m_old = m_ref[...]
        m_new = jnp.maximum(m_old, jnp.max(s, axis=0, keepdims=True))
        p_t = jnp.exp2(s - m_new)
        alpha = jnp.exp2(m_old - m_new)
        l_ref[...] = alpha * l_ref[...] + jnp.sum(p_t, axis=0, keepdims=True)
        acc_ref[...] = alpha * acc_ref[...] + own_head(_dot_tn(vals, p_t))
        m_ref[...] = m_new

    @pl.when(p == 0)
    def _():
        q = q_ref[0] * (NSA_HD ** -0.5 * LOG2_E)
        r_kv = lax.broadcasted_iota(jnp.int32, (cols, k_w), 0) // per_kv
        c_kv = lax.broadcasted_iota(jnp.int32, (cols, k_w), 1) // NSA_HD
        qr_ref[...] = jnp.where(r_kv == c_kv, jnp.tile(q, (1, NSA_KV)), 0.0).astype(BF16)
        cm = cm_ref[0]
        blk = lax.broadcasted_iota(jnp.int32, (n_cmp, 1), 0)
        p_t = _masked_softmax2(scores(cm[:, :k_w]), (blk + 1) * NSA_BLOCK - 1 <= q_pos, axis=0)
        oc_ref[...] = own_head(_dot_tn(cm[:, k_w:], p_t))
        same = _ind(lax.broadcasted_iota(jnp.int32, (cols, cols), 0) // NSA_GROUP
                    == lax.broadcasted_iota(jnp.int32, (cols, cols), 1) // NSA_GROUP, BF16)
        imp = sum(jnp.dot(t, same, preferred_element_type=F32) for t in _split3(p_t))
        imp = jnp.concatenate([imp, jnp.zeros((n_sel_pad - n_cmp, cols), F32)], axis=0)
        blk_s = lax.broadcasted_iota(jnp.int32, (n_sel_pad, 1), 0)
        cur = q_pos // NSA_BLOCK
        forced = (blk_s == 0) | ((blk_s >= cur - 1) & (blk_s <= cur))
        score = jnp.where((blk_s > cur) | (blk_s >= n_sel), -jnp.inf, jnp.where(forced, jnp.inf, imp))
        blk_f = blk_s.astype(F32)

        def pick(_, carry):
            left, bias = carry
            top = jnp.max(left, axis=0, keepdims=True)
            first = jnp.min(jnp.where(left == top, blk_f, float(n_sel_pad)), axis=0, keepdims=True)
            hit = (blk_f == first) & (top > -jnp.inf)
            return jnp.where(hit, -jnp.inf, left), jnp.where(hit, 0.0, bias)

        _, bias = lax.fori_loop(0, NSA_TOPK, pick, (score, jnp.full(score.shape, NEG_BIG, F32)))
        sel_ref[...] = bias
        n_wb = wb_ref.shape[1] // KV_ROWS
        kw_all = jnp.concatenate([heads_on_lanes(wb_ref, n_wb, 0), kwn_ref[0][:, :k_w]], axis=0)
        vw_all = jnp.concatenate([heads_on_lanes(wb_ref, n_wb, 1), kwn_ref[0][:, k_w:]], axis=0)
        n_w = kw_all.shape[0]
        w_row = lax.broadcasted_iota(jnp.int32, (n_w, 1), 0)
        w_pos = past - n_wb + w_row
        d = q_pos - w_pos
        ow_ref[...] = attend_once(kw_all, vw_all,
                                  (d >= 0) & (d < NSA_WINDOW) & (w_pos >= 0) & (w_row < n_wb + n_new))
        m_ref[...] = jnp.full(m_ref.shape, NEG_BIG, F32)
        l_ref[...] = jnp.zeros(l_ref.shape, F32)
        acc_ref[...] = jnp.zeros(acc_ref.shape, F32)

    def page_part(part):
        return jnp.concatenate([heads_on_lanes(pg_ref, PAGE, part) for pg_ref in pg_refs], axis=0)

    blk_per_step = per_step * PAGE // NSA_BLOCK
    online(page_part(0), page_part(1), jnp.concatenate(
        [jnp.broadcast_to(sel_ref[pl.ds(p * blk_per_step + j, 1), :], (NSA_BLOCK, cols))
         for j in range(blk_per_step)], axis=0))

    @pl.when(p == n_steps - 1)
    def _():
        kvn = kvn_ref[0]
        n_row = lax.broadcasted_iota(jnp.int32, (kvn.shape[0], 1), 0)
        bias_new = jnp.where((n_row < n_new) & (past + n_row <= q_pos), 0.0, NEG_BIG)
        online(kvn[:, :k_w], kvn[:, k_w:], bias_new + sel_ref[pl.ds(past // NSA_BLOCK, 1), :])
        o_s = acc_ref[...] / jnp.maximum(l_ref[...], TINY)
        gates = jax.nn.sigmoid(gt_ref[0] + bg_ref[...])
        o_t = gates[0:1, :] * oc_ref[...] + gates[1:2, :] * o_s + gates[2:3, :] * ow_ref[...]
        eye = _ind(lax.broadcasted_iota(jnp.int32, (NSA_HD, NSA_HD), 0)
                   == lax.broadcasted_iota(jnp.int32, (NSA_HD, NSA_HD), 1), BF16)
        o_ref[0] = sum(lax.dot_general(t, eye, (((0,), (0,)), ((), ())), preferred_element_type=F32)
                       for t in _split3(o_t))


def _nsa_sample(q_rows, cmp_means, cache_slc, page_table, kv_new, win_buf, kw_new, gates, b_gate,
                past, n_new):
    seqs, n_pages = page_table.shape
    cols = q_rows.shape[1]
    n_cmp = cmp_means.shape[1]
    n_sel_pad = -(-(-(-(past + n_new) // NSA_BLOCK)) // 8) * 8
    n_wb = win_buf.shape[1]
    pad_new = kv_new.shape[1]
    per_step = 8
    n_steps = n_pages // per_step
    seq_map = lambda b, p, pt: (b, 0, 0)
    return pl.pallas_call(
        functools.partial(_nsa_sample_kernel, past=past, n_new=n_new, n_steps=n_steps, per_step=per_step),
        grid_spec=pltpu.PrefetchScalarGridSpec(
            num_scalar_prefetch=1,
            grid=(seqs, n_steps),
            in_specs=[
                pl.BlockSpec((1, cols, NSA_HD), seq_map),
                pl.BlockSpec((1, n_cmp, NSA_KV_W), seq_map),
                *_page_specs(per_step),
                pl.BlockSpec((1, pad_new, NSA_KV_W), seq_map),
                pl.BlockSpec((1, n_wb, NSA_HD), seq_map),
                pl.BlockSpec((1, pad_new, NSA_KV_W), seq_map),
                pl.BlockSpec((1, 3, cols), seq_map),
                pl.BlockSpec((3, cols), lambda b, p, pt: (0, 0)),
            ],
            out_specs=pl.BlockSpec((1, cols, NSA_HD), seq_map),
            scratch_shapes=[
                pltpu.VMEM((cols, NSA_KV * NSA_HD), BF16),
                pltpu.VMEM((n_sel_pad, cols), F32),
                pltpu.VMEM((1, cols), F32),
                pltpu.VMEM((1, cols), F32),
                pltpu.VMEM((NSA_HD, cols), F32),
                pltpu.VMEM((NSA_HD, cols), F32),
                pltpu.VMEM((NSA_HD, cols), F32),
            ],
        ),
        out_shape=jax.ShapeDtypeStruct((seqs, cols, NSA_HD), F32),
        compiler_params=_params("parallel", "arbitrary"),
        name="nsa_sample",
    )(page_table, q_rows, cmp_means, *([cache_slc] * per_step), kv_new, win_buf, kw_new, gates, b_gate)


def kernel(x_prompt, x_sample, c_prompt, c_sample, state_gla, cache_cmp_kv, cache_slc_kv, cache_win_kv,
           page_table, ada_w, ada_b, gla_w_in, gla_w_alpha, gla_b_alpha, gla_norm_g, gla_w_out, nsa_w_in,
           nsa_b_gate, nsa_w_out, ln_mix_g, ln_mix_b, ffn_w_in, ffn_w_out, ln_ffn_g, ln_ffn_b):
    batch, seq, _ = x_prompt.shape
    seqs, n_new, _ = x_sample.shape
    n_pages = page_table.shape[1]
    past = n_pages * PAGE
    rows_s = seqs * n_new
    tm_p = seq

    pad_rows = 16 - (batch + seqs)
    c_all = jnp.concatenate([c_prompt, c_sample, jnp.zeros((pad_rows, D_MODEL), F32)], axis=0)
    mod = _adaln(c_all, ada_w, ada_b).reshape(DEPTH, 16, 6, D_MODEL)

    def mods(layer):
        mp = [mod[layer, :batch, i].reshape(batch, 1, D_MODEL) for i in range(6)]
        ms = [jnp.repeat(mod[layer, batch:batch + seqs, i], n_new, axis=0).reshape(1, rows_s, D_MODEL)
              for i in range(6)]
        return mp, ms

    def mixer_out(o_p, o_s, xp, xs, layer, mp, ms, tag):
        return _out_ln((o_p, xp, mp[2], mp[4], mp[3]), (o_s, xs, ms[2], ms[4], ms[3]), w_out_mix[layer % 2],
                       layer // 2, ln_mix_g, ln_mix_b, layer, 512, tag)

    def ffn(hp, hs, xp, xs, layer, mp, ms, next_mods):
        act_p, act_s = _ffn_in(hp, hs, ffn_w_in, layer, tm_p, "ffn_in_%d" % layer)
        nxt_p, nxt_s = ((), ()) if next_mods is None else ((next_mods[0][1], next_mods[0][0]),
                                                            (next_mods[1][1], next_mods[1][0]))
        return _out_ln((act_p, xp, mp[5], *nxt_p), (act_s, xs, ms[5], *nxt_s), w_out_ffn, layer,
                       ln_ffn_g, ln_ffn_b, layer, 256, "ffn_out_%d" % layer)

    xp = x_prompt.reshape(batch * seq, D_MODEL)
    xs = x_sample.reshape(rows_s, D_MODEL)

    mp, ms = mods(0)
    main_p, tail_p, main_s, tail_s = _proj((xp, mp[1], mp[0]), (xs, ms[1], ms[0]), jnp.swapaxes(gla_w_in, 1, 2), 0,
                                           GLA_MAIN_W, tm_p, "gla_in")
    rows2d = lambda w: w.reshape(-1, w.shape[-1])
    gated_p, gla_state_p, *w_bf16 = _gla(main_p, tail_p, gla_w_alpha[0], gla_b_alpha[0], gla_norm_g[0], None,
                                         batch, seq, 256, 256,
                                         to_bf16=(rows2d(gla_w_out), rows2d(nsa_w_out), rows2d(ffn_w_out)))
    w_out_mix = (w_bf16[0].reshape(gla_w_out.shape), w_bf16[1].reshape(nsa_w_out.shape))
    w_out_ffn = w_bf16[2].reshape(ffn_w_out.shape)
    chunk_s = GLA_SUB

    def pad_steps(a):
        a = a.reshape(seqs, n_new, a.shape[-1])
        return jnp.pad(a, ((0, 0), (0, chunk_s - n_new), (0, 0))).reshape(seqs * chunk_s, a.shape[-1])

    gated_s, gla_state_s = _gla(pad_steps(main_s), pad_steps(tail_s), gla_w_alpha[0], gla_b_alpha[0],
                                gla_norm_g[0], state_gla[:, 0], seqs, chunk_s, chunk_s, n_new)
    gated_s = gated_s.reshape(seqs, chunk_s, -1)[:, :n_new].reshape(rows_s, -1)
    xp, xs, hp, hs = mixer_out(gated_p, gated_s, xp, xs, 0, mp, ms, "gla_out")
    xp, xs, hp, hs = ffn(hp, hs, xp, xs, 0, mp, ms, mods(1))

    mp, ms = mods(1)
    main_p, tail_p, main_s, tail_s = _proj(hp, hs, jnp.swapaxes(nsa_w_in, 1, 2), 0, NSA_MAIN_W, tm_p, "nsa_in")
    n_gate = 3 * NSA_HEADS
    g_w = 3 * NSA_GROUP
    gates_p = tail_p[:, :n_gate].reshape(batch * seq, NSA_KV, g_w).transpose(1, 2, 0)
    b_gate_p = nsa_b_gate[0].reshape(NSA_KV, g_w, 1)
    n_phys = cache_cmp_kv.shape[0]
    page_rows = lambda cache: cache.reshape(n_phys, PAGE * KV_ROWS, NSA_HD)
    o_p, cmp_means = _nsa_prompt(main_p, gates_p, b_gate_p, batch, seq, page_rows(cache_cmp_kv), page_table)

    kv_rows = lambda m, br: m[:, NSA_Q_W + br * NSA_KV_W:NSA_Q_W + (br + 1) * NSA_KV_W]
    kv_shape = (2, NSA_KV, NSA_HD)
    keep_p = min(NSA_WINDOW, seq)
    cmp_rows, slc_rows, win_rows = _kv_rows_out(main_p, batch, seq, keep_p)
    cmp_kv_p = cmp_rows.reshape(batch, 1, seq, *kv_shape)
    slc_kv_p = slc_rows.reshape(batch, 1, seq, *kv_shape)
    win_kv_p = win_rows.reshape(batch, 1, keep_p, *kv_shape)

    cmp_kv_s = kv_rows(main_s, 0).reshape(seqs, 1, n_new, *kv_shape)
    slc_kv_s = kv_rows(main_s, 1).reshape(seqs, 1, n_new, *kv_shape)
    kw_new = kv_rows(main_s, 2).reshape(seqs, n_new, NSA_KV_W)
    n_wb = cache_win_kv.shape[2]
    win_buf = cache_win_kv[:, 0].reshape(seqs, n_wb * KV_ROWS, NSA_HD)
    all_w = jnp.concatenate([win_buf, kw_new.reshape(seqs, n_new * KV_ROWS, NSA_HD)], axis=1)
    keep_s = min(NSA_WINDOW, n_wb + n_new)
    win_kv_s = all_w[:, (n_wb + n_new - keep_s) * KV_ROWS:].reshape(seqs, 1, keep_s, *kv_shape)

    cmp_means = cmp_means.reshape(seqs, past // NSA_BLOCK, NSA_KV_W)
    pad_new = 8
    pad8 = lambda a: jnp.pad(a, ((0, 0), (0, pad_new - n_new), (0, 0)))
    cols = NSA_KV * n_new * NSA_GROUP
    q_rows = (main_s[:, :NSA_Q_W].reshape(seqs, n_new, NSA_KV, NSA_GROUP, NSA_HD)
              .transpose(0, 2, 1, 3, 4).reshape(seqs, cols, NSA_HD))
    gates_s = (tail_s[:, :n_gate].reshape(seqs, n_new, NSA_KV, NSA_GROUP, 3)
               .transpose(0, 4, 2, 1, 3).reshape(seqs, 3, cols))
    b_gate_s = jnp.broadcast_to(nsa_b_gate[0].reshape(1, NSA_KV, NSA_GROUP, 3), (n_new, NSA_KV, NSA_GROUP, 3))
    b_gate_s = b_gate_s.transpose(3, 1, 0, 2).reshape(3, cols)
    o_s = _nsa_sample(q_rows, cmp_means, page_rows(cache_slc_kv), page_table,
                      pad8(slc_kv_s.reshape(seqs, n_new, NSA_KV_W)), win_buf, pad8(kw_new), gates_s, b_gate_s,
                      past, n_new)
    o_s = (o_s.reshape(seqs, NSA_KV, n_new, NSA_GROUP, NSA_HD).transpose(0, 2, 1, 3, 4)
           .reshape(rows_s, NSA_Q_W).astype(BF16))

    xp, xs, hp, hs = mixer_out(o_p, o_s, xp, xs, 1, mp, ms, "nsa_out")
    xp, xs = ffn(hp, hs, xp, xs, 1, mp, ms, None)

    return (xp.reshape(batch, seq, D_MODEL), xs.reshape(seqs, n_new, D_MODEL),
            gla_state_p[:, None], gla_state_s[:, None], cmp_kv_p, cmp_kv_s, slc_kv_p, slc_kv_s,
            win_kv_p, win_kv_s)
```

```python
import functools

import jax
import jax.numpy as jnp
from jax import lax
from jax.experimental import pallas as pl
from jax.experimental.pallas import tpu as pltpu

F32 = jnp.float32
BF16 = jnp.bfloat16

D_MODEL = 2048
DEPTH = 2
DN_ALPHA = (2 * DEPTH) ** 0.25
LN_EPS = 1e-5
D_FF = 5632
GLA_HEADS = 4
GLA_DK = 256
GLA_DV = 512
GLA_RANK = 16
GLA_TAU = 16.0
GLA_SUB = 32
GLA_MAIN_W = 2 * GLA_HEADS * GLA_DK + 2 * GLA_HEADS * GLA_DV
NSA_HEADS = 16
NSA_KV = 4
NSA_HD = 128
NSA_GROUP = NSA_HEADS // NSA_KV
NSA_BLOCK = 64
NSA_TOPK = 16
NSA_WINDOW = 512
NSA_Q_W = NSA_HEADS * NSA_HD
NSA_KV_W = 2 * NSA_KV * NSA_HD
NSA_MAIN_W = NSA_Q_W + 3 * NSA_KV_W
PAGE = 128
KV_ROWS = 2 * NSA_KV

LANES = 128
BF16_ROWS = 16
LOG2_E = 1.4426950408889634
V7X_VMEM_BYTES = 64 * 1024 * 1024
VMEM_LIMIT = V7X_VMEM_BYTES * 7 // 8

NEG_BIG = -1e30
TINY = float(jnp.finfo(jnp.float32).tiny)


def _params(*sem):
    return pltpu.CompilerParams(dimension_semantics=sem, vmem_limit_bytes=VMEM_LIMIT)


def _dot(a, b):
    return jnp.dot(a.astype(BF16), b.astype(BF16), preferred_element_type=F32)


def _dot_nt(a, b):
    return lax.dot_general(a.astype(BF16), b.astype(BF16), (((1,), (1,)), ((), ())),
                           preferred_element_type=F32)


def _dot_tn(a, b):
    return lax.dot_general(a.astype(BF16), b.astype(BF16), (((0,), (0,)), ((), ())),
                           preferred_element_type=F32)


def _split3(x):
    hi = x.astype(BF16)
    r1 = x - hi.astype(F32)
    mid = r1.astype(BF16)
    lo = (r1 - mid.astype(F32)).astype(BF16)
    return hi, mid, lo


def _silu(x):
    return x * jax.nn.sigmoid(x)


def _ind(cond, dtype=F32):
    return jnp.where(cond, 1.0, 0.0).astype(dtype)


def _masked_softmax2(s2, mask, axis):
    s2 = jnp.where(mask, s2, -jnp.inf)
    m = jnp.max(s2, axis=axis, keepdims=True)
    m = jnp.where(jnp.isfinite(m), m, 0.0)
    e = jnp.where(mask, jnp.exp2(s2 - m), 0.0)
    return e / jnp.maximum(jnp.sum(e, axis=axis, keepdims=True), TINY)


def _adaln_kernel(c_ref, w_ref, b_ref, o_ref):
    o_ref[...] = _dot(_silu(c_ref[...]), w_ref[...]) + b_ref[...]


def _adaln(c_all, ada_w, ada_b):
    rows = c_all.shape[0]
    n = ada_w.shape[2]
    tn = 1024
    return pl.pallas_call(
        _adaln_kernel,
        grid=(DEPTH, n // tn),
        in_specs=[
            pl.BlockSpec((rows, D_MODEL), lambda l, j: (0, 0)),
            pl.BlockSpec((None, D_MODEL, tn), lambda l, j: (l, 0, j)),
            pl.BlockSpec((None, 1, tn), lambda l, j: (l, 0, j)),
        ],
        out_specs=pl.BlockSpec((None, rows, tn), lambda l, j: (l, 0, j)),
        out_shape=jax.ShapeDtypeStruct((DEPTH, rows, n), F32),
        compiler_params=_params("parallel", "parallel"),
        name="adaln",
    )(c_all, ada_w, ada_b.reshape(DEPTH, 1, n))


def _modulate(x_ref, sc_ref, sh_ref):
    return (x_ref[...] * (1.0 + sc_ref[...]) + sh_ref[...]).astype(BF16)


def _proj_kernel(*refs, n_i, modulated):
    if modulated:
        h_ref, hs_ref, w_ref, wt_ref, o_ref, ot_ref, os_ref, ots_ref = refs
    else:
        (x_ref, sc_ref, sh_ref, xs_ref, scs_ref, shs_ref, w_ref, wt_ref,
         o_ref, ot_ref, os_ref, ots_ref, h_ref, hs_ref) = refs
    i, j = pl.program_id(0), pl.program_id(1)

    @pl.when(j == 0)
    def _():
        if not modulated:
            h_ref[...] = _modulate(x_ref, sc_ref, sh_ref)
        ot_ref[...] = _dot_nt(h_ref[...], wt_ref[...])

    o_ref[...] = _dot_nt(h_ref[...], w_ref[...])

    @pl.when(i == n_i - 1)
    def _():
        @pl.when(j == 0)
        def _():
            if not modulated:
                hs_ref[...] = _modulate(xs_ref, scs_ref, shs_ref)
            ots_ref[...] = _dot_nt(hs_ref[...], wt_ref[...])

        os_ref[...] = _dot_nt(hs_ref[...], w_ref[...])


def _proj(prompt, sample, w_t, layer, n_main, tm, name):
    modulated = not isinstance(prompt, tuple)
    m = (prompt if modulated else prompt[0]).shape[0]
    ms = (sample if modulated else sample[0]).shape[0]
    n_tail = w_t.shape[1] - n_main
    w_tail = jnp.pad(w_t[layer, n_main:, :], ((0, LANES - n_tail), (0, 0)))
    tn = 512
    n_i = m // tm
    const2 = lambda i, j: (0, 0)
    const3 = lambda i, j: (0, 0, 0)
    last_tile_col = lambda i, j: (0, jnp.where(i == n_i - 1, j, 0))
    if modulated:
        row_specs = [pl.BlockSpec((tm, D_MODEL), lambda i, j: (i, 0)), pl.BlockSpec((ms, D_MODEL), const2)]
        rows, scratch = (prompt, sample), []
    else:
        tiles_per_group = n_i // prompt[1].shape[0]
        mod_spec = pl.BlockSpec((None, 1, D_MODEL), lambda i, j: (i // tiles_per_group, 0, 0))
        row_specs = [pl.BlockSpec((tm, D_MODEL), lambda i, j: (i, 0), pipeline_mode=pl.Buffered(1)),
                     mod_spec, mod_spec, pl.BlockSpec((ms, D_MODEL), const2),
                     pl.BlockSpec((None, ms, D_MODEL), const3), pl.BlockSpec((None, ms, D_MODEL), const3)]
        rows = (*prompt, *sample)
        scratch = [pltpu.VMEM((tm, D_MODEL), BF16), pltpu.VMEM((ms, D_MODEL), BF16)]
    return pl.pallas_call(
        functools.partial(_proj_kernel, n_i=n_i, modulated=modulated),
        grid=(n_i, n_main // tn),
        in_specs=row_specs + [
            pl.BlockSpec((None, tn, D_MODEL), lambda i, j: (layer, j, 0)),
            pl.BlockSpec((LANES, D_MODEL), const2),
        ],
        out_specs=[
            pl.BlockSpec((tm, tn), lambda i, j: (i, j)),
            pl.BlockSpec((tm, LANES), lambda i, j: (i, 0)),
            pl.BlockSpec((ms, tn), last_tile_col),
            pl.BlockSpec((ms, LANES), const2),
        ],
        out_shape=[jax.ShapeDtypeStruct((m, n_main), F32), jax.ShapeDtypeStruct((m, LANES), F32),
                   jax.ShapeDtypeStruct((ms, n_main), F32), jax.ShapeDtypeStruct((ms, LANES), F32)],
        scratch_shapes=scratch,
        compiler_params=_params("arbitrary", "arbitrary"),
        name=name,
    )(*rows, w_t, w_tail)


def _out_ln_kernel(*refs, n_i, with_next):
    if with_next:
        (a_ref, w_ref, x_ref, gt_ref, sc_ref, sh_ref, as_ref, xs_ref, gts_ref, scs_ref, shs_ref, g_ref, b_ref,
         o_ref, os_ref, h_ref, hs_ref) = refs
    else:
        a_ref, w_ref, x_ref, gt_ref, as_ref, xs_ref, gts_ref, g_ref, b_ref, o_ref, os_ref = refs

    def deep_norm(a, x, gate):
        y = DN_ALPHA * x + gate * jnp.dot(a, w_ref[...], preferred_element_type=F32)
        mu = jnp.mean(y, axis=-1, keepdims=True)
        yc = y - mu
        var = jnp.mean(yc * yc, axis=-1, keepdims=True)
        return yc * lax.rsqrt(var + LN_EPS) * g_ref[...] + b_ref[...]

    o_ref[...] = deep_norm(a_ref[...], x_ref[...], gt_ref[...])
    if with_next:
        h_ref[...] = _modulate(o_ref, sc_ref, sh_ref)

    @pl.when(pl.program_id(0) == n_i - 1)
    def _():
        os_ref[...] = deep_norm(as_ref[...], xs_ref[...], gts_ref[...])
        if with_next:
            hs_ref[...] = _modulate(os_ref, scs_ref, shs_ref)


def _out_ln(prompt, sample, w, w_layer, ln_g, ln_b, ln_layer, tm, name):
    with_next = len(prompt) == 5
    a, x = prompt[:2]
    m, kdim = a.shape
    ms = sample[0].shape[0]
    n_i = m // tm
    tiles_per_group = n_i // prompt[2].shape[0]
    rows_p = lambda width: pl.BlockSpec((tm, width), lambda i: (i, 0))
    rows_s = lambda width: pl.BlockSpec((ms, width), lambda i: (0, 0))
    mod_p = pl.BlockSpec((None, 1, D_MODEL), lambda i: (i // tiles_per_group, 0, 0))
    mod_s = pl.BlockSpec((None, ms, D_MODEL), lambda i: (0, 0, 0))
    vec_spec = pl.BlockSpec((None, 1, D_MODEL), lambda i: (ln_layer, 0, 0))
    n_mod = len(prompt) - 2
    out_specs = [rows_p(D_MODEL), rows_s(D_MODEL)]
    out_shape = [jax.ShapeDtypeStruct((m, D_MODEL), F32), jax.ShapeDtypeStruct((ms, D_MODEL), F32)]
    if with_next:
        out_specs += [rows_p(D_MODEL), rows_s(D_MODEL)]
        out_shape += [jax.ShapeDtypeStruct((m, D_MODEL), BF16), jax.ShapeDtypeStruct((ms, D_MODEL), BF16)]
    return pl.pallas_call(
        functools.partial(_out_ln_kernel, n_i=n_i, with_next=with_next),
        grid=(n_i,),
        in_specs=[rows_p(kdim),
                  pl.BlockSpec((None, kdim, D_MODEL), lambda i: (w_layer, 0, 0), pipeline_mode=pl.Buffered(1)),
                  rows_p(D_MODEL)] + [mod_p] * n_mod
        + [rows_s(kdim), rows_s(D_MODEL)] + [mod_s] * n_mod + [vec_spec, vec_spec],
        out_specs=out_specs,
        out_shape=out_shape,
        compiler_params=_params("arbitrary"),
        name=name,
    )(a, w, x, *prompt[2:], *sample, ln_g.reshape(-1, 1, D_MODEL), ln_b.reshape(-1, 1, D_MODEL))


def _ffn_in_kernel(h_ref, hs_ref, wa_ref, wu_ref, o_ref, os_ref, *, n_i):
    wa = wa_ref[...].astype(BF16)
    wu = wu_ref[...].astype(BF16)

    def swiglu_in(h):
        return (_silu(_dot(h, wa)) * _dot(h, wu)).astype(BF16)

    o_ref[...] = swiglu_in(h_ref[...])

    @pl.when(pl.program_id(0) == n_i - 1)
    def _():
        os_ref[...] = swiglu_in(hs_ref[...])


def _ffn_in(h, hs, w, layer, tm, name):
    m, ms = h.shape[0], hs.shape[0]
    tn = 512
    nj = D_FF // tn
    n_i = m // tm
    return pl.pallas_call(
        functools.partial(_ffn_in_kernel, n_i=n_i),
        grid=(n_i, nj),
        in_specs=[
            pl.BlockSpec((tm, D_MODEL), lambda i, j: (i, 0)),
            pl.BlockSpec((ms, D_MODEL), lambda i, j: (0, 0)),
            pl.BlockSpec((None, D_MODEL, tn), lambda i, j: (layer, 0, j)),
            pl.BlockSpec((None, D_MODEL, tn), lambda i, j: (layer, 0, j + nj)),
        ],
        out_specs=[
            pl.BlockSpec((tm, tn), lambda i, j: (i, j)),
            pl.BlockSpec((ms, tn), lambda i, j: (0, jnp.where(i == n_i - 1, j, 0))),
        ],
        out_shape=[jax.ShapeDtypeStruct((m, D_FF), BF16), jax.ShapeDtypeStruct((ms, D_FF), BF16)],
        compiler_params=_params("arbitrary", "arbitrary"),
        name=name,
    )(h, hs, w, w)


def _gla_kernel(*refs, chunk, t_valid, has_s0, n_cast):
    n_in = 8 + has_s0
    q_ref, k_ref, v_ref, r_ref, a_ref, wa_ref, ba_ref, g_ref = refs[:8]
    s0_ref = refs[8] if has_s0 else None
    o_ref, st_ref = refs[n_in + n_cast:n_in + n_cast + 2]
    for src, dst in zip(refs[n_in:n_in + n_cast], refs[n_in + n_cast + 2:]):
        dst[...] = src[...].astype(BF16)
    sub = GLA_SUB
    n_sub = chunk // sub
    anchor = sub // 2 - 1

    @pl.when(pl.program_id(1) == 0)
    def _():
        if has_s0:
            st_ref[...] = s0_ref[...]
        else:
            st_ref[...] = jnp.zeros(st_ref.shape, F32)

    row = lax.broadcasted_iota(jnp.int32, (chunk, 1), 0)
    tri = _ind(lax.broadcasted_iota(jnp.int32, (chunk, chunk), 1)
               <= lax.broadcasted_iota(jnp.int32, (chunk, chunk), 0), BF16)
    ones = jnp.ones((chunk, LANES), BF16)
    causal = (lax.broadcasted_iota(jnp.int32, (sub, sub), 1)
              <= lax.broadcasted_iota(jnp.int32, (sub, sub), 0))
    heads = range(GLA_HEADS)
    dk = [slice(h * GLA_DK, (h + 1) * GLA_DK) for h in heads]
    dv = [slice(h * GLA_DV, (h + 1) * GLA_DV) for h in heads]
    subs = [slice(s * sub, (s + 1) * sub) for s in range(n_sub)]

    z = _dot(a_ref[...], wa_ref[...]) + ba_ref[...]
    lb = (jnp.minimum(z, 0.0) - jnp.log1p(jnp.exp(-jnp.abs(z)))) * (1.0 / GLA_TAU)
    k = k_ref[...]
    if t_valid < chunk:
        lb = jnp.where(row < t_valid, lb, 0.0)
        k = jnp.where(row < t_valid, k, 0.0)
    q = q_ref[...] * (GLA_DK ** -0.5)
    vb = v_ref[...].astype(BF16)
    lb3 = _split3(lb)
    b = sum(jnp.dot(tri, t, preferred_element_type=F32) for t in lb3)
    dec = jnp.exp(sum(lax.dot_general(t, ones, (((0,), (0,)), ((), ())), preferred_element_type=F32)
                      for t in lb3))
    b_last = b[chunk - 1:chunk, :]
    q_in = (q * jnp.exp(b)).astype(BF16)
    k_out = (k * jnp.exp(b_last - b)).astype(BF16)
    q_diag, k_diag, q_off, k_off = [], [], [], []
    for s, rs in enumerate(subs):
        b_mid = b[s * sub + anchor:s * sub + anchor + 1, :]
        q_diag.append((q[rs] * jnp.exp(b[rs] - b_mid)).astype(BF16))
        k_diag.append((k[rs] * jnp.exp(b_mid - b[rs])).astype(BF16))
        if s > 0:
            prev = slice(0, s * sub)
            b_in = b[s * sub - 1:s * sub, :]
            q_off.append((q[rs] * jnp.exp(b[rs] - b_in)).astype(BF16))
            k_off.append((k[prev] * jnp.exp(b_in - b[prev])).astype(BF16))

    s_old = [st_ref[0, h] for h in heads]
    o_inter = [_dot(q_in[:, dk[h]], s_old[h]) for h in heads]
    a_diag = [[_dot_nt(q_diag[s][:, dk[h]], k_diag[s][:, dk[h]]) for h in heads] for s in range(n_sub)]
    a_off = [[_dot_nt(q_off[s][:, dk[h]], k_off[s][:, dk[h]]) for h in heads] for s in range(n_sub - 1)]
    a_diag = [[jnp.where(causal, a, 0.0).astype(BF16) for a in row_] for row_ in a_diag]
    a_off = [[a.astype(BF16) for a in row_] for row_ in a_off]
    o_intra = []
    for s, rs in enumerate(subs):
        o_s = [_dot(a_diag[s][h], vb[rs, dv[h]]) for h in heads]
        if s > 0:
            o_s = [o_s[h] + _dot(a_off[s - 1][h], vb[0:s * sub, dv[h]]) for h in heads]
        o_intra.append(o_s)
    kv = [_dot_tn(k_out[:, dk[h]], vb[:, dv[h]]) for h in heads]
    for h in heads:
        st_ref[0, h] = s_old[h] * jnp.tile(dec[dk[h], :], (1, GLA_DV // LANES)) + kv[h]
    for h in heads:
        o = o_inter[h] + jnp.concatenate([o_intra[s][h] for s in range(n_sub)], axis=0)
        o = o * lax.rsqrt(jnp.mean(o * o, axis=-1, keepdims=True) + LN_EPS)
        o_ref[:, dv[h]] = (o * g_ref[:, dv[h]] * _silu(r_ref[:, dv[h]])).astype(BF16)


def _gla(main, tail, w_alpha, b_alpha, norm_g, s0, batch, seq, chunk, t_valid, to_bf16=()):
    n_chunks = seq // chunk
    qk_w = GLA_HEADS * GLA_DK
    v_w = GLA_HEADS * GLA_DV
    wa = jnp.pad(w_alpha, ((0, LANES - GLA_RANK), (0, 0)))
    row_map = lambda b, c: (b * n_chunks + c, 0)
    const = lambda b, c: (0, 0)
    in_specs = [
        pl.BlockSpec((chunk, qk_w), row_map),
        pl.BlockSpec((chunk, qk_w), lambda b, c: (b * n_chunks + c, 1)),
        pl.BlockSpec((chunk, v_w), lambda b, c: (b * n_chunks + c, 1)),
        pl.BlockSpec((chunk, v_w), lambda b, c: (b * n_chunks + c, 2)),
        pl.BlockSpec((chunk, LANES), row_map),
        pl.BlockSpec((LANES, qk_w), const),
        pl.BlockSpec((1, qk_w), const),
        pl.BlockSpec((1, v_w), const),
    ]
    args = [main, main, main, main, tail, wa, b_alpha.reshape(1, qk_w), norm_g.reshape(1, v_w)]
    state_spec = pl.BlockSpec((1, GLA_HEADS, GLA_DK, GLA_DV), lambda b, c: (b, 0, 0, 0))
    if s0 is not None:
        in_specs.append(state_spec)
        args.append(s0)
    n_steps = batch * n_chunks
    cast_specs = [pl.BlockSpec((a.shape[0] // n_steps, a.shape[1]), row_map) for a in to_bf16]
    assert all(a.shape[0] % (n_steps * BF16_ROWS) == 0 for a in to_bf16)
    return pl.pallas_call(
        functools.partial(_gla_kernel, chunk=chunk, t_valid=t_valid, has_s0=s0 is not None,
                          n_cast=len(to_bf16)),
        grid=(batch, n_chunks),
        in_specs=in_specs + cast_specs,
        out_specs=[pl.BlockSpec((chunk, v_w), row_map), state_spec] + cast_specs,
        out_shape=[jax.ShapeDtypeStruct((batch * seq, v_w), BF16),
                   jax.ShapeDtypeStruct((batch, GLA_HEADS, GLA_DK, GLA_DV), F32)]
        + [jax.ShapeDtypeStruct(a.shape, BF16) for a in to_bf16],
        compiler_params=_params("parallel", "arbitrary"),
        name="gla_prompt" if s0 is None else "gla_sample",
    )(*args, *to_bf16)


def _select_blocks(imp_t, blk, cur):
    n = imp_t.shape[0]
    forced = (blk == 0) | ((blk >= cur - 1) & (blk <= cur))
    score = jnp.where(blk > cur, -jnp.inf, jnp.where(forced, jnp.inf, imp_t))
    rank = jnp.zeros(score.shape, F32)
    for i in range(n):
        si = score[i:i + 1, :]
        rank = rank + jnp.where(blk > i, _ind(si >= score), _ind(si > score))
    return (rank < NSA_TOPK) & (score > -jnp.inf)


def _nsa_prompt_kernel(pt_ref, q_ref, kc_ref, vc_ref, ks_ref, vs_ref, kw_ref, vw_ref, gt_ref, bg_ref, *refs,
                       seq, tq, tk, pages_per_step):
    page_refs = refs[:pages_per_step]
    (o_ref, means_ref, kcm, vcm, ksb, vst, kwb, vwt, qs_ref, sel_ref,
     ms_ref, accs_ref, mw_ref, accw_ref) = refs[pages_per_step:]
    qi = pl.program_id(2)
    n_blk = seq // NSA_BLOCK
    n_kt = seq // tk
    blk_per_kt = tk // NSA_BLOCK

    @pl.when(qi == 0)
    def _():
        kcm[...] = jnp.sum(kc_ref[...].reshape(n_blk, NSA_BLOCK, NSA_HD), axis=1) * (1.0 / NSA_BLOCK)
        vcm[...] = jnp.sum(vc_ref[...].reshape(n_blk, NSA_BLOCK, NSA_HD), axis=1) * (1.0 / NSA_BLOCK)
        ksb[...] = ks_ref[...].astype(BF16)
        kwb[...] = kw_ref[...].astype(BF16)
        ones_rows = _ind(lax.broadcasted_iota(jnp.int32, (BF16_ROWS, tk), 0) == 0, BF16)
        for kt in range(n_kt):
            rows = slice(kt * tk, (kt + 1) * tk)
            vst[kt] = jnp.concatenate([vs_ref[rows, :].T.astype(BF16), ones_rows], axis=0)
            vwt[kt] = jnp.concatenate([vw_ref[rows, :].T.astype(BF16), ones_rows], axis=0)

    t0 = qi * tq
    q = q_ref[...] * (NSA_HD ** -0.5 * LOG2_E)
    qs_ref[...] = jnp.concatenate(
        [q[:, g * NSA_HD:(g + 1) * NSA_HD].T for g in range(NSA_GROUP)], axis=1).astype(BF16)
    qs = qs_ref[...]
    t_row = t0 + lax.broadcasted_iota(jnp.int32, (1, tq), 1)
    t_row_g = jnp.concatenate([t_row] * NSA_GROUP, axis=1)

    kc_b = kcm[...].astype(BF16)
    blk_col = lax.broadcasted_iota(jnp.int32, (n_blk, 1), 0)
    p_t = _masked_softmax2(_dot(kc_b, qs), (blk_col + 1) * NSA_BLOCK - 1 <= t_row_g, axis=0)
    o_c = _dot_tn(vcm[...], p_t)
    imp_t = sum(p_t[:, g * tq:(g + 1) * tq] for g in range(NSA_GROUP))
    sel_ref[...] = jnp.where(_select_blocks(imp_t, blk_col, t_row // NSA_BLOCK), 0.0, NEG_BIG)

    n_loc = lax.broadcasted_iota(jnp.int32, (tk, tq), 0)
    t_loc = lax.broadcasted_iota(jnp.int32, (tk, tq), 1)
    on_lanes = lambda a: jnp.concatenate([a] * NSA_GROUP, axis=1)
    causal = on_lanes(jnp.where(n_loc <= t_loc, 0.0, NEG_BIG))
    far = on_lanes(jnp.where(n_loc > t_loc, 0.0, NEG_BIG))

    def sel_bias(kt):
        return on_lanes(jnp.concatenate(
            [jnp.broadcast_to(sel_ref[pl.ds(kt * blk_per_kt + j, 1), :], (NSA_BLOCK, tq))
             for j in range(blk_per_kt)], axis=0))

    sel_state, win_state = (ms_ref, accs_ref), (mw_ref, accw_ref)
    for m_ref, acc_ref in (sel_state, win_state):
        m_ref[...] = jnp.full(m_ref.shape, NEG_BIG, F32)
        acc_ref[...] = jnp.zeros(acc_ref.shape, F32)

    def steps(*work):
        s = [_dot(k_ref[pl.ds(pl.multiple_of(kt * tk, tk), tk), :], qs_ref[...]) + bias
             for _, k_ref, _, kt, bias in work]
        m_old = [state[0][...] for state, *_ in work]
        m_new = [jnp.maximum(mo, jnp.max(si, axis=0, keepdims=True)) for mo, si in zip(m_old, s)]
        p = [jnp.exp2(si - mn).astype(BF16) for si, mn in zip(s, m_new)]
        pv = [_dot(vt_ref[kt], pi) for (_, _, vt_ref, kt, _), pi in zip(work, p)]
        for (state, *_), mo, mn, pvi in zip(work, m_old, m_new, pv):
            state[1][...] = jnp.exp2(mo - mn) * state[1][...] + pvi
            state[0][...] = mn

    def result(acc_ref):
        acc = acc_ref[...]
        return acc[:NSA_HD] / jnp.maximum(acc[NSA_HD:NSA_HD + 1], TINY)

    steps((sel_state, ksb, vst, qi, sel_bias(qi) + causal), (win_state, kwb, vwt, qi, causal))

    @pl.when(qi >= 1)
    def _():
        steps((sel_state, ksb, vst, qi - 1, sel_bias(qi - 1)), (win_state, kwb, vwt, qi - 1, far))

    def sel_body(i, carry):
        kt = qi - 2 - i
        steps((sel_state, ksb, vst, kt, sel_bias(kt)))
        return carry

    lax.fori_loop(0, qi - 1, sel_body, 0)

    gates = jax.nn.sigmoid(gt_ref[...] + bg_ref[...])
    g_c, g_s, g_w = (jnp.concatenate([gates[3 * g + br:3 * g + br + 1, :] for g in range(NSA_GROUP)], axis=1)
                     for br in range(3))
    o_t = g_c * o_c + g_s * result(accs_ref) + g_w * result(accw_ref)
    for g in range(NSA_GROUP):
        o_ref[:, g * NSA_HD:(g + 1) * NSA_HD] = o_t[:, g * tq:(g + 1) * tq].T.astype(BF16)

    for i, x_ref in enumerate(page_refs):
        x = x_ref[0].reshape(PAGE // NSA_BLOCK, NSA_BLOCK, KV_ROWS, NSA_HD)
        means_ref[0, i] = jnp.sum(x, axis=1) * (1.0 / NSA_BLOCK)


def _nsa_prompt(main, gates, b_gate, batch, seq, cache_cmp, page_table):
    tq = tk = NSA_WINDOW
    assert seq % tk == 0
    vt_rows = NSA_HD + BF16_ROWS
    nq = seq // tq
    q_lanes = NSA_GROUP * NSA_HD
    col0 = NSA_Q_W // NSA_HD
    seqs, n_pages = page_table.shape
    n_steps = batch * NSA_KV * nq
    pages_per_step = seqs * n_pages // n_steps
    groups = n_pages // pages_per_step
    assert pages_per_step * n_steps == seqs * n_pages and groups * pages_per_step == n_pages
    per_page = PAGE // NSA_BLOCK

    def kv_spec(branch, part):
        off = col0 + (2 * branch + part) * NSA_KV
        return pl.BlockSpec((seq, NSA_HD), lambda b, kv, qi, pt: (b, off + kv))

    def step_of(b, kv, qi):
        return (b * NSA_KV + kv) * nq + qi

    def page_spec(i):
        def index(b, kv, qi, pt):
            s = step_of(b, kv, qi)
            return (pt[s // groups, (s % groups) * pages_per_step + i], 0, 0)
        return pl.BlockSpec((1, PAGE * KV_ROWS, NSA_HD), index)

    def means_index(b, kv, qi, pt):
        s = step_of(b, kv, qi)
        return (s // groups, s % groups, 0, 0, 0)

    g_w = 3 * NSA_GROUP
    g_rows = NSA_GROUP * tq
    grid_spec = pltpu.PrefetchScalarGridSpec(
        num_scalar_prefetch=1,
        grid=(batch, NSA_KV, nq),
        in_specs=[pl.BlockSpec((tq, q_lanes), lambda b, kv, qi, pt: (b * nq + qi, kv))]
        + [kv_spec(br, part) for br in range(3) for part in range(2)]
        + [pl.BlockSpec((None, g_w, tq), lambda b, kv, qi, pt: (kv, 0, b * nq + qi)),
           pl.BlockSpec((None, g_w, 1), lambda b, kv, qi, pt: (kv, 0, 0))]
        + [page_spec(i) for i in range(pages_per_step)],
        out_specs=[pl.BlockSpec((tq, q_lanes), lambda b, kv, qi, pt: (b * nq + qi, kv)),
                   pl.BlockSpec((1, pages_per_step, per_page, KV_ROWS, NSA_HD), means_index)],
        scratch_shapes=[
            pltpu.VMEM((seq // NSA_BLOCK, NSA_HD), F32),
            pltpu.VMEM((seq // NSA_BLOCK, NSA_HD), F32),
            pltpu.VMEM((seq, NSA_HD), BF16),
            pltpu.VMEM((seq // tk, vt_rows, tk), BF16),
            pltpu.VMEM((seq, NSA_HD), BF16),
            pltpu.VMEM((seq // tk, vt_rows, tk), BF16),
            pltpu.VMEM((NSA_HD, g_rows), BF16),
            pltpu.VMEM((seq // NSA_BLOCK, tq), F32),
            pltpu.VMEM((1, g_rows), F32),
            pltpu.VMEM((vt_rows, g_rows), F32),
            pltpu.VMEM((1, g_rows), F32),
            pltpu.VMEM((vt_rows, g_rows), F32),
        ],
    )
    return pl.pallas_call(
        functools.partial(_nsa_prompt_kernel, seq=seq, tq=tq, tk=tk, pages_per_step=pages_per_step),
        grid_spec=grid_spec,
        out_shape=[jax.ShapeDtypeStruct((batch * seq, NSA_Q_W), BF16),
                   jax.ShapeDtypeStruct((seqs, n_pages, per_page, KV_ROWS, NSA_HD), F32)],
        compiler_params=_params("arbitrary", "arbitrary", "arbitrary"),
        name="nsa_prompt",
    )(page_table, main, main, main, main, main, main, main, gates, b_gate, *([cache_cmp] * pages_per_step))


def _kv_rows_out_kernel(c_ref, s_ref, w_ref, oc_ref, os_ref, ow_ref, *, rows, n_c):
    def spread(src, dst):
        for r in range(KV_ROWS):
            dst[pl.ds(r, rows, stride=KV_ROWS), :] = src[:, r * NSA_HD:(r + 1) * NSA_HD]

    spread(c_ref, oc_ref)
    spread(s_ref, os_ref)

    @pl.when(pl.program_id(1) == n_c - 1)
    def _():
        spread(w_ref, ow_ref)


def _kv_rows_out(main, batch, seq, keep):
    rows = keep
    n_c = seq // rows
    col0 = NSA_Q_W // NSA_KV_W
    tile = lambda br: pl.BlockSpec((rows, NSA_KV_W), lambda b, c: (b * n_c + c, col0 + br))
    out_tile = pl.BlockSpec((rows * KV_ROWS, NSA_HD), lambda b, c: (b * n_c + c, 0))
    return pl.pallas_call(
        functools.partial(_kv_rows_out_kernel, rows=rows, n_c=n_c),
        grid=(batch, n_c),
        in_specs=[tile(0), tile(1),
                  pl.BlockSpec((rows, NSA_KV_W), lambda b, c: (b * n_c + n_c - 1, col0 + 2))],
        out_specs=[out_tile, out_tile, pl.BlockSpec((rows * KV_ROWS, NSA_HD), lambda b, c: (b, 0))],
        out_shape=[jax.ShapeDtypeStruct((batch * seq * KV_ROWS, NSA_HD), F32)] * 2
        + [jax.ShapeDtypeStruct((batch * keep * KV_ROWS, NSA_HD), F32)],
        compiler_params=_params("arbitrary", "arbitrary"),
        name="nsa_kv_rows_out",
    )(main, main, main)


def _page_specs(n):
    return [pl.BlockSpec((1, PAGE * KV_ROWS, NSA_HD), lambda b, p, pt, i=i: (pt[b, p * n + i], 0, 0))
            for i in range(n)]


def _nsa_sample_kernel(pt_ref, q_ref, cm_ref, *refs, past, n_new, n_steps, per_step):
    pg_refs = refs[:per_step]
    (kvn_ref, wb_ref, kwn_ref, gt_ref, bg_ref, o_ref,
     qr_ref, sel_ref, m_ref, l_ref, acc_ref, oc_ref, ow_ref) = refs[per_step:]
    p = pl.program_id(1)
    k_w = NSA_KV * NSA_HD
    cols = NSA_KV * n_new * NSA_GROUP
    per_kv = n_new * NSA_GROUP
    n_cmp = past // NSA_BLOCK
    n_sel = -(-(past + n_new) // NSA_BLOCK)
    n_sel_pad = sel_ref.shape[0]
    lane = lax.broadcasted_iota(jnp.int32, (1, cols), 1)
    q_pos = past + (lane // NSA_GROUP) % n_new

    def heads_on_lanes(ref, n, part):
        return jnp.concatenate(
            [ref[0, pl.ds(part * NSA_KV + kv, n, stride=KV_ROWS), :] for kv in range(NSA_KV)], axis=1)

    def scores(keys):
        return _dot_nt(keys, qr_ref[...])

    def own_head(x_t):
        out = jnp.zeros((NSA_HD, cols), F32)
        for kv in range(NSA_KV):
            mine = (lane // per_kv) == kv
            out = out + jnp.where(mine, x_t[kv * NSA_HD:(kv + 1) * NSA_HD, :], 0.0)
        return out

    def attend_once(keys, vals, mask):
        p_t = _masked_softmax2(scores(keys), mask, axis=0)
        return own_head(_dot_tn(vals, p_t))

    def online(keys, vals, bias):
        s = scores(keys) + bias
        m_old = m_ref[...]
        m_new = jnp.maximum(m_old, jnp.max(s, axis=0, keepdims=True))
        p_t = jnp.exp2(s - m_new)
        alpha = jnp.exp2(m_old - m_new)
        l_ref[...] = alpha * l_ref[...] + jnp.sum(p_t, axis=0, keepdims=True)
        acc_ref[...] = alpha * acc_ref[...] + own_head(_dot_tn(vals, p_t))
        m_ref[...] = m_new

    @pl.when(p == 0)
    def _():
        q = q_ref[0] * (NSA_HD ** -0.5 * LOG2_E)
        r_kv = lax.broadcasted_iota(jnp.int32, (cols, k_w), 0) // per_kv
        c_kv = lax.broadcasted_iota(jnp.int32, (cols, k_w), 1) // NSA_HD
        qr_ref[...] = jnp.where(r_kv == c_kv, jnp.tile(q, (1, NSA_KV)), 0.0).astype(BF16)
        cm = cm_ref[0]
        blk = lax.broadcasted_iota(jnp.int32, (n_cmp, 1), 0)
        p_t = _masked_softmax2(scores(cm[:, :k_w]), (blk + 1) * NSA_BLOCK - 1 <= q_pos, axis=0)
        oc_ref[...] = own_head(_dot_tn(cm[:, k_w:], p_t))
        same = _ind(lax.broadcasted_iota(jnp.int32, (cols, cols), 0) // NSA_GROUP
                    == lax.broadcasted_iota(jnp.int32, (cols, cols), 1) // NSA_GROUP, BF16)
        imp = sum(jnp.dot(t, same, preferred_element_type=F32) for t in _split3(p_t))
        imp = jnp.concatenate([imp, jnp.zeros((n_sel_pad - n_cmp, cols), F32)], axis=0)
        blk_s = lax.broadcasted_iota(jnp.int32, (n_sel_pad, 1), 0)
        cur = q_pos // NSA_BLOCK
        forced = (blk_s == 0) | ((blk_s >= cur - 1) & (blk_s <= cur))
        score = jnp.where((blk_s > cur) | (blk_s >= n_sel), -jnp.inf, jnp.where(forced, jnp.inf, imp))
        blk_f = blk_s.astype(F32)

        def pick(_, carry):
            left, bias = carry
            top = jnp.max(left, axis=0, keepdims=True)
            first = jnp.min(jnp.where(left == top, blk_f, float(n_sel_pad)), axis=0, keepdims=True)
            hit = (blk_f == first) & (top > -jnp.inf)
            return jnp.where(hit, -jnp.inf, left), jnp.where(hit, 0.0, bias)

        _, bias = lax.fori_loop(0, NSA_TOPK, pick, (score, jnp.full(score.shape, NEG_BIG, F32)))
        sel_ref[...] = bias
        n_wb = wb_ref.shape[1] // KV_ROWS
        kw_all = jnp.concatenate([heads_on_lanes(wb_ref, n_wb, 0), kwn_ref[0][:, :k_w]], axis=0)
        vw_all = jnp.concatenate([heads_on_lanes(wb_ref, n_wb, 1), kwn_ref[0][:, k_w:]], axis=0)
        n_w = kw_all.shape[0]
        w_row = lax.broadcasted_iota(jnp.int32, (n_w, 1), 0)
        w_pos = past - n_wb + w_row
        d = q_pos - w_pos
        ow_ref[...] = attend_once(kw_all, vw_all,
                                  (d >= 0) & (d < NSA_WINDOW) & (w_pos >= 0) & (w_row < n_wb + n_new))
        m_ref[...] = jnp.full(m_ref.shape, NEG_BIG, F32)
        l_ref[...] = jnp.zeros(l_ref.shape, F32)
        acc_ref[...] = jnp.zeros(acc_ref.shape, F32)

    def page_part(part):
        return jnp.concatenate([heads_on_lanes(pg_ref, PAGE, part) for pg_ref in pg_refs], axis=0)

    blk_per_step = per_step * PAGE // NSA_BLOCK
    online(page_part(0), page_part(1), jnp.concatenate(
        [jnp.broadcast_to(sel_ref[pl.ds(p * blk_per_step + j, 1), :], (NSA_BLOCK, cols))
         for j in range(blk_per_step)], axis=0))

    @pl.when(p == n_steps - 1)
    def _():
        kvn = kvn_ref[0]
        n_row = lax.broadcasted_iota(jnp.int32, (kvn.shape[0], 1), 0)
        bias_new = jnp.where((n_row < n_new) & (past + n_row <= q_pos), 0.0, NEG_BIG)
        online(kvn[:, :k_w], kvn[:, k_w:], bias_new + sel_ref[pl.ds(past // NSA_BLOCK, 1), :])
        o_s = acc_ref[...] / jnp.maximum(l_ref[...], TINY)
        gates = jax.nn.sigmoid(gt_ref[0] + bg_ref[...])
        o_t = gates[0:1, :] * oc_ref[...] + gates[1:2, :] * o_s + gates[2:3, :] * ow_ref[...]
        eye = _ind(lax.broadcasted_iota(jnp.int32, (NSA_HD, NSA_HD), 0)
                   == lax.broadcasted_iota(jnp.int32, (NSA_HD, NSA_HD), 1), BF16)
        o_ref[0] = sum(lax.dot_general(t, eye, (((0,), (0,)), ((), ())), preferred_element_type=F32)
                       for t in _split3(o_t))


def _nsa_sample(q_rows, cmp_means, cache_slc, page_table, kv_new, win_buf, kw_new, gates, b_gate,
                past, n_new):
    seqs, n_pages = page_table.shape
    cols = q_rows.shape[1]
    n_cmp = cmp_means.shape[1]
    n_sel_pad = -(-(-(-(past + n_new) // NSA_BLOCK)) // 8) * 8
    n_wb = win_buf.shape[1]
    pad_new = kv_new.shape[1]
    per_step = 8
    n_steps = n_pages // per_step
    seq_map = lambda b, p, pt: (b, 0, 0)
    return pl.pallas_call(
        functools.partial(_nsa_sample_kernel, past=past, n_new=n_new, n_steps=n_steps, per_step=per_step),
        grid_spec=pltpu.PrefetchScalarGridSpec(
            num_scalar_prefetch=1,
            grid=(seqs, n_steps),
            in_specs=[
                pl.BlockSpec((1, cols, NSA_HD), seq_map),
                pl.BlockSpec((1, n_cmp, NSA_KV_W), seq_map),
                *_page_specs(per_step),
                pl.BlockSpec((1, pad_new, NSA_KV_W), seq_map),
                pl.BlockSpec((1, n_wb, NSA_HD), seq_map),
                pl.BlockSpec((1, pad_new, NSA_KV_W), seq_map),
                pl.BlockSpec((1, 3, cols), seq_map),
                pl.BlockSpec((3, cols), lambda b, p, pt: (0, 0)),
            ],
            out_specs=pl.BlockSpec((1, cols, NSA_HD), seq_map),
            scratch_shapes=[
                pltpu.VMEM((cols, NSA_KV * NSA_HD), BF16),
                pltpu.VMEM((n_sel_pad, cols), F32),
                pltpu.VMEM((1, cols), F32),
                pltpu.VMEM((1, cols), F32),
                pltpu.VMEM((NSA_HD, cols), F32),
                pltpu.VMEM((NSA_HD, cols), F32),
                pltpu.VMEM((NSA_HD, cols), F32),
            ],
        ),
        out_shape=jax.ShapeDtypeStruct((seqs, cols, NSA_HD), F32),
        compiler_params=_params("parallel", "arbitrary"),
        name="nsa_sample",
    )(page_table, q_rows, cmp_means, *([cache_slc] * per_step), kv_new, win_buf, kw_new, gates, b_gate)


def kernel(x_prompt, x_sample, c_prompt, c_sample, state_gla, cache_cmp_kv, cache_slc_kv, cache_win_kv,
           page_table, ada_w, ada_b, gla_w_in, gla_w_alpha, gla_b_alpha, gla_norm_g, gla_w_out, nsa_w_in,
           nsa_b_gate, nsa_w_out, ln_mix_g, ln_mix_b, ffn_w_in, ffn_w_out, ln_ffn_g, ln_ffn_b):
    batch, seq, _ = x_prompt.shape
    seqs, n_new, _ = x_sample.shape
    n_pages = page_table.shape[1]
    past = n_pages * PAGE
    rows_s = seqs * n_new
    tm_p = seq

    pad_rows = 16 - (batch + seqs)
    c_all = jnp.concatenate([c_prompt, c_sample, jnp.zeros((pad_rows, D_MODEL), F32)], axis=0)
    mod = _adaln(c_all, ada_w, ada_b).reshape(DEPTH, 16, 6, D_MODEL)

    def mods(layer):
        mp = [mod[layer, :batch, i].reshape(batch, 1, D_MODEL) for i in range(6)]
        ms = [jnp.repeat(mod[layer, batch:batch + seqs, i], n_new, axis=0).reshape(1, rows_s, D_MODEL)
              for i in range(6)]
        return mp, ms

    def mixer_out(o_p, o_s, xp, xs, layer, mp, ms, tag):
        return _out_ln((o_p, xp, mp[2], mp[4], mp[3]), (o_s, xs, ms[2], ms[4], ms[3]), w_out_mix[layer % 2],
                       layer // 2, ln_mix_g, ln_mix_b, layer, 512, tag)

    def ffn(hp, hs, xp, xs, layer, mp, ms, next_mods):
        act_p, act_s = _ffn_in(hp, hs, ffn_w_in, layer, tm_p, "ffn_in_%d" % layer)
        nxt_p, nxt_s = ((), ()) if next_mods is None else ((next_mods[0][1], next_mods[0][0]),
                                                            (next_mods[1][1], next_mods[1][0]))
        return _out_ln((act_p, xp, mp[5], *nxt_p), (act_s, xs, ms[5], *nxt_s), w_out_ffn, layer,
                       ln_ffn_g, ln_ffn_b, layer, 256, "ffn_out_%d" % layer)

    xp = x_prompt.reshape(batch * seq, D_MODEL)
    xs = x_sample.reshape(rows_s, D_MODEL)

    mp, ms = mods(0)
    main_p, tail_p, main_s, tail_s = _proj((xp, mp[1], mp[0]), (xs, ms[1], ms[0]), jnp.swapaxes(gla_w_in, 1, 2), 0,
                                           GLA_MAIN_W, tm_p, "gla_in")
    rows2d = lambda w: w.reshape(-1, w.shape[-1])
    gated_p, gla_state_p, *w_bf16 = _gla(main_p, tail_p, gla_w_alpha[0], gla_b_alpha[0], gla_norm_g[0], None,
                                         batch, seq, 256, 256,
                                         to_bf16=(rows2d(gla_w_out), rows2d(nsa_w_out), rows2d(ffn_w_out)))
    w_out_mix = (w_bf16[0].reshape(gla_w_out.shape), w_bf16[1].reshape(nsa_w_out.shape))
    w_out_ffn = w_bf16[2].reshape(ffn_w_out.shape)
    chunk_s = GLA_SUB

    def pad_steps(a):
        a = a.reshape(seqs, n_new, a.shape[-1])
        return jnp.pad(a, ((0, 0), (0, chunk_s - n_new), (0, 0))).reshape(seqs * chunk_s, a.shape[-1])

    gated_s, gla_state_s = _gla(pad_steps(main_s), pad_steps(tail_s), gla_w_alpha[0], gla_b_alpha[0],
                                gla_norm_g[0], state_gla[:, 0], seqs, chunk_s, chunk_s, n_new)
    gated_s = gated_s.reshape(seqs, chunk_s, -1)[:, :n_new].reshape(rows_s, -1)
    xp, xs, hp, hs = mixer_out(gated_p, gated_s, xp, xs, 0, mp, ms, "gla_out")
    xp, xs, hp, hs = ffn(hp, hs, xp, xs, 0, mp, ms, mods(1))

    mp, ms = mods(1)
    main_p, tail_p, main_s, tail_s = _proj(hp, hs, jnp.swapaxes(nsa_w_in, 1, 2), 0, NSA_MAIN_W, tm_p, "nsa_in")
    n_gate = 3 * NSA_HEADS
    g_w = 3 * NSA_GROUP
    gates_p = tail_p[:, :n_gate].reshape(batch * seq, NSA_KV, g_w).transpose(1, 2, 0)
    b_gate_p = nsa_b_gate[0].reshape(NSA_KV, g_w, 1)
    n_phys = cache_cmp_kv.shape[0]
    page_rows = lambda cache: cache.reshape(n_phys, PAGE * KV_ROWS, NSA_HD)
    o_p, cmp_means = _nsa_prompt(main_p, gates_p, b_gate_p, batch, seq, page_rows(cache_cmp_kv), page_table)

    kv_rows = lambda m, br: m[:, NSA_Q_W + br * NSA_KV_W:NSA_Q_W + (br + 1) * NSA_KV_W]
    kv_shape = (2, NSA_KV, NSA_HD)
    keep_p = min(NSA_WINDOW, seq)
    cmp_rows, slc_rows, win_rows = _kv_rows_out(main_p, batch, seq, keep_p)
    cmp_kv_p = cmp_rows.reshape(batch, 1, seq, *kv_shape)
    slc_kv_p = slc_rows.reshape(batch, 1, seq, *kv_shape)
    win_kv_p = win_rows.reshape(batch, 1, keep_p, *kv_shape)

    cmp_kv_s = kv_rows(main_s, 0).reshape(seqs, 1, n_new, *kv_shape)
    slc_kv_s = kv_rows(main_s, 1).reshape(seqs, 1, n_new, *kv_shape)
    kw_new = kv_rows(main_s, 2).reshape(seqs, n_new, NSA_KV_W)
    n_wb = cache_win_kv.shape[2]
    win_buf = cache_win_kv[:, 0].reshape(seqs, n_wb * KV_ROWS, NSA_HD)
    all_w = jnp.concatenate([win_buf, kw_new.reshape(seqs, n_new * KV_ROWS, NSA_HD)], axis=1)
    keep_s = min(NSA_WINDOW, n_wb + n_new)
    win_kv_s = all_w[:, (n_wb + n_new - keep_s) * KV_ROWS:].reshape(seqs, 1, keep_s, *kv_shape)

    cmp_means = cmp_means.reshape(seqs, past // NSA_BLOCK, NSA_KV_W)
    pad_new = 8
    pad8 = lambda a: jnp.pad(a, ((0, 0), (0, pad_new - n_new), (0, 0)))
    cols = NSA_KV * n_new * NSA_GROUP
    q_rows = (main_s[:, :NSA_Q_W].reshape(seqs, n_new, NSA_KV, NSA_GROUP, NSA_HD)
              .transpose(0, 2, 1, 3, 4).reshape(seqs, cols, NSA_HD))
    gates_s = (tail_s[:, :n_gate].reshape(seqs, n_new, NSA_KV, NSA_GROUP, 3)
               .transpose(0, 4, 2, 1, 3).reshape(seqs, 3, cols))
    b_gate_s = jnp.broadcast_to(nsa_b_gate[0].reshape(1, NSA_KV, NSA_GROUP, 3), (n_new, NSA_KV, NSA_GROUP, 3))
    b_gate_s = b_gate_s.transpose(3, 1, 0, 2).reshape(3, cols)
    o_s = _nsa_sample(q_rows, cmp_means, page_rows(cache_slc_kv), page_table,
                      pad8(slc_kv_s.reshape(seqs, n_new, NSA_KV_W)), win_buf, pad8(kw_new), gates_s, b_gate_s,
                      past, n_new)
    o_s = (o_s.reshape(seqs, NSA_KV, n_new, NSA_GROUP, NSA_HD).transpose(0, 2, 1, 3, 4)
           .reshape(rows_s, NSA_Q_W).astype(BF16))

    xp, xs, hp, hs = mixer_out(o_p, o_s, xp, xs, 1, mp, ms, "nsa_out")
    xp, xs = ffn(hp, hs, xp, xs, 1, mp, ms, None)

    return (xp.reshape(batch, seq, D_MODEL), xs.reshape(seqs, n_new, D_MODEL),
            gla_state_p[:, None], gla_state_s[:, None], cmp_kv_p, cmp_kv_s, slc_kv_p, slc_kv_s,
            win_kv_p, win_kv_s)
```

```python
import functools

import jax
import jax.numpy as jnp
from jax import lax
from jax.experimental import pallas as pl
from jax.experimental.pallas import tpu as pltpu

F32 = jnp.float32
BF16 = jnp.bfloat16

D_MODEL = 2048
DEPTH = 2
DN_ALPHA = (2 * DEPTH) ** 0.25
LN_EPS = 1e-5
D_FF = 5632
GLA_HEADS = 4
GLA_DK = 256
GLA_DV = 512
GLA_RANK = 16
GLA_TAU = 16.0
GLA_SUB = 32
GLA_MAIN_W = 2 * GLA_HEADS * GLA_DK + 2 * GLA_HEADS * GLA_DV
NSA_HEADS = 16
NSA_KV = 4
NSA_HD = 128
NSA_GROUP = NSA_HEADS // NSA_KV
NSA_BLOCK = 64
NSA_TOPK = 16
NSA_WINDOW = 512
NSA_Q_W = NSA_HEADS * NSA_HD
NSA_KV_W = 2 * NSA_KV * NSA_HD
NSA_MAIN_W = NSA_Q_W + 3 * NSA_KV_W
PAGE = 128
KV_ROWS = 2 * NSA_KV

LANES = 128
BF16_ROWS = 16
LOG2_E = 1.4426950408889634
V7X_VMEM_BYTES = 64 * 1024 * 1024
VMEM_LIMIT = V7X_VMEM_BYTES * 7 // 8

NEG_BIG = -1e30
TINY = float(jnp.finfo(jnp.float32).tiny)


def _params(*sem):
    return pltpu.CompilerParams(dimension_semantics=sem, vmem_limit_bytes=VMEM_LIMIT)


def _dot(a, b):
    return jnp.dot(a.astype(BF16), b.astype(BF16), preferred_element_type=F32)


def _dot_nt(a, b):
    return lax.dot_general(a.astype(BF16), b.astype(BF16), (((1,), (1,)), ((), ())),
                           preferred_element_type=F32)


def _dot_tn(a, b):
    return lax.dot_general(a.astype(BF16), b.astype(BF16), (((0,), (0,)), ((), ())),
                           preferred_element_type=F32)


def _split3(x):
    hi = x.astype(BF16)
    r1 = x - hi.astype(F32)
    mid = r1.astype(BF16)
    lo = (r1 - mid.astype(F32)).astype(BF16)
    return hi, mid, lo


def _silu(x):
    return x * jax.nn.sigmoid(x)


def _ind(cond, dtype=F32):
    return jnp.where(cond, 1.0, 0.0).astype(dtype)


def _masked_softmax2(s2, mask, axis):
    s2 = jnp.where(mask, s2, -jnp.inf)
    m = jnp.max(s2, axis=axis, keepdims=True)
    m = jnp.where(jnp.isfinite(m), m, 0.0)
    e = jnp.where(mask, jnp.exp2(s2 - m), 0.0)
    return e / jnp.maximum(jnp.sum(e, axis=axis, keepdims=True), TINY)


def _adaln_kernel(c_ref, w_ref, b_ref, o_ref):
    o_ref[...] = _dot(_silu(c_ref[...]), w_ref[...]) + b_ref[...]


def _adaln(c_all, ada_w, ada_b):
    rows = c_all.shape[0]
    n = ada_w.shape[2]
    tn = 1024
    return pl.pallas_call(
        _adaln_kernel,
        grid=(DEPTH, n // tn),
        in_specs=[
            pl.BlockSpec((rows, D_MODEL), lambda l, j: (0, 0)),
            pl.BlockSpec((None, D_MODEL, tn), lambda l, j: (l, 0, j)),
            pl.BlockSpec((None, 1, tn), lambda l, j: (l, 0, j)),
        ],
        out_specs=pl.BlockSpec((None, rows, tn), lambda l, j: (l, 0, j)),
        out_shape=jax.ShapeDtypeStruct((DEPTH, rows, n), F32),
        compiler_params=_params("parallel", "parallel"),
        name="adaln",
    )(c_all, ada_w, ada_b.reshape(DEPTH, 1, n))


def _modulate(x_ref, sc_ref, sh_ref):
    return (x_ref[...] * (1.0 + sc_ref[...]) + sh_ref[...]).astype(BF16)


def _proj_kernel(*refs, n_i, modulated):
    if modulated:
        h_ref, hs_ref, w_ref, wt_ref, o_ref, ot_ref, os_ref, ots_ref = refs
    else:
        (x_ref, sc_ref, sh_ref, xs_ref, scs_ref, shs_ref, w_ref, wt_ref,
         o_ref, ot_ref, os_ref, ots_ref, h_ref, hs_ref) = refs
    i, j = pl.program_id(0), pl.program_id(1)

    @pl.when(j == 0)
    def _():
        if not modulated:
            h_ref[...] = _modulate(x_ref, sc_ref, sh_ref)
        ot_ref[...] = _dot_nt(h_ref[...], wt_ref[...])

    o_ref[...] = _dot_nt(h_ref[...], w_ref[...])

    @pl.when(i == n_i - 1)
    def _():
        @pl.when(j == 0)
        def _():
            if not modulated:
                hs_ref[...] = _modulate(xs_ref, scs_ref, shs_ref)
            ots_ref[...] = _dot_nt(hs_ref[...], wt_ref[...])

        os_ref[...] = _dot_nt(hs_ref[...], w_ref[...])


def _proj(prompt, sample, w_t, layer, n_main, tm, name):
    modulated = not isinstance(prompt, tuple)
    m = (prompt if modulated else prompt[0]).shape[0]
    ms = (sample if modulated else sample[0]).shape[0]
    n_tail = w_t.shape[1] - n_main
    w_tail = jnp.pad(w_t[layer, n_main:, :], ((0, LANES - n_tail), (0, 0)))
    tn = 512
    n_i = m // tm
    const2 = lambda i, j: (0, 0)
    const3 = lambda i, j: (0, 0, 0)
    last_tile_col = lambda i, j: (0, jnp.where(i == n_i - 1, j, 0))
    if modulated:
        row_specs = [pl.BlockSpec((tm, D_MODEL), lambda i, j: (i, 0)), pl.BlockSpec((ms, D_MODEL), const2)]
        rows, scratch = (prompt, sample), []
    else:
        tiles_per_group = n_i // prompt[1].shape[0]
        mod_spec = pl.BlockSpec((None, 1, D_MODEL), lambda i, j: (i // tiles_per_group, 0, 0))
        row_specs = [pl.BlockSpec((tm, D_MODEL), lambda i, j: (i, 0), pipeline_mode=pl.Buffered(1)),
                     mod_spec, mod_spec, pl.BlockSpec((ms, D_MODEL), const2),
                     pl.BlockSpec((None, ms, D_MODEL), const3), pl.BlockSpec((None, ms, D_MODEL), const3)]
        rows = (*prompt, *sample)
        scratch = [pltpu.VMEM((tm, D_MODEL), BF16), pltpu.VMEM((ms, D_MODEL), BF16)]
    return pl.pallas_call(
        functools.partial(_proj_kernel, n_i=n_i, modulated=modulated),
        grid=(n_i, n_main // tn),
        in_specs=row_specs + [
            pl.BlockSpec((None, tn, D_MODEL), lambda i, j: (layer, j, 0)),
            pl.BlockSpec((LANES, D_MODEL), const2),
        ],
        out_specs=[
            pl.BlockSpec((tm, tn), lambda i, j: (i, j)),
            pl.BlockSpec((tm, LANES), lambda i, j: (i, 0)),
            pl.BlockSpec((ms, tn), last_tile_col),
            pl.BlockSpec((ms, LANES), const2),
        ],
        out_shape=[jax.ShapeDtypeStruct((m, n_main), F32), jax.ShapeDtypeStruct((m, LANES), F32),
                   jax.ShapeDtypeStruct((ms, n_main), F32), jax.ShapeDtypeStruct((ms, LANES), F32)],
        scratch_shapes=scratch,
        compiler_params=_params("arbitrary", "arbitrary"),
        name=name,
    )(*rows, w_t, w_tail)


def _out_ln_kernel(*refs, n_i, with_next):
    if with_next:
        (a_ref, w_ref, x_ref, gt_ref, sc_ref, sh_ref, as_ref, xs_ref, gts_ref, scs_ref, shs_ref, g_ref, b_ref,
         o_ref, os_ref, h_ref, hs_ref) = refs
    else:
        a_ref, w_ref, x_ref, gt_ref, as_ref, xs_ref, gts_ref, g_ref, b_ref, o_ref, os_ref = refs

    def deep_norm(a, x, gate):
        y = DN_ALPHA * x + gate * jnp.dot(a, w_ref[...], preferred_element_type=F32)
        mu = jnp.mean(y, axis=-1, keepdims=True)
        yc = y - mu
        var = jnp.mean(yc * yc, axis=-1, keepdims=True)
        return yc * lax.rsqrt(var + LN_EPS) * g_ref[...] + b_ref[...]

    o_ref[...] = deep_norm(a_ref[...], x_ref[...], gt_ref[...])
    if with_next:
        h_ref[...] = _modulate(o_ref, sc_ref, sh_ref)

    @pl.when(pl.program_id(0) == n_i - 1)
    def _():
        os_ref[...] = deep_norm(as_ref[...], xs_ref[...], gts_ref[...])
        if with_next:
            hs_ref[...] = _modulate(os_ref, scs_ref, shs_ref)


def _out_ln(prompt, sample, w, w_layer, ln_g, ln_b, ln_layer, tm, name):
    with_next = len(prompt) == 5
    a, x = prompt[:2]
    m, kdim = a.shape
    ms = sample[0].shape[0]
    n_i = m // tm
    tiles_per_group = n_i // prompt[2].shape[0]
    rows_p = lambda width: pl.BlockSpec((tm, width), lambda i: (i, 0))
    rows_s = lambda width: pl.BlockSpec((ms, width), lambda i: (0, 0))
    mod_p = pl.BlockSpec((None, 1, D_MODEL), lambda i: (i // tiles_per_group, 0, 0))
    mod_s = pl.BlockSpec((None, ms, D_MODEL), lambda i: (0, 0, 0))
    vec_spec = pl.BlockSpec((None, 1, D_MODEL), lambda i: (ln_layer, 0, 0))
    n_mod = len(prompt) - 2
    out_specs = [rows_p(D_MODEL), rows_s(D_MODEL)]
    out_shape = [jax.ShapeDtypeStruct((m, D_MODEL), F32), jax.ShapeDtypeStruct((ms, D_MODEL), F32)]
    if with_next:
        out_specs += [rows_p(D_MODEL), rows_s(D_MODEL)]
        out_shape += [jax.ShapeDtypeStruct((m, D_MODEL), BF16), jax.ShapeDtypeStruct((ms, D_MODEL), BF16)]
    return pl.pallas_call(
        functools.partial(_out_ln_kernel, n_i=n_i, with_next=with_next),
        grid=(n_i,),
        in_specs=[rows_p(kdim),
                  pl.BlockSpec((None, kdim, D_MODEL), lambda i: (w_layer, 0, 0), pipeline_mode=pl.Buffered(1)),
                  rows_p(D_MODEL)] + [mod_p] * n_mod
        + [rows_s(kdim), rows_s(D_MODEL)] + [mod_s] * n_mod + [vec_spec, vec_spec],
        out_specs=out_specs,
        out_shape=out_shape,
        compiler_params=_params("arbitrary"),
        name=name,
    )(a, w, x, *prompt[2:], *sample, ln_g.reshape(-1, 1, D_MODEL), ln_b.reshape(-1, 1, D_MODEL))


def _ffn_in_kernel(h_ref, hs_ref, wa_ref, wu_ref, o_ref, os_ref, *, n_i):
    wa = wa_ref[...].astype(BF16)
    wu = wu_ref[...].astype(BF16)

    def swiglu_in(h):
        return (_silu(_dot(h, wa)) * _dot(h, wu)).astype(BF16)

    o_ref[...] = swiglu_in(h_ref[...])

    @pl.when(pl.program_id(0) == n_i - 1)
    def _():
        os_ref[...] = swiglu_in(hs_ref[...])


def _ffn_in(h, hs, w, layer, tm, name):
    m, ms = h.shape[0], hs.shape[0]
    tn = 512
    nj = D_FF // tn
    n_i = m // tm
    return pl.pallas_call(
        functools.partial(_ffn_in_kernel, n_i=n_i),
        grid=(n_i, nj),
        in_specs=[
            pl.BlockSpec((tm, D_MODEL), lambda i, j: (i, 0)),
            pl.BlockSpec((ms, D_MODEL), lambda i, j: (0, 0)),
            pl.BlockSpec((None, D_MODEL, tn), lambda i, j: (layer, 0, j)),
            pl.BlockSpec((None, D_MODEL, tn), lambda i, j: (layer, 0, j + nj)),
        ],
        out_specs=[
            pl.BlockSpec((tm, tn), lambda i, j: (i, j)),
            pl.BlockSpec((ms, tn), lambda i, j: (0, jnp.where(i == n_i - 1, j, 0))),
        ],
        out_shape=[jax.ShapeDtypeStruct((m, D_FF), BF16), jax.ShapeDtypeStruct((ms, D_FF), BF16)],
        compiler_params=_params("arbitrary", "arbitrary"),
        name=name,
    )(h, hs, w, w)


def _gla_kernel(*refs, chunk, t_valid, has_s0, n_cast):
    n_in = 8 + has_s0
    q_ref, k_ref, v_ref, r_ref, a_ref, wa_ref, ba_ref, g_ref = refs[:8]
    s0_ref = refs[8] if has_s0 else None
    o_ref, st_ref = refs[n_in + n_cast:n_in + n_cast + 2]
    for src, dst in zip(refs[n_in:n_in + n_cast], refs[n_in + n_cast + 2:]):
        dst[...] = src[...].astype(BF16)
    sub = GLA_SUB
    n_sub = chunk // sub
    anchor = sub // 2 - 1

    @pl.when(pl.program_id(1) == 0)
    def _():
        if has_s0:
            st_ref[...] = s0_ref[...]
        else:
            st_ref[...] = jnp.zeros(st_ref.shape, F32)

    row = lax.broadcasted_iota(jnp.int32, (chunk, 1), 0)
    tri = _ind(lax.broadcasted_iota(jnp.int32, (chunk, chunk), 1)
               <= lax.broadcasted_iota(jnp.int32, (chunk, chunk), 0), BF16)
    ones = jnp.ones((chunk, LANES), BF16)
    causal = (lax.broadcasted_iota(jnp.int32, (sub, sub), 1)
              <= lax.broadcasted_iota(jnp.int32, (sub, sub), 0))
    heads = range(GLA_HEADS)
    dk = [slice(h * GLA_DK, (h + 1) * GLA_DK) for h in heads]
    dv = [slice(h * GLA_DV, (h + 1) * GLA_DV) for h in heads]
    subs = [slice(s * sub, (s + 1) * sub) for s in range(n_sub)]

    z = _dot(a_ref[...], wa_ref[...]) + ba_ref[...]
    lb = (jnp.minimum(z, 0.0) - jnp.log1p(jnp.exp(-jnp.abs(z)))) * (1.0 / GLA_TAU)
    k = k_ref[...]
    if t_valid < chunk:
        lb = jnp.where(row < t_valid, lb, 0.0)
        k = jnp.where(row < t_valid, k, 0.0)
    q = q_ref[...] * (GLA_DK ** -0.5)
    vb = v_ref[...].astype(BF16)
    lb3 = _split3(lb)
    b = sum(jnp.dot(tri, t, preferred_element_type=F32) for t in lb3)
    dec = jnp.exp(sum(lax.dot_general(t, ones, (((0,), (0,)), ((), ())), preferred_element_type=F32)
                      for t in lb3))
    b_last = b[chunk - 1:chunk, :]
    q_in = (q * jnp.exp(b)).astype(BF16)
    k_out = (k * jnp.exp(b_last - b)).astype(BF16)
    q_diag, k_diag, q_off, k_off = [], [], [], []
    for s, rs in enumerate(subs):
        b_mid = b[s * sub + anchor:s * sub + anchor + 1, :]
        q_diag.append((q[rs] * jnp.exp(b[rs] - b_mid)).astype(BF16))
        k_diag.append((k[rs] * jnp.exp(b_mid - b[rs])).astype(BF16))
        if s > 0:
            prev = slice(0, s * sub)
            b_in = b[s * sub - 1:s * sub, :]
            q_off.append((q[rs] * jnp.exp(b[rs] - b_in)).astype(BF16))
            k_off.append((k[prev] * jnp.exp(b_in - b[prev])).astype(BF16))

    s_old = [st_ref[0, h] for h in heads]
    o_inter = [_dot(q_in[:, dk[h]], s_old[h]) for h in heads]
    a_diag = [[_dot_nt(q_diag[s][:, dk[h]], k_diag[s][:, dk[h]]) for h in heads] for s in range(n_sub)]
    a_off = [[_dot_nt(q_off[s][:, dk[h]], k_off[s][:, dk[h]]) for h in heads] for s in range(n_sub - 1)]
    a_diag = [[jnp.where(causal, a, 0.0).astype(BF16) for a in row_] for row_ in a_diag]
    a_off = [[a.astype(BF16) for a in row_] for row_ in a_off]
    o_intra = []
    for s, rs in enumerate(subs):
        o_s = [_dot(a_diag[s][h], vb[rs, dv[h]]) for h in heads]
        if s > 0:
            o_s = [o_s[h] + _dot(a_off[s - 1][h], vb[0:s * sub, dv[h]]) for h in heads]
        o_intra.append(o_s)
    kv = [_dot_tn(k_out[:, dk[h]], vb[:, dv[h]]) for h in heads]
    for h in heads:
        st_ref[0, h] = s_old[h] * jnp.tile(dec[dk[h], :], (1, GLA_DV // LANES)) + kv[h]
    for h in heads:
        o = o_inter[h] + jnp.concatenate([o_intra[s][h] for s in range(n_sub)], axis=0)
        o = o * lax.rsqrt(jnp.mean(o * o, axis=-1, keepdims=True) + LN_EPS)
        o_ref[:, dv[h]] = (o * g_ref[:, dv[h]] * _silu(r_ref[:, dv[h]])).astype(BF16)


def _gla(main, tail, w_alpha, b_alpha, norm_g, s0, batch, seq, chunk, t_valid, to_bf16=()):
    n_chunks = seq // chunk
    qk_w = GLA_HEADS * GLA_DK
    v_w = GLA_HEADS * GLA_DV
    wa = jnp.pad(w_alpha, ((0, LANES - GLA_RANK), (0, 0)))
    row_map = lambda b, c: (b * n_chunks + c, 0)
    const = lambda b, c: (0, 0)
    in_specs = [
        pl.BlockSpec((chunk, qk_w), row_map),
        pl.BlockSpec((chunk, qk_w), lambda b, c: (b * n_chunks + c, 1)),
        pl.BlockSpec((chunk, v_w), lambda b, c: (b * n_chunks + c, 1)),
        pl.BlockSpec((chunk, v_w), lambda b, c: (b * n_chunks + c, 2)),
        pl.BlockSpec((chunk, LANES), row_map),
        pl.BlockSpec((LANES, qk_w), const),
        pl.BlockSpec((1, qk_w), const),
        pl.BlockSpec((1, v_w), const),
    ]
    args = [main, main, main, main, tail, wa, b_alpha.reshape(1, qk_w), norm_g.reshape(1, v_w)]
    state_spec = pl.BlockSpec((1, GLA_HEADS, GLA_DK, GLA_DV), lambda b, c: (b, 0, 0, 0))
    if s0 is not None:
        in_specs.append(state_spec)
        args.append(s0)
    n_steps = batch * n_chunks
    cast_specs = [pl.BlockSpec((a.shape[0] // n_steps, a.shape[1]), row_map) for a in to_bf16]
    assert all(a.shape[0] % (n_steps * BF16_ROWS) == 0 for a in to_bf16)
    return pl.pallas_call(
        functools.partial(_gla_kernel, chunk=chunk, t_valid=t_valid, has_s0=s0 is not None,
                          n_cast=len(to_bf16)),
        grid=(batch, n_chunks),
        in_specs=in_specs + cast_specs,
        out_specs=[pl.BlockSpec((chunk, v_w), row_map), state_spec] + cast_specs,
        out_shape=[jax.ShapeDtypeStruct((batch * seq, v_w), BF16),
                   jax.ShapeDtypeStruct((batch, GLA_HEADS, GLA_DK, GLA_DV), F32)]
        + [jax.ShapeDtypeStruct(a.shape, BF16) for a in to_bf16],
        compiler_params=_params("parallel", "arbitrary"),
        name="gla_prompt" if s0 is None else "gla_sample",
    )(*args, *to_bf16)


def _select_blocks(imp_t, blk, cur):
    n = imp_t.shape[0]
    forced = (blk == 0) | ((blk >= cur - 1) & (blk <= cur))
    score = jnp.where(blk > cur, -jnp.inf, jnp.where(forced, jnp.inf, imp_t))
    rank = jnp.zeros(score.shape, F32)
    for i in range(n):
        si = score[i:i + 1, :]
        rank = rank + jnp.where(blk > i, _ind(si >= score), _ind(si > score))
    return (rank < NSA_TOPK) & (score > -jnp.inf)


def _nsa_prompt_kernel(pt_ref, q_ref, kc_ref, vc_ref, ks_ref, vs_ref, kw_ref, vw_ref, gt_ref, bg_ref, *refs,
                       seq, tq, tk, pages_per_step):
    page_refs = refs[:pages_per_step]
    (o_ref, means_ref, kcm, vcm, ksb, vst, kwb, vwt, qs_ref, sel_ref,
     ms_ref, accs_ref, mw_ref, accw_ref) = refs[pages_per_step:]
    qi = pl.program_id(2)
    n_blk = seq // NSA_BLOCK
    n_kt = seq // tk
    blk_per_kt = tk // NSA_BLOCK

    @pl.when(qi == 0)
    def _():
        kcm[...] = jnp.sum(kc_ref[...].reshape(n_blk, NSA_BLOCK, NSA_HD), axis=1) * (1.0 / NSA_BLOCK)
        vcm[...] = jnp.sum(vc_ref[...].reshape(n_blk, NSA_BLOCK, NSA_HD), axis=1) * (1.0 / NSA_BLOCK)
        ksb[...] = ks_ref[...].astype(BF16)
        kwb[...] = kw_ref[...].astype(BF16)
        ones_rows = _ind(lax.broadcasted_iota(jnp.int32, (BF16_ROWS, tk), 0) == 0, BF16)
        for kt in range(n_kt):
            rows = slice(kt * tk, (kt + 1) * tk)
            vst[kt] = jnp.concatenate([vs_ref[rows, :].T.astype(BF16), ones_rows], axis=0)
            vwt[kt] = jnp.concatenate([vw_ref[rows, :].T.astype(BF16), ones_rows], axis=0)

    t0 = qi * tq
    q = q_ref[...] * (NSA_HD ** -0.5 * LOG2_E)
    qs_ref[...] = jnp.concatenate(
        [q[:, g * NSA_HD:(g + 1) * NSA_HD] for g in range(NSA_GROUP)], axis=0).astype(BF16)
    qs = qs_ref[...]
    t_row = t0 + lax.broadcasted_iota(jnp.int32, (1, tq), 1)
    t_row_g = jnp.concatenate([t_row] * NSA_GROUP, axis=1)

    kc_b = kcm[...].astype(BF16)
    blk_col = lax.broadcasted_iota(jnp.int32, (n_blk, 1), 0)
    p_t = _masked_softmax2(_dot_nt(kc_b, qs), (blk_col + 1) * NSA_BLOCK - 1 <= t_row_g, axis=0)
    o_c = _dot_tn(vcm[...], p_t)
    imp_t = sum(p_t[:, g * tq:(g + 1) * tq] for g in range(NSA_GROUP))
    sel_ref[...] = jnp.where(_select_blocks(imp_t, blk_col, t_row // NSA_BLOCK), 0.0, NEG_BIG)

    n_loc = lax.broadcasted_iota(jnp.int32, (tk, tq), 0)
    t_loc = lax.broadcasted_iota(jnp.int32, (tk, tq), 1)
    on_lanes = lambda a: jnp.concatenate([a] * NSA_GROUP, axis=1)
    causal = on_lanes(jnp.where(n_loc <= t_loc, 0.0, NEG_BIG))
    far = on_lanes(jnp.where(n_loc > t_loc, 0.0, NEG_BIG))

    def sel_bias(kt):
        return on_lanes(jnp.concatenate(
            [jnp.broadcast_to(sel_ref[pl.ds(kt * blk_per_kt + j, 1), :], (NSA_BLOCK, tq))
             for j in range(blk_per_kt)], axis=0))

    sel_state, win_state = (ms_ref, accs_ref), (mw_ref, accw_ref)
    for m_ref, acc_ref in (sel_state, win_state):
        m_ref[...] = jnp.full(m_ref.shape, NEG_BIG, F32)
        acc_ref[...] = jnp.zeros(acc_ref.shape, F32)

    def steps(*work):
        s = [_dot_nt(k_ref[pl.ds(pl.multiple_of(kt * tk, tk), tk), :], qs_ref[...]) + bias
             for _, k_ref, _, kt, bias in work]
        m_old = [state[0][...] for state, *_ in work]
        m_new = [jnp.maximum(mo, jnp.max(si, axis=0, keepdims=True)) for mo, si in zip(m_old, s)]
        p = [jnp.exp2(si - mn).astype(BF16) for si, mn in zip(s, m_new)]
        pv = [_dot(vt_ref[kt], pi) for (_, _, vt_ref, kt, _), pi in zip(work, p)]
        for (state, *_), mo, mn, pvi in zip(work, m_old, m_new, pv):
            state[1][...] = jnp.exp2(mo - mn) * state[1][...] + pvi
            state[0][...] = mn

    def result(acc_ref):
        acc = acc_ref[...]
        return acc[:NSA_HD] / jnp.maximum(acc[NSA_HD:NSA_HD + 1], TINY)

    steps((sel_state, ksb, vst, qi, sel_bias(qi) + causal), (win_state, kwb, vwt, qi, causal))

    @pl.when(qi >= 1)
    def _():
        steps((sel_state, ksb, vst, qi - 1, sel_bias(qi - 1)), (win_state, kwb, vwt, qi - 1, far))

    def sel_body(i, carry):
        kt = qi - 2 - i
        steps((sel_state, ksb, vst, kt, sel_bias(kt)))
        return carry

    lax.fori_loop(0, qi - 1, sel_body, 0)

    gates = jax.nn.sigmoid(gt_ref[...] + bg_ref[...])
    g_c, g_s, g_w = (jnp.concatenate([gates[3 * g + br:3 * g + br + 1, :] for g in range(NSA_GROUP)], axis=1)
                     for br in range(3))
    o_t = g_c * o_c + g_s * result(accs_ref) + g_w * result(accw_ref)
    for g in range(NSA_GROUP):
        o_ref[:, g * NSA_HD:(g + 1) * NSA_HD] = o_t[:, g * tq:(g + 1) * tq].T.astype(BF16)

    for i, x_ref in enumerate(page_refs):
        x = x_ref[0].reshape(PAGE // NSA_BLOCK, NSA_BLOCK, KV_ROWS, NSA_HD)
        means_ref[0, i] = jnp.sum(x, axis=1) * (1.0 / NSA_BLOCK)


def _nsa_prompt(main, gates, b_gate, batch, seq, cache_cmp, page_table):
    tq = tk = NSA_WINDOW
    assert seq % tk == 0
    vt_rows = NSA_HD + BF16_ROWS
    nq = seq // tq
    q_lanes = NSA_GROUP * NSA_HD
    col0 = NSA_Q_W // NSA_HD
    seqs, n_pages = page_table.shape
    n_steps = batch * NSA_KV * nq
    pages_per_step = seqs * n_pages // n_steps
    groups = n_pages // pages_per_step
    assert pages_per_step * n_steps == seqs * n_pages and groups * pages_per_step == n_pages
    per_page = PAGE // NSA_BLOCK

    def kv_spec(branch, part):
        off = col0 + (2 * branch + part) * NSA_KV
        return pl.BlockSpec((seq, NSA_HD), lambda b, kv, qi, pt: (b, off + kv))

    def step_of(b, kv, qi):
        return (b * NSA_KV + kv) * nq + qi

    def page_spec(i):
        def index(b, kv, qi, pt):
            s = step_of(b, kv, qi)
            return (pt[s // groups, (s % groups) * pages_per_step + i], 0, 0)
        return pl.BlockSpec((1, PAGE * KV_ROWS, NSA_HD), index)

    def means_index(b, kv, qi, pt):
        s = step_of(b, kv, qi)
        return (s // groups, s % groups, 0, 0, 0)

    g_w = 3 * NSA_GROUP
    g_rows = NSA_GROUP * tq
    grid_spec = pltpu.PrefetchScalarGridSpec(
        num_scalar_prefetch=1,
        grid=(batch, NSA_KV, nq),
        in_specs=[pl.BlockSpec((tq, q_lanes), lambda b, kv, qi, pt: (b * nq + qi, kv))]
        + [kv_spec(br, part) for br in range(3) for part in range(2)]
        + [pl.BlockSpec((None, g_w, tq), lambda b, kv, qi, pt: (kv, 0, b * nq + qi)),
           pl.BlockSpec((None, g_w, 1), lambda b, kv, qi, pt: (kv, 0, 0))]
        + [page_spec(i) for i in range(pages_per_step)],
        out_specs=[pl.BlockSpec((tq, q_lanes), lambda b, kv, qi, pt: (b * nq + qi, kv)),
                   pl.BlockSpec((1, pages_per_step, per_page, KV_ROWS, NSA_HD), means_index)],
        scratch_shapes=[
            pltpu.VMEM((seq // NSA_BLOCK, NSA_HD), F32),
            pltpu.VMEM((seq // NSA_BLOCK, NSA_HD), F32),
            pltpu.VMEM((seq, NSA_HD), BF16),
            pltpu.VMEM((seq // tk, vt_rows, tk), BF16),
            pltpu.VMEM((seq, NSA_HD), BF16),
            pltpu.VMEM((seq // tk, vt_rows, tk), BF16),
            pltpu.VMEM((g_rows, NSA_HD), BF16),
            pltpu.VMEM((seq // NSA_BLOCK, tq), F32),
            pltpu.VMEM((1, g_rows), F32),
            pltpu.VMEM((vt_rows, g_rows), F32),
            pltpu.VMEM((1, g_rows), F32),
            pltpu.VMEM((vt_rows, g_rows), F32),
        ],
    )
    return pl.pallas_call(
        functools.partial(_nsa_prompt_kernel, seq=seq, tq=tq, tk=tk, pages_per_step=pages_per_step),
        grid_spec=grid_spec,
        out_shape=[jax.ShapeDtypeStruct((batch * seq, NSA_Q_W), BF16),
                   jax.ShapeDtypeStruct((seqs, n_pages, per_page, KV_ROWS, NSA_HD), F32)],
        compiler_params=_params("arbitrary", "arbitrary", "arbitrary"),
        name="nsa_prompt",
    )(page_table, main, main, main, main, main, main, main, gates, b_gate, *([cache_cmp] * pages_per_step))


def _kv_rows_out_kernel(c_ref, s_ref, w_ref, oc_ref, os_ref, ow_ref, *, rows, n_c):
    def spread(src, dst):
        for r in range(KV_ROWS):
            dst[pl.ds(r, rows, stride=KV_ROWS), :] = src[:, r * NSA_HD:(r + 1) * NSA_HD]

    spread(c_ref, oc_ref)
    spread(s_ref, os_ref)

    @pl.when(pl.program_id(1) == n_c - 1)
    def _():
        spread(w_ref, ow_ref)


def _kv_rows_out(main, batch, seq, keep):
    rows = keep
    n_c = seq // rows
    col0 = NSA_Q_W // NSA_KV_W
    tile = lambda br: pl.BlockSpec((rows, NSA_KV_W), lambda b, c: (b * n_c + c, col0 + br))
    out_tile = pl.BlockSpec((rows * KV_ROWS, NSA_HD), lambda b, c: (b * n_c + c, 0))
    return pl.pallas_call(
        functools.partial(_kv_rows_out_kernel, rows=rows, n_c=n_c),
        grid=(batch, n_c),
        in_specs=[tile(0), tile(1),
                  pl.BlockSpec((rows, NSA_KV_W), lambda b, c: (b * n_c + n_c - 1, col0 + 2))],
        out_specs=[out_tile, out_tile, pl.BlockSpec((rows * KV_ROWS, NSA_HD), lambda b, c: (b, 0))],
        out_shape=[jax.ShapeDtypeStruct((batch * seq * KV_ROWS, NSA_HD), F32)] * 2
        + [jax.ShapeDtypeStruct((batch * keep * KV_ROWS, NSA_HD), F32)],
        compiler_params=_params("arbitrary", "arbitrary"),
        name="nsa_kv_rows_out",
    )(main, main, main)


def _page_specs(n):
    return [pl.BlockSpec((1, PAGE * KV_ROWS, NSA_HD), lambda b, p, pt, i=i: (pt[b, p * n + i], 0, 0))
            for i in range(n)]


def _nsa_sample_kernel(pt_ref, q_ref, cm_ref, *refs, past, n_new, n_steps, per_step):
    pg_refs = refs[:per_step]
    (kvn_ref, wb_ref, kwn_ref, gt_ref, bg_ref, o_ref,
     qr_ref, sel_ref, m_ref, l_ref, acc_ref, oc_ref, ow_ref) = refs[per_step:]
    p = pl.program_id(1)
    k_w = NSA_KV * NSA_HD
    cols = NSA_KV * n_new * NSA_GROUP
    per_kv = n_new * NSA_GROUP
    n_cmp = past // NSA_BLOCK
    n_sel = -(-(past + n_new) // NSA_BLOCK)
    n_sel_pad = sel_ref.shape[0]
    lane = lax.broadcasted_iota(jnp.int32, (1, cols), 1)
    q_pos = past + (lane // NSA_GROUP) % n_new

    def heads_on_lanes(ref, n, part):
        return jnp.concatenate(
            [ref[0, pl.ds(part * NSA_KV + kv, n, stride=KV_ROWS), :] for kv in range(NSA_KV)], axis=1)

    def scores(keys):
        return _dot_nt(keys, qr_ref[...])

    def own_head(x_t):
        out = jnp.zeros((NSA_HD, cols), F32)
        for kv in range(NSA_KV):
            mine = (lane // per_kv) == kv
            out = out + jnp.where(mine, x_t[kv * NSA_HD:(kv + 1) * NSA_HD, :], 0.0)
        return out

    def attend_once(keys, vals, mask):
        p_t = _masked_softmax2(scores(keys), mask, axis=0)
        return own_head(_dot_tn(vals, p_t))

    def online(keys, vals, bias):
        s = scores(keys) + bias
        m_old = m_ref[...]
        m_new = jnp.maximum(m_old, jnp.max(s, axis=0, keepdims=True))
        p_t = jnp.exp2(s - m_new)
        alpha = jnp.exp2(m_old - m_new)
        l_ref[...] = alpha * l_ref[...] + jnp.sum(p_t, axis=0, keepdims=True)
        acc_ref[...] = alpha * acc_ref[...] + own_head(_dot_tn(vals, p_t))
        m_ref[...] = m_new

    @pl.when(p == 0)
    def _():
        q = q_ref[0] * (NSA_HD ** -0.5 * LOG2_E)
        r_kv = lax.broadcasted_iota(jnp.int32, (cols, k_w), 0) // per_kv
        c_kv = lax.broadcasted_iota(jnp.int32, (cols, k_w), 1) // NSA_HD
        qr_ref[...] = jnp.where(r_kv == c_kv, jnp.tile(q, (1, NSA_KV)), 0.0).astype(BF16)
        blk = lax.broadcasted_iota(jnp.int32, (n_cmp, 1), 0)
        p_t = _masked_softmax2(scores(heads_on_lanes(cm_ref, n_cmp, 0)),
                               (blk + 1) * NSA_BLOCK - 1 <= q_pos, axis=0)
        oc_ref[...] = own_head(_dot_tn(heads_on_lanes(cm_ref, n_cmp, 1), p_t))
        same = _ind(lax.broadcasted_iota(jnp.int32, (cols, cols), 0) // NSA_GROUP
                    == lax.broadcasted_iota(jnp.int32, (cols, cols), 1) // NSA_GROUP, BF16)
        imp = sum(jnp.dot(t, same, preferred_element_type=F32) for t in _split3(p_t))
        imp = jnp.concatenate([imp, jnp.zeros((n_sel_pad - n_cmp, cols), F32)], axis=0)
        blk_s = lax.broadcasted_iota(jnp.int32, (n_sel_pad, 1), 0)
        cur = q_pos // NSA_BLOCK
        forced = (blk_s == 0) | ((blk_s >= cur - 1) & (blk_s <= cur))
        score = jnp.where((blk_s > cur) | (blk_s >= n_sel), -jnp.inf, jnp.where(forced, jnp.inf, imp))
        blk_f = blk_s.astype(F32)

        def pick(_, carry):
            left, bias = carry
            top = jnp.max(left, axis=0, keepdims=True)
            first = jnp.min(jnp.where(left == top, blk_f, float(n_sel_pad)), axis=0, keepdims=True)
            hit = (blk_f == first) & (top > -jnp.inf)
            return jnp.where(hit, -jnp.inf, left), jnp.where(hit, 0.0, bias)

        _, bias = lax.fori_loop(0, NSA_TOPK, pick, (score, jnp.full(score.shape, NEG_BIG, F32)))
        sel_ref[...] = bias
        n_wb = wb_ref.shape[1] // KV_ROWS
        kw_all = jnp.concatenate([heads_on_lanes(wb_ref, n_wb, 0), kwn_ref[0][:, :k_w]], axis=0)
        vw_all = jnp.concatenate([heads_on_lanes(wb_ref, n_wb, 1), kwn_ref[0][:, k_w:]], axis=0)
        n_w = kw_all.shape[0]
        w_row = lax.broadcasted_iota(jnp.int32, (n_w, 1), 0)
        w_pos = past - n_wb + w_row
        d = q_pos - w_pos
        ow_ref[...] = attend_once(kw_all, vw_all,
                                  (d >= 0) & (d < NSA_WINDOW) & (w_pos >= 0) & (w_row < n_wb + n_new))
        m_ref[...] = jnp.full(m_ref.shape, NEG_BIG, F32)
        l_ref[...] = jnp.zeros(l_ref.shape, F32)
        acc_ref[...] = jnp.zeros(acc_ref.shape, F32)

    def page_part(part):
        return jnp.concatenate([heads_on_lanes(pg_ref, PAGE, part) for pg_ref in pg_refs], axis=0)

    blk_per_step = per_step * PAGE // NSA_BLOCK
    online(page_part(0), page_part(1), jnp.concatenate(
        [jnp.broadcast_to(sel_ref[pl.ds(p * blk_per_step + j, 1), :], (NSA_BLOCK, cols))
         for j in range(blk_per_step)], axis=0))

    @pl.when(p == n_steps - 1)
    def _():
        kvn = kvn_ref[0]
        n_row = lax.broadcasted_iota(jnp.int32, (kvn.shape[0], 1), 0)
        bias_new = jnp.where((n_row < n_new) & (past + n_row <= q_pos), 0.0, NEG_BIG)
        online(kvn[:, :k_w], kvn[:, k_w:], bias_new + sel_ref[pl.ds(past // NSA_BLOCK, 1), :])
        o_s = acc_ref[...] / jnp.maximum(l_ref[...], TINY)
        gates = jax.nn.sigmoid(gt_ref[0] + bg_ref[...])
        o_t = gates[0:1, :] * oc_ref[...] + gates[1:2, :] * o_s + gates[2:3, :] * ow_ref[...]
        eye = _ind(lax.broadcasted_iota(jnp.int32, (NSA_HD, NSA_HD), 0)
                   == lax.broadcasted_iota(jnp.int32, (NSA_HD, NSA_HD), 1), BF16)
        o_ref[0] = sum(lax.dot_general(t, eye, (((0,), (0,)), ((), ())), preferred_element_type=F32)
                       for t in _split3(o_t))


def _nsa_sample(q_rows, cmp_means, cache_slc, page_table, kv_new, win_buf, kw_new, gates, b_gate,
                past, n_new):
    seqs, n_pages = page_table.shape
    cols = q_rows.shape[1]
    cmp_rows = cmp_means.shape[1]
    n_sel_pad = -(-(-(-(past + n_new) // NSA_BLOCK)) // 8) * 8
    n_wb = win_buf.shape[1]
    pad_new = kv_new.shape[1]
    per_step = 16
    n_steps = n_pages // per_step
    seq_map = lambda b, p, pt: (b, 0, 0)
    return pl.pallas_call(
        functools.partial(_nsa_sample_kernel, past=past, n_new=n_new, n_steps=n_steps, per_step=per_step),
        grid_spec=pltpu.PrefetchScalarGridSpec(
            num_scalar_prefetch=1,
            grid=(seqs, n_steps),
            in_specs=[
                pl.BlockSpec((1, cols, NSA_HD), seq_map),
                pl.BlockSpec((1, cmp_rows, NSA_HD), seq_map),
                *_page_specs(per_step),
                pl.BlockSpec((1, pad_new, NSA_KV_W), seq_map),
                pl.BlockSpec((1, n_wb, NSA_HD), seq_map),
                pl.BlockSpec((1, pad_new, NSA_KV_W), seq_map),
                pl.BlockSpec((1, 3, cols), seq_map),
                pl.BlockSpec((3, cols), lambda b, p, pt: (0, 0)),
            ],
            out_specs=pl.BlockSpec((1, cols, NSA_HD), seq_map),
            scratch_shapes=[
                pltpu.VMEM((cols, NSA_KV * NSA_HD), BF16),
                pltpu.VMEM((n_sel_pad, cols), F32),
                pltpu.VMEM((1, cols), F32),
                pltpu.VMEM((1, cols), F32),
                pltpu.VMEM((NSA_HD, cols), F32),
                pltpu.VMEM((NSA_HD, cols), F32),
                pltpu.VMEM((NSA_HD, cols), F32),
            ],
        ),
        out_shape=jax.ShapeDtypeStruct((seqs, cols, NSA_HD), F32),
        compiler_params=_params("parallel", "arbitrary"),
        name="nsa_sample",
    )(page_table, q_rows, cmp_means, *([cache_slc] * per_step), kv_new, win_buf, kw_new, gates, b_gate)


def kernel(x_prompt, x_sample, c_prompt, c_sample, state_gla, cache_cmp_kv, cache_slc_kv, cache_win_kv,
           page_table, ada_w, ada_b, gla_w_in, gla_w_alpha, gla_b_alpha, gla_norm_g, gla_w_out, nsa_w_in,
           nsa_b_gate, nsa_w_out, ln_mix_g, ln_mix_b, ffn_w_in, ffn_w_out, ln_ffn_g, ln_ffn_b):
    batch, seq, _ = x_prompt.shape
    seqs, n_new, _ = x_sample.shape
    n_pages = page_table.shape[1]
    past = n_pages * PAGE
    rows_s = seqs * n_new
    tm_p = seq

    pad_rows = 16 - (batch + seqs)
    c_all = jnp.concatenate([c_prompt, c_sample, jnp.zeros((pad_rows, D_MODEL), F32)], axis=0)
    mod = _adaln(c_all, ada_w, ada_b).reshape(DEPTH, 16, 6, D_MODEL)

    def mods(layer):
        mp = [mod[layer, :batch, i].reshape(batch, 1, D_MODEL) for i in range(6)]
        ms = [jnp.repeat(mod[layer, batch:batch + seqs, i], n_new, axis=0).reshape(1, rows_s, D_MODEL)
              for i in range(6)]
        return mp, ms

    def mixer_out(o_p, o_s, xp, xs, layer, mp, ms, tag):
        return _out_ln((o_p, xp, mp[2], mp[4], mp[3]), (o_s, xs, ms[2], ms[4], ms[3]), w_out_mix[layer % 2],
                       layer // 2, ln_mix_g, ln_mix_b, layer, 512, tag)

    def ffn(hp, hs, xp, xs, layer, mp, ms, next_mods):
        act_p, act_s = _ffn_in(hp, hs, ffn_w_in, layer, tm_p, "ffn_in_%d" % layer)
        nxt_p, nxt_s = ((), ()) if next_mods is None else ((next_mods[0][1], next_mods[0][0]),
                                                            (next_mods[1][1], next_mods[1][0]))
        return _out_ln((act_p, xp, mp[5], *nxt_p), (act_s, xs, ms[5], *nxt_s), w_out_ffn, layer,
                       ln_ffn_g, ln_ffn_b, layer, 256, "ffn_out_%d" % layer)

    xp = x_prompt.reshape(batch * seq, D_MODEL)
    xs = x_sample.reshape(rows_s, D_MODEL)

    mp, ms = mods(0)
    main_p, tail_p, main_s, tail_s = _proj((xp, mp[1], mp[0]), (xs, ms[1], ms[0]), jnp.swapaxes(gla_w_in, 1, 2), 0,
                                           GLA_MAIN_W, tm_p, "gla_in")
    rows2d = lambda w: w.reshape(-1, w.shape[-1])
    gated_p, gla_state_p, *w_bf16 = _gla(main_p, tail_p, gla_w_alpha[0], gla_b_alpha[0], gla_norm_g[0], None,
                                         batch, seq, 256, 256,
                                         to_bf16=(rows2d(gla_w_out), rows2d(nsa_w_out), rows2d(ffn_w_out)))
    w_out_mix = (w_bf16[0].reshape(gla_w_out.shape), w_bf16[1].reshape(nsa_w_out.shape))
    w_out_ffn = w_bf16[2].reshape(ffn_w_out.shape)
    chunk_s = GLA_SUB

    def pad_steps(a):
        a = a.reshape(seqs, n_new, a.shape[-1])
        return jnp.pad(a, ((0, 0), (0, chunk_s - n_new), (0, 0))).reshape(seqs * chunk_s, a.shape[-1])

    gated_s, gla_state_s = _gla(pad_steps(main_s), pad_steps(tail_s), gla_w_alpha[0], gla_b_alpha[0],
                                gla_norm_g[0], state_gla[:, 0], seqs, chunk_s, chunk_s, n_new)
    gated_s = gated_s.reshape(seqs, chunk_s, -1)[:, :n_new].reshape(rows_s, -1)
    xp, xs, hp, hs = mixer_out(gated_p, gated_s, xp, xs, 0, mp, ms, "gla_out")
    xp, xs, hp, hs = ffn(hp, hs, xp, xs, 0, mp, ms, mods(1))

    mp, ms = mods(1)
    main_p, tail_p, main_s, tail_s = _proj(hp, hs, jnp.swapaxes(nsa_w_in, 1, 2), 0, NSA_MAIN_W, tm_p, "nsa_in")
    n_gate = 3 * NSA_HEADS
    g_w = 3 * NSA_GROUP
    gates_p = tail_p[:, :n_gate].reshape(batch * seq, NSA_KV, g_w).transpose(1, 2, 0)
    b_gate_p = nsa_b_gate[0].reshape(NSA_KV, g_w, 1)
    n_phys = cache_cmp_kv.shape[0]
    page_rows = lambda cache: cache.reshape(n_phys, PAGE * KV_ROWS, NSA_HD)
    o_p, cmp_means = _nsa_prompt(main_p, gates_p, b_gate_p, batch, seq, page_rows(cache_cmp_kv), page_table)

    kv_rows = lambda m, br: m[:, NSA_Q_W + br * NSA_KV_W:NSA_Q_W + (br + 1) * NSA_KV_W]
    kv_shape = (2, NSA_KV, NSA_HD)
    keep_p = min(NSA_WINDOW, seq)
    cmp_rows, slc_rows, win_rows = _kv_rows_out(main_p, batch, seq, keep_p)
    cmp_kv_p = cmp_rows.reshape(batch, 1, seq, *kv_shape)
    slc_kv_p = slc_rows.reshape(batch, 1, seq, *kv_shape)
    win_kv_p = win_rows.reshape(batch, 1, keep_p, *kv_shape)

    cmp_kv_s = kv_rows(main_s, 0).reshape(seqs, 1, n_new, *kv_shape)
    slc_kv_s = kv_rows(main_s, 1).reshape(seqs, 1, n_new, *kv_shape)
    kw_new = kv_rows(main_s, 2).reshape(seqs, n_new, NSA_KV_W)
    n_wb = cache_win_kv.shape[2]
    win_buf = cache_win_kv[:, 0].reshape(seqs, n_wb * KV_ROWS, NSA_HD)
    all_w = jnp.concatenate([win_buf, kw_new.reshape(seqs, n_new * KV_ROWS, NSA_HD)], axis=1)
    keep_s = min(NSA_WINDOW, n_wb + n_new)
    win_kv_s = all_w[:, (n_wb + n_new - keep_s) * KV_ROWS:].reshape(seqs, 1, keep_s, *kv_shape)

    cmp_means = cmp_means.reshape(seqs, past // NSA_BLOCK * KV_ROWS, NSA_HD)
    pad_new = 8
    pad8 = lambda a: jnp.pad(a, ((0, 0), (0, pad_new - n_new), (0, 0)))
    cols = NSA_KV * n_new * NSA_GROUP
    q_rows = (main_s[:, :NSA_Q_W].reshape(seqs, n_new, NSA_KV, NSA_GROUP, NSA_HD)
              .transpose(0, 2, 1, 3, 4).reshape(seqs, cols, NSA_HD))
    gates_s = (tail_s[:, :n_gate].reshape(seqs, n_new, NSA_KV, NSA_GROUP, 3)
               .transpose(0, 4, 2, 1, 3).reshape(seqs, 3, cols))
    b_gate_s = jnp.broadcast_to(nsa_b_gate[0].reshape(1, NSA_KV, NSA_GROUP, 3), (n_new, NSA_KV, NSA_GROUP, 3))
    b_gate_s = b_gate_s.transpose(3, 1, 0, 2).reshape(3, cols)
    o_s = _nsa_sample(q_rows, cmp_means, page_rows(cache_slc_kv), page_table,
                      pad8(slc_kv_s.reshape(seqs, n_new, NSA_KV_W)), win_buf, pad8(kw_new), gates_s, b_gate_s,
                      past, n_new)
    o_s = (o_s.reshape(seqs, NSA_KV, n_new, NSA_GROUP, NSA_HD).transpose(0, 2, 1, 3, 4)
           .reshape(rows_s, NSA_Q_W).astype(BF16))

    xp, xs, hp, hs = mixer_out(o_p, o_s, xp, xs, 1, mp, ms, "nsa_out")
    xp, xs = ffn(hp, hs, xp, xs, 1, mp, ms, None)

    return (xp.reshape(batch, seq, D_MODEL), xs.reshape(seqs, n_new, D_MODEL),
            gla_state_p[:, None], gla_state_s[:, None], cmp_kv_p, cmp_kv_s, slc_kv_p, slc_kv_s,
            win_kv_p, win_kv_s)
```

```python
import functools

import jax
import jax.numpy as jnp
from jax import lax
from jax.experimental import pallas as pl
from jax.experimental.pallas import tpu as pltpu

F32 = jnp.float32
BF16 = jnp.bfloat16

D_MODEL = 2048
DEPTH = 2
DN_ALPHA = (2 * DEPTH) ** 0.25
LN_EPS = 1e-5
D_FF = 5632
GLA_HEADS = 4
GLA_DK = 256
GLA_DV = 512
GLA_RANK = 16
GLA_TAU = 16.0
GLA_SUB = 32
GLA_MAIN_W = 2 * GLA_HEADS * GLA_DK + 2 * GLA_HEADS * GLA_DV
NSA_HEADS = 16
NSA_KV = 4
NSA_HD = 128
NSA_GROUP = NSA_HEADS // NSA_KV
NSA_BLOCK = 64
NSA_TOPK = 16
NSA_WINDOW = 512
NSA_Q_W = NSA_HEADS * NSA_HD
NSA_KV_W = 2 * NSA_KV * NSA_HD
NSA_MAIN_W = NSA_Q_W + 3 * NSA_KV_W
PAGE = 128
KV_ROWS = 2 * NSA_KV

LANES = 128
BF16_ROWS = 16
LOG2_E = 1.4426950408889634
V7X_VMEM_BYTES = 64 * 1024 * 1024
VMEM_LIMIT = V7X_VMEM_BYTES * 7 // 8

NEG_BIG = -1e30
TINY = float(jnp.finfo(jnp.float32).tiny)


def _params(*sem):
    return pltpu.CompilerParams(dimension_semantics=sem, vmem_limit_bytes=VMEM_LIMIT)


def _dot(a, b):
    return jnp.dot(a.astype(BF16), b.astype(BF16), preferred_element_type=F32)


def _dot_nt(a, b):
    return lax.dot_general(a.astype(BF16), b.astype(BF16), (((1,), (1,)), ((), ())),
                           preferred_element_type=F32)


def _dot_tn(a, b):
    return lax.dot_general(a.astype(BF16), b.astype(BF16), (((0,), (0,)), ((), ())),
                           preferred_element_type=F32)


def _split3(x):
    hi = x.astype(BF16)
    r1 = x - hi.astype(F32)
    mid = r1.astype(BF16)
    lo = (r1 - mid.astype(F32)).astype(BF16)
    return hi, mid, lo


def _silu(x):
    return x * jax.nn.sigmoid(x)


def _ind(cond, dtype=F32):
    return jnp.where(cond, 1.0, 0.0).astype(dtype)


def _masked_softmax2(s2, mask, axis):
    s2 = jnp.where(mask, s2, -jnp.inf)
    m = jnp.max(s2, axis=axis, keepdims=True)
    m = jnp.where(jnp.isfinite(m), m, 0.0)
    e = jnp.where(mask, jnp.exp2(s2 - m), 0.0)
    return e / jnp.maximum(jnp.sum(e, axis=axis, keepdims=True), TINY)


def _adaln_kernel(c_ref, w_ref, b_ref, o_ref):
    o_ref[...] = _dot(_silu(c_ref[...]), w_ref[...]) + b_ref[...]


def _adaln(c_all, ada_w, ada_b):
    rows = c_all.shape[0]
    n = ada_w.shape[2]
    tn = 1024
    return pl.pallas_call(
        _adaln_kernel,
        grid=(DEPTH, n // tn),
        in_specs=[
            pl.BlockSpec((rows, D_MODEL), lambda l, j: (0, 0)),
            pl.BlockSpec((None, D_MODEL, tn), lambda l, j: (l, 0, j)),
            pl.BlockSpec((None, 1, tn), lambda l, j: (l, 0, j)),
        ],
        out_specs=pl.BlockSpec((None, rows, tn), lambda l, j: (l, 0, j)),
        out_shape=jax.ShapeDtypeStruct((DEPTH, rows, n), F32),
        compiler_params=_params("parallel", "parallel"),
        name="adaln",
    )(c_all, ada_w, ada_b.reshape(DEPTH, 1, n))


def _modulate(x_ref, sc_ref, sh_ref):
    return (x_ref[...] * (1.0 + sc_ref[...]) + sh_ref[...]).astype(BF16)


def _proj_kernel(*refs, n_i, modulated):
    if modulated:
        h_ref, hs_ref, w_ref, wt_ref, o_ref, ot_ref, os_ref, ots_ref = refs
    else:
        (x_ref, sc_ref, sh_ref, xs_ref, scs_ref, shs_ref, w_ref, wt_ref,
         o_ref, ot_ref, os_ref, ots_ref, h_ref, hs_ref) = refs
    i, j = pl.program_id(0), pl.program_id(1)

    @pl.when(j == 0)
    def _():
        if not modulated:
            h_ref[...] = _modulate(x_ref, sc_ref, sh_ref)
        ot_ref[...] = _dot_nt(h_ref[...], wt_ref[...])

    o_ref[...] = _dot_nt(h_ref[...], w_ref[...])

    @pl.when(i == n_i - 1)
    def _():
        @pl.when(j == 0)
        def _():
            if not modulated:
                hs_ref[...] = _modulate(xs_ref, scs_ref, shs_ref)
            ots_ref[...] = _dot_nt(hs_ref[...], wt_ref[...])

        os_ref[...] = _dot_nt(hs_ref[...], w_ref[...])


def _proj(prompt, sample, w_t, layer, n_main, tm, name):
    modulated = not isinstance(prompt, tuple)
    m = (prompt if modulated else prompt[0]).shape[0]
    ms = (sample if modulated else sample[0]).shape[0]
    n_tail = w_t.shape[1] - n_main
    w_tail = jnp.pad(w_t[layer, n_main:, :], ((0, LANES - n_tail), (0, 0)))
    tn = 512
    n_i = m // tm
    const2 = lambda i, j: (0, 0)
    const3 = lambda i, j: (0, 0, 0)
    last_tile_col = lambda i, j: (0, jnp.where(i == n_i - 1, j, 0))
    if modulated:
        row_specs = [pl.BlockSpec((tm, D_MODEL), lambda i, j: (i, 0)), pl.BlockSpec((ms, D_MODEL), const2)]
        rows, scratch = (prompt, sample), []
    else:
        tiles_per_group = n_i // prompt[1].shape[0]
        mod_spec = pl.BlockSpec((None, 1, D_MODEL), lambda i, j: (i // tiles_per_group, 0, 0))
        row_specs = [pl.BlockSpec((tm, D_MODEL), lambda i, j: (i, 0), pipeline_mode=pl.Buffered(1)),
                     mod_spec, mod_spec, pl.BlockSpec((ms, D_MODEL), const2),
                     pl.BlockSpec((None, ms, D_MODEL), const3), pl.BlockSpec((None, ms, D_MODEL), const3)]
        rows = (*prompt, *sample)
        scratch = [pltpu.VMEM((tm, D_MODEL), BF16), pltpu.VMEM((ms, D_MODEL), BF16)]
    return pl.pallas_call(
        functools.partial(_proj_kernel, n_i=n_i, modulated=modulated),
        grid=(n_i, n_main // tn),
        in_specs=row_specs + [
            pl.BlockSpec((None, tn, D_MODEL), lambda i, j: (layer, j, 0)),
            pl.BlockSpec((LANES, D_MODEL), const2),
        ],
        out_specs=[
            pl.BlockSpec((tm, tn), lambda i, j: (i, j)),
            pl.BlockSpec((tm, LANES), lambda i, j: (i, 0)),
            pl.BlockSpec((ms, tn), last_tile_col),
            pl.BlockSpec((ms, LANES), const2),
        ],
        out_shape=[jax.ShapeDtypeStruct((m, n_main), F32), jax.ShapeDtypeStruct((m, LANES), F32),
                   jax.ShapeDtypeStruct((ms, n_main), F32), jax.ShapeDtypeStruct((ms, LANES), F32)],
        scratch_shapes=scratch,
        compiler_params=_params("arbitrary", "arbitrary"),
        name=name,
    )(*rows, w_t, w_tail)


def _out_ln_kernel(*refs, n_i, with_next):
    if with_next:
        (a_ref, w_ref, x_ref, gt_ref, sc_ref, sh_ref, as_ref, xs_ref, gts_ref, scs_ref, shs_ref, g_ref, b_ref,
         o_ref, os_ref, h_ref, hs_ref) = refs
    else:
        a_ref, w_ref, x_ref, gt_ref, as_ref, xs_ref, gts_ref, g_ref, b_ref, o_ref, os_ref = refs

    def deep_norm(a, x, gate):
        y = DN_ALPHA * x + gate * jnp.dot(a, w_ref[...], preferred_element_type=F32)
        mu = jnp.mean(y, axis=-1, keepdims=True)
        yc = y - mu
        var = jnp.mean(yc * yc, axis=-1, keepdims=True)
        return yc * lax.rsqrt(var + LN_EPS) * g_ref[...] + b_ref[...]

    o_ref[...] = deep_norm(a_ref[...], x_ref[...], gt_ref[...])
    if with_next:
        h_ref[...] = _modulate(o_ref, sc_ref, sh_ref)

    @pl.when(pl.program_id(0) == n_i - 1)
    def _():
        os_ref[...] = deep_norm(as_ref[...], xs_ref[...], gts_ref[...])
        if with_next:
            hs_ref[...] = _modulate(os_ref, scs_ref, shs_ref)


def _out_ln(prompt, sample, w, w_layer, ln_g, ln_b, ln_layer, tm, name):
    with_next = len(prompt) == 5
    a, x = prompt[:2]
    m, kdim = a.shape
    ms = sample[0].shape[0]
    n_i = m // tm
    tiles_per_group = n_i // prompt[2].shape[0]
    rows_p = lambda width: pl.BlockSpec((tm, width), lambda i: (i, 0))
    rows_s = lambda width: pl.BlockSpec((ms, width), lambda i: (0, 0))
    mod_p = pl.BlockSpec((None, 1, D_MODEL), lambda i: (i // tiles_per_group, 0, 0))
    mod_s = pl.BlockSpec((None, ms, D_MODEL), lambda i: (0, 0, 0))
    vec_spec = pl.BlockSpec((None, 1, D_MODEL), lambda i: (ln_layer, 0, 0))
    n_mod = len(prompt) - 2
    out_specs = [rows_p(D_MODEL), rows_s(D_MODEL)]
    out_shape = [jax.ShapeDtypeStruct((m, D_MODEL), F32), jax.ShapeDtypeStruct((ms, D_MODEL), F32)]
    if with_next:
        out_specs += [rows_p(D_MODEL), rows_s(D_MODEL)]
        out_shape += [jax.ShapeDtypeStruct((m, D_MODEL), BF16), jax.ShapeDtypeStruct((ms, D_MODEL), BF16)]
    return pl.pallas_call(
        functools.partial(_out_ln_kernel, n_i=n_i, with_next=with_next),
        grid=(n_i,),
        in_specs=[rows_p(kdim),
                  pl.BlockSpec((None, kdim, D_MODEL), lambda i: (w_layer, 0, 0), pipeline_mode=pl.Buffered(1)),
                  rows_p(D_MODEL)] + [mod_p] * n_mod
        + [rows_s(kdim), rows_s(D_MODEL)] + [mod_s] * n_mod + [vec_spec, vec_spec],
        out_specs=out_specs,
        out_shape=out_shape,
        compiler_params=_params("arbitrary"),
        name=name,
    )(a, w, x, *prompt[2:], *sample, ln_g.reshape(-1, 1, D_MODEL), ln_b.reshape(-1, 1, D_MODEL))


def _ffn_in_kernel(h_ref, hs_ref, wa_ref, wu_ref, o_ref, os_ref, *, n_i):
    wa = wa_ref[...].astype(BF16)
    wu = wu_ref[...].astype(BF16)

    def swiglu_in(h):
        return (_silu(_dot(h, wa)) * _dot(h, wu)).astype(BF16)

    o_ref[...] = swiglu_in(h_ref[...])

    @pl.when(pl.program_id(0) == n_i - 1)
    def _():
        os_ref[...] = swiglu_in(hs_ref[...])


def _ffn_in(h, hs, w, layer, tm, name):
    m, ms = h.shape[0], hs.shape[0]
    tn = 512
    nj = D_FF // tn
    n_i = m // tm
    return pl.pallas_call(
        functools.partial(_ffn_in_kernel, n_i=n_i),
        grid=(n_i, nj),
        in_specs=[
            pl.BlockSpec((tm, D_MODEL), lambda i, j: (i, 0)),
            pl.BlockSpec((ms, D_MODEL), lambda i, j: (0, 0)),
            pl.BlockSpec((None, D_MODEL, tn), lambda i, j: (layer, 0, j)),
            pl.BlockSpec((None, D_MODEL, tn), lambda i, j: (layer, 0, j + nj)),
        ],
        out_specs=[
            pl.BlockSpec((tm, tn), lambda i, j: (i, j)),
            pl.BlockSpec((ms, tn), lambda i, j: (0, jnp.where(i == n_i - 1, j, 0))),
        ],
        out_shape=[jax.ShapeDtypeStruct((m, D_FF), BF16), jax.ShapeDtypeStruct((ms, D_FF), BF16)],
        compiler_params=_params("arbitrary", "arbitrary"),
        name=name,
    )(h, hs, w, w)


def _gla_kernel(*refs, chunk, t_valid, has_s0, n_cast):
    n_in = 8 + has_s0
    q_ref, k_ref, v_ref, r_ref, a_ref, wa_ref, ba_ref, g_ref = refs[:8]
    s0_ref = refs[8] if has_s0 else None
    o_ref, st_ref = refs[n_in + n_cast:n_in + n_cast + 2]
    for src, dst in zip(refs[n_in:n_in + n_cast], refs[n_in + n_cast + 2:]):
        dst[...] = src[...].astype(BF16)
    sub = GLA_SUB
    n_sub = chunk // sub
    anchor = sub // 2 - 1

    @pl.when(pl.program_id(1) == 0)
    def _():
        if has_s0:
            st_ref[...] = s0_ref[...]
        else:
            st_ref[...] = jnp.zeros(st_ref.shape, F32)

    row = lax.broadcasted_iota(jnp.int32, (chunk, 1), 0)
    tri = _ind(lax.broadcasted_iota(jnp.int32, (chunk, chunk), 1)
               <= lax.broadcasted_iota(jnp.int32, (chunk, chunk), 0), BF16)
    ones = jnp.ones((chunk, LANES), BF16)
    causal = (lax.broadcasted_iota(jnp.int32, (sub, sub), 1)
              <= lax.broadcasted_iota(jnp.int32, (sub, sub), 0))
    heads = range(GLA_HEADS)
    dk = [slice(h * GLA_DK, (h + 1) * GLA_DK) for h in heads]
    dv = [slice(h * GLA_DV, (h + 1) * GLA_DV) for h in heads]
    subs = [slice(s * sub, (s + 1) * sub) for s in range(n_sub)]

    z = _dot(a_ref[...], wa_ref[...]) + ba_ref[...]
    lb = (jnp.minimum(z, 0.0) - jnp.log1p(jnp.exp(-jnp.abs(z)))) * (1.0 / GLA_TAU)
    k = k_ref[...]
    if t_valid < chunk:
        lb = jnp.where(row < t_valid, lb, 0.0)
        k = jnp.where(row < t_valid, k, 0.0)
    q = q_ref[...] * (GLA_DK ** -0.5)
    vb = v_ref[...].astype(BF16)
    lb3 = _split3(lb)
    b = sum(jnp.dot(tri, t, preferred_element_type=F32) for t in lb3)
    dec = jnp.exp(sum(lax.dot_general(t, ones, (((0,), (0,)), ((), ())), preferred_element_type=F32)
                      for t in lb3))
    b_last = b[chunk - 1:chunk, :]
    q_in = (q * jnp.exp(b)).astype(BF16)
    k_out = (k * jnp.exp(b_last - b)).astype(BF16)
    q_diag, k_diag, q_off, k_off = [], [], [], []
    for s, rs in enumerate(subs):
        b_mid = b[s * sub + anchor:s * sub + anchor + 1, :]
        q_diag.append((q[rs] * jnp.exp(b[rs] - b_mid)).astype(BF16))
        k_diag.append((k[rs] * jnp.exp(b_mid - b[rs])).astype(BF16))
        if s > 0:
            prev = slice(0, s * sub)
            b_in = b[s * sub - 1:s * sub, :]
            q_off.append((q[rs] * jnp.exp(b[rs] - b_in)).astype(BF16))
            k_off.append((k[prev] * jnp.exp(b_in - b[prev])).astype(BF16))

    s_old = [st_ref[0, h] for h in heads]
    o_inter = [_dot(q_in[:, dk[h]], s_old[h]) for h in heads]
    a_diag = [[_dot_nt(q_diag[s][:, dk[h]], k_diag[s][:, dk[h]]) for h in heads] for s in range(n_sub)]
    a_off = [[_dot_nt(q_off[s][:, dk[h]], k_off[s][:, dk[h]]) for h in heads] for s in range(n_sub - 1)]
    a_diag = [[jnp.where(causal, a, 0.0).astype(BF16) for a in row_] for row_ in a_diag]
    a_off = [[a.astype(BF16) for a in row_] for row_ in a_off]
    o_intra = []
    for s, rs in enumerate(subs):
        o_s = [_dot(a_diag[s][h], vb[rs, dv[h]]) for h in heads]
        if s > 0:
            o_s = [o_s[h] + _dot(a_off[s - 1][h], vb[0:s * sub, dv[h]]) for h in heads]
        o_intra.append(o_s)
    kv = [_dot_tn(k_out[:, dk[h]], vb[:, dv[h]]) for h in heads]
    for h in heads:
        st_ref[0, h] = s_old[h] * jnp.tile(dec[dk[h], :], (1, GLA_DV // LANES)) + kv[h]
    for h in heads:
        o = o_inter[h] + jnp.concatenate([o_intra[s][h] for s in range(n_sub)], axis=0)
        o = o * lax.rsqrt(jnp.mean(o * o, axis=-1, keepdims=True) + LN_EPS)
        o_ref[:, dv[h]] = (o * g_ref[:, dv[h]] * _silu(r_ref[:, dv[h]])).astype(BF16)


def _gla(main, tail, w_alpha, b_alpha, norm_g, s0, batch, seq, chunk, t_valid, to_bf16=()):
    n_chunks = seq // chunk
    qk_w = GLA_HEADS * GLA_DK
    v_w = GLA_HEADS * GLA_DV
    wa = jnp.pad(w_alpha, ((0, LANES - GLA_RANK), (0, 0)))
    row_map = lambda b, c: (b * n_chunks + c, 0)
    const = lambda b, c: (0, 0)
    in_specs = [
        pl.BlockSpec((chunk, qk_w), row_map),
        pl.BlockSpec((chunk, qk_w), lambda b, c: (b * n_chunks + c, 1)),
        pl.BlockSpec((chunk, v_w), lambda b, c: (b * n_chunks + c, 1)),
        pl.BlockSpec((chunk, v_w), lambda b, c: (b * n_chunks + c, 2)),
        pl.BlockSpec((chunk, LANES), row_map),
        pl.BlockSpec((LANES, qk_w), const),
        pl.BlockSpec((1, qk_w), const),
        pl.BlockSpec((1, v_w), const),
    ]
    args = [main, main, main, main, tail, wa, b_alpha.reshape(1, qk_w), norm_g.reshape(1, v_w)]
    state_spec = pl.BlockSpec((1, GLA_HEADS, GLA_DK, GLA_DV), lambda b, c: (b, 0, 0, 0))
    if s0 is not None:
        in_specs.append(state_spec)
        args.append(s0)
    n_steps = batch * n_chunks
    cast_specs = [pl.BlockSpec((a.shape[0] // n_steps, a.shape[1]), row_map) for a in to_bf16]
    assert all(a.shape[0] % (n_steps * BF16_ROWS) == 0 for a in to_bf16)
    return pl.pallas_call(
        functools.partial(_gla_kernel, chunk=chunk, t_valid=t_valid, has_s0=s0 is not None,
                          n_cast=len(to_bf16)),
        grid=(batch, n_chunks),
        in_specs=in_specs + cast_specs,
        out_specs=[pl.BlockSpec((chunk, v_w), row_map), state_spec] + cast_specs,
        out_shape=[jax.ShapeDtypeStruct((batch * seq, v_w), BF16),
                   jax.ShapeDtypeStruct((batch, GLA_HEADS, GLA_DK, GLA_DV), F32)]
        + [jax.ShapeDtypeStruct(a.shape, BF16) for a in to_bf16],
        compiler_params=_params("parallel", "arbitrary"),
        name="gla_prompt" if s0 is None else "gla_sample",
    )(*args, *to_bf16)


def _select_blocks(imp_t, blk, cur):
    n = imp_t.shape[0]
    forced = (blk == 0) | ((blk >= cur - 1) & (blk <= cur))
    score = jnp.where(blk > cur, -jnp.inf, jnp.where(forced, jnp.inf, imp_t))
    rank = jnp.zeros(score.shape, F32)
    for i in range(n):
        si = score[i:i + 1, :]
        rank = rank + jnp.where(blk > i, _ind(si >= score), _ind(si > score))
    return (rank < NSA_TOPK) & (score > -jnp.inf)


def _nsa_prompt_kernel(pt_ref, q_ref, kc_ref, vc_ref, ks_ref, vs_ref, kw_ref, vw_ref, gt_ref, bg_ref, *refs,
                       seq, tq, tk, pages_per_step):
    page_refs = refs[:pages_per_step]
    (o_ref, means_ref, kcm, vcm, ksb, vst, kwb, vwt, qs_ref, sel_ref,
     ms_ref, accs_ref, mw_ref, accw_ref) = refs[pages_per_step:]
    qi = pl.program_id(2)
    n_blk = seq // NSA_BLOCK
    n_kt = seq // tk
    blk_per_kt = tk // NSA_BLOCK

    @pl.when(qi == 0)
    def _():
        kcm[...] = jnp.sum(kc_ref[...].reshape(n_blk, NSA_BLOCK, NSA_HD), axis=1) * (1.0 / NSA_BLOCK)
        vcm[...] = jnp.sum(vc_ref[...].reshape(n_blk, NSA_BLOCK, NSA_HD), axis=1) * (1.0 / NSA_BLOCK)
        ksb[...] = ks_ref[...].astype(BF16)
        kwb[...] = kw_ref[...].astype(BF16)
        ones_rows = _ind(lax.broadcasted_iota(jnp.int32, (BF16_ROWS, tk), 0) == 0, BF16)
        for kt in range(n_kt):
            rows = slice(kt * tk, (kt + 1) * tk)
            vst[kt] = jnp.concatenate([vs_ref[rows, :].T.astype(BF16), ones_rows], axis=0)
            vwt[kt] = jnp.concatenate([vw_ref[rows, :].T.astype(BF16), ones_rows], axis=0)

    t0 = qi * tq
    q = q_ref[...] * (NSA_HD ** -0.5 * LOG2_E)
    qs_ref[...] = jnp.concatenate(
        [q[:, g * NSA_HD:(g + 1) * NSA_HD] for g in range(NSA_GROUP)], axis=0).astype(BF16)
    qs = qs_ref[...]
    t_row = t0 + lax.broadcasted_iota(jnp.int32, (1, tq), 1)
    t_row_g = jnp.concatenate([t_row] * NSA_GROUP, axis=1)

    kc_b = kcm[...].astype(BF16)
    blk_col = lax.broadcasted_iota(jnp.int32, (n_blk, 1), 0)
    p_t = _masked_softmax2(_dot_nt(kc_b, qs), (blk_col + 1) * NSA_BLOCK - 1 <= t_row_g, axis=0)
    o_c = _dot_tn(vcm[...], p_t)
    imp_t = sum(p_t[:, g * tq:(g + 1) * tq] for g in range(NSA_GROUP))
    sel_ref[...] = jnp.where(_select_blocks(imp_t, blk_col, t_row // NSA_BLOCK), 0.0, NEG_BIG)

    n_loc = lax.broadcasted_iota(jnp.int32, (tk, tq), 0)
    t_loc = lax.broadcasted_iota(jnp.int32, (tk, tq), 1)
    on_lanes = lambda a: jnp.concatenate([a] * NSA_GROUP, axis=1)
    causal = on_lanes(jnp.where(n_loc <= t_loc, 0.0, NEG_BIG))
    far = on_lanes(jnp.where(n_loc > t_loc, 0.0, NEG_BIG))

    def sel_bias(kt):
        return on_lanes(jnp.concatenate(
            [jnp.broadcast_to(sel_ref[pl.ds(kt * blk_per_kt + j, 1), :], (NSA_BLOCK, tq))
             for j in range(blk_per_kt)], axis=0))

    sel_state, win_state = (ms_ref, accs_ref), (mw_ref, accw_ref)
    for m_ref, acc_ref in (sel_state, win_state):
        m_ref[...] = jnp.full(m_ref.shape, NEG_BIG, F32)
        acc_ref[...] = jnp.zeros(acc_ref.shape, F32)

    def steps(*work):
        s = [_dot_nt(k_ref[pl.ds(pl.multiple_of(kt * tk, tk), tk), :], qs_ref[...]) + bias
             for _, k_ref, _, kt, bias in work]
        m_old = [state[0][...] for state, *_ in work]
        m_new = [jnp.maximum(mo, jnp.max(si, axis=0, keepdims=True)) for mo, si in zip(m_old, s)]
        p = [jnp.exp2(si - mn).astype(BF16) for si, mn in zip(s, m_new)]
        pv = [_dot(vt_ref[kt], pi) for (_, _, vt_ref, kt, _), pi in zip(work, p)]
        for (state, *_), mo, mn, pvi in zip(work, m_old, m_new, pv):
            state[1][...] = jnp.exp2(mo - mn) * state[1][...] + pvi
            state[0][...] = mn

    def result(acc_ref):
        acc = acc_ref[...]
        return acc[:NSA_HD] / jnp.maximum(acc[NSA_HD:NSA_HD + 1], TINY)

    steps((sel_state, ksb, vst, qi, sel_bias(qi) + causal), (win_state, kwb, vwt, qi, causal))

    @pl.when(qi >= 1)
    def _():
        steps((sel_state, ksb, vst, qi - 1, sel_bias(qi - 1)), (win_state, kwb, vwt, qi - 1, far))

    def sel_body(i, carry):
        kt = qi - 2 - i
        steps((sel_state, ksb, vst, kt, sel_bias(kt)))
        return carry

    lax.fori_loop(0, qi - 1, sel_body, 0)

    gates = jax.nn.sigmoid(gt_ref[...] + bg_ref[...])
    g_c, g_s, g_w = (jnp.concatenate([gates[3 * g + br:3 * g + br + 1, :] for g in range(NSA_GROUP)], axis=1)
                     for br in range(3))
    o_t = g_c * o_c + g_s * result(accs_ref) + g_w * result(accw_ref)
    for g in range(NSA_GROUP):
        o_ref[:, g * NSA_HD:(g + 1) * NSA_HD] = o_t[:, g * tq:(g + 1) * tq].T.astype(BF16)

    for i, x_ref in enumerate(page_refs):
        x = x_ref[0].reshape(PAGE // NSA_BLOCK, NSA_BLOCK, KV_ROWS, NSA_HD)
        means_ref[0, i] = jnp.sum(x, axis=1) * (1.0 / NSA_BLOCK)


def _nsa_prompt(main, gates, b_gate, batch, seq, cache_cmp, page_table):
    tq = tk = NSA_WINDOW
    assert seq % tk == 0
    vt_rows = NSA_HD + BF16_ROWS
    nq = seq // tq
    q_lanes = NSA_GROUP * NSA_HD
    col0 = NSA_Q_W // NSA_HD
    seqs, n_pages = page_table.shape
    n_steps = batch * NSA_KV * nq
    pages_per_step = seqs * n_pages // n_steps
    groups = n_pages // pages_per_step
    assert pages_per_step * n_steps == seqs * n_pages and groups * pages_per_step == n_pages
    per_page = PAGE // NSA_BLOCK

    def kv_spec(branch, part):
        off = col0 + (2 * branch + part) * NSA_KV
        return pl.BlockSpec((seq, NSA_HD), lambda b, kv, qi, pt: (b, off + kv))

    def step_of(b, kv, qi):
        return (b * NSA_KV + kv) * nq + qi

    def page_spec(i):
        def index(b, kv, qi, pt):
            s = step_of(b, kv, qi)
            return (pt[s // groups, (s % groups) * pages_per_step + i], 0, 0)
        return pl.BlockSpec((1, PAGE * KV_ROWS, NSA_HD), index)

    def means_index(b, kv, qi, pt):
        s = step_of(b, kv, qi)
        return (s // groups, s % groups, 0, 0, 0)

    g_w = 3 * NSA_GROUP
    g_rows = NSA_GROUP * tq
    grid_spec = pltpu.PrefetchScalarGridSpec(
        num_scalar_prefetch=1,
        grid=(batch, NSA_KV, nq),
        in_specs=[pl.BlockSpec((tq, q_lanes), lambda b, kv, qi, pt: (b * nq + qi, kv))]
        + [kv_spec(br, part) for br in range(3) for part in range(2)]
        + [pl.BlockSpec((None, g_w, tq), lambda b, kv, qi, pt: (kv, 0, b * nq + qi)),
           pl.BlockSpec((None, g_w, 1), lambda b, kv, qi, pt: (kv, 0, 0))]
        + [page_spec(i) for i in range(pages_per_step)],
        out_specs=[pl.BlockSpec((tq, q_lanes), lambda b, kv, qi, pt: (b * nq + qi, kv)),
                   pl.BlockSpec((1, pages_per_step, per_page, KV_ROWS, NSA_HD), means_index)],
        scratch_shapes=[
            pltpu.VMEM((seq // NSA_BLOCK, NSA_HD), F32),
            pltpu.VMEM((seq // NSA_BLOCK, NSA_HD), F32),
            pltpu.VMEM((seq, NSA_HD), BF16),
            pltpu.VMEM((seq // tk, vt_rows, tk), BF16),
            pltpu.VMEM((seq, NSA_HD), BF16),
            pltpu.VMEM((seq // tk, vt_rows, tk), BF16),
            pltpu.VMEM((g_rows, NSA_HD), BF16),
            pltpu.VMEM((seq // NSA_BLOCK, tq), F32),
            pltpu.VMEM((1, g_rows), F32),
            pltpu.VMEM((vt_rows, g_rows), F32),
            pltpu.VMEM((1, g_rows), F32),
            pltpu.VMEM((vt_rows, g_rows), F32),
        ],
    )
    return pl.pallas_call(
        functools.partial(_nsa_prompt_kernel, seq=seq, tq=tq, tk=tk, pages_per_step=pages_per_step),
        grid_spec=grid_spec,
        out_shape=[jax.ShapeDtypeStruct((batch * seq, NSA_Q_W), BF16),
                   jax.ShapeDtypeStruct((seqs, n_pages, per_page, KV_ROWS, NSA_HD), F32)],
        compiler_params=_params("arbitrary", "arbitrary", "arbitrary"),
        name="nsa_prompt",
    )(page_table, main, main, main, main, main, main, main, gates, b_gate, *([cache_cmp] * pages_per_step))


def _kv_rows_out_kernel(c_ref, s_ref, w_ref, oc_ref, os_ref, ow_ref, *, rows, n_c):
    def spread(src, dst):
        for r in range(KV_ROWS):
            dst[pl.ds(r, rows, stride=KV_ROWS), :] = src[:, r * NSA_HD:(r + 1) * NSA_HD]

    spread(c_ref, oc_ref)
    spread(s_ref, os_ref)

    @pl.when(pl.program_id(1) == n_c - 1)
    def _():
        spread(w_ref, ow_ref)


def _kv_rows_out(main, batch, seq, keep):
    rows = keep
    n_c = seq // rows
    col0 = NSA_Q_W // NSA_KV_W
    tile = lambda br: pl.BlockSpec((rows, NSA_KV_W), lambda b, c: (b * n_c + c, col0 + br))
    out_tile = pl.BlockSpec((rows * KV_ROWS, NSA_HD), lambda b, c: (b * n_c + c, 0))
    return pl.pallas_call(
        functools.partial(_kv_rows_out_kernel, rows=rows, n_c=n_c),
        grid=(batch, n_c),
        in_specs=[tile(0), tile(1),
                  pl.BlockSpec((rows, NSA_KV_W), lambda b, c: (b * n_c + n_c - 1, col0 + 2))],
        out_specs=[out_tile, out_tile, pl.BlockSpec((rows * KV_ROWS, NSA_HD), lambda b, c: (b, 0))],
        out_shape=[jax.ShapeDtypeStruct((batch * seq * KV_ROWS, NSA_HD), F32)] * 2
        + [jax.ShapeDtypeStruct((batch * keep * KV_ROWS, NSA_HD), F32)],
        compiler_params=_params("arbitrary", "arbitrary"),
        name="nsa_kv_rows_out",
    )(main, main, main)


def _page_specs(n):
    return [pl.BlockSpec((1, PAGE * KV_ROWS, NSA_HD), lambda b, p, pt, i=i: (pt[b, p * n + i], 0, 0))
            for i in range(n)]


def _nsa_sample_kernel(pt_ref, q_ref, cm_ref, *refs, past, n_new, n_steps, per_step):
    pg_refs = refs[:per_step]
    (kvn_ref, wb_ref, kwn_ref, gt_ref, bg_ref, o_ref,
     qr_ref, sel_ref, m_ref, l_ref, acc_ref, oc_ref, ow_ref) = refs[per_step:]
    p = pl.program_id(1)
    k_w = NSA_KV * NSA_HD
    cols = NSA_KV * n_new * NSA_GROUP
    per_kv = n_new * NSA_GROUP
    n_cmp = past // NSA_BLOCK
    n_sel = -(-(past + n_new) // NSA_BLOCK)
    n_sel_pad = sel_ref.shape[0]
    lane = lax.broadcasted_iota(jnp.int32, (1, cols), 1)
    q_pos = past + (lane // NSA_GROUP) % n_new

    def heads_on_lanes(ref, n, part):
        return jnp.concatenate(
            [ref[0, pl.ds(part * NSA_KV + kv, n, stride=KV_ROWS), :] for kv in range(NSA_KV)], axis=1)

    def scores(keys):
        return _dot_nt(keys, qr_ref[...])

    def own_head(x_t):
        out = jnp.zeros((NSA_HD, cols), F32)
        for kv in range(NSA_KV):
            mine = (lane // per_kv) == kv
            out = out + jnp.where(mine, x_t[kv * NSA_HD:(kv + 1) * NSA_HD, :], 0.0)
        return out

    def attend_once(keys, vals, mask):
        p_t = _masked_softmax2(scores(keys), mask, axis=0)
        return own_head(_dot_tn(vals, p_t))

    def online(keys, vals, bias):
        s = scores(keys) + bias
        m_old = m_ref[...]
        m_new = jnp.maximum(m_old, jnp.max(s, axis=0, keepdims=True))
        p_t = jnp.exp2(s - m_new)
        alpha = jnp.exp2(m_old - m_new)
        l_ref[...] = alpha * l_ref[...] + jnp.sum(p_t, axis=0, keepdims=True)
        acc_ref[...] = alpha * acc_ref[...] + own_head(_dot_tn(vals, p_t))
        m_ref[...] = m_new

    @pl.when(p == 0)
    def _():
        q = q_ref[0] * (NSA_HD ** -0.5 * LOG2_E)
        r_kv = lax.broadcasted_iota(jnp.int32, (cols, k_w), 0) // per_kv
        c_kv = lax.broadcasted_iota(jnp.int32, (cols, k_w), 1) // NSA_HD
        qr_ref[...] = jnp.where(r_kv == c_kv, jnp.tile(q, (1, NSA_KV)), 0.0).astype(BF16)
        blk = lax.broadcasted_iota(jnp.int32, (n_cmp, 1), 0)
        p_t = _masked_softmax2(scores(heads_on_lanes(cm_ref, n_cmp, 0)),
                               (blk + 1) * NSA_BLOCK - 1 <= q_pos, axis=0)
        oc_ref[...] = own_head(_dot_tn(heads_on_lanes(cm_ref, n_cmp, 1), p_t))
        same = _ind(lax.broadcasted_iota(jnp.int32, (cols, cols), 0) // NSA_GROUP
                    == lax.broadcasted_iota(jnp.int32, (cols, cols), 1) // NSA_GROUP, BF16)
        imp = sum(jnp.dot(t, same, preferred_element_type=F32) for t in _split3(p_t))
        imp = jnp.concatenate([imp, jnp.zeros((n_sel_pad - n_cmp, cols), F32)], axis=0)
        blk_s = lax.broadcasted_iota(jnp.int32, (n_sel_pad, 1), 0)
        cur = q_pos // NSA_BLOCK
        forced = (blk_s == 0) | ((blk_s >= cur - 1) & (blk_s <= cur))
        score = jnp.where((blk_s > cur) | (blk_s >= n_sel), -jnp.inf, jnp.where(forced, jnp.inf, imp))
        blk_f = blk_s.astype(F32)

        def pick(_, carry):
            left, bias = carry
            top = jnp.max(left, axis=0, keepdims=True)
            first = jnp.min(jnp.where(left == top, blk_f, float(n_sel_pad)), axis=0, keepdims=True)
            hit = (blk_f == first) & (top > -jnp.inf)
            return jnp.where(hit, -jnp.inf, left), jnp.where(hit, 0.0, bias)

        _, bias = lax.fori_loop(0, NSA_TOPK, pick, (score, jnp.full(score.shape, NEG_BIG, F32)))
        sel_ref[...] = bias
        n_wb = wb_ref.shape[1] // KV_ROWS
        kw_all = jnp.concatenate([heads_on_lanes(wb_ref, n_wb, 0), kwn_ref[0][:, :k_w]], axis=0)
        vw_all = jnp.concatenate([heads_on_lanes(wb_ref, n_wb, 1), kwn_ref[0][:, k_w:]], axis=0)
        n_w = kw_all.shape[0]
        w_row = lax.broadcasted_iota(jnp.int32, (n_w, 1), 0)
        w_pos = past - n_wb + w_row
        d = q_pos - w_pos
        ow_ref[...] = attend_once(kw_all, vw_all,
                                  (d >= 0) & (d < NSA_WINDOW) & (w_pos >= 0) & (w_row < n_wb + n_new))
        m_ref[...] = jnp.full(m_ref.shape, NEG_BIG, F32)
        l_ref[...] = jnp.zeros(l_ref.shape, F32)
        acc_ref[...] = jnp.zeros(acc_ref.shape, F32)

    def page_part(part):
        return jnp.concatenate([heads_on_lanes(pg_ref, PAGE, part) for pg_ref in pg_refs], axis=0)

    blk_per_step = per_step * PAGE // NSA_BLOCK
    online(page_part(0), page_part(1), jnp.concatenate(
        [jnp.broadcast_to(sel_ref[pl.ds(p * blk_per_step + j, 1), :], (NSA_BLOCK, cols))
         for j in range(blk_per_step)], axis=0))

    @pl.when(p == n_steps - 1)
    def _():
        kvn = kvn_ref[0]
        n_row = lax.broadcasted_iota(jnp.int32, (kvn.shape[0], 1), 0)
        bias_new = jnp.where((n_row < n_new) & (past + n_row <= q_pos), 0.0, NEG_BIG)
        online(kvn[:, :k_w], kvn[:, k_w:], bias_new + sel_ref[pl.ds(past // NSA_BLOCK, 1), :])
        o_s = acc_ref[...] / jnp.maximum(l_ref[...], TINY)
        gates = jax.nn.sigmoid(gt_ref[0] + bg_ref[...])
        o_t = gates[0:1, :] * oc_ref[...] + gates[1:2, :] * o_s + gates[2:3, :] * ow_ref[...]
        eye = _ind(lax.broadcasted_iota(jnp.int32, (NSA_HD, NSA_HD), 0)
                   == lax.broadcasted_iota(jnp.int32, (NSA_HD, NSA_HD), 1), BF16)
        o_ref[0] = sum(lax.dot_general(t, eye, (((0,), (0,)), ((), ())), preferred_element_type=F32)
                       for t in _split3(o_t))


def _nsa_sample(q_rows, cmp_means, cache_slc, page_table, kv_new, win_buf, kw_new, gates, b_gate,
                past, n_new):
    seqs, n_pages = page_table.shape
    cols = q_rows.shape[1]
    cmp_rows = cmp_means.shape[1]
    n_sel_pad = -(-(-(-(past + n_new) // NSA_BLOCK)) // 8) * 8
    n_wb = win_buf.shape[1]
    pad_new = kv_new.shape[1]
    per_step = 32
    n_steps = n_pages // per_step
    seq_map = lambda b, p, pt: (b, 0, 0)
    return pl.pallas_call(
        functools.partial(_nsa_sample_kernel, past=past, n_new=n_new, n_steps=n_steps, per_step=per_step),
        grid_spec=pltpu.PrefetchScalarGridSpec(
            num_scalar_prefetch=1,
            grid=(seqs, n_steps),
            in_specs=[
                pl.BlockSpec((1, cols, NSA_HD), seq_map),
                pl.BlockSpec((1, cmp_rows, NSA_HD), seq_map),
                *_page_specs(per_step),
                pl.BlockSpec((1, pad_new, NSA_KV_W), seq_map),
                pl.BlockSpec((1, n_wb, NSA_HD), seq_map),
                pl.BlockSpec((1, pad_new, NSA_KV_W), seq_map),
                pl.BlockSpec((1, 3, cols), seq_map),
                pl.BlockSpec((3, cols), lambda b, p, pt: (0, 0)),
            ],
            out_specs=pl.BlockSpec((1, cols, NSA_HD), seq_map),
            scratch_shapes=[
                pltpu.VMEM((cols, NSA_KV * NSA_HD), BF16),
                pltpu.VMEM((n_sel_pad, cols), F32),
                pltpu.VMEM((1, cols), F32),
                pltpu.VMEM((1, cols), F32),
                pltpu.VMEM((NSA_HD, cols), F32),
                pltpu.VMEM((NSA_HD, cols), F32),
                pltpu.VMEM((NSA_HD, cols), F32),
            ],
        ),
        out_shape=jax.ShapeDtypeStruct((seqs, cols, NSA_HD), F32),
        compiler_params=_params("parallel", "arbitrary"),
        name="nsa_sample",
    )(page_table, q_rows, cmp_means, *([cache_slc] * per_step), kv_new, win_buf, kw_new, gates, b_gate)


def kernel(x_prompt, x_sample, c_prompt, c_sample, state_gla, cache_cmp_kv, cache_slc_kv, cache_win_kv,
           page_table, ada_w, ada_b, gla_w_in, gla_w_alpha, gla_b_alpha, gla_norm_g, gla_w_out, nsa_w_in,
           nsa_b_gate, nsa_w_out, ln_mix_g, ln_mix_b, ffn_w_in, ffn_w_out, ln_ffn_g, ln_ffn_b):
    batch, seq, _ = x_prompt.shape
    seqs, n_new, _ = x_sample.shape
    n_pages = page_table.shape[1]
    past = n_pages * PAGE
    rows_s = seqs * n_new
    tm_p = seq

    pad_rows = 16 - (batch + seqs)
    c_all = jnp.concatenate([c_prompt, c_sample, jnp.zeros((pad_rows, D_MODEL), F32)], axis=0)
    mod = _adaln(c_all, ada_w, ada_b).reshape(DEPTH, 16, 6, D_MODEL)

    def mods(layer):
        mp = [mod[layer, :batch, i].reshape(batch, 1, D_MODEL) for i in range(6)]
        ms = [jnp.repeat(mod[layer, batch:batch + seqs, i], n_new, axis=0).reshape(1, rows_s, D_MODEL)
              for i in range(6)]
        return mp, ms

    def mixer_out(o_p, o_s, xp, xs, layer, mp, ms, tag):
        return _out_ln((o_p, xp, mp[2], mp[4], mp[3]), (o_s, xs, ms[2], ms[4], ms[3]), w_out_mix[layer % 2],
                       layer // 2, ln_mix_g, ln_mix_b, layer, 512, tag)

    def ffn(hp, hs, xp, xs, layer, mp, ms, next_mods):
        act_p, act_s = _ffn_in(hp, hs, ffn_w_in, layer, tm_p, "ffn_in_%d" % layer)
        nxt_p, nxt_s = ((), ()) if next_mods is None else ((next_mods[0][1], next_mods[0][0]),
                                                            (next_mods[1][1], next_mods[1][0]))
        return _out_ln((act_p, xp, mp[5], *nxt_p), (act_s, xs, ms[5], *nxt_s), w_out_ffn, layer,
                       ln_ffn_g, ln_ffn_b, layer, 256, "ffn_out_%d" % layer)

    xp = x_prompt.reshape(batch * seq, D_MODEL)
    xs = x_sample.reshape(rows_s, D_MODEL)

    mp, ms = mods(0)
    main_p, tail_p, main_s, tail_s = _proj((xp, mp[1], mp[0]), (xs, ms[1], ms[0]), jnp.swapaxes(gla_w_in, 1, 2), 0,
                                           GLA_MAIN_W, tm_p, "gla_in")
    rows2d = lambda w: w.reshape(-1, w.shape[-1])
    gated_p, gla_state_p, *w_bf16 = _gla(main_p, tail_p, gla_w_alpha[0], gla_b_alpha[0], gla_norm_g[0], None,
                                         batch, seq, 256, 256,
                                         to_bf16=(rows2d(gla_w_out), rows2d(nsa_w_out), rows2d(ffn_w_out)))
    w_out_mix = (w_bf16[0].reshape(gla_w_out.shape), w_bf16[1].reshape(nsa_w_out.shape))
    w_out_ffn = w_bf16[2].reshape(ffn_w_out.shape)
    chunk_s = GLA_SUB

    def pad_steps(a):
        a = a.reshape(seqs, n_new, a.shape[-1])
        return jnp.pad(a, ((0, 0), (0, chunk_s - n_new), (0, 0))).reshape(seqs * chunk_s, a.shape[-1])

    gated_s, gla_state_s = _gla(pad_steps(main_s), pad_steps(tail_s), gla_w_alpha[0], gla_b_alpha[0],
                                gla_norm_g[0], state_gla[:, 0], seqs, chunk_s, chunk_s, n_new)
    gated_s = gated_s.reshape(seqs, chunk_s, -1)[:, :n_new].reshape(rows_s, -1)
    xp, xs, hp, hs = mixer_out(gated_p, gated_s, xp, xs, 0, mp, ms, "gla_out")
    xp, xs, hp, hs = ffn(hp, hs, xp, xs, 0, mp, ms, mods(1))

    mp, ms = mods(1)
    main_p, tail_p, main_s, tail_s = _proj(hp, hs, jnp.swapaxes(nsa_w_in, 1, 2), 0, NSA_MAIN_W, tm_p, "nsa_in")
    n_gate = 3 * NSA_HEADS
    g_w = 3 * NSA_GROUP
    gates_p = tail_p[:, :n_gate].reshape(batch * seq, NSA_KV, g_w).transpose(1, 2, 0)
    b_gate_p = nsa_b_gate[0].reshape(NSA_KV, g_w, 1)
    n_phys = cache_cmp_kv.shape[0]
    page_rows = lambda cache: cache.reshape(n_phys, PAGE * KV_ROWS, NSA_HD)
    o_p, cmp_means = _nsa_prompt(main_p, gates_p, b_gate_p, batch, seq, page_rows(cache_cmp_kv), page_table)

    kv_rows = lambda m, br: m[:, NSA_Q_W + br * NSA_KV_W:NSA_Q_W + (br + 1) * NSA_KV_W]
    kv_shape = (2, NSA_KV, NSA_HD)
    keep_p = min(NSA_WINDOW, seq)
    cmp_rows, slc_rows, win_rows = _kv_rows_out(main_p, batch, seq, keep_p)
    cmp_kv_p = cmp_rows.reshape(batch, 1, seq, *kv_shape)
    slc_kv_p = slc_rows.reshape(batch, 1, seq, *kv_shape)
    win_kv_p = win_rows.reshape(batch, 1, keep_p, *kv_shape)

    cmp_kv_s = kv_rows(main_s, 0).reshape(seqs, 1, n_new, *kv_shape)
    slc_kv_s = kv_rows(main_s, 1).reshape(seqs, 1, n_new, *kv_shape)
    kw_new = kv_rows(main_s, 2).reshape(seqs, n_new, NSA_KV_W)
    n_wb = cache_win_kv.shape[2]
    win_buf = cache_win_kv[:, 0].reshape(seqs, n_wb * KV_ROWS, NSA_HD)
    all_w = jnp.concatenate([win_buf, kw_new.reshape(seqs, n_new * KV_ROWS, NSA_HD)], axis=1)
    keep_s = min(NSA_WINDOW, n_wb + n_new)
    win_kv_s = all_w[:, (n_wb + n_new - keep_s) * KV_ROWS:].reshape(seqs, 1, keep_s, *kv_shape)

    cmp_means = cmp_means.reshape(seqs, past // NSA_BLOCK * KV_ROWS, NSA_HD)
    pad_new = 8
    pad8 = lambda a: jnp.pad(a, ((0, 0), (0, pad_new - n_new), (0, 0)))
    cols = NSA_KV * n_new * NSA_GROUP
    q_rows = (main_s[:, :NSA_Q_W].reshape(seqs, n_new, NSA_KV, NSA_GROUP, NSA_HD)
              .transpose(0, 2, 1, 3, 4).reshape(seqs, cols, NSA_HD))
    gates_s = (tail_s[:, :n_gate].reshape(seqs, n_new, NSA_KV, NSA_GROUP, 3)
               .transpose(0, 4, 2, 1, 3).reshape(seqs, 3, cols))
    b_gate_s = jnp.broadcast_to(nsa_b_gate[0].reshape(1, NSA_KV, NSA_GROUP, 3), (n_new, NSA_KV, NSA_GROUP, 3))
    b_gate_s = b_gate_s.transpose(3, 1, 0, 2).reshape(3, cols)
    o_s = _nsa_sample(q_rows, cmp_means, page_rows(cache_slc_kv), page_table,
                      pad8(slc_kv_s.reshape(seqs, n_new, NSA_KV_W)), win_buf, pad8(kw_new), gates_s, b_gate_s,
                      past, n_new)
    o_s = (o_s.reshape(seqs, NSA_KV, n_new, NSA_GROUP, NSA_HD).transpose(0, 2, 1, 3, 4)
           .reshape(rows_s, NSA_Q_W).astype(BF16))

    xp, xs, hp, hs = mixer_out(o_p, o_s, xp, xs, 1, mp, ms, "nsa_out")
    xp, xs = ffn(hp, hs, xp, xs, 1, mp, ms, None)

    return (xp.reshape(batch, seq, D_MODEL), xs.reshape(seqs, n_new, D_MODEL),
            gla_state_p[:, None], gla_state_s[:, None], cmp_kv_p, cmp_kv_s, slc_kv_p, slc_kv_s,
            win_kv_p, win_kv_s)
```

```python
import functools

import jax
import jax.numpy as jnp
from jax import lax
from jax.experimental import pallas as pl
from jax.experimental.pallas import tpu as pltpu

F32 = jnp.float32
BF16 = jnp.bfloat16

D_MODEL = 2048
DEPTH = 2
DN_ALPHA = (2 * DEPTH) ** 0.25
LN_EPS = 1e-5
D_FF = 5632
GLA_HEADS = 4
GLA_DK = 256
GLA_DV = 512
GLA_RANK = 16
GLA_TAU = 16.0
GLA_SUB = 32
GLA_MAIN_W = 2 * GLA_HEADS * GLA_DK + 2 * GLA_HEADS * GLA_DV
NSA_HEADS = 16
NSA_KV = 4
NSA_HD = 128
NSA_GROUP = NSA_HEADS // NSA_KV
NSA_BLOCK = 64
NSA_TOPK = 16
NSA_WINDOW = 512
NSA_Q_W = NSA_HEADS * NSA_HD
NSA_KV_W = 2 * NSA_KV * NSA_HD
NSA_MAIN_W = NSA_Q_W + 3 * NSA_KV_W
PAGE = 128
KV_ROWS = 2 * NSA_KV

LANES = 128
BF16_ROWS = 16
LOG2_E = 1.4426950408889634
V7X_VMEM_BYTES = 64 * 1024 * 1024
VMEM_LIMIT = V7X_VMEM_BYTES * 7 // 8

NEG_BIG = -1e30
TINY = float(jnp.finfo(jnp.float32).tiny)


def _params(*sem):
    return pltpu.CompilerParams(dimension_semantics=sem, vmem_limit_bytes=VMEM_LIMIT)


def _dot(a, b):
    return jnp.dot(a.astype(BF16), b.astype(BF16), preferred_element_type=F32)


def _dot_nt(a, b):
    return lax.dot_general(a.astype(BF16), b.astype(BF16), (((1,), (1,)), ((), ())),
                           preferred_element_type=F32)


def _dot_tn(a, b):
    return lax.dot_general(a.astype(BF16), b.astype(BF16), (((0,), (0,)), ((), ())),
                           preferred_element_type=F32)


def _split3(x):
    hi = x.astype(BF16)
    r1 = x - hi.astype(F32)
    mid = r1.astype(BF16)
    lo = (r1 - mid.astype(F32)).astype(BF16)
    return hi, mid, lo


def _silu(x):
    return x * jax.nn.sigmoid(x)


def _ind(cond, dtype=F32):
    return jnp.where(cond, 1.0, 0.0).astype(dtype)


def _masked_softmax2(s2, mask, axis):
    s2 = jnp.where(mask, s2, -jnp.inf)
    m = jnp.max(s2, axis=axis, keepdims=True)
    m = jnp.where(jnp.isfinite(m), m, 0.0)
    e = jnp.where(mask, jnp.exp2(s2 - m), 0.0)
    return e / jnp.maximum(jnp.sum(e, axis=axis, keepdims=True), TINY)


def _adaln_kernel(c_ref, w_ref, b_ref, o_ref):
    o_ref[...] = _dot(_silu(c_ref[...]), w_ref[...]) + b_ref[...]


def _adaln(c_all, ada_w, ada_b):
    rows = c_all.shape[0]
    n = ada_w.shape[2]
    tn = 1024
    return pl.pallas_call(
        _adaln_kernel,
        grid=(DEPTH, n // tn),
        in_specs=[
            pl.BlockSpec((rows, D_MODEL), lambda l, j: (0, 0)),
            pl.BlockSpec((None, D_MODEL, tn), lambda l, j: (l, 0, j)),
            pl.BlockSpec((None, 1, tn), lambda l, j: (l, 0, j)),
        ],
        out_specs=pl.BlockSpec((None, rows, tn), lambda l, j: (l, 0, j)),
        out_shape=jax.ShapeDtypeStruct((DEPTH, rows, n), F32),
        compiler_params=_params("parallel", "parallel"),
        name="adaln",
    )(c_all, ada_w, ada_b.reshape(DEPTH, 1, n))


def _modulate(x_ref, sc_ref, sh_ref):
    return (x_ref[...] * (1.0 + sc_ref[...]) + sh_ref[...]).astype(BF16)


def _proj_kernel(*refs, n_i, modulated):
    if modulated:
        h_ref, hs_ref, w_ref, wt_ref, o_ref, ot_ref, os_ref, ots_ref = refs
    else:
        (x_ref, sc_ref, sh_ref, xs_ref, scs_ref, shs_ref, w_ref, wt_ref,
         o_ref, ot_ref, os_ref, ots_ref, h_ref, hs_ref) = refs
    i, j = pl.program_id(0), pl.program_id(1)

    @pl.when(j == 0)
    def _():
        if not modulated:
            h_ref[...] = _modulate(x_ref, sc_ref, sh_ref)
        ot_ref[...] = _dot_nt(h_ref[...], wt_ref[...])

    o_ref[...] = _dot_nt(h_ref[...], w_ref[...])

    @pl.when(i == n_i - 1)
    def _():
        @pl.when(j == 0)
        def _():
            if not modulated:
                hs_ref[...] = _modulate(xs_ref, scs_ref, shs_ref)
            ots_ref[...] = _dot_nt(hs_ref[...], wt_ref[...])

        os_ref[...] = _dot_nt(hs_ref[...], w_ref[...])


def _proj(prompt, sample, w_t, layer, n_main, tm, name):
    modulated = not isinstance(prompt, tuple)
    m = (prompt if modulated else prompt[0]).shape[0]
    ms = (sample if modulated else sample[0]).shape[0]
    n_tail = w_t.shape[1] - n_main
    w_tail = jnp.pad(w_t[layer, n_main:, :], ((0, LANES - n_tail), (0, 0)))
    tn = 512
    n_i = m // tm
    const2 = lambda i, j: (0, 0)
    const3 = lambda i, j: (0, 0, 0)
    last_tile_col = lambda i, j: (0, jnp.where(i == n_i - 1, j, 0))
    if modulated:
        row_specs = [pl.BlockSpec((tm, D_MODEL), lambda i, j: (i, 0)), pl.BlockSpec((ms, D_MODEL), const2)]
        rows, scratch = (prompt, sample), []
    else:
        tiles_per_group = n_i // prompt[1].shape[0]
        mod_spec = pl.BlockSpec((None, 1, D_MODEL), lambda i, j: (i // tiles_per_group, 0, 0))
        row_specs = [pl.BlockSpec((tm, D_MODEL), lambda i, j: (i, 0), pipeline_mode=pl.Buffered(1)),
                     mod_spec, mod_spec, pl.BlockSpec((ms, D_MODEL), const2),
                     pl.BlockSpec((None, ms, D_MODEL), const3), pl.BlockSpec((None, ms, D_MODEL), const3)]
        rows = (*prompt, *sample)
        scratch = [pltpu.VMEM((tm, D_MODEL), BF16), pltpu.VMEM((ms, D_MODEL), BF16)]
    return pl.pallas_call(
        functools.partial(_proj_kernel, n_i=n_i, modulated=modulated),
        grid=(n_i, n_main // tn),
        in_specs=row_specs + [
            pl.BlockSpec((None, tn, D_MODEL), lambda i, j: (layer, j, 0)),
            pl.BlockSpec((LANES, D_MODEL), const2),
        ],
        out_specs=[
            pl.BlockSpec((tm, tn), lambda i, j: (i, j)),
            pl.BlockSpec((tm, LANES), lambda i, j: (i, 0)),
            pl.BlockSpec((ms, tn), last_tile_col),
            pl.BlockSpec((ms, LANES), const2),
        ],
        out_shape=[jax.ShapeDtypeStruct((m, n_main), F32), jax.ShapeDtypeStruct((m, LANES), F32),
                   jax.ShapeDtypeStruct((ms, n_main), F32), jax.ShapeDtypeStruct((ms, LANES), F32)],
        scratch_shapes=scratch,
        compiler_params=_params("arbitrary", "arbitrary"),
        name=name,
    )(*rows, w_t, w_tail)


def _out_ln_kernel(*refs, n_i, with_next):
    if with_next:
        (a_ref, w_ref, x_ref, gt_ref, sc_ref, sh_ref, as_ref, xs_ref, gts_ref, scs_ref, shs_ref, g_ref, b_ref,
         o_ref, os_ref, h_ref, hs_ref) = refs
    else:
        a_ref, w_ref, x_ref, gt_ref, as_ref, xs_ref, gts_ref, g_ref, b_ref, o_ref, os_ref = refs

    def deep_norm(a, x, gate):
        y = DN_ALPHA * x + gate * jnp.dot(a, w_ref[...], preferred_element_type=F32)
        mu = jnp.mean(y, axis=-1, keepdims=True)
        yc = y - mu
        var = jnp.mean(yc * yc, axis=-1, keepdims=True)
        return yc * lax.rsqrt(var + LN_EPS) * g_ref[...] + b_ref[...]

    o_ref[...] = deep_norm(a_ref[...], x_ref[...], gt_ref[...])
    if with_next:
        h_ref[...] = _modulate(o_ref, sc_ref, sh_ref)

    @pl.when(pl.program_id(0) == n_i - 1)
    def _():
        os_ref[...] = deep_norm(as_ref[...], xs_ref[...], gts_ref[...])
        if with_next:
            hs_ref[...] = _modulate(os_ref, scs_ref, shs_ref)


def _out_ln(prompt, sample, w, w_layer, ln_g, ln_b, ln_layer, tm, name):
    with_next = len(prompt) == 5
    a, x = prompt[:2]
    m, kdim = a.shape
    ms = sample[0].shape[0]
    n_i = m // tm
    tiles_per_group = n_i // prompt[2].shape[0]
    rows_p = lambda width: pl.BlockSpec((tm, width), lambda i: (i, 0))
    rows_s = lambda width: pl.BlockSpec((ms, width), lambda i: (0, 0))
    mod_p = pl.BlockSpec((None, 1, D_MODEL), lambda i: (i // tiles_per_group, 0, 0))
    mod_s = pl.BlockSpec((None, ms, D_MODEL), lambda i: (0, 0, 0))
    vec_spec = pl.BlockSpec((None, 1, D_MODEL), lambda i: (ln_layer, 0, 0))
    n_mod = len(prompt) - 2
    out_specs = [rows_p(D_MODEL), rows_s(D_MODEL)]
    out_shape = [jax.ShapeDtypeStruct((m, D_MODEL), F32), jax.ShapeDtypeStruct((ms, D_MODEL), F32)]
    if with_next:
        out_specs += [rows_p(D_MODEL), rows_s(D_MODEL)]
        out_shape += [jax.ShapeDtypeStruct((m, D_MODEL), BF16), jax.ShapeDtypeStruct((ms, D_MODEL), BF16)]
    return pl.pallas_call(
        functools.partial(_out_ln_kernel, n_i=n_i, with_next=with_next),
        grid=(n_i,),
        in_specs=[rows_p(kdim),
                  pl.BlockSpec((None, kdim, D_MODEL), lambda i: (w_layer, 0, 0), pipeline_mode=pl.Buffered(1)),
                  rows_p(D_MODEL)] + [mod_p] * n_mod
        + [rows_s(kdim), rows_s(D_MODEL)] + [mod_s] * n_mod + [vec_spec, vec_spec],
        out_specs=out_specs,
        out_shape=out_shape,
        compiler_params=_params("arbitrary"),
        name=name,
    )(a, w, x, *prompt[2:], *sample, ln_g.reshape(-1, 1, D_MODEL), ln_b.reshape(-1, 1, D_MODEL))


def _ffn_in_kernel(h_ref, hs_ref, wa_ref, wu_ref, o_ref, os_ref, *, n_i):
    wa = wa_ref[...].astype(BF16)
    wu = wu_ref[...].astype(BF16)

    def swiglu_in(h):
        return (_silu(_dot(h, wa)) * _dot(h, wu)).astype(BF16)

    o_ref[...] = swiglu_in(h_ref[...])

    @pl.when(pl.program_id(0) == n_i - 1)
    def _():
        os_ref[...] = swiglu_in(hs_ref[...])


def _ffn_in(h, hs, w, layer, tm, name):
    m, ms = h.shape[0], hs.shape[0]
    tn = 512
    nj = D_FF // tn
    n_i = m // tm
    return pl.pallas_call(
        functools.partial(_ffn_in_kernel, n_i=n_i),
        grid=(n_i, nj),
        in_specs=[
            pl.BlockSpec((tm, D_MODEL), lambda i, j: (i, 0)),
            pl.BlockSpec((ms, D_MODEL), lambda i, j: (0, 0)),
            pl.BlockSpec((None, D_MODEL, tn), lambda i, j: (layer, 0, j)),
            pl.BlockSpec((None, D_MODEL, tn), lambda i, j: (layer, 0, j + nj)),
        ],
        out_specs=[
            pl.BlockSpec((tm, tn), lambda i, j: (i, j)),
            pl.BlockSpec((ms, tn), lambda i, j: (0, jnp.where(i == n_i - 1, j, 0))),
        ],
        out_shape=[jax.ShapeDtypeStruct((m, D_FF), BF16), jax.ShapeDtypeStruct((ms, D_FF), BF16)],
        compiler_params=_params("arbitrary", "arbitrary"),
        name=name,
    )(h, hs, w, w)


def _gla_kernel(*refs, chunk, t_valid, has_s0, n_cast):
    n_in = 8 + has_s0
    q_ref, k_ref, v_ref, r_ref, a_ref, wa_ref, ba_ref, g_ref = refs[:8]
    s0_ref = refs[8] if has_s0 else None
    o_ref, st_ref = refs[n_in + n_cast:n_in + n_cast + 2]
    for src, dst in zip(refs[n_in:n_in + n_cast], refs[n_in + n_cast + 2:]):
        dst[...] = src[...].astype(BF16)
    sub = GLA_SUB
    n_sub = chunk // sub
    anchor = sub // 2 - 1

    @pl.when(pl.program_id(1) == 0)
    def _():
        if has_s0:
            st_ref[...] = s0_ref[...]
        else:
            st_ref[...] = jnp.zeros(st_ref.shape, F32)

    row = lax.broadcasted_iota(jnp.int32, (chunk, 1), 0)
    tri = _ind(lax.broadcasted_iota(jnp.int32, (chunk, chunk), 1)
               <= lax.broadcasted_iota(jnp.int32, (chunk, chunk), 0), BF16)
    ones = jnp.ones((chunk, LANES), BF16)
    causal = (lax.broadcasted_iota(jnp.int32, (sub, sub), 1)
              <= lax.broadcasted_iota(jnp.int32, (sub, sub), 0))
    heads = range(GLA_HEADS)
    dk = [slice(h * GLA_DK, (h + 1) * GLA_DK) for h in heads]
    dv = [slice(h * GLA_DV, (h + 1) * GLA_DV) for h in heads]
    subs = [slice(s * sub, (s + 1) * sub) for s in range(n_sub)]

    z = _dot(a_ref[...], wa_ref[...]) + ba_ref[...]
    lb = (jnp.minimum(z, 0.0) - jnp.log1p(jnp.exp(-jnp.abs(z)))) * (1.0 / GLA_TAU)
    k = k_ref[...]
    if t_valid < chunk:
        lb = jnp.where(row < t_valid, lb, 0.0)
        k = jnp.where(row < t_valid, k, 0.0)
    q = q_ref[...] * (GLA_DK ** -0.5)
    vb = v_ref[...].astype(BF16)
    lb3 = _split3(lb)
    b = sum(jnp.dot(tri, t, preferred_element_type=F32) for t in lb3)
    dec = jnp.exp(sum(lax.dot_general(t, ones, (((0,), (0,)), ((), ())), preferred_element_type=F32)
                      for t in lb3))
    b_last = b[chunk - 1:chunk, :]
    q_in = (q * jnp.exp(b)).astype(BF16)
    k_out = (k * jnp.exp(b_last - b)).astype(BF16)
    q_diag, k_diag, q_off, k_off = [], [], [], []
    for s, rs in enumerate(subs):
        b_mid = b[s * sub + anchor:s * sub + anchor + 1, :]
        q_diag.append((q[rs] * jnp.exp(b[rs] - b_mid)).astype(BF16))
        k_diag.append((k[rs] * jnp.exp(b_mid - b[rs])).astype(BF16))
        if s > 0:
            prev = slice(0, s * sub)
            b_in = b[s * sub - 1:s * sub, :]
            q_off.append((q[rs] * jnp.exp(b[rs] - b_in)).astype(BF16))
            k_off.append((k[prev] * jnp.exp(b_in - b[prev])).astype(BF16))

    s_old = [st_ref[0, h] for h in heads]
    o_inter = [_dot(q_in[:, dk[h]], s_old[h]) for h in heads]
    a_diag = [[_dot_nt(q_diag[s][:, dk[h]], k_diag[s][:, dk[h]]) for h in heads] for s in range(n_sub)]
    a_off = [[_dot_nt(q_off[s][:, dk[h]], k_off[s][:, dk[h]]) for h in heads] for s in range(n_sub - 1)]
    a_diag = [[jnp.where(causal, a, 0.0).astype(BF16) for a in row_] for row_ in a_diag]
    a_off = [[a.astype(BF16) for a in row_] for row_ in a_off]
    o_intra = []
    for s, rs in enumerate(subs):
        o_s = [_dot(a_diag[s][h], vb[rs, dv[h]]) for h in heads]
        if s > 0:
            o_s = [o_s[h] + _dot(a_off[s - 1][h], vb[0:s * sub, dv[h]]) for h in heads]
        o_intra.append(o_s)
    kv = [_dot_tn(k_out[:, dk[h]], vb[:, dv[h]]) for h in heads]
    for h in heads:
        st_ref[0, h] = s_old[h] * jnp.tile(dec[dk[h], :], (1, GLA_DV // LANES)) + kv[h]
    for h in heads:
        o = o_inter[h] + jnp.concatenate([o_intra[s][h] for s in range(n_sub)], axis=0)
        o = o * lax.rsqrt(jnp.mean(o * o, axis=-1, keepdims=True) + LN_EPS)
        o_ref[:, dv[h]] = (o * g_ref[:, dv[h]] * _silu(r_ref[:, dv[h]])).astype(BF16)


def _gla(main, tail, w_alpha, b_alpha, norm_g, s0, batch, seq, chunk, t_valid, to_bf16=()):
    n_chunks = seq // chunk
    qk_w = GLA_HEADS * GLA_DK
    v_w = GLA_HEADS * GLA_DV
    wa = jnp.pad(w_alpha, ((0, LANES - GLA_RANK), (0, 0)))
    row_map = lambda b, c: (b * n_chunks + c, 0)
    const = lambda b, c: (0, 0)
    in_specs = [
        pl.BlockSpec((chunk, qk_w), row_map),
        pl.BlockSpec((chunk, qk_w), lambda b, c: (b * n_chunks + c, 1)),
        pl.BlockSpec((chunk, v_w), lambda b, c: (b * n_chunks + c, 1)),
        pl.BlockSpec((chunk, v_w), lambda b, c: (b * n_chunks + c, 2)),
        pl.BlockSpec((chunk, LANES), row_map),
        pl.BlockSpec((LANES, qk_w), const),
        pl.BlockSpec((1, qk_w), const),
        pl.BlockSpec((1, v_w), const),
    ]
    args = [main, main, main, main, tail, wa, b_alpha.reshape(1, qk_w), norm_g.reshape(1, v_w)]
    state_spec = pl.BlockSpec((1, GLA_HEADS, GLA_DK, GLA_DV), lambda b, c: (b, 0, 0, 0))
    if s0 is not None:
        in_specs.append(state_spec)
        args.append(s0)
    n_steps = batch * n_chunks
    cast_specs = [pl.BlockSpec((a.shape[0] // n_steps, a.shape[1]), row_map) for a in to_bf16]
    assert all(a.shape[0] % (n_steps * BF16_ROWS) == 0 for a in to_bf16)
    return pl.pallas_call(
        functools.partial(_gla_kernel, chunk=chunk, t_valid=t_valid, has_s0=s0 is not None,
                          n_cast=len(to_bf16)),
        grid=(batch, n_chunks),
        in_specs=in_specs + cast_specs,
        out_specs=[pl.BlockSpec((chunk, v_w), row_map), state_spec] + cast_specs,
        out_shape=[jax.ShapeDtypeStruct((batch * seq, v_w), BF16),
                   jax.ShapeDtypeStruct((batch, GLA_HEADS, GLA_DK, GLA_DV), F32)]
        + [jax.ShapeDtypeStruct(a.shape, BF16) for a in to_bf16],
        compiler_params=_params("parallel", "arbitrary"),
        name="gla_prompt" if s0 is None else "gla_sample",
    )(*args, *to_bf16)


def _select_blocks(imp_t, blk, cur):
    n = imp_t.shape[0]
    forced = (blk == 0) | ((blk >= cur - 1) & (blk <= cur))
    score = jnp.where(blk > cur, -jnp.inf, jnp.where(forced, jnp.inf, imp_t))
    rank = jnp.zeros(score.shape, F32)
    for i in range(n):
        si = score[i:i + 1, :]
        rank = rank + jnp.where(blk > i, _ind(si >= score), _ind(si > score))
    return (rank < NSA_TOPK) & (score > -jnp.inf)


def _nsa_prompt_kernel(pt_ref, q_ref, kc_ref, vc_ref, ks_ref, vs_ref, kw_ref, vw_ref, gt_ref, bg_ref, *refs,
                       seq, tq, tk, pages_per_step):
    page_refs = refs[:pages_per_step]
    (o_ref, means_ref, kcm, vcm, ksb, vst, kwb, vwt, qs_ref, sel_ref,
     ms_ref, accs_ref, mw_ref, accw_ref) = refs[pages_per_step:]
    qi = pl.program_id(2)
    n_blk = seq // NSA_BLOCK
    n_kt = seq // tk
    blk_per_kt = tk // NSA_BLOCK

    @pl.when(qi == 0)
    def _():
        kcm[...] = jnp.sum(kc_ref[...].reshape(n_blk, NSA_BLOCK, NSA_HD), axis=1) * (1.0 / NSA_BLOCK)
        vcm[...] = jnp.sum(vc_ref[...].reshape(n_blk, NSA_BLOCK, NSA_HD), axis=1) * (1.0 / NSA_BLOCK)
        ksb[...] = ks_ref[...].astype(BF16)
        kwb[...] = kw_ref[...].astype(BF16)
        ones_rows = _ind(lax.broadcasted_iota(jnp.int32, (BF16_ROWS, tk), 0) == 0, BF16)
        for kt in range(n_kt):
            rows = slice(kt * tk, (kt + 1) * tk)
            vst[kt] = jnp.concatenate([vs_ref[rows, :].T.astype(BF16), ones_rows], axis=0)
            vwt[kt] = jnp.concatenate([vw_ref[rows, :].T.astype(BF16), ones_rows], axis=0)

    t0 = qi * tq
    q = q_ref[...] * (NSA_HD ** -0.5 * LOG2_E)
    qs_ref[...] = jnp.concatenate(
        [q[:, g * NSA_HD:(g + 1) * NSA_HD] for g in range(NSA_GROUP)], axis=0).astype(BF16)
    qs = qs_ref[...]
    t_row = t0 + lax.broadcasted_iota(jnp.int32, (1, tq), 1)
    t_row_g = jnp.concatenate([t_row] * NSA_GROUP, axis=1)

    kc_b = kcm[...].astype(BF16)
    blk_col = lax.broadcasted_iota(jnp.int32, (n_blk, 1), 0)
    p_t = _masked_softmax2(_dot_nt(kc_b, qs), (blk_col + 1) * NSA_BLOCK - 1 <= t_row_g, axis=0)
    o_c = _dot_tn(vcm[...], p_t)
    imp_t = sum(p_t[:, g * tq:(g + 1) * tq] for g in range(NSA_GROUP))
    sel_ref[...] = jnp.where(_select_blocks(imp_t, blk_col, t_row // NSA_BLOCK), 0.0, NEG_BIG)

    n_loc = lax.broadcasted_iota(jnp.int32, (tk, tq), 0)
    t_loc = lax.broadcasted_iota(jnp.int32, (tk, tq), 1)
    on_lanes = lambda a: jnp.concatenate([a] * NSA_GROUP, axis=1)
    causal = on_lanes(jnp.where(n_loc <= t_loc, 0.0, NEG_BIG))
    far = on_lanes(jnp.where(n_loc > t_loc, 0.0, NEG_BIG))

    def sel_bias(kt):
        return on_lanes(jnp.concatenate(
            [jnp.broadcast_to(sel_ref[pl.ds(kt * blk_per_kt + j, 1), :], (NSA_BLOCK, tq))
             for j in range(blk_per_kt)], axis=0))

    sel_state, win_state = (ms_ref, accs_ref), (mw_ref, accw_ref)
    for m_ref, acc_ref in (sel_state, win_state):
        m_ref[...] = jnp.full(m_ref.shape, NEG_BIG, F32)
        acc_ref[...] = jnp.zeros(acc_ref.shape, F32)

    def steps(*work):
        half = NSA_GROUP * tq // 2
        for ls in (slice(0, half), slice(half, 2 * half)):
            s = [_dot_nt(k_ref[pl.ds(pl.multiple_of(kt * tk, tk), tk), :], qs_ref[ls, :]) + bias[:, ls]
                 for _, k_ref, _, kt, bias in work]
            m_old = [state[0][:, ls] for state, *_ in work]
            m_new = [jnp.maximum(mo, jnp.max(si, axis=0, keepdims=True)) for mo, si in zip(m_old, s)]
            p = [jnp.exp2(si - mn).astype(BF16) for si, mn in zip(s, m_new)]
            pv = [_dot(vt_ref[kt], pi) for (_, _, vt_ref, kt, _), pi in zip(work, p)]
            for (state, *_), mo, mn, pvi in zip(work, m_old, m_new, pv):
                state[1][:, ls] = jnp.exp2(mo - mn) * state[1][:, ls] + pvi
                state[0][:, ls] = mn

    def result(acc_ref):
        acc = acc_ref[...]
        return acc[:NSA_HD] / jnp.maximum(acc[NSA_HD:NSA_HD + 1], TINY)

    steps((sel_state, ksb, vst, qi, sel_bias(qi) + causal), (win_state, kwb, vwt, qi, causal))

    @pl.when(qi >= 1)
    def _():
        steps((sel_state, ksb, vst, qi - 1, sel_bias(qi - 1)), (win_state, kwb, vwt, qi - 1, far))

    def sel_body(i, carry):
        kt = qi - 2 - i
        steps((sel_state, ksb, vst, kt, sel_bias(kt)))
        return carry

    lax.fori_loop(0, qi - 1, sel_body, 0)

    gates = jax.nn.sigmoid(gt_ref[...] + bg_ref[...])
    g_c, g_s, g_w = (jnp.concatenate([gates[3 * g + br:3 * g + br + 1, :] for g in range(NSA_GROUP)], axis=1)
                     for br in range(3))
    o_t = g_c * o_c + g_s * result(accs_ref) + g_w * result(accw_ref)
    for g in range(NSA_GROUP):
        o_ref[:, g * NSA_HD:(g + 1) * NSA_HD] = o_t[:, g * tq:(g + 1) * tq].T.astype(BF16)

    for i, x_ref in enumerate(page_refs):
        x = x_ref[0].reshape(PAGE // NSA_BLOCK, NSA_BLOCK, KV_ROWS, NSA_HD)
        means_ref[0, i] = jnp.sum(x, axis=1) * (1.0 / NSA_BLOCK)


def _nsa_prompt(main, gates, b_gate, batch, seq, cache_cmp, page_table):
    tq = tk = NSA_WINDOW
    assert seq % tk == 0
    vt_rows = NSA_HD + BF16_ROWS
    nq = seq // tq
    q_lanes = NSA_GROUP * NSA_HD
    col0 = NSA_Q_W // NSA_HD
    seqs, n_pages = page_table.shape
    n_steps = batch * NSA_KV * nq
    pages_per_step = seqs * n_pages // n_steps
    groups = n_pages // pages_per_step
    assert pages_per_step * n_steps == seqs * n_pages and groups * pages_per_step == n_pages
    per_page = PAGE // NSA_BLOCK

    def kv_spec(branch, part):
        off = col0 + (2 * branch + part) * NSA_KV
        return pl.BlockSpec((seq, NSA_HD), lambda b, kv, qi, pt: (b, off + kv))

    def step_of(b, kv, qi):
        return (b * NSA_KV + kv) * nq + qi

    def page_spec(i):
        def index(b, kv, qi, pt):
            s = step_of(b, kv, qi)
            return (pt[s // groups, (s % groups) * pages_per_step + i], 0, 0)
        return pl.BlockSpec((1, PAGE * KV_ROWS, NSA_HD), index)

    def means_index(b, kv, qi, pt):
        s = step_of(b, kv, qi)
        return (s // groups, s % groups, 0, 0, 0)

    g_w = 3 * NSA_GROUP
    g_rows = NSA_GROUP * tq
    grid_spec = pltpu.PrefetchScalarGridSpec(
        num_scalar_prefetch=1,
        grid=(batch, NSA_KV, nq),
        in_specs=[pl.BlockSpec((tq, q_lanes), lambda b, kv, qi, pt: (b * nq + qi, kv))]
        + [kv_spec(br, part) for br in range(3) for part in range(2)]
        + [pl.BlockSpec((None, g_w, tq), lambda b, kv, qi, pt: (kv, 0, b * nq + qi)),
           pl.BlockSpec((None, g_w, 1), lambda b, kv, qi, pt: (kv, 0, 0))]
        + [page_spec(i) for i in range(pages_per_step)],
        out_specs=[pl.BlockSpec((tq, q_lanes), lambda b, kv, qi, pt: (b * nq + qi, kv)),
                   pl.BlockSpec((1, pages_per_step, per_page, KV_ROWS, NSA_HD), means_index)],
        scratch_shapes=[
            pltpu.VMEM((seq // NSA_BLOCK, NSA_HD), F32),
            pltpu.VMEM((seq // NSA_BLOCK, NSA_HD), F32),
            pltpu.VMEM((seq, NSA_HD), BF16),
            pltpu.VMEM((seq // tk, vt_rows, tk), BF16),
            pltpu.VMEM((seq, NSA_HD), BF16),
            pltpu.VMEM((seq // tk, vt_rows, tk), BF16),
            pltpu.VMEM((g_rows, NSA_HD), BF16),
            pltpu.VMEM((seq // NSA_BLOCK, tq), F32),
            pltpu.VMEM((1, g_rows), F32),
            pltpu.VMEM((vt_rows, g_rows), F32),
            pltpu.VMEM((1, g_rows), F32),
            pltpu.VMEM((vt_rows, g_rows), F32),
        ],
    )
    return pl.pallas_call(
        functools.partial(_nsa_prompt_kernel, seq=seq, tq=tq, tk=tk, pages_per_step=pages_per_step),
        grid_spec=grid_spec,
        out_shape=[jax.ShapeDtypeStruct((batch * seq, NSA_Q_W), BF16),
                   jax.ShapeDtypeStruct((seqs, n_pages, per_page, KV_ROWS, NSA_HD), F32)],
        compiler_params=_params("arbitrary", "arbitrary", "arbitrary"),
        name="nsa_prompt",
    )(page_table, main, main, main, main, main, main, main, gates, b_gate, *([cache_cmp] * pages_per_step))


def _kv_rows_out_kernel(c_ref, s_ref, w_ref, oc_ref, os_ref, ow_ref, *, rows, n_c):
    def spread(src, dst):
        for r in range(KV_ROWS):
            dst[pl.ds(r, rows, stride=KV_ROWS), :] = src[:, r * NSA_HD:(r + 1) * NSA_HD]

    spread(c_ref, oc_ref)
    spread(s_ref, os_ref)

    @pl.when(pl.program_id(1) == n_c - 1)
    def _():
        spread(w_ref, ow_ref)


def _kv_rows_out(main, batch, seq, keep):
    rows = keep
    n_c = seq // rows
    col0 = NSA_Q_W // NSA_KV_W
    tile = lambda br: pl.BlockSpec((rows, NSA_KV_W), lambda b, c: (b * n_c + c, col0 + br))
    out_tile = pl.BlockSpec((rows * KV_ROWS, NSA_HD), lambda b, c: (b * n_c + c, 0))
    return pl.pallas_call(
        functools.partial(_kv_rows_out_kernel, rows=rows, n_c=n_c),
        grid=(batch, n_c),
        in_specs=[tile(0), tile(1),
                  pl.BlockSpec((rows, NSA_KV_W), lambda b, c: (b * n_c + n_c - 1, col0 + 2))],
        out_specs=[out_tile, out_tile, pl.BlockSpec((rows * KV_ROWS, NSA_HD), lambda b, c: (b, 0))],
        out_shape=[jax.ShapeDtypeStruct((batch * seq * KV_ROWS, NSA_HD), F32)] * 2
        + [jax.ShapeDtypeStruct((batch * keep * KV_ROWS, NSA_HD), F32)],
        compiler_params=_params("arbitrary", "arbitrary"),
        name="nsa_kv_rows_out",
    )(main, main, main)


def _page_specs(n):
    return [pl.BlockSpec((1, PAGE * KV_ROWS, NSA_HD), lambda b, p, pt, i=i: (pt[b, p * n + i], 0, 0))
            for i in range(n)]


def _nsa_sample_kernel(pt_ref, q_ref, cm_ref, *refs, past, n_new, n_steps, per_step):
    pg_refs = refs[:per_step]
    (kvn_ref, wb_ref, kwn_ref, gt_ref, bg_ref, o_ref,
     qr_ref, sel_ref, m_ref, l_ref, acc_ref, oc_ref, ow_ref) = refs[per_step:]
    p = pl.program_id(1)
    k_w = NSA_KV * NSA_HD
    cols = NSA_KV * n_new * NSA_GROUP
    per_kv = n_new * NSA_GROUP
    n_cmp = past // NSA_BLOCK
    n_sel = -(-(past + n_new) // NSA_BLOCK)
    n_sel_pad = sel_ref.shape[0]
    lane = lax.broadcasted_iota(jnp.int32, (1, cols), 1)
    q_pos = past + (lane // NSA_GROUP) % n_new

    def heads_on_lanes(ref, n, part):
        return jnp.concatenate(
            [ref[0, pl.ds(part * NSA_KV + kv, n, stride=KV_ROWS), :] for kv in range(NSA_KV)], axis=1)

    def scores(keys):
        return _dot_nt(keys, qr_ref[...])

    def own_head(x_t):
        out = jnp.zeros((NSA_HD, cols), F32)
        for kv in range(NSA_KV):
            mine = (lane // per_kv) == kv
            out = out + jnp.where(mine, x_t[kv * NSA_HD:(kv + 1) * NSA_HD, :], 0.0)
        return out

    def attend_once(keys, vals, mask):
        p_t = _masked_softmax2(scores(keys), mask, axis=0)
        return own_head(_dot_tn(vals, p_t))

    def online(keys, vals, bias):
        s = scores(keys) + bias
        m_old = m_ref[...]
        m_new = jnp.maximum(m_old, jnp.max(s, axis=0, keepdims=True))
        p_t = jnp.exp2(s - m_new)
        alpha = jnp.exp2(m_old - m_new)
        l_ref[...] = alpha * l_ref[...] + jnp.sum(p_t, axis=0, keepdims=True)
        acc_ref[...] = alpha * acc_ref[...] + own_head(_dot_tn(vals, p_t))
        m_ref[...] = m_new

    @pl.when(p == 0)
    def _():
        q = q_ref[0] * (NSA_HD ** -0.5 * LOG2_E)
        r_kv = lax.broadcasted_iota(jnp.int32, (cols, k_w), 0) // per_kv
        c_kv = lax.broadcasted_iota(jnp.int32, (cols, k_w), 1) // NSA_HD
        qr_ref[...] = jnp.where(r_kv == c_kv, jnp.tile(q, (1, NSA_KV)), 0.0).astype(BF16)
        blk = lax.broadcasted_iota(jnp.int32, (n_cmp, 1), 0)
        p_t = _masked_softmax2(scores(heads_on_lanes(cm_ref, n_cmp, 0)),
                               (blk + 1) * NSA_BLOCK - 1 <= q_pos, axis=0)
        oc_ref[...] = own_head(_dot_tn(heads_on_lanes(cm_ref, n_cmp, 1), p_t))
        same = _ind(lax.broadcasted_iota(jnp.int32, (cols, cols), 0) // NSA_GROUP
                    == lax.broadcasted_iota(jnp.int32, (cols, cols), 1) // NSA_GROUP, BF16)
        imp = sum(jnp.dot(t, same, preferred_element_type=F32) for t in _split3(p_t))
        imp = jnp.concatenate([imp, jnp.zeros((n_sel_pad - n_cmp, cols), F32)], axis=0)
        blk_s = lax.broadcasted_iota(jnp.int32, (n_sel_pad, 1), 0)
        cur = q_pos // NSA_BLOCK
        forced = (blk_s == 0) | ((blk_s >= cur - 1) & (blk_s <= cur))
        score = jnp.where((blk_s > cur) | (blk_s >= n_sel), -jnp.inf, jnp.where(forced, jnp.inf, imp))
        blk_f = blk_s.astype(F32)

        def pick(_, carry):
            left, bias = carry
            top = jnp.max(left, axis=0, keepdims=True)
            first = jnp.min(jnp.where(left == top, blk_f, float(n_sel_pad)), axis=0, keepdims=True)
            hit = (blk_f == first) & (top > -jnp.inf)
            return jnp.where(hit, -jnp.inf, left), jnp.where(hit, 0.0, bias)

        _, bias = lax.fori_loop(0, NSA_TOPK, pick, (score, jnp.full(score.shape, NEG_BIG, F32)))
        sel_ref[...] = bias
        n_wb = wb_ref.shape[1] // KV_ROWS
        kw_all = jnp.concatenate([heads_on_lanes(wb_ref, n_wb, 0), kwn_ref[0][:, :k_w]], axis=0)
        vw_all = jnp.concatenate([heads_on_lanes(wb_ref, n_wb, 1), kwn_ref[0][:, k_w:]], axis=0)
        n_w = kw_all.shape[0]
        w_row = lax.broadcasted_iota(jnp.int32, (n_w, 1), 0)
        w_pos = past - n_wb + w_row
        d = q_pos - w_pos
        ow_ref[...] = attend_once(kw_all, vw_all,
                                  (d >= 0) & (d < NSA_WINDOW) & (w_pos >= 0) & (w_row < n_wb + n_new))
        m_ref[...] = jnp.full(m_ref.shape, NEG_BIG, F32)
        l_ref[...] = jnp.zeros(l_ref.shape, F32)
        acc_ref[...] = jnp.zeros(acc_ref.shape, F32)

    def page_part(part):
        return jnp.concatenate([heads_on_lanes(pg_ref, PAGE, part) for pg_ref in pg_refs], axis=0)

    blk_per_step = per_step * PAGE // NSA_BLOCK
    online(page_part(0), page_part(1), jnp.concatenate(
        [jnp.broadcast_to(sel_ref[pl.ds(p * blk_per_step + j, 1), :], (NSA_BLOCK, cols))
         for j in range(blk_per_step)], axis=0))

    @pl.when(p == n_steps - 1)
    def _():
        kvn = kvn_ref[0]
        n_row = lax.broadcasted_iota(jnp.int32, (kvn.shape[0], 1), 0)
        bias_new = jnp.where((n_row < n_new) & (past + n_row <= q_pos), 0.0, NEG_BIG)
        online(kvn[:, :k_w], kvn[:, k_w:], bias_new + sel_ref[pl.ds(past // NSA_BLOCK, 1), :])
        o_s = acc_ref[...] / jnp.maximum(l_ref[...], TINY)
        gates = jax.nn.sigmoid(gt_ref[0] + bg_ref[...])
        o_t = gates[0:1, :] * oc_ref[...] + gates[1:2, :] * o_s + gates[2:3, :] * ow_ref[...]
        eye = _ind(lax.broadcasted_iota(jnp.int32, (NSA_HD, NSA_HD), 0)
                   == lax.broadcasted_iota(jnp.int32, (NSA_HD, NSA_HD), 1), BF16)
        o_ref[0] = sum(lax.dot_general(t, eye, (((0,), (0,)), ((), ())), preferred_element_type=F32)
                       for t in _split3(o_t))


def _nsa_sample(q_rows, cmp_means, cache_slc, page_table, kv_new, win_buf, kw_new, gates, b_gate,
                past, n_new):
    seqs, n_pages = page_table.shape
    cols = q_rows.shape[1]
    cmp_rows = cmp_means.shape[1]
    n_sel_pad = -(-(-(-(past + n_new) // NSA_BLOCK)) // 8) * 8
    n_wb = win_buf.shape[1]
    pad_new = kv_new.shape[1]
    per_step = 32
    n_steps = n_pages // per_step
    seq_map = lambda b, p, pt: (b, 0, 0)
    return pl.pallas_call(
        functools.partial(_nsa_sample_kernel, past=past, n_new=n_new, n_steps=n_steps, per_step=per_step),
        grid_spec=pltpu.PrefetchScalarGridSpec(
            num_scalar_prefetch=1,
            grid=(seqs, n_steps),
            in_specs=[
                pl.BlockSpec((1, cols, NSA_HD), seq_map),
                pl.BlockSpec((1, cmp_rows, NSA_HD), seq_map),
                *_page_specs(per_step),
                pl.BlockSpec((1, pad_new, NSA_KV_W), seq_map),
                pl.BlockSpec((1, n_wb, NSA_HD), seq_map),
                pl.BlockSpec((1, pad_new, NSA_KV_W), seq_map),
                pl.BlockSpec((1, 3, cols), seq_map),
                pl.BlockSpec((3, cols), lambda b, p, pt: (0, 0)),
            ],
            out_specs=pl.BlockSpec((1, cols, NSA_HD), seq_map),
            scratch_shapes=[
                pltpu.VMEM((cols, NSA_KV * NSA_HD), BF16),
                pltpu.VMEM((n_sel_pad, cols), F32),
                pltpu.VMEM((1, cols), F32),
                pltpu.VMEM((1, cols), F32),
                pltpu.VMEM((NSA_HD, cols), F32),
                pltpu.VMEM((NSA_HD, cols), F32),
                pltpu.VMEM((NSA_HD, cols), F32),
            ],
        ),
        out_shape=jax.ShapeDtypeStruct((seqs, cols, NSA_HD), F32),
        compiler_params=_params("parallel", "arbitrary"),
        name="nsa_sample",
    )(page_table, q_rows, cmp_means, *([cache_slc] * per_step), kv_new, win_buf, kw_new, gates, b_gate)


def kernel(x_prompt, x_sample, c_prompt, c_sample, state_gla, cache_cmp_kv, cache_slc_kv, cache_win_kv,
           page_table, ada_w, ada_b, gla_w_in, gla_w_alpha, gla_b_alpha, gla_norm_g, gla_w_out, nsa_w_in,
           nsa_b_gate, nsa_w_out, ln_mix_g, ln_mix_b, ffn_w_in, ffn_w_out, ln_ffn_g, ln_ffn_b):
    batch, seq, _ = x_prompt.shape
    seqs, n_new, _ = x_sample.shape
    n_pages = page_table.shape[1]
    past = n_pages * PAGE
    rows_s = seqs * n_new
    tm_p = seq

    pad_rows = 16 - (batch + seqs)
    c_all = jnp.concatenate([c_prompt, c_sample, jnp.zeros((pad_rows, D_MODEL), F32)], axis=0)
    mod = _adaln(c_all, ada_w, ada_b).reshape(DEPTH, 16, 6, D_MODEL)

    def mods(layer):
        mp = [mod[layer, :batch, i].reshape(batch, 1, D_MODEL) for i in range(6)]
        ms = [jnp.repeat(mod[layer, batch:batch + seqs, i], n_new, axis=0).reshape(1, rows_s, D_MODEL)
              for i in range(6)]
        return mp, ms

    def mixer_out(o_p, o_s, xp, xs, layer, mp, ms, tag):
        return _out_ln((o_p, xp, mp[2], mp[4], mp[3]), (o_s, xs, ms[2], ms[4], ms[3]), w_out_mix[layer % 2],
                       layer // 2, ln_mix_g, ln_mix_b, layer, 512, tag)

    def ffn(hp, hs, xp, xs, layer, mp, ms, next_mods):
        act_p, act_s = _ffn_in(hp, hs, ffn_w_in, layer, tm_p, "ffn_in_%d" % layer)
        nxt_p, nxt_s = ((), ()) if next_mods is None else ((next_mods[0][1], next_mods[0][0]),
                                                            (next_mods[1][1], next_mods[1][0]))
        return _out_ln((act_p, xp, mp[5], *nxt_p), (act_s, xs, ms[5], *nxt_s), w_out_ffn, layer,
                       ln_ffn_g, ln_ffn_b, layer, 256, "ffn_out_%d" % layer)

    xp = x_prompt.reshape(batch * seq, D_MODEL)
    xs = x_sample.reshape(rows_s, D_MODEL)

    mp, ms = mods(0)
    main_p, tail_p, main_s, tail_s = _proj((xp, mp[1], mp[0]), (xs, ms[1], ms[0]), jnp.swapaxes(gla_w_in, 1, 2), 0,
                                           GLA_MAIN_W, tm_p, "gla_in")
    rows2d = lambda w: w.reshape(-1, w.shape[-1])
    gated_p, gla_state_p, *w_bf16 = _gla(main_p, tail_p, gla_w_alpha[0], gla_b_alpha[0], gla_norm_g[0], None,
                                         batch, seq, 256, 256,
                                         to_bf16=(rows2d(gla_w_out), rows2d(nsa_w_out), rows2d(ffn_w_out)))
    w_out_mix = (w_bf16[0].reshape(gla_w_out.shape), w_bf16[1].reshape(nsa_w_out.shape))
    w_out_ffn = w_bf16[2].reshape(ffn_w_out.shape)
    chunk_s = GLA_SUB

    def pad_steps(a):
        a = a.reshape(seqs, n_new, a.shape[-1])
        return jnp.pad(a, ((0, 0), (0, chunk_s - n_new), (0, 0))).reshape(seqs * chunk_s, a.shape[-1])

    gated_s, gla_state_s = _gla(pad_steps(main_s), pad_steps(tail_s), gla_w_alpha[0], gla_b_alpha[0],
                                gla_norm_g[0], state_gla[:, 0], seqs, chunk_s, chunk_s, n_new)
    gated_s = gated_s.reshape(seqs, chunk_s, -1)[:, :n_new].reshape(rows_s, -1)
    xp, xs, hp, hs = mixer_out(gated_p, gated_s, xp, xs, 0, mp, ms, "gla_out")
    xp, xs, hp, hs = ffn(hp, hs, xp, xs, 0, mp, ms, mods(1))

    mp, ms = mods(1)
    main_p, tail_p, main_s, tail_s = _proj(hp, hs, jnp.swapaxes(nsa_w_in, 1, 2), 0, NSA_MAIN_W, tm_p, "nsa_in")
    n_gate = 3 * NSA_HEADS
    g_w = 3 * NSA_GROUP
    gates_p = tail_p[:, :n_gate].reshape(batch * seq, NSA_KV, g_w).transpose(1, 2, 0)
    b_gate_p = nsa_b_gate[0].reshape(NSA_KV, g_w, 1)
    n_phys = cache_cmp_kv.shape[0]
    page_rows = lambda cache: cache.reshape(n_phys, PAGE * KV_ROWS, NSA_HD)
    o_p, cmp_means = _nsa_prompt(main_p, gates_p, b_gate_p, batch, seq, page_rows(cache_cmp_kv), page_table)

    kv_rows = lambda m, br: m[:, NSA_Q_W + br * NSA_KV_W:NSA_Q_W + (br + 1) * NSA_KV_W]
    kv_shape = (2, NSA_KV, NSA_HD)
    keep_p = min(NSA_WINDOW, seq)
    cmp_rows, slc_rows, win_rows = _kv_rows_out(main_p, batch, seq, keep_p)
    cmp_kv_p = cmp_rows.reshape(batch, 1, seq, *kv_shape)
    slc_kv_p = slc_rows.reshape(batch, 1, seq, *kv_shape)
    win_kv_p = win_rows.reshape(batch, 1, keep_p, *kv_shape)

    cmp_kv_s = kv_rows(main_s, 0).reshape(seqs, 1, n_new, *kv_shape)
    slc_kv_s = kv_rows(main_s, 1).reshape(seqs, 1, n_new, *kv_shape)
    kw_new = kv_rows(main_s, 2).reshape(seqs, n_new, NSA_KV_W)
    n_wb = cache_win_kv.shape[2]
    win_buf = cache_win_kv[:, 0].reshape(seqs, n_wb * KV_ROWS, NSA_HD)
    all_w = jnp.concatenate([win_buf, kw_new.reshape(seqs, n_new * KV_ROWS, NSA_HD)], axis=1)
    keep_s = min(NSA_WINDOW, n_wb + n_new)
    win_kv_s = all_w[:, (n_wb + n_new - keep_s) * KV_ROWS:].reshape(seqs, 1, keep_s, *kv_shape)

    cmp_means = cmp_means.reshape(seqs, past // NSA_BLOCK * KV_ROWS, NSA_HD)
    pad_new = 8
    pad8 = lambda a: jnp.pad(a, ((0, 0), (0, pad_new - n_new), (0, 0)))
    cols = NSA_KV * n_new * NSA_GROUP
    q_rows = (main_s[:, :NSA_Q_W].reshape(seqs, n_new, NSA_KV, NSA_GROUP, NSA_HD)
              .transpose(0, 2, 1, 3, 4).reshape(seqs, cols, NSA_HD))
    gates_s = (tail_s[:, :n_gate].reshape(seqs, n_new, NSA_KV, NSA_GROUP, 3)
               .transpose(0, 4, 2, 1, 3).reshape(seqs, 3, cols))
    b_gate_s = jnp.broadcast_to(nsa_b_gate[0].reshape(1, NSA_KV, NSA_GROUP, 3), (n_new, NSA_KV, NSA_GROUP, 3))
    b_gate_s = b_gate_s.transpose(3, 1, 0, 2).reshape(3, cols)
    o_s = _nsa_sample(q_rows, cmp_means, page_rows(cache_slc_kv), page_table,
                      pad8(slc_kv_s.reshape(seqs, n_new, NSA_KV_W)), win_buf, pad8(kw_new), gates_s, b_gate_s,
                      past, n_new)
    o_s = (o_s.reshape(seqs, NSA_KV, n_new, NSA_GROUP, NSA_HD).transpose(0, 2, 1, 3, 4)
           .reshape(rows_s, NSA_Q_W).astype(BF16))

    xp, xs, hp, hs = mixer_out(o_p, o_s, xp, xs, 1, mp, ms, "nsa_out")
    xp, xs = ffn(hp, hs, xp, xs, 1, mp, ms, None)

    return (xp.reshape(batch, seq, D_MODEL), xs.reshape(seqs, n_new, D_MODEL),
            gla_state_p[:, None], gla_state_s[:, None], cmp_kv_p, cmp_kv_s, slc_kv_p, slc_kv_s,
            win_kv_p, win_kv_s)
```
